```python
import math
import jax, jax.numpy as jnp
from jax import lax
import numpy as np

D_MODEL = 2048
BATCH = 4
SEQ = 2048
DEPTH = 1

N_MEM = 256
HEAD_DIM = 128
FOX_HEADS = 8
FOX_W = FOX_HEADS * HEAD_DIM
LRU_W = D_MODEL - FOX_W
LRU_BLOCKS = 8
LRU_BLOCK = LRU_W // LRU_BLOCKS
LRU_C = 8.0
CONV_W = 4
MIX_W = FOX_W + LRU_W
IN_W = 3 * FOX_W + FOX_HEADS + 2 * LRU_W
XATT_HEADS = 4
XATT_W = XATT_HEADS * HEAD_DIM
FFN_HIDDEN = int(math.ceil((8 * D_MODEL / 3) / 256) * 256)
Q_BLOCK = 128
RMS_EPS = 1e-6

SPLITS = (FOX_W, 2 * FOX_W, 3 * FOX_W, 3 * FOX_W + FOX_HEADS, 3 * FOX_W + FOX_HEADS + LRU_W)

kernel_name = "hymba_fox_rglru_memxattn_block"


def rmsnorm(x, g):
    xf = x.astype(jnp.float32)
    y = xf * lax.rsqrt(jnp.mean(xf * xf, axis=-1, keepdims=True) + RMS_EPS)
    return (y * g.astype(jnp.float32)).astype(x.dtype)


def forgetting_attention(q, k, v, c):
    B, H, S, dh = q.shape
    nb = S // Q_BLOCK
    scale = 1.0 / math.sqrt(dh)
    qb = q.reshape(B, H, nb, Q_BLOCK, dh).transpose(2, 0, 1, 3, 4)
    cb = c.reshape(B, H, nb, Q_BLOCK).transpose(2, 0, 1, 3)
    k_pos = jnp.arange(S)

    def one_block(args):
        q_i, c_i, i = args
        s = jnp.einsum('bhqd,bhkd->bhqk', q_i, k, preferred_element_type=jnp.float32) * scale
        s = s + c_i[..., None] - c[:, :, None, :]
        q_pos = i * Q_BLOCK + jnp.arange(Q_BLOCK)
        causal = k_pos[None, :] <= q_pos[:, None]
        s = jnp.where(causal, s, -jnp.inf)
        p = jax.nn.softmax(s, axis=-1)
        return jnp.einsum('bhqk,bhkd->bhqd', p.astype(v.dtype), v)

    o = lax.map(one_block, (qb, cb, jnp.arange(nb)))
    return o.transpose(1, 2, 0, 3, 4).reshape(B, H, S, dh)


def causal_depthwise_conv(u, w, b):
    S = u.shape[1]
    up = jnp.pad(u, ((0, 0), (CONV_W - 1, 0), (0, 0)))
    return b + sum(w[j] * up[:, j:j + S] for j in range(CONV_W))


def rg_lru(u, w_ra, b_ra, w_ri, b_ri, lam):
    B, S, W = u.shape
    ub = u.reshape(B, S, LRU_BLOCKS, LRU_BLOCK)
    r = jax.nn.sigmoid(jnp.einsum('bsnc,ncd->bsnd', ub, w_ra).reshape(B, S, W) + b_ra)
    i = jax.nn.sigmoid(jnp.einsum('bsnc,ncd->bsnd', ub, w_ri).reshape(B, S, W) + b_ri)
    log_a = -LRU_C * r.astype(jnp.float32) * jax.nn.softplus(-lam.astype(jnp.float32))
    a = jnp.exp(log_a)
    b_in = jnp.sqrt(-jnp.expm1(2.0 * log_a)) * (i * u).astype(jnp.float32)

    def combine(left, right):
        a1, b1 = left
        a2, b2 = right
        return a1 * a2, a2 * b1 + b2

    _, h = lax.associative_scan(combine, (a, b_in), axis=1)
    return h.astype(u.dtype)


def setup_inputs(seed: int = 0) -> dict:
    key = jax.random.key(seed)
    ks = jax.random.split(key, 32)
    f32 = jnp.float32

    def nrm(k, shape, scale):
        return jax.random.normal(k, shape, f32) * scale

    def gain(k, shape):
        return 1.0 + 0.02 * jax.random.normal(k, shape, f32)

    L = DEPTH
    a_c = jax.random.uniform(ks[13], (L, LRU_W), f32, 0.9, 0.999)
    s_lam = a_c ** (1.0 / LRU_C)
    lam = jnp.log(s_lam) - jnp.log1p(-s_lam)
    return {
        "x": nrm(ks[0], (BATCH, SEQ, D_MODEL), 1.0),
        "mem": nrm(ks[1], (BATCH, N_MEM, D_MODEL), 1.0),
        "g_mix": gain(ks[2], (L, D_MODEL)),
        "w_in": nrm(ks[3], (L, D_MODEL, IN_W), D_MODEL ** -0.5),
        "b_f": jax.random.uniform(ks[4], (L, FOX_HEADS), f32, 3.0, 5.0),
        "g_q": gain(ks[5], (L, HEAD_DIM)),
        "g_k": gain(ks[6], (L, HEAD_DIM)),
        "conv_w": nrm(ks[7], (L, CONV_W, LRU_W), CONV_W ** -0.5),
        "conv_b": nrm(ks[8], (L, LRU_W), 0.02),
        "w_ra": nrm(ks[9], (L, LRU_BLOCKS, LRU_BLOCK, LRU_BLOCK), LRU_BLOCK ** -0.5),
        "b_ra": nrm(ks[10], (L, LRU_W), 0.02),
        "w_ri": nrm(ks[11], (L, LRU_BLOCKS, LRU_BLOCK, LRU_BLOCK), LRU_BLOCK ** -0.5),
        "b_ri": nrm(ks[12], (L, LRU_W), 0.02),
        "lam": lam,
        "g_fox_out": gain(ks[14], (L, FOX_W)),
        "g_lru_out": gain(ks[15], (L, LRU_W)),
        "w_out": nrm(ks[16], (L, MIX_W, D_MODEL), MIX_W ** -0.5),
        "g_xattn": gain(ks[17], (L, D_MODEL)),
        "g_mem": gain(ks[18], (L, D_MODEL)),
        "w_cq": nrm(ks[19], (L, D_MODEL, XATT_W), D_MODEL ** -0.5),
        "w_ckv": nrm(ks[20], (L, D_MODEL, 2 * XATT_W), D_MODEL ** -0.5),
        "g_cq": gain(ks[21], (L, HEAD_DIM)),
        "g_ck": gain(ks[22], (L, HEAD_DIM)),
        "w_co": nrm(ks[23], (L, XATT_W, D_MODEL), XATT_W ** -0.5),
        "g_ffn": gain(ks[24], (L, D_MODEL)),
        "w_gate_up": nrm(ks[25], (L, D_MODEL, 2 * FFN_HIDDEN), D_MODEL ** -0.5),
        "w_down": nrm(ks[26], (L, FFN_HIDDEN, D_MODEL), FFN_HIDDEN ** -0.5),
    }


def reference(x, mem, g_mix, w_in, b_f, g_q, g_k, conv_w, conv_b, w_ra, b_ra, w_ri, b_ri,
              lam, g_fox_out, g_lru_out, w_out, g_xattn, g_mem, w_cq, w_ckv, g_cq, g_ck,
              w_co, g_ffn, w_gate_up, w_down):
    B, S, _ = x.shape
    M = mem.shape[1]
    for l in range(DEPTH):
        h = rmsnorm(x, g_mix[l])
        proj = h @ w_in[l]
        q, k, v, f_logit, u, gate = jnp.split(proj, SPLITS, axis=-1)
        q = rmsnorm(q.reshape(B, S, FOX_HEADS, HEAD_DIM), g_q[l]).transpose(0, 2, 1, 3)
        k = rmsnorm(k.reshape(B, S, FOX_HEADS, HEAD_DIM), g_k[l]).transpose(0, 2, 1, 3)
        v = v.reshape(B, S, FOX_HEADS, HEAD_DIM).transpose(0, 2, 1, 3)
        log_f = jax.nn.log_sigmoid((f_logit + b_f[l]).astype(jnp.float32))
        c = lax.cumsum(log_f, axis=1).transpose(0, 2, 1)
        o_fox = forgetting_attention(q, k, v, c)
        o_fox = o_fox.transpose(0, 2, 1, 3).reshape(B, S, FOX_W)

        u = causal_depthwise_conv(u, conv_w[l], conv_b[l])
        y_lru = rg_lru(u, w_ra[l], b_ra[l], w_ri[l], b_ri[l], lam[l]) * jax.nn.gelu(gate)

        mix = jnp.concatenate([rmsnorm(o_fox, g_fox_out[l]), rmsnorm(y_lru, g_lru_out[l])], axis=-1)
        x = x + mix @ w_out[l]

        hq = rmsnorm(x, g_xattn[l])
        mn = rmsnorm(mem, g_mem[l])
        cq = rmsnorm((hq @ w_cq[l]).reshape(B, S, XATT_HEADS, HEAD_DIM), g_cq[l])
        ck, cv = jnp.split(mn @ w_ckv[l], 2, axis=-1)
        ck = rmsnorm(ck.reshape(B, M, XATT_HEADS, HEAD_DIM), g_ck[l])
        cv = cv.reshape(B, M, XATT_HEADS, HEAD_DIM)
        s = jnp.einsum('bshd,bmhd->bhsm', cq, ck, preferred_element_type=jnp.float32) / math.sqrt(HEAD_DIM)
        p = jax.nn.softmax(s, axis=-1)
        o_x = jnp.einsum('bhsm,bmhd->bshd', p.astype(cv.dtype), cv).reshape(B, S, XATT_W)
        x = x + o_x @ w_co[l]

        hf = rmsnorm(x, g_ffn[l])
        f_gate, f_up = jnp.split(hf @ w_gate_up[l], 2, axis=-1)
        x = x + (jax.nn.silu(f_gate) * f_up) @ w_down[l]
    return x
```

```python
import functools
import math

import jax
import jax.numpy as jnp
from jax import lax
from jax.experimental import pallas as pl
from jax.experimental.pallas import tpu as pltpu

F32 = jnp.float32
BF16 = jnp.bfloat16

HEAD_DIM = 128
FOX_HEADS = 8
FOX_W = FOX_HEADS * HEAD_DIM
LRU_BLOCKS = 8
LRU_BLOCK = 128
LRU_W = LRU_BLOCKS * LRU_BLOCK
LRU_C = 8.0
CONV_W = 4
XATT_HEADS = 4
XATT_W = XATT_HEADS * HEAD_DIM
RMS_EPS = 1e-6
F_ROWS = 16
Q_BLOCK = 256

V7X_VMEM_LIMIT_BYTES = 56 * 1024 * 1024


def _rms(x, g):
    ms = jnp.mean(x * x, axis=-1, keepdims=True)
    return x * lax.rsqrt(ms + RMS_EPS) * g


def _dot(a, b):
    return jnp.dot(a, b, preferred_element_type=F32)


def _dot_nt(a, b):
    return lax.dot_general(a, b, (((1,), (1,)), ((), ())), preferred_element_type=F32)


def _softplus(x):
    return jnp.maximum(x, 0.0) + jnp.log1p(jnp.exp(-jnp.abs(x)))


def _params(*sem):
    return pltpu.CompilerParams(dimension_semantics=sem,
                                vmem_limit_bytes=V7X_VMEM_LIMIT_BYTES)


def _proj_kernel(*refs, n_norm_tiles, with_f, tn):
    if with_f:
        x_ref, g_ref, w_ref, gh_ref, wf_ref, o_ref, f_ref, h_scr = refs
    else:
        x_ref, g_ref, w_ref, gh_ref, o_ref, h_scr = refs
    j = pl.program_id(1)

    @pl.when(j == 0)
    def _():
        h = _rms(x_ref[...], g_ref[...]).astype(BF16)
        h_scr[...] = h
        if with_f:
            f_ref[...] = _dot_nt(wf_ref[...], h)

    acc = _dot(h_scr[...], w_ref[...])

    if n_norm_tiles > 0:
        @pl.when(j < n_norm_tiles)
        def _():
            for hh in range(tn // HEAD_DIM):
                sl = slice(hh * HEAD_DIM, (hh + 1) * HEAD_DIM)
                o_ref[:, sl] = _rms(acc[:, sl], gh_ref[...]).astype(o_ref.dtype)

        @pl.when(j >= n_norm_tiles)
        def _():
            o_ref[...] = acc.astype(o_ref.dtype)
    else:
        o_ref[...] = acc.astype(o_ref.dtype)


def _proj(x2d, g, w, gh, wf, *, tm, tn, n_norm_tiles, out_dtype):
    T, D = x2d.shape
    N = w.shape[1]
    with_f = wf is not None
    n_gh = gh.shape[0]
    in_specs = [
        pl.BlockSpec((tm, D), lambda i, j: (i, 0)),
        pl.BlockSpec((1, D), lambda i, j: (0, 0)),
        pl.BlockSpec((D, tn), lambda i, j: (0, j)),
        pl.BlockSpec((None, 1, HEAD_DIM), lambda i, j: (jnp.minimum(j, n_gh - 1), 0, 0)),
    ]
    args = [x2d, g, w, gh]
    out_shape = [jax.ShapeDtypeStruct((T, N), out_dtype)]
    out_specs = [pl.BlockSpec((tm, tn), lambda i, j: (i, j))]
    if with_f:
        in_specs.append(pl.BlockSpec((F_ROWS, D), lambda i, j: (0, 0)))
        args.append(wf)
        out_shape.append(jax.ShapeDtypeStruct((F_ROWS, T), F32))
        out_specs.append(pl.BlockSpec((F_ROWS, tm), lambda i, j: (0, i)))
    res = pl.pallas_call(
        functools.partial(_proj_kernel, n_norm_tiles=n_norm_tiles, with_f=with_f, tn=tn),
        grid=(T // tm, N // tn),
        in_specs=in_specs,
        out_specs=out_specs,
        out_shape=out_shape,
        scratch_shapes=[pltpu.VMEM((tm, D), BF16)],
        compiler_params=_params("parallel", "arbitrary"),
    )(*args)
    return res if with_f else res[0]


def _fox_kernel(q_ref, k_ref, v_ref, f_ref, bf_ref, o_ref, c_scr, *, seq):
    h = pl.program_id(1)

    @pl.when(h == 0)
    def _():
        z = f_ref[...] + bf_ref[...]
        c = jnp.minimum(z, 0.0) - jnp.log1p(jnp.exp(-jnp.abs(z)))
        lane = lax.broadcasted_iota(jnp.int32, c.shape, 1)
        d = 1
        while d < seq:
            c = c + jnp.where(lane >= d, pltpu.roll(c, d, axis=1), 0.0)
            d *= 2
        c_scr[...] = c

    crow = c_scr[pl.ds(h, 1), :]
    scale = 1.0 / math.sqrt(HEAD_DIM)
    tq = Q_BLOCK
    row = lax.broadcasted_iota(jnp.int32, (tq, tq), 0)
    col = lax.broadcasted_iota(jnp.int32, (tq, tq), 1)
    causal = row >= col
    for qi in range(seq // tq):
        lo, hi = qi * tq, (qi + 1) * tq
        q = q_ref[lo:hi, :]
        sd = _dot_nt(q, k_ref[lo:hi, :]) * scale - crow[:, lo:hi]
        sd = jnp.where(causal, sd, -jnp.inf)
        m = jnp.max(sd, axis=-1, keepdims=True)
        if qi > 0:
            sp = _dot_nt(q, k_ref[0:lo, :]) * scale - crow[:, 0:lo]
            m = jnp.maximum(m, jnp.max(sp, axis=-1, keepdims=True))
            pp = jnp.exp(sp - m)
            l = jnp.sum(pp, axis=-1, keepdims=True)
            acc = _dot(pp.astype(BF16), v_ref[0:lo, :])
        pd = jnp.exp(sd - m)
        if qi > 0:
            l = l + jnp.sum(pd, axis=-1, keepdims=True)
            acc = acc + _dot(pd.astype(BF16), v_ref[lo:hi, :])
        else:
            l = jnp.sum(pd, axis=-1, keepdims=True)
            acc = _dot(pd.astype(BF16), v_ref[lo:hi, :])
        o_ref[lo:hi, :] = acc / l


def _fox_attention(qkv, f_t, b_f, *, batch, seq):
    nh = FOX_HEADS
    return pl.pallas_call(
        functools.partial(_fox_kernel, seq=seq),
        grid=(batch, nh),
        in_specs=[
            pl.BlockSpec((None, seq, HEAD_DIM), lambda b, h: (b, 0, h)),
            pl.BlockSpec((None, seq, HEAD_DIM), lambda b, h: (b, 0, nh + h)),
            pl.BlockSpec((None, seq, HEAD_DIM), lambda b, h: (b, 0, 2 * nh + h)),
            pl.BlockSpec((F_ROWS, seq), lambda b, h: (0, b)),
            pl.BlockSpec((F_ROWS, 1), lambda b, h: (0, 0)),
        ],
        out_specs=pl.BlockSpec((None, seq, HEAD_DIM), lambda b, h: (b, 0, h)),
        out_shape=jax.ShapeDtypeStruct((batch, seq, FOX_W), F32),
        scratch_shapes=[pltpu.VMEM((F_ROWS, seq), F32)],
        compiler_params=_params("parallel", "arbitrary"),
    )(qkv, qkv, qkv, f_t, b_f)


def _lru_kernel(u_ref, gate_ref, cw_ref, cb_ref, wg_ref, bra_ref, bri_ref, lam_ref, o_ref, *, seq):
    u = u_ref[...]
    row = lax.broadcasted_iota(jnp.int32, u.shape, 0)

    def shift(x, d, fill):
        return jnp.where(row >= d, pltpu.roll(x, d, axis=0), fill)

    cw = cw_ref[...]
    uc = cw[0:1, :] * shift(u, 3, 0.0)
    uc = uc + cw[1:2, :] * shift(u, 2, 0.0)
    uc = uc + cw[2:3, :] * shift(u, 1, 0.0)
    uc = uc + cw[3:4, :] * u
    uc = cb_ref[...] + uc

    gates = _dot(uc.astype(BF16), wg_ref[...])
    r = jax.nn.sigmoid(gates[:, :LRU_BLOCK] + bra_ref[...])
    i = jax.nn.sigmoid(gates[:, LRU_BLOCK:] + bri_ref[...])
    log_a = (-LRU_C * r) * _softplus(-lam_ref[...])
    a = jnp.exp(log_a)
    t = jnp.tanh(log_a)
    b = jnp.sqrt(-2.0 * t / (1.0 - t)) * (i * uc)

    d = 1
    while d < seq:
        b = a * shift(b, d, 0.0) + b
        if 2 * d < seq:
            a = a * shift(a, d, 1.0)
        d *= 2
    o_ref[...] = b * jax.nn.gelu(gate_ref[...])


def _rg_lru(ug, cw, cb, wg, bra, bri, lam, *, batch, seq):
    nb = LRU_BLOCKS
    vec = lambda: pl.BlockSpec((1, LRU_BLOCK), lambda b, n: (0, n))
    return pl.pallas_call(
        functools.partial(_lru_kernel, seq=seq),
        grid=(batch, nb),
        in_specs=[
            pl.BlockSpec((None, seq, LRU_BLOCK), lambda b, n: (b, 0, n)),
            pl.BlockSpec((None, seq, LRU_BLOCK), lambda b, n: (b, 0, nb + n)),
            pl.BlockSpec((CONV_W, LRU_BLOCK), lambda b, n: (0, n)),
            vec(),
            pl.BlockSpec((None, LRU_BLOCK, 2 * LRU_BLOCK), lambda b, n: (n, 0, 0)),
            vec(), vec(), vec(),
        ],
        out_specs=pl.BlockSpec((None, seq, LRU_BLOCK), lambda b, n: (b, 0, n)),
        out_shape=jax.ShapeDtypeStruct((batch, seq, LRU_W), F32),
        compiler_params=_params("parallel", "arbitrary"),
    )(ug, ug, cw, cb, wg, bra, bri, lam)


def _out_kernel(of_ref, yl_ref, gf_ref, gl_ref, w_ref, x_ref, o_ref, mix_scr):
    @pl.when(pl.program_id(1) == 0)
    def _():
        mix_scr[:, :FOX_W] = _rms(of_ref[...], gf_ref[...]).astype(BF16)
        mix_scr[:, FOX_W:] = _rms(yl_ref[...], gl_ref[...]).astype(BF16)

    o_ref[...] = x_ref[...] + _dot(mix_scr[...], w_ref[...])


def _out_proj(o_fox, y_lru, gf, gl, w, x2d, *, tm, tn):
    T, D = x2d.shape
    return pl.pallas_call(
        _out_kernel,
        grid=(T // tm, D // tn),
        in_specs=[
            pl.BlockSpec((tm, FOX_W), lambda i, j: (i, 0)),
            pl.BlockSpec((tm, LRU_W), lambda i, j: (i, 0)),
            pl.BlockSpec((1, FOX_W), lambda i, j: (0, 0)),
            pl.BlockSpec((1, LRU_W), lambda i, j: (0, 0)),
            pl.BlockSpec((FOX_W + LRU_W, tn), lambda i, j: (0, j)),
            pl.BlockSpec((tm, tn), lambda i, j: (i, j)),
        ],
        out_specs=pl.BlockSpec((tm, tn), lambda i, j: (i, j)),
        out_shape=jax.ShapeDtypeStruct((T, D), F32),
        scratch_shapes=[pltpu.VMEM((tm, FOX_W + LRU_W), BF16)],
        compiler_params=_params("parallel", "arbitrary"),
    )(o_fox, y_lru, gf, gl, w, x2d)


def _xattn_kernel(x_ref, g_ref, wq_ref, gq_ref, ckv_ref, wo_ref, o_ref):
    x = x_ref[...]
    cq = _dot(_rms(x, g_ref[...]).astype(BF16), wq_ref[...])
    scale = 1.0 / math.sqrt(HEAD_DIM)
    heads = []
    for hh in range(XATT_HEADS):
        sl = slice(hh * HEAD_DIM, (hh + 1) * HEAD_DIM)
        qh = _rms(cq[:, sl], gq_ref[...]).astype(BF16)
        kh = ckv_ref[:, sl]
        vh = ckv_ref[:, XATT_W + hh * HEAD_DIM:XATT_W + (hh + 1) * HEAD_DIM]
        s = _dot_nt(qh, kh) * scale
        e = jnp.exp(s - jnp.max(s, axis=-1, keepdims=True))
        l = jnp.sum(e, axis=-1, keepdims=True)
        heads.append((_dot(e.astype(BF16), vh) / l).astype(BF16))
    ox = jnp.concatenate(heads, axis=-1)
    o_ref[...] = x + _dot(ox, wo_ref[...])


def _cross_attention(x3d, g, wq, gq, ckv, wo, *, tm):
    B, S, D = x3d.shape
    M = ckv.shape[1]
    return pl.pallas_call(
        _xattn_kernel,
        grid=(B, S // tm),
        in_specs=[
            pl.BlockSpec((None, tm, D), lambda b, i: (b, i, 0)),
            pl.BlockSpec((1, D), lambda b, i: (0, 0)),
            pl.BlockSpec((D, XATT_W), lambda b, i: (0, 0)),
            pl.BlockSpec((1, HEAD_DIM), lambda b, i: (0, 0)),
            pl.BlockSpec((None, M, 2 * XATT_W), lambda b, i: (b, 0, 0)),
            pl.BlockSpec((XATT_W, D), lambda b, i: (0, 0)),
        ],
        out_specs=pl.BlockSpec((None, tm, D), lambda b, i: (b, i, 0)),
        out_shape=jax.ShapeDtypeStruct((B, S, D), F32),
        compiler_params=_params("parallel", "parallel"),
    )(x3d, g, wq, gq, ckv, wo)


def _ffn_kernel(x_ref, g_ref, wg_ref, wu_ref, wd_ref, o_ref, h_scr):
    @pl.when(pl.program_id(1) == 0)
    def _():
        x = x_ref[...]
        h_scr[...] = _rms(x, g_ref[...]).astype(BF16)
        o_ref[...] = x

    h = h_scr[...]
    gate = _dot(h, wg_ref[...])
    up = _dot(h, wu_ref[...])
    act = (jax.nn.silu(gate) * up).astype(BF16)
    o_ref[...] += _dot(act, wd_ref[...])


def _ffn(x2d, g, w_gu, w_d, *, tm, th):
    T, D = x2d.shape
    H = w_d.shape[0]
    nk = H // th
    return pl.pallas_call(
        _ffn_kernel,
        grid=(T // tm, nk),
        in_specs=[
            pl.BlockSpec((tm, D), lambda i, k: (i, 0)),
            pl.BlockSpec((1, D), lambda i, k: (0, 0)),
            pl.BlockSpec((D, th), lambda i, k: (0, k)),
            pl.BlockSpec((D, th), lambda i, k: (0, nk + k)),
            pl.BlockSpec((th, D), lambda i, k: (k, 0)),
        ],
        out_specs=pl.BlockSpec((tm, D), lambda i, k: (i, 0)),
        out_shape=jax.ShapeDtypeStruct((T, D), F32),
        scratch_shapes=[pltpu.VMEM((tm, D), BF16)],
        compiler_params=_params("parallel", "arbitrary"),
    )(x2d, g, w_gu, w_gu, w_d)


def kernel(x, mem, g_mix, w_in, b_f, g_q, g_k, conv_w, conv_b, w_ra, b_ra, w_ri, b_ri, lam,
           g_fox_out, g_lru_out, w_out, g_xattn, g_mem, w_cq, w_ckv, g_cq, g_ck, w_co, g_ffn,
           w_gate_up, w_down):
    B, S, D = x.shape
    M = mem.shape[1]
    T = B * S
    depth = g_mix.shape[0]
    row = lambda v: v.reshape(1, -1).astype(F32)

    for l in range(depth):
        wl = w_in[l]
        w_qkv = wl[:, :3 * FOX_W].astype(BF16)
        w_f = jnp.zeros((F_ROWS, D), BF16).at[:FOX_HEADS].set(
            wl[:, 3 * FOX_W:3 * FOX_W + FOX_HEADS].T.astype(BF16))
        w_ug = wl[:, 3 * FOX_W + FOX_HEADS:].astype(BF16)
        g_qk = jnp.stack([g_q[l], g_k[l]]).reshape(2, 1, HEAD_DIM).astype(F32)
        bf_col = jnp.zeros((F_ROWS, 1), F32).at[:FOX_HEADS, 0].set(b_f[l])
        w_gates = jnp.concatenate([w_ra[l], w_ri[l]], axis=-1).astype(BF16)
        g_ckh = g_ck[l].reshape(1, 1, HEAD_DIM).astype(F32)

        x2d = x.reshape(T, D)

        qkv, f_t = _proj(x2d, row(g_mix[l]), w_qkv, g_qk, w_f,
                         tm=1024, tn=1024, n_norm_tiles=2, out_dtype=BF16)
        ug = _proj(x2d, row(g_mix[l]), w_ug, g_qk, None,
                   tm=1024, tn=1024, n_norm_tiles=0, out_dtype=F32)
        o_fox = _fox_attention(qkv.reshape(B, S, 3 * FOX_W), f_t, bf_col, batch=B, seq=S)
        y_lru = _rg_lru(ug.reshape(B, S, 2 * LRU_W), conv_w[l], row(conv_b[l]), w_gates,
                        row(b_ra[l]), row(b_ri[l]), row(lam[l]), batch=B, seq=S)
        x2d = _out_proj(o_fox.reshape(T, FOX_W), y_lru.reshape(T, LRU_W),
                        row(g_fox_out[l]), row(g_lru_out[l]), w_out[l].astype(BF16), x2d,
                        tm=512, tn=1024)

        ckv = _proj(mem.reshape(B * M, D), row(g_mem[l]), w_ckv[l].astype(BF16), g_ckh, None,
                    tm=B * M, tn=XATT_W, n_norm_tiles=1, out_dtype=BF16)
        x3d = _cross_attention(x2d.reshape(B, S, D), row(g_xattn[l]), w_cq[l].astype(BF16),
                               row(g_cq[l]), ckv.reshape(B, M, 2 * XATT_W),
                               w_co[l].astype(BF16), tm=512)

        x2d = _ffn(x3d.reshape(T, D), row(g_ffn[l]), w_gate_up[l].astype(BF16),
                   w_down[l].astype(BF16), tm=512, th=512)
        x = x2d.reshape(B, S, D)
    return x
```

```python
import functools
import math

import jax
import jax.numpy as jnp
from jax import lax
from jax.experimental import pallas as pl
from jax.experimental.pallas import tpu as pltpu

F32 = jnp.float32
BF16 = jnp.bfloat16

HEAD_DIM = 128
FOX_HEADS = 8
FOX_W = FOX_HEADS * HEAD_DIM
LRU_BLOCKS = 8
LRU_BLOCK = 128
LRU_W = LRU_BLOCKS * LRU_BLOCK
LRU_C = 8.0
CONV_W = 4
XATT_HEADS = 4
XATT_W = XATT_HEADS * HEAD_DIM
RMS_EPS = 1e-6
F_ROWS = 16
Q_BLOCK = 256
QK_TILES = 2
PROJ_CHUNK = 256
SUBLANES = 8

V7X_VMEM_LIMIT_BYTES = 60 * 1024 * 1024


def _rms(x, g):
    ms = jnp.mean(x * x, axis=-1, keepdims=True)
    return x * lax.rsqrt(ms + RMS_EPS) * g


def _dot(a, b):
    return jnp.dot(a, b, preferred_element_type=F32)


def _dot_nt(a, b):
    return lax.dot_general(a, b, (((1,), (1,)), ((), ())), preferred_element_type=F32)


def _softplus(x):
    return jnp.maximum(x, 0.0) + jnp.log1p(jnp.exp(-jnp.abs(x)))


def _params(*sem):
    return pltpu.CompilerParams(dimension_semantics=sem,
                                vmem_limit_bytes=V7X_VMEM_LIMIT_BYTES)


def _inproj_kernel(x_ref, g_ref, w_ref, gh_ref, wf_ref, qkv_ref, ug_ref, f_ref, h_scr, *, tn):
    j = pl.program_id(1)

    @pl.when(j == 0)
    def _():
        h = _rms(x_ref[...], g_ref[...]).astype(BF16)
        h_scr[...] = h
        f_ref[...] = _dot_nt(wf_ref[...], h)

    def chunked(emit):
        for c in range(tn // PROJ_CHUNK):
            sl = slice(c * PROJ_CHUNK, (c + 1) * PROJ_CHUNK)
            emit(sl, _dot(h_scr[...], w_ref[:, sl]))

    def emit_normed(sl, acc):
        for hh in range(PROJ_CHUNK // HEAD_DIM):
            lo = hh * HEAD_DIM
            qkv_ref[:, sl.start + lo:sl.start + lo + HEAD_DIM] = _rms(
                acc[:, lo:lo + HEAD_DIM], gh_ref[...]).astype(BF16)

    def emit_v(sl, acc):
        qkv_ref[:, sl] = acc.astype(BF16)

    def emit_ug(sl, acc):
        ug_ref[:, sl] = acc

    pl.when(j < QK_TILES)(lambda: chunked(emit_normed))
    pl.when(j == QK_TILES)(lambda: chunked(emit_v))
    pl.when(j > QK_TILES)(lambda: chunked(emit_ug))


def _in_proj(x2d, g, w, g_qk, wf, *, tm, tn):
    T, D = x2d.shape
    n_qkv = 3 * FOX_W // tn
    n_ug = 2 * LRU_W // tn
    return pl.pallas_call(
        functools.partial(_inproj_kernel, tn=tn),
        grid=(T // tm, n_qkv + n_ug),
        in_specs=[
            pl.BlockSpec((tm, D), lambda i, j: (i, 0)),
            pl.BlockSpec((1, D), lambda i, j: (0, 0)),
            pl.BlockSpec((D, tn), lambda i, j: (0, j)),
            pl.BlockSpec((None, 1, HEAD_DIM), lambda i, j: (jnp.minimum(j, QK_TILES - 1), 0, 0)),
            pl.BlockSpec((F_ROWS, D), lambda i, j: (0, 0)),
        ],
        out_specs=[
            pl.BlockSpec((tm, tn), lambda i, j: (i, jnp.minimum(j, n_qkv - 1))),
            pl.BlockSpec((tm, tn), lambda i, j: (i, jnp.maximum(j - n_qkv, 0))),
            pl.BlockSpec((F_ROWS, tm), lambda i, j: (0, i)),
        ],
        out_shape=[
            jax.ShapeDtypeStruct((T, 3 * FOX_W), BF16),
            jax.ShapeDtypeStruct((T, 2 * LRU_W), F32),
            jax.ShapeDtypeStruct((F_ROWS, T), F32),
        ],
        scratch_shapes=[pltpu.VMEM((tm, D), BF16)],
        compiler_params=_params("parallel", "arbitrary"),
    )(x2d, g, w, g_qk, wf)


def _ckv_kernel(m_ref, g_ref, w_ref, gh_ref, o_ref):
    h = _rms(m_ref[...], g_ref[...]).astype(BF16)
    acc = _dot(h, w_ref[...])
    for hh in range(XATT_HEADS):
        sl = slice(hh * HEAD_DIM, (hh + 1) * HEAD_DIM)
        o_ref[:, sl] = _rms(acc[:, sl], gh_ref[...]).astype(BF16)
    o_ref[:, XATT_W:] = acc[:, XATT_W:].astype(BF16)


def _ckv_proj(mem2d, g, w, gh):
    R, D = mem2d.shape
    full = lambda shape: pl.BlockSpec(shape, lambda i: (0,) * len(shape))
    return pl.pallas_call(
        _ckv_kernel,
        grid=(1,),
        in_specs=[full((R, D)), full((1, D)), full((D, 2 * XATT_W)), full((1, HEAD_DIM))],
        out_specs=full((R, 2 * XATT_W)),
        out_shape=jax.ShapeDtypeStruct((R, 2 * XATT_W), BF16),
        compiler_params=_params("arbitrary"),
    )(mem2d, g, w, gh)


def _fox_kernel(q_ref, k_ref, v_ref, f_ref, bf_ref, o_ref, c_scr, *, seq):
    h = pl.program_id(1)

    @pl.when(h == 0)
    def _():
        z = f_ref[...] + bf_ref[...]
        c = jnp.minimum(z, 0.0) - jnp.log1p(jnp.exp(-jnp.abs(z)))
        lane = lax.broadcasted_iota(jnp.int32, c.shape, 1)
        d = 1
        while d < seq:
            c = c + jnp.where(lane >= d, pltpu.roll(c, d, axis=1), 0.0)
            d *= 2
        c_scr[...] = c

    crow = c_scr[pl.ds(h, 1), :]
    scale = 1.0 / math.sqrt(HEAD_DIM)
    tq = Q_BLOCK
    row = lax.broadcasted_iota(jnp.int32, (tq, tq), 0)
    col = lax.broadcasted_iota(jnp.int32, (tq, tq), 1)
    causal = row >= col
    for qi in range(seq // tq):
        lo, hi = qi * tq, (qi + 1) * tq
        q = q_ref[lo:hi, :]
        sd = _dot_nt(q, k_ref[lo:hi, :]) * scale - crow[:, lo:hi]
        sd = jnp.where(causal, sd, -jnp.inf)
        m = jnp.max(sd, axis=-1, keepdims=True)
        if qi > 0:
            sp = _dot_nt(q, k_ref[0:lo, :]) * scale - crow[:, 0:lo]
            m = jnp.maximum(m, jnp.max(sp, axis=-1, keepdims=True))
            pp = jnp.exp(sp - m)
            l = jnp.sum(pp, axis=-1, keepdims=True)
            acc = _dot(pp.astype(BF16), v_ref[0:lo, :])
        pd = jnp.exp(sd - m)
        if qi > 0:
            l = l + jnp.sum(pd, axis=-1, keepdims=True)
            acc = acc + _dot(pd.astype(BF16), v_ref[lo:hi, :])
        else:
            l = jnp.sum(pd, axis=-1, keepdims=True)
            acc = _dot(pd.astype(BF16), v_ref[lo:hi, :])
        o_ref[lo:hi, :] = acc / l


def _fox_attention(qkv, f_t, b_f, *, batch, seq):
    nh = FOX_HEADS
    return pl.pallas_call(
        functools.partial(_fox_kernel, seq=seq),
        grid=(batch, nh),
        in_specs=[
            pl.BlockSpec((None, seq, HEAD_DIM), lambda b, h: (b, 0, h)),
            pl.BlockSpec((None, seq, HEAD_DIM), lambda b, h: (b, 0, nh + h)),
            pl.BlockSpec((None, seq, HEAD_DIM), lambda b, h: (b, 0, 2 * nh + h)),
            pl.BlockSpec((F_ROWS, seq), lambda b, h: (0, b)),
            pl.BlockSpec((F_ROWS, 1), lambda b, h: (0, 0)),
        ],
        out_specs=pl.BlockSpec((None, seq, HEAD_DIM), lambda b, h: (b, 0, h)),
        out_shape=jax.ShapeDtypeStruct((batch, seq, FOX_W), F32),
        scratch_shapes=[pltpu.VMEM((F_ROWS, seq), F32)],
        compiler_params=_params("parallel", "arbitrary"),
    )(qkv, qkv, qkv, f_t, b_f)


def _lru_kernel(u_ref, gate_ref, cw_ref, cb_ref, wg_ref, bra_ref, bri_ref, lam_ref, o_ref,
                at_scr, bt_scr, hp_scr, *, seq):
    u = u_ref[...]
    row = lax.broadcasted_iota(jnp.int32, (SUBLANES, LRU_BLOCK), 0)

    def shift(x, d):
        r = pltpu.roll(x, d, axis=0)
        head = jnp.where(row >= d, r[:SUBLANES], 0.0)
        return jnp.concatenate([head, r[SUBLANES:]], axis=0)

    cw = cw_ref[...]
    uc = cw[0:1, :] * shift(u, 3)
    uc = uc + cw[1:2, :] * shift(u, 2)
    uc = uc + cw[2:3, :] * shift(u, 1)
    uc = uc + cw[3:4, :] * u
    uc = cb_ref[...] + uc

    gates = _dot(uc.astype(BF16), wg_ref[...])
    half_c = (-0.5 * LRU_C) * _softplus(-lam_ref[...])
    log_a = half_c * jnp.tanh(0.5 * (gates[:, :LRU_BLOCK] + bra_ref[...])) + half_c
    i = 0.5 * jnp.tanh(0.5 * (gates[:, LRU_BLOCK:] + bri_ref[...])) + 0.5
    a = jnp.exp(log_a)
    t = jnp.tanh(log_a)
    w = -2.0 * t
    sqrt_w = jnp.where(w > 0.0, w * lax.rsqrt(w), 0.0)
    b = (sqrt_w * lax.rsqrt(1.0 - t)) * (i * uc)

    groups = seq // SUBLANES
    gshape = (groups, SUBLANES, LRU_BLOCK)
    a3 = a.reshape(gshape)
    b3 = b.reshape(gshape)
    sub = lax.broadcasted_iota(jnp.int32, gshape, 1)
    d = 1
    while d < SUBLANES:
        keep = sub >= d
        b3 = a3 * jnp.where(keep, pltpu.roll(b3, d, axis=1), 0.0) + b3
        a3 = a3 * jnp.where(keep, pltpu.roll(a3, d, axis=1), 1.0)
        d *= 2
    at_scr[...] = jnp.broadcast_to(a3[:, SUBLANES - 1:, :], gshape).reshape(seq, LRU_BLOCK)
    bt_scr[...] = jnp.broadcast_to(b3[:, SUBLANES - 1:, :], gshape).reshape(seq, LRU_BLOCK)

    def carry(g, h):
        rows = pl.ds(pl.multiple_of(g * SUBLANES, SUBLANES), SUBLANES)
        hp_scr[rows, :] = h
        return at_scr[rows, :] * h + bt_scr[rows, :]

    lax.fori_loop(0, groups, carry, jnp.zeros((SUBLANES, LRU_BLOCK), F32), unroll=8)
    hs = b3.reshape(seq, LRU_BLOCK) + a3.reshape(seq, LRU_BLOCK) * hp_scr[...]
    o_ref[...] = hs * jax.nn.gelu(gate_ref[...])


def _rg_lru(ug, cw, cb, wg, bra, bri, lam, *, batch, seq):
    nb = LRU_BLOCKS
    vec = lambda: pl.BlockSpec((1, LRU_BLOCK), lambda b, n: (0, n))
    return pl.pallas_call(
        functools.partial(_lru_kernel, seq=seq),
        grid=(batch, nb),
        in_specs=[
            pl.BlockSpec((None, seq, LRU_BLOCK), lambda b, n: (b, 0, n)),
            pl.BlockSpec((None, seq, LRU_BLOCK), lambda b, n: (b, 0, nb + n)),
            pl.BlockSpec((CONV_W, LRU_BLOCK), lambda b, n: (0, n)),
            vec(),
            pl.BlockSpec((None, LRU_BLOCK, 2 * LRU_BLOCK), lambda b, n: (n, 0, 0)),
            vec(), vec(), vec(),
        ],
        out_specs=pl.BlockSpec((None, seq, LRU_BLOCK), lambda b, n: (b, 0, n)),
        out_shape=jax.ShapeDtypeStruct((batch, seq, LRU_W), F32),
        scratch_shapes=[pltpu.VMEM((seq, LRU_BLOCK), F32)] * 3,
        compiler_params=_params("parallel", "arbitrary"),
    )(ug, ug, cw, cb, wg, bra, bri, lam)


def _out_kernel(of_ref, yl_ref, gf_ref, gl_ref, w_ref, x_ref, o_ref, *, chunk):
    mf = _rms(of_ref[...], gf_ref[...]).astype(BF16)
    ml = _rms(yl_ref[...], gl_ref[...]).astype(BF16)
    for c in range(o_ref.shape[1] // chunk):
        sl = slice(c * chunk, (c + 1) * chunk)
        acc = _dot(mf, w_ref[:FOX_W, sl]) + _dot(ml, w_ref[FOX_W:, sl])
        o_ref[:, sl] = x_ref[:, sl] + acc


def _out_proj(o_fox, y_lru, gf, gl, w, x2d, *, tm):
    T, D = x2d.shape
    return pl.pallas_call(
        functools.partial(_out_kernel, chunk=2 * PROJ_CHUNK),
        grid=(T // tm,),
        in_specs=[
            pl.BlockSpec((tm, FOX_W), lambda i: (i, 0)),
            pl.BlockSpec((tm, LRU_W), lambda i: (i, 0)),
            pl.BlockSpec((1, FOX_W), lambda i: (0, 0)),
            pl.BlockSpec((1, LRU_W), lambda i: (0, 0)),
            pl.BlockSpec((FOX_W + LRU_W, D), lambda i: (0, 0)),
            pl.BlockSpec((tm, D), lambda i: (i, 0)),
        ],
        out_specs=pl.BlockSpec((tm, D), lambda i: (i, 0)),
        out_shape=jax.ShapeDtypeStruct((T, D), F32),
        compiler_params=_params("parallel"),
    )(o_fox, y_lru, gf, gl, w, x2d)


def _xattn_kernel(x_ref, g_ref, wq_ref, gq_ref, ckv_ref, wo_ref, o_ref):
    x = x_ref[...]
    cq = _dot(_rms(x, g_ref[...]).astype(BF16), wq_ref[...])
    scale = 1.0 / math.sqrt(HEAD_DIM)
    heads = []
    for hh in range(XATT_HEADS):
        sl = slice(hh * HEAD_DIM, (hh + 1) * HEAD_DIM)
        qh = _rms(cq[:, sl], gq_ref[...]).astype(BF16)
        kh = ckv_ref[:, sl]
        vh = ckv_ref[:, XATT_W + hh * HEAD_DIM:XATT_W + (hh + 1) * HEAD_DIM]
        s = _dot_nt(qh, kh) * scale
        e = jnp.exp(s - jnp.max(s, axis=-1, keepdims=True))
        l = jnp.sum(e, axis=-1, keepdims=True)
        heads.append((_dot(e.astype(BF16), vh) / l).astype(BF16))
    ox = jnp.concatenate(heads, axis=-1)
    o_ref[...] = x + _dot(ox, wo_ref[...])


def _cross_attention(x3d, g, wq, gq, ckv, wo, *, tm):
    B, S, D = x3d.shape
    M = ckv.shape[1]
    return pl.pallas_call(
        _xattn_kernel,
        grid=(B, S // tm),
        in_specs=[
            pl.BlockSpec((None, tm, D), lambda b, i: (b, i, 0)),
            pl.BlockSpec((1, D), lambda b, i: (0, 0)),
            pl.BlockSpec((D, XATT_W), lambda b, i: (0, 0)),
            pl.BlockSpec((1, HEAD_DIM), lambda b, i: (0, 0)),
            pl.BlockSpec((None, M, 2 * XATT_W), lambda b, i: (b, 0, 0)),
            pl.BlockSpec((XATT_W, D), lambda b, i: (0, 0)),
        ],
        out_specs=pl.BlockSpec((None, tm, D), lambda b, i: (b, i, 0)),
        out_shape=jax.ShapeDtypeStruct((B, S, D), F32),
        compiler_params=_params("parallel", "parallel"),
    )(x3d, g, wq, gq, ckv, wo)


def _ffn_kernel(x_ref, g_ref, wg_ref, wu_ref, wd_ref, o_ref, h_scr):
    @pl.when(pl.program_id(1) == 0)
    def _():
        x = x_ref[...]
        h_scr[...] = _rms(x, g_ref[...]).astype(BF16)
        o_ref[...] = x

    h = h_scr[...]
    gate = _dot(h, wg_ref[...])
    up = _dot(h, wu_ref[...])
    act = (jax.nn.silu(gate) * up).astype(BF16)
    o_ref[...] += _dot(act, wd_ref[...])


def _ffn(x2d, g, w_gu, w_d, *, tm, th):
    T, D = x2d.shape
    H = w_d.shape[0]
    nk = H // th
    return pl.pallas_call(
        _ffn_kernel,
        grid=(T // tm, nk),
        in_specs=[
            pl.BlockSpec((tm, D), lambda i, k: (i, 0)),
            pl.BlockSpec((1, D), lambda i, k: (0, 0)),
            pl.BlockSpec((D, th), lambda i, k: (0, k)),
            pl.BlockSpec((D, th), lambda i, k: (0, nk + k)),
            pl.BlockSpec((th, D), lambda i, k: (k, 0)),
        ],
        out_specs=pl.BlockSpec((tm, D), lambda i, k: (i, 0)),
        out_shape=jax.ShapeDtypeStruct((T, D), F32),
        scratch_shapes=[pltpu.VMEM((tm, D), BF16)],
        compiler_params=_params("parallel", "arbitrary"),
    )(x2d, g, w_gu, w_gu, w_d)


def kernel(x, mem, g_mix, w_in, b_f, g_q, g_k, conv_w, conv_b, w_ra, b_ra, w_ri, b_ri, lam,
           g_fox_out, g_lru_out, w_out, g_xattn, g_mem, w_cq, w_ckv, g_cq, g_ck, w_co, g_ffn,
           w_gate_up, w_down):
    B, S, D = x.shape
    M = mem.shape[1]
    T = B * S
    depth = g_mix.shape[0]
    row = lambda v: v.reshape(1, -1).astype(F32)

    for l in range(depth):
        wl = w_in[l]
        f_lo = 3 * FOX_W
        w_proj = jnp.concatenate([wl[:, :f_lo], wl[:, f_lo + FOX_HEADS:]], axis=1).astype(BF16)
        w_f = jnp.zeros((F_ROWS, D), BF16).at[:FOX_HEADS].set(
            wl[:, f_lo:f_lo + FOX_HEADS].T.astype(BF16))
        g_qk = jnp.stack([g_q[l], g_k[l]]).reshape(2, 1, HEAD_DIM).astype(F32)
        bf_col = jnp.zeros((F_ROWS, 1), F32).at[:FOX_HEADS, 0].set(b_f[l])
        w_gates = jnp.concatenate([w_ra[l], w_ri[l]], axis=-1).astype(BF16)

        x2d = x.reshape(T, D)

        qkv, ug, f_t = _in_proj(x2d, row(g_mix[l]), w_proj, g_qk, w_f, tm=1024, tn=1024)
        o_fox = _fox_attention(qkv.reshape(B, S, 3 * FOX_W), f_t, bf_col, batch=B, seq=S)
        y_lru = _rg_lru(ug.reshape(B, S, 2 * LRU_W), conv_w[l], row(conv_b[l]), w_gates,
                        row(b_ra[l]), row(b_ri[l]), row(lam[l]), batch=B, seq=S)
        x2d = _out_proj(o_fox.reshape(T, FOX_W), y_lru.reshape(T, LRU_W),
                        row(g_fox_out[l]), row(g_lru_out[l]), w_out[l].astype(BF16), x2d,
                        tm=512)

        ckv = _ckv_proj(mem.reshape(B * M, D), row(g_mem[l]), w_ckv[l].astype(BF16),
                        row(g_ck[l]))
        x3d = _cross_attention(x2d.reshape(B, S, D), row(g_xattn[l]), w_cq[l].astype(BF16),
                               row(g_cq[l]), ckv.reshape(B, M, 2 * XATT_W),
                               w_co[l].astype(BF16), tm=512)

        x2d = _ffn(x3d.reshape(T, D), row(g_ffn[l]), w_gate_up[l].astype(BF16),
                   w_down[l].astype(BF16), tm=1024, th=512)
        x = x2d.reshape(B, S, D)
    return x
```

```python
import functools
import math

import jax
import jax.numpy as jnp
from jax import lax
from jax.experimental import pallas as pl
from jax.experimental.pallas import tpu as pltpu

F32 = jnp.float32
BF16 = jnp.bfloat16

HEAD_DIM = 128
FOX_HEADS = 8
FOX_W = FOX_HEADS * HEAD_DIM
LRU_BLOCKS = 8
LRU_BLOCK = 128
LRU_W = LRU_BLOCKS * LRU_BLOCK
LRU_C = 8.0
CONV_W = 4
XATT_HEADS = 4
XATT_W = XATT_HEADS * HEAD_DIM
RMS_EPS = 1e-6
F_ROWS = 16
Q_BLOCK = 512
K_CHUNK = 256
SCORE_LOOKAHEAD = 3
V_PAD_ROWS = 16
PROJ_CHUNK = 256
SUBLANES = 8
BIAS_PIECES = 3

V7X_VMEM_LIMIT_BYTES = 60 * 1024 * 1024


def _rms(x, g):
    ms = jnp.mean(x * x, axis=-1, keepdims=True)
    return x * lax.rsqrt(ms + RMS_EPS) * g


def _dot(a, b):
    return jnp.dot(a, b, preferred_element_type=F32)


def _dot_nt(a, b):
    return lax.dot_general(a, b, (((1,), (1,)), ((), ())), preferred_element_type=F32)


def _softplus(x):
    return jnp.maximum(x, 0.0) + jnp.log1p(jnp.exp(-jnp.abs(x)))


def _params(*sem):
    return pltpu.CompilerParams(dimension_semantics=sem,
                                vmem_limit_bytes=V7X_VMEM_LIMIT_BYTES)


def _wstage_kernel(w_ref, o_ref):
    o_ref[:, :2 * FOX_W] = w_ref[:, :2 * FOX_W].astype(BF16)
    ug_lo = 3 * FOX_W + FOX_HEADS
    o_ref[:, 2 * FOX_W:] = w_ref[:, ug_lo:ug_lo + 2 * LRU_W].astype(BF16)


def _stage_in_weights(w, *, tr):
    D, N = w.shape
    n_out = 2 * FOX_W + 2 * LRU_W
    return pl.pallas_call(
        _wstage_kernel,
        grid=(D // tr,),
        in_specs=[pl.BlockSpec((tr, N), lambda i: (i, 0))],
        out_specs=pl.BlockSpec((tr, n_out), lambda i: (i, 0)),
        out_shape=jax.ShapeDtypeStruct((D, n_out), BF16),
        compiler_params=_params("parallel"),
    )(w)


def _inproj_kernel(x_ref, g_ref, w_ref, wvt_ref, gh_ref, wf_ref,
                   qk_ref, vt_ref, ug_ref, f_ref, h_scr, *, tn, n_qk):
    j = pl.program_id(1)

    @pl.when(j == 0)
    def _():
        h = _rms(x_ref[...], g_ref[...]).astype(BF16)
        h_scr[...] = h
        f_ref[...] = _dot_nt(wf_ref[...], h)

    @pl.when(j < n_qk)
    def _():
        for c in range(tn // PROJ_CHUNK):
            acc = _dot(h_scr[...], w_ref[:, c * PROJ_CHUNK:(c + 1) * PROJ_CHUNK])
            for hh in range(PROJ_CHUNK // HEAD_DIM):
                lo = c * PROJ_CHUNK + hh * HEAD_DIM
                qk_ref[:, lo:lo + HEAD_DIM] = _rms(
                    acc[:, hh * HEAD_DIM:(hh + 1) * HEAD_DIM], gh_ref[...]).astype(BF16)

    @pl.when(j == n_qk)
    def _():
        for c in range(vt_ref.shape[0] // PROJ_CHUNK):
            rows = slice(c * PROJ_CHUNK, (c + 1) * PROJ_CHUNK)
            vt_ref[rows, :] = _dot_nt(wvt_ref[rows, :], h_scr[...]).astype(BF16)

    @pl.when(j > n_qk)
    def _():
        for c in range(tn // PROJ_CHUNK):
            sl = slice(c * PROJ_CHUNK, (c + 1) * PROJ_CHUNK)
            ug_ref[:, sl] = _dot(h_scr[...], w_ref[:, sl])


def _in_proj(x2d, g, w, w_vt, g_qk, wf, *, tm, tn):
    T, D = x2d.shape
    n_qk = 2 * FOX_W // tn
    n_ug = 2 * LRU_W // tn
    tiles_per_gain = FOX_W // tn
    clip = lambda v, n: jnp.clip(v, 0, n - 1)
    w_tile = lambda j: jnp.where(j <= n_qk, clip(j, n_qk), j - 1)
    return pl.pallas_call(
        functools.partial(_inproj_kernel, tn=tn, n_qk=n_qk),
        grid=(T // tm, n_qk + 1 + n_ug),
        in_specs=[
            pl.BlockSpec((tm, D), lambda i, j: (i, 0)),
            pl.BlockSpec((1, D), lambda i, j: (0, 0)),
            pl.BlockSpec((D, tn), lambda i, j: (0, w_tile(j))),
            pl.BlockSpec((FOX_W, D), lambda i, j: (0, 0), pipeline_mode=pl.Buffered(1)),
            pl.BlockSpec((None, 1, HEAD_DIM), lambda i, j: (clip(j // tiles_per_gain, 2), 0, 0)),
            pl.BlockSpec((F_ROWS, D), lambda i, j: (0, 0)),
        ],
        out_specs=[
            pl.BlockSpec((tm, tn), lambda i, j: (i, clip(j, n_qk))),
            pl.BlockSpec((FOX_W, tm), lambda i, j: (0, i)),
            pl.BlockSpec((tm, tn), lambda i, j: (i, clip(j - n_qk - 1, n_ug))),
            pl.BlockSpec((F_ROWS, tm), lambda i, j: (0, i)),
        ],
        out_shape=[
            jax.ShapeDtypeStruct((T, 2 * FOX_W), BF16),
            jax.ShapeDtypeStruct((FOX_W, T), BF16),
            jax.ShapeDtypeStruct((T, 2 * LRU_W), F32),
            jax.ShapeDtypeStruct((F_ROWS, T), F32),
        ],
        scratch_shapes=[pltpu.VMEM((tm, D), BF16)],
        compiler_params=_params("parallel", "arbitrary"),
    )(x2d, g, w, w_vt, g_qk, wf)


def _ckv_kernel(m_ref, g_ref, w_ref, gh_ref, o_ref):
    h = _rms(m_ref[...], g_ref[...]).astype(BF16)
    acc = _dot(h, w_ref[...])
    for hh in range(XATT_HEADS):
        sl = slice(hh * HEAD_DIM, (hh + 1) * HEAD_DIM)
        o_ref[:, sl] = _rms(acc[:, sl], gh_ref[...]).astype(BF16)
    o_ref[:, XATT_W:] = acc[:, XATT_W:].astype(BF16)


def _ckv_proj(mem2d, g, w, gh):
    R, D = mem2d.shape
    full = lambda shape: pl.BlockSpec(shape, lambda i: (0,) * len(shape))
    return pl.pallas_call(
        _ckv_kernel,
        grid=(1,),
        in_specs=[full((R, D)), full((1, D)), full((D, 2 * XATT_W)), full((1, HEAD_DIM))],
        out_specs=full((R, 2 * XATT_W)),
        out_shape=jax.ShapeDtypeStruct((R, 2 * XATT_W), BF16),
        compiler_params=_params("arbitrary"),
    )(mem2d, g, w, gh)


def _fox_kernel(q_ref, k_ref, vt_ref, f_ref, bf_ref, o_ref, kaug_scr, vaug_scr, *, seq):
    h = pl.program_id(1)

    @pl.when(h == 0)
    def _():
        z = f_ref[...] + bf_ref[...]
        c = jnp.minimum(z, 0.0) - jnp.log1p(jnp.exp(-jnp.abs(z)))
        lane = lax.broadcasted_iota(jnp.int32, c.shape, 1)
        d = 1
        while d < seq:
            c = c + jnp.where(lane >= d, pltpu.roll(c, d, axis=1), 0.0)
            d *= 2
        bias = c[:FOX_HEADS] * (-math.sqrt(HEAD_DIM))
        pieces = []
        for _ in range(BIAS_PIECES):
            p = bias.astype(BF16).astype(F32)
            pieces.append(p)
            bias = bias - p
        pad = jnp.zeros((HEAD_DIM - BIAS_PIECES * FOX_HEADS, seq), F32)
        kaug_scr[:, HEAD_DIM:] = jnp.concatenate(pieces + [pad], axis=0).T.astype(BF16)

    kaug_scr[:, :HEAD_DIM] = k_ref[...]
    vaug_scr[:HEAD_DIM, :] = vt_ref[...]
    ones_row = lax.broadcasted_iota(jnp.int32, (V_PAD_ROWS, seq), 0) == 0
    vaug_scr[HEAD_DIM:, :] = jnp.where(ones_row, 1.0, 0.0).astype(BF16)

    tq, tk = Q_BLOCK, K_CHUNK
    lane = lax.broadcasted_iota(jnp.int32, (tq, HEAD_DIM), 1)
    mine = lane == h
    for p in range(1, BIAS_PIECES):
        mine = mine | (lane == h + p * FOX_HEADS)
    selector = jnp.where(mine, 1.0, 0.0).astype(BF16)
    key = lax.broadcasted_iota(jnp.int32, (tk, tq), 0)
    qry = lax.broadcasted_iota(jnp.int32, (tk, tq), 1)
    exp2_scale = math.log2(math.e) / math.sqrt(HEAD_DIM)

    def scores(lo, k0):
        q_aug = jnp.concatenate([q_ref[lo:lo + tq, :], selector], axis=1)
        return _dot_nt(kaug_scr[k0:k0 + tk, :], q_aug)

    steps = [(qi * tq, kc * tk) for qi in range(seq // tq) for kc in range((qi + 1) * tq // tk)]
    pending = [scores(*st) for st in steps[:SCORE_LOOKAHEAD]]
    for idx, (lo, k0) in enumerate(steps):
        t = pending.pop(0)
        if idx + SCORE_LOOKAHEAD < len(steps):
            pending.append(scores(*steps[idx + SCORE_LOOKAHEAD]))
        if k0 + tk > lo:
            t = jnp.where(key + (k0 - lo) <= qry, t, -jnp.inf)
        cm = jnp.max(t, axis=0, keepdims=True)
        m_new = cm if k0 == 0 else jnp.maximum(m, cm)
        p = jnp.exp2((t - m_new) * exp2_scale).astype(BF16)
        pv = _dot(vaug_scr[:, k0:k0 + tk], p)
        if k0 == 0:
            acc = pv
        else:
            acc = acc * jnp.exp2((m - m_new) * exp2_scale) + pv
        m = m_new
        if k0 + tk == lo + tq:
            inv_l = 1.0 / acc[HEAD_DIM:HEAD_DIM + 1, :]
            o_ref[lo:lo + tq, :] = (acc[:HEAD_DIM, :] * inv_l).T


def _fox_attention(qk, v_t, f_t, b_f, *, batch, seq):
    nh = FOX_HEADS
    return pl.pallas_call(
        functools.partial(_fox_kernel, seq=seq),
        grid=(batch, nh),
        in_specs=[
            pl.BlockSpec((None, seq, HEAD_DIM), lambda b, h: (b, 0, h)),
            pl.BlockSpec((None, seq, HEAD_DIM), lambda b, h: (b, 0, nh + h)),
            pl.BlockSpec((HEAD_DIM, seq), lambda b, h: (h, b)),
            pl.BlockSpec((F_ROWS, seq), lambda b, h: (0, b)),
            pl.BlockSpec((F_ROWS, 1), lambda b, h: (0, 0)),
        ],
        out_specs=pl.BlockSpec((None, seq, HEAD_DIM), lambda b, h: (b, 0, h)),
        out_shape=jax.ShapeDtypeStruct((batch, seq, FOX_W), F32),
        scratch_shapes=[pltpu.VMEM((seq, 2 * HEAD_DIM), BF16),
                        pltpu.VMEM((HEAD_DIM + V_PAD_ROWS, seq), BF16)],
        compiler_params=_params("parallel", "arbitrary"),
    )(qk, qk, v_t, f_t, b_f)


def _lru_kernel(u_ref, gate_ref, cw_ref, cb_ref, wg_ref, bra_ref, bri_ref, lam_ref, o_ref,
                at_scr, bt_scr, hp_scr, *, seq):
    row = lax.broadcasted_iota(jnp.int32, (SUBLANES, LRU_BLOCK), 0)

    def shifted(d):
        head = jnp.where(row >= d, pltpu.roll(u_ref[0:SUBLANES, :], d, axis=0), 0.0)
        return jnp.concatenate([head, u_ref[SUBLANES - d:seq - d, :]], axis=0)

    cw = cw_ref[...]
    uc = cw[0:1, :] * shifted(3)
    uc = uc + cw[1:2, :] * shifted(2)
    uc = uc + cw[2:3, :] * shifted(1)
    uc = uc + cw[3:4, :] * u_ref[...]
    uc = cb_ref[...] + uc

    gates = _dot(uc.astype(BF16), wg_ref[...])
    half_c = (-0.5 * LRU_C) * _softplus(-lam_ref[...])
    log_a = half_c * jnp.tanh(0.5 * (gates[:, :LRU_BLOCK] + bra_ref[...])) + half_c
    i = 0.5 * jnp.tanh(0.5 * (gates[:, LRU_BLOCK:] + bri_ref[...])) + 0.5
    a = jnp.exp(log_a)
    t = jnp.tanh(log_a)
    w = -2.0 * t
    sqrt_w = jnp.where(w > 0.0, w * lax.rsqrt(w), 0.0)
    b = (sqrt_w * lax.rsqrt(1.0 - t)) * (i * uc)

    groups = seq // SUBLANES
    gshape = (groups, SUBLANES, LRU_BLOCK)
    a3 = a.reshape(gshape)
    b3 = b.reshape(gshape)
    sub = lax.broadcasted_iota(jnp.int32, gshape, 1)
    d = 1
    while d < SUBLANES:
        keep = sub >= d
        b3 = a3 * jnp.where(keep, pltpu.roll(b3, d, axis=1), 0.0) + b3
        a3 = a3 * jnp.where(keep, pltpu.roll(a3, d, axis=1), 1.0)
        d *= 2
    at_scr[...] = jnp.broadcast_to(a3[:, SUBLANES - 1:, :], gshape).reshape(seq, LRU_BLOCK)
    bt_scr[...] = jnp.broadcast_to(b3[:, SUBLANES - 1:, :], gshape).reshape(seq, LRU_BLOCK)

    def carry(g, h):
        rows = pl.ds(pl.multiple_of(g * SUBLANES, SUBLANES), SUBLANES)
        hp_scr[rows, :] = h
        return at_scr[rows, :] * h + bt_scr[rows, :]

    lax.fori_loop(0, groups, carry, jnp.zeros((SUBLANES, LRU_BLOCK), F32), unroll=8)
    hs = b3.reshape(seq, LRU_BLOCK) + a3.reshape(seq, LRU_BLOCK) * hp_scr[...]
    o_ref[...] = hs * jax.nn.gelu(gate_ref[...])


def _rg_lru(ug, cw, cb, wg, bra, bri, lam, *, batch, seq):
    nb = LRU_BLOCKS
    vec = lambda: pl.BlockSpec((1, LRU_BLOCK), lambda b, n: (0, n))
    return pl.pallas_call(
        functools.partial(_lru_kernel, seq=seq),
        grid=(batch, nb),
        in_specs=[
            pl.BlockSpec((None, seq, LRU_BLOCK), lambda b, n: (b, 0, n)),
            pl.BlockSpec((None, seq, LRU_BLOCK), lambda b, n: (b, 0, nb + n)),
            pl.BlockSpec((CONV_W, LRU_BLOCK), lambda b, n: (0, n)),
            vec(),
            pl.BlockSpec((None, LRU_BLOCK, 2 * LRU_BLOCK), lambda b, n: (n, 0, 0)),
            vec(), vec(), vec(),
        ],
        out_specs=pl.BlockSpec((None, seq, LRU_BLOCK), lambda b, n: (b, 0, n)),
        out_shape=jax.ShapeDtypeStruct((batch, seq, LRU_W), F32),
        scratch_shapes=[pltpu.VMEM((seq, LRU_BLOCK), F32)] * 3,
        compiler_params=_params("parallel", "arbitrary"),
    )(ug, ug, cw, cb, wg, bra, bri, lam)


def _out_kernel(of_ref, yl_ref, gf_ref, gl_ref, w_ref, x_ref, o_ref, *, chunk):
    mf = _rms(of_ref[...], gf_ref[...]).astype(BF16)
    ml = _rms(yl_ref[...], gl_ref[...]).astype(BF16)
    for c in range(o_ref.shape[1] // chunk):
        sl = slice(c * chunk, (c + 1) * chunk)
        acc = _dot(mf, w_ref[:FOX_W, sl]) + _dot(ml, w_ref[FOX_W:, sl])
        o_ref[:, sl] = x_ref[:, sl] + acc


def _out_proj(o_fox, y_lru, gf, gl, w, x2d, *, tm):
    T, D = x2d.shape
    return pl.pallas_call(
        functools.partial(_out_kernel, chunk=2 * PROJ_CHUNK),
        grid=(T // tm,),
        in_specs=[
            pl.BlockSpec((tm, FOX_W), lambda i: (i, 0)),
            pl.BlockSpec((tm, LRU_W), lambda i: (i, 0)),
            pl.BlockSpec((1, FOX_W), lambda i: (0, 0)),
            pl.BlockSpec((1, LRU_W), lambda i: (0, 0)),
            pl.BlockSpec((FOX_W + LRU_W, D), lambda i: (0, 0)),
            pl.BlockSpec((tm, D), lambda i: (i, 0)),
        ],
        out_specs=pl.BlockSpec((tm, D), lambda i: (i, 0)),
        out_shape=jax.ShapeDtypeStruct((T, D), F32),
        compiler_params=_params("parallel"),
    )(o_fox, y_lru, gf, gl, w, x2d)


def _xattn_kernel(x_ref, g_ref, wq_ref, gq_ref, ckv_ref, wo_ref, o_ref):
    x = x_ref[...]
    cq = _dot(_rms(x, g_ref[...]).astype(BF16), wq_ref[...])
    scale = 1.0 / math.sqrt(HEAD_DIM)
    heads = []
    for hh in range(XATT_HEADS):
        sl = slice(hh * HEAD_DIM, (hh + 1) * HEAD_DIM)
        qh = _rms(cq[:, sl], gq_ref[...]).astype(BF16)
        kh = ckv_ref[:, sl]
        vh = ckv_ref[:, XATT_W + hh * HEAD_DIM:XATT_W + (hh + 1) * HEAD_DIM]
        s = _dot_nt(qh, kh) * scale
        e = jnp.exp(s - jnp.max(s, axis=-1, keepdims=True))
        l = jnp.sum(e, axis=-1, keepdims=True)
        heads.append((_dot(e.astype(BF16), vh) / l).astype(BF16))
    ox = jnp.concatenate(heads, axis=-1)
    o_ref[...] = x + _dot(ox, wo_ref[...])


def _cross_attention(x3d, g, wq, gq, ckv, wo, *, tm):
    B, S, D = x3d.shape
    M = ckv.shape[1]
    return pl.pallas_call(
        _xattn_kernel,
        grid=(B, S // tm),
        in_specs=[
            pl.BlockSpec((None, tm, D), lambda b, i: (b, i, 0)),
            pl.BlockSpec((1, D), lambda b, i: (0, 0)),
            pl.BlockSpec((D, XATT_W), lambda b, i: (0, 0)),
            pl.BlockSpec((1, HEAD_DIM), lambda b, i: (0, 0)),
            pl.BlockSpec((None, M, 2 * XATT_W), lambda b, i: (b, 0, 0)),
            pl.BlockSpec((XATT_W, D), lambda b, i: (0, 0)),
        ],
        out_specs=pl.BlockSpec((None, tm, D), lambda b, i: (b, i, 0)),
        out_shape=jax.ShapeDtypeStruct((B, S, D), F32),
        compiler_params=_params("parallel", "parallel"),
    )(x3d, g, wq, gq, ckv, wo)


def _ffn_kernel(x_ref, g_ref, wg_ref, wu_ref, wd_ref, o_ref, h_scr):
    @pl.when(pl.program_id(1) == 0)
    def _():
        x = x_ref[...]
        h_scr[...] = _rms(x, g_ref[...]).astype(BF16)
        o_ref[...] = x

    h = h_scr[...]
    gate = _dot(h, wg_ref[...])
    up = _dot(h, wu_ref[...])
    act = (jax.nn.silu(gate) * up).astype(BF16)
    o_ref[...] += _dot(act, wd_ref[...])


def _ffn(x2d, g, w_gu, w_d, *, tm, th):
    T, D = x2d.shape
    H = w_d.shape[0]
    nk = H // th
    return pl.pallas_call(
        _ffn_kernel,
        grid=(T // tm, nk),
        in_specs=[
            pl.BlockSpec((tm, D), lambda i, k: (i, 0)),
            pl.BlockSpec((1, D), lambda i, k: (0, 0)),
            pl.BlockSpec((D, th), lambda i, k: (0, k)),
            pl.BlockSpec((D, th), lambda i, k: (0, nk + k)),
            pl.BlockSpec((th, D), lambda i, k: (k, 0)),
        ],
        out_specs=pl.BlockSpec((tm, D), lambda i, k: (i, 0)),
        out_shape=jax.ShapeDtypeStruct((T, D), F32),
        scratch_shapes=[pltpu.VMEM((tm, D), BF16)],
        compiler_params=_params("parallel", "arbitrary"),
    )(x2d, g, w_gu, w_gu, w_d)


def kernel(x, mem, g_mix, w_in, b_f, g_q, g_k, conv_w, conv_b, w_ra, b_ra, w_ri, b_ri, lam,
           g_fox_out, g_lru_out, w_out, g_xattn, g_mem, w_cq, w_ckv, g_cq, g_ck, w_co, g_ffn,
           w_gate_up, w_down):
    B, S, D = x.shape
    M = mem.shape[1]
    T = B * S
    depth = g_mix.shape[0]
    row = lambda v: v.reshape(1, -1).astype(F32)

    for l in range(depth):
        wl = w_in[l]
        w_qkug = _stage_in_weights(wl, tr=256)
        w_vt = wl[:, 2 * FOX_W:3 * FOX_W].T.astype(BF16)
        w_f = jnp.zeros((F_ROWS, D), BF16).at[:FOX_HEADS].set(
            wl[:, 3 * FOX_W:3 * FOX_W + FOX_HEADS].T.astype(BF16))
        g_qk = jnp.stack([g_q[l], g_k[l]]).reshape(2, 1, HEAD_DIM).astype(F32)
        bf_col = jnp.zeros((F_ROWS, 1), F32).at[:FOX_HEADS, 0].set(b_f[l])
        w_gates = jnp.concatenate([w_ra[l], w_ri[l]], axis=-1).astype(BF16)

        x2d = x.reshape(T, D)

        qk, v_t, ug, f_t = _in_proj(x2d, row(g_mix[l]), w_qkug, w_vt, g_qk, w_f,
                                    tm=1024, tn=1024)
        o_fox = _fox_attention(qk.reshape(B, S, 2 * FOX_W), v_t, f_t, bf_col, batch=B, seq=S)
        y_lru = _rg_lru(ug.reshape(B, S, 2 * LRU_W), conv_w[l], row(conv_b[l]), w_gates,
                        row(b_ra[l]), row(b_ri[l]), row(lam[l]), batch=B, seq=S)
        x2d = _out_proj(o_fox.reshape(T, FOX_W), y_lru.reshape(T, LRU_W),
                        row(g_fox_out[l]), row(g_lru_out[l]), w_out[l].astype(BF16), x2d,
                        tm=512)

        ckv = _ckv_proj(mem.reshape(B * M, D), row(g_mem[l]), w_ckv[l].astype(BF16),
                        row(g_ck[l]))
        x3d = _cross_attention(x2d.reshape(B, S, D), row(g_xattn[l]), w_cq[l].astype(BF16),
                               row(g_cq[l]), ckv.reshape(B, M, 2 * XATT_W),
                               w_co[l].astype(BF16), tm=512)

        x2d = _ffn(x3d.reshape(T, D), row(g_ffn[l]), w_gate_up[l].astype(BF16),
                   w_down[l].astype(BF16), tm=1024, th=512)
        x = x2d.reshape(B, S, D)
    return x
```

```python
import functools
import math

import jax
import jax.numpy as jnp
from jax import lax
from jax.experimental import pallas as pl
from jax.experimental.pallas import tpu as pltpu

F32 = jnp.float32
BF16 = jnp.bfloat16

HEAD_DIM = 128
FOX_HEADS = 8
FOX_W = FOX_HEADS * HEAD_DIM
LRU_BLOCKS = 8
LRU_BLOCK = 128
LRU_W = LRU_BLOCKS * LRU_BLOCK
LRU_C = 8.0
CONV_W = 4
XATT_HEADS = 4
XATT_W = XATT_HEADS * HEAD_DIM
RMS_EPS = 1e-6
F_ROWS = 16
Q_BLOCK = 512
K_CHUNK = 256
SCORE_LOOKAHEAD = 3
V_PAD_ROWS = 16
PROJ_CHUNK = 256
SUBLANES = 8
BIAS_PIECES = 3

V7X_VMEM_LIMIT_BYTES = 60 * 1024 * 1024


def _rms(x, g):
    ms = jnp.mean(x * x, axis=-1, keepdims=True)
    return x * lax.rsqrt(ms + RMS_EPS) * g


def _dot(a, b):
    return jnp.dot(a, b, preferred_element_type=F32)


def _dot_nt(a, b):
    return lax.dot_general(a, b, (((1,), (1,)), ((), ())), preferred_element_type=F32)


def _softplus(x):
    return jnp.maximum(x, 0.0) + jnp.log1p(jnp.exp(-jnp.abs(x)))


def _params(*sem):
    return pltpu.CompilerParams(dimension_semantics=sem,
                                vmem_limit_bytes=V7X_VMEM_LIMIT_BYTES)


def _wstage_kernel(w_ref, wmain_ref, wvt_ref, wft_ref):
    v_lo, f_lo = 2 * FOX_W, 3 * FOX_W
    ug_lo = f_lo + FOX_HEADS
    wmain_ref[:, :v_lo] = w_ref[:, :v_lo].astype(BF16)
    wmain_ref[:, v_lo:] = w_ref[:, ug_lo:ug_lo + 2 * LRU_W].astype(BF16)
    wvt_ref[...] = w_ref[:, v_lo:f_lo].T.astype(BF16)
    ft = w_ref[:, f_lo:f_lo + HEAD_DIM].T
    keep = lax.broadcasted_iota(jnp.int32, ft.shape, 0) < FOX_HEADS
    wft_ref[...] = jnp.where(keep, ft, 0.0).astype(BF16)


def _stage_in_weights(w_in, layer, *, tr):
    _, D, N = w_in.shape
    n_main = 2 * FOX_W + 2 * LRU_W
    return pl.pallas_call(
        _wstage_kernel,
        grid=(D // tr,),
        in_specs=[pl.BlockSpec((None, tr, N), lambda i: (layer, i, 0))],
        out_specs=[pl.BlockSpec((tr, n_main), lambda i: (i, 0)),
                   pl.BlockSpec((FOX_W, tr), lambda i: (0, i)),
                   pl.BlockSpec((HEAD_DIM, tr), lambda i: (0, i))],
        out_shape=[jax.ShapeDtypeStruct((D, n_main), BF16),
                   jax.ShapeDtypeStruct((FOX_W, D), BF16),
                   jax.ShapeDtypeStruct((HEAD_DIM, D), BF16)],
        compiler_params=_params("parallel"),
    )(w_in)


def _inproj_kernel(x_ref, g_ref, w_ref, wvt_ref, gh_ref, wf_ref,
                   qk_ref, vt_ref, ug_ref, f_ref, h_scr, *, tn, n_qk):
    j = pl.program_id(1)

    @pl.when(j == 0)
    def _():
        h = _rms(x_ref[...], g_ref[...]).astype(BF16)
        h_scr[...] = h
        f_ref[...] = _dot_nt(wf_ref[...], h)

    @pl.when(j < n_qk)
    def _():
        for c in range(tn // PROJ_CHUNK):
            acc = _dot(h_scr[...], w_ref[:, c * PROJ_CHUNK:(c + 1) * PROJ_CHUNK])
            for hh in range(PROJ_CHUNK // HEAD_DIM):
                lo = c * PROJ_CHUNK + hh * HEAD_DIM
                qk_ref[:, lo:lo + HEAD_DIM] = _rms(
                    acc[:, hh * HEAD_DIM:(hh + 1) * HEAD_DIM], gh_ref[...]).astype(BF16)

    @pl.when(j == n_qk)
    def _():
        for c in range(vt_ref.shape[0] // PROJ_CHUNK):
            rows = slice(c * PROJ_CHUNK, (c + 1) * PROJ_CHUNK)
            vt_ref[rows, :] = _dot_nt(wvt_ref[rows, :], h_scr[...]).astype(BF16)

    @pl.when(j > n_qk)
    def _():
        for c in range(tn // PROJ_CHUNK):
            sl = slice(c * PROJ_CHUNK, (c + 1) * PROJ_CHUNK)
            ug_ref[:, sl] = _dot(h_scr[...], w_ref[:, sl])


def _in_proj(x2d, g, w, w_vt, g_qk, wf, *, tm, tn):
    T, D = x2d.shape
    n_qk = 2 * FOX_W // tn
    n_ug = 2 * LRU_W // tn
    tiles_per_gain = FOX_W // tn
    clip = lambda v, n: jnp.clip(v, 0, n - 1)
    w_tile = lambda j: jnp.where(j <= n_qk, clip(j, n_qk), j - 1)
    return pl.pallas_call(
        functools.partial(_inproj_kernel, tn=tn, n_qk=n_qk),
        grid=(T // tm, n_qk + 1 + n_ug),
        in_specs=[
            pl.BlockSpec((tm, D), lambda i, j: (i, 0)),
            pl.BlockSpec((1, D), lambda i, j: (0, 0)),
            pl.BlockSpec((D, tn), lambda i, j: (0, w_tile(j))),
            pl.BlockSpec((FOX_W, D), lambda i, j: (0, 0), pipeline_mode=pl.Buffered(1)),
            pl.BlockSpec((None, 1, HEAD_DIM), lambda i, j: (clip(j // tiles_per_gain, 2), 0, 0)),
            pl.BlockSpec((F_ROWS, D), lambda i, j: (0, 0)),
        ],
        out_specs=[
            pl.BlockSpec((tm, tn), lambda i, j: (i, clip(j, n_qk))),
            pl.BlockSpec((FOX_W, tm), lambda i, j: (0, i)),
            pl.BlockSpec((tm, tn), lambda i, j: (i, clip(j - n_qk - 1, n_ug))),
            pl.BlockSpec((F_ROWS, tm), lambda i, j: (0, i)),
        ],
        out_shape=[
            jax.ShapeDtypeStruct((T, 2 * FOX_W), BF16),
            jax.ShapeDtypeStruct((FOX_W, T), BF16),
            jax.ShapeDtypeStruct((T, 2 * LRU_W), F32),
            jax.ShapeDtypeStruct((F_ROWS, T), F32),
        ],
        scratch_shapes=[pltpu.VMEM((tm, D), BF16)],
        compiler_params=_params("parallel", "arbitrary"),
    )(x2d, g, w, w_vt, g_qk, wf)


def _ckv_kernel(m_ref, g_ref, w_ref, gh_ref, o_ref):
    h = _rms(m_ref[...], g_ref[...]).astype(BF16)
    acc = _dot(h, w_ref[...])
    for hh in range(XATT_HEADS):
        sl = slice(hh * HEAD_DIM, (hh + 1) * HEAD_DIM)
        o_ref[:, sl] = _rms(acc[:, sl], gh_ref[...]).astype(BF16)
    o_ref[:, XATT_W:] = acc[:, XATT_W:].astype(BF16)


def _ckv_proj(mem2d, g, w, gh):
    R, D = mem2d.shape
    full = lambda shape: pl.BlockSpec(shape, lambda i: (0,) * len(shape))
    return pl.pallas_call(
        _ckv_kernel,
        grid=(1,),
        in_specs=[full((R, D)), full((1, D)), full((D, 2 * XATT_W)), full((1, HEAD_DIM))],
        out_specs=full((R, 2 * XATT_W)),
        out_shape=jax.ShapeDtypeStruct((R, 2 * XATT_W), BF16),
        compiler_params=_params("arbitrary"),
    )(mem2d, g, w, gh)


def _fox_kernel(q_ref, k_ref, vt_ref, f_ref, bf_ref, o_ref, kaug_scr, vaug_scr, *, seq):
    h = pl.program_id(1)

    @pl.when(h == 0)
    def _():
        z = f_ref[...] + bf_ref[...]
        c = jnp.minimum(z, 0.0) - jnp.log1p(jnp.exp(-jnp.abs(z)))
        lane = lax.broadcasted_iota(jnp.int32, c.shape, 1)
        d = 1
        while d < seq:
            c = c + jnp.where(lane >= d, pltpu.roll(c, d, axis=1), 0.0)
            d *= 2
        bias = c[:FOX_HEADS] * (-math.sqrt(HEAD_DIM))
        pieces = []
        for _ in range(BIAS_PIECES):
            p = bias.astype(BF16).astype(F32)
            pieces.append(p)
            bias = bias - p
        pad = jnp.zeros((HEAD_DIM - BIAS_PIECES * FOX_HEADS, seq), F32)
        kaug_scr[:, HEAD_DIM:] = jnp.concatenate(pieces + [pad], axis=0).T.astype(BF16)

    kaug_scr[:, :HEAD_DIM] = k_ref[...]
    vaug_scr[:HEAD_DIM, :] = vt_ref[...]
    ones_row = lax.broadcasted_iota(jnp.int32, (V_PAD_ROWS, seq), 0) == 0
    vaug_scr[HEAD_DIM:, :] = jnp.where(ones_row, 1.0, 0.0).astype(BF16)

    tq, tk = Q_BLOCK, K_CHUNK
    lane = lax.broadcasted_iota(jnp.int32, (tq, HEAD_DIM), 1)
    mine = lane == h
    for p in range(1, BIAS_PIECES):
        mine = mine | (lane == h + p * FOX_HEADS)
    selector = jnp.where(mine, 1.0, 0.0).astype(BF16)
    key = lax.broadcasted_iota(jnp.int32, (tk, tq), 0)
    qry = lax.broadcasted_iota(jnp.int32, (tk, tq), 1)
    exp2_scale = math.log2(math.e) / math.sqrt(HEAD_DIM)

    def scores(lo, k0):
        q_aug = jnp.concatenate([q_ref[lo:lo + tq, :], selector], axis=1)
        return _dot_nt(kaug_scr[k0:k0 + tk, :], q_aug)

    steps = [(qi * tq, kc * tk) for qi in range(seq // tq) for kc in range((qi + 1) * tq // tk)]
    pending = [scores(*st) for st in steps[:SCORE_LOOKAHEAD]]
    for idx, (lo, k0) in enumerate(steps):
        t = pending.pop(0)
        if idx + SCORE_LOOKAHEAD < len(steps):
            pending.append(scores(*steps[idx + SCORE_LOOKAHEAD]))
        if k0 + tk > lo:
            t = jnp.where(key + (k0 - lo) <= qry, t, -jnp.inf)
        cm = jnp.max(t, axis=0, keepdims=True)
        m_new = cm if k0 == 0 else jnp.maximum(m, cm)
        p = jnp.exp2((t - m_new) * exp2_scale).astype(BF16)
        pv = _dot(vaug_scr[:, k0:k0 + tk], p)
        if k0 == 0:
            acc = pv
        else:
            acc = acc * jnp.exp2((m - m_new) * exp2_scale) + pv
        m = m_new
        if k0 + tk == lo + tq:
            inv_l = 1.0 / acc[HEAD_DIM:HEAD_DIM + 1, :]
            o_ref[lo:lo + tq, :] = (acc[:HEAD_DIM, :] * inv_l).T


def _fox_attention(qk, v_t, f_t, b_f, *, batch, seq):
    nh = FOX_HEADS
    return pl.pallas_call(
        functools.partial(_fox_kernel, seq=seq),
        grid=(batch, nh),
        in_specs=[
            pl.BlockSpec((None, seq, HEAD_DIM), lambda b, h: (b, 0, h)),
            pl.BlockSpec((None, seq, HEAD_DIM), lambda b, h: (b, 0, nh + h)),
            pl.BlockSpec((HEAD_DIM, seq), lambda b, h: (h, b)),
            pl.BlockSpec((F_ROWS, seq), lambda b, h: (0, b)),
            pl.BlockSpec((F_ROWS, 1), lambda b, h: (0, 0)),
        ],
        out_specs=pl.BlockSpec((None, seq, HEAD_DIM), lambda b, h: (b, 0, h)),
        out_shape=jax.ShapeDtypeStruct((batch, seq, FOX_W), F32),
        scratch_shapes=[pltpu.VMEM((seq, 2 * HEAD_DIM), BF16),
                        pltpu.VMEM((HEAD_DIM + V_PAD_ROWS, seq), BF16)],
        compiler_params=_params("parallel", "arbitrary"),
    )(qk, qk, v_t, f_t, b_f)


def _lru_kernel(u_ref, gate_ref, cw_ref, cb_ref, wg_ref, bra_ref, bri_ref, lam_ref, o_ref,
                at_scr, bt_scr, hp_scr, *, seq):
    row = lax.broadcasted_iota(jnp.int32, (SUBLANES, LRU_BLOCK), 0)

    def shifted(d):
        head = jnp.where(row >= d, pltpu.roll(u_ref[0:SUBLANES, :], d, axis=0), 0.0)
        return jnp.concatenate([head, u_ref[SUBLANES - d:seq - d, :]], axis=0)

    cw = cw_ref[...]
    uc = cw[0:1, :] * shifted(3)
    uc = uc + cw[1:2, :] * shifted(2)
    uc = uc + cw[2:3, :] * shifted(1)
    uc = uc + cw[3:4, :] * u_ref[...]
    uc = cb_ref[...] + uc

    gates = _dot(uc.astype(BF16), wg_ref[...])
    half_c = (-0.5 * LRU_C) * _softplus(-lam_ref[...])
    log_a = half_c * jnp.tanh(0.5 * (gates[:, :LRU_BLOCK] + bra_ref[...])) + half_c
    i = 0.5 * jnp.tanh(0.5 * (gates[:, LRU_BLOCK:] + bri_ref[...])) + 0.5
    a = jnp.exp(log_a)
    t = jnp.tanh(log_a)
    w = -2.0 * t
    sqrt_w = jnp.where(w > 0.0, w * lax.rsqrt(w), 0.0)
    b = (sqrt_w * lax.rsqrt(1.0 - t)) * (i * uc)

    groups = seq // SUBLANES
    gshape = (groups, SUBLANES, LRU_BLOCK)
    a3 = a.reshape(gshape)
    b3 = b.reshape(gshape)
    sub = lax.broadcasted_iota(jnp.int32, gshape, 1)
    d = 1
    while d < SUBLANES:
        keep = sub >= d
        b3 = a3 * jnp.where(keep, pltpu.roll(b3, d, axis=1), 0.0) + b3
        a3 = a3 * jnp.where(keep, pltpu.roll(a3, d, axis=1), 1.0)
        d *= 2
    at_scr[...] = jnp.broadcast_to(a3[:, SUBLANES - 1:, :], gshape).reshape(seq, LRU_BLOCK)
    bt_scr[...] = jnp.broadcast_to(b3[:, SUBLANES - 1:, :], gshape).reshape(seq, LRU_BLOCK)

    def carry(g, h):
        rows = pl.ds(pl.multiple_of(g * SUBLANES, SUBLANES), SUBLANES)
        hp_scr[rows, :] = h
        return at_scr[rows, :] * h + bt_scr[rows, :]

    lax.fori_loop(0, groups, carry, jnp.zeros((SUBLANES, LRU_BLOCK), F32), unroll=8)
    hs = b3.reshape(seq, LRU_BLOCK) + a3.reshape(seq, LRU_BLOCK) * hp_scr[...]
    o_ref[...] = hs * jax.nn.gelu(gate_ref[...])


def _rg_lru(ug, cw, cb, wg, bra, bri, lam, *, batch, seq):
    nb = LRU_BLOCKS
    vec = lambda: pl.BlockSpec((1, LRU_BLOCK), lambda b, n: (0, n))
    return pl.pallas_call(
        functools.partial(_lru_kernel, seq=seq),
        grid=(batch, nb),
        in_specs=[
            pl.BlockSpec((None, seq, LRU_BLOCK), lambda b, n: (b, 0, n)),
            pl.BlockSpec((None, seq, LRU_BLOCK), lambda b, n: (b, 0, nb + n)),
            pl.BlockSpec((CONV_W, LRU_BLOCK), lambda b, n: (0, n)),
            vec(),
            pl.BlockSpec((None, LRU_BLOCK, 2 * LRU_BLOCK), lambda b, n: (n, 0, 0)),
            vec(), vec(), vec(),
        ],
        out_specs=pl.BlockSpec((None, seq, LRU_BLOCK), lambda b, n: (b, 0, n)),
        out_shape=jax.ShapeDtypeStruct((batch, seq, LRU_W), F32),
        scratch_shapes=[pltpu.VMEM((seq, LRU_BLOCK), F32)] * 3,
        compiler_params=_params("parallel", "arbitrary"),
    )(ug, ug, cw, cb, wg, bra, bri, lam)


def _out_kernel(of_ref, yl_ref, gf_ref, gl_ref, w_ref, x_ref, o_ref, *, chunk):
    mf = _rms(of_ref[...], gf_ref[...]).astype(BF16)
    ml = _rms(yl_ref[...], gl_ref[...]).astype(BF16)
    for c in range(o_ref.shape[1] // chunk):
        sl = slice(c * chunk, (c + 1) * chunk)
        acc = _dot(mf, w_ref[:FOX_W, sl]) + _dot(ml, w_ref[FOX_W:, sl])
        o_ref[:, sl] = x_ref[:, sl] + acc


def _out_proj(o_fox, y_lru, gf, gl, w, x2d, *, tm):
    T, D = x2d.shape
    return pl.pallas_call(
        functools.partial(_out_kernel, chunk=2 * PROJ_CHUNK),
        grid=(T // tm,),
        in_specs=[
            pl.BlockSpec((tm, FOX_W), lambda i: (i, 0)),
            pl.BlockSpec((tm, LRU_W), lambda i: (i, 0)),
            pl.BlockSpec((1, FOX_W), lambda i: (0, 0)),
            pl.BlockSpec((1, LRU_W), lambda i: (0, 0)),
            pl.BlockSpec((FOX_W + LRU_W, D), lambda i: (0, 0)),
            pl.BlockSpec((tm, D), lambda i: (i, 0)),
        ],
        out_specs=pl.BlockSpec((tm, D), lambda i: (i, 0)),
        out_shape=jax.ShapeDtypeStruct((T, D), F32),
        compiler_params=_params("parallel"),
    )(o_fox, y_lru, gf, gl, w, x2d)


def _xattn_kernel(x_ref, g_ref, wq_ref, gq_ref, ckv_ref, wo_ref, o_ref):
    x = x_ref[...]
    cq = _dot(_rms(x, g_ref[...]).astype(BF16), wq_ref[...])
    scale = 1.0 / math.sqrt(HEAD_DIM)
    scores = []
    for hh in range(XATT_HEADS):
        sl = slice(hh * HEAD_DIM, (hh + 1) * HEAD_DIM)
        qh = _rms(cq[:, sl], gq_ref[...]).astype(BF16)
        scores.append(_dot_nt(qh, ckv_ref[:, sl]))
    heads = []
    for hh in range(XATT_HEADS):
        vh = ckv_ref[:, XATT_W + hh * HEAD_DIM:XATT_W + (hh + 1) * HEAD_DIM]
        s = scores[hh] * scale
        e = jnp.exp(s - jnp.max(s, axis=-1, keepdims=True))
        l = jnp.sum(e, axis=-1, keepdims=True)
        heads.append((_dot(e.astype(BF16), vh) / l).astype(BF16))
    ox = jnp.concatenate(heads, axis=-1)
    o_ref[...] = x + _dot(ox, wo_ref[...])


def _cross_attention(x3d, g, wq, gq, ckv, wo, *, tm):
    B, S, D = x3d.shape
    M = ckv.shape[1]
    return pl.pallas_call(
        _xattn_kernel,
        grid=(B, S // tm),
        in_specs=[
            pl.BlockSpec((None, tm, D), lambda b, i: (b, i, 0)),
            pl.BlockSpec((1, D), lambda b, i: (0, 0)),
            pl.BlockSpec((D, XATT_W), lambda b, i: (0, 0)),
            pl.BlockSpec((1, HEAD_DIM), lambda b, i: (0, 0)),
            pl.BlockSpec((None, M, 2 * XATT_W), lambda b, i: (b, 0, 0)),
            pl.BlockSpec((XATT_W, D), lambda b, i: (0, 0)),
        ],
        out_specs=pl.BlockSpec((None, tm, D), lambda b, i: (b, i, 0)),
        out_shape=jax.ShapeDtypeStruct((B, S, D), F32),
        compiler_params=_params("parallel", "parallel"),
    )(x3d, g, wq, gq, ckv, wo)


def _ffn_kernel(x_ref, g_ref, wg_ref, wu_ref, wd_ref, o_ref, h_scr):
    @pl.when(pl.program_id(1) == 0)
    def _():
        x = x_ref[...]
        h_scr[...] = _rms(x, g_ref[...]).astype(BF16)
        o_ref[...] = x

    h = h_scr[...]
    gate = _dot(h, wg_ref[...])
    up = _dot(h, wu_ref[...])
    act = (jax.nn.silu(gate) * up).astype(BF16)
    o_ref[...] += _dot(act, wd_ref[...])


def _ffn(x2d, g, w_gu, w_d, *, tm, th):
    T, D = x2d.shape
    H = w_d.shape[0]
    nk = H // th
    return pl.pallas_call(
        _ffn_kernel,
        grid=(T // tm, nk),
        in_specs=[
            pl.BlockSpec((tm, D), lambda i, k: (i, 0)),
            pl.BlockSpec((1, D), lambda i, k: (0, 0)),
            pl.BlockSpec((D, th), lambda i, k: (0, k)),
            pl.BlockSpec((D, th), lambda i, k: (0, nk + k)),
            pl.BlockSpec((th, D), lambda i, k: (k, 0)),
        ],
        out_specs=pl.BlockSpec((tm, D), lambda i, k: (i, 0)),
        out_shape=jax.ShapeDtypeStruct((T, D), F32),
        scratch_shapes=[pltpu.VMEM((tm, D), BF16)],
        compiler_params=_params("parallel", "arbitrary"),
    )(x2d, g, w_gu, w_gu, w_d)


def kernel(x, mem, g_mix, w_in, b_f, g_q, g_k, conv_w, conv_b, w_ra, b_ra, w_ri, b_ri, lam,
           g_fox_out, g_lru_out, w_out, g_xattn, g_mem, w_cq, w_ckv, g_cq, g_ck, w_co, g_ffn,
           w_gate_up, w_down):
    B, S, D = x.shape
    M = mem.shape[1]
    T = B * S
    depth = g_mix.shape[0]
    row = lambda v: v.reshape(1, -1).astype(F32)

    for l in range(depth):
        w_qkug, w_vt, w_f = _stage_in_weights(w_in, l, tr=256)
        g_qk = jnp.stack([g_q[l], g_k[l]]).reshape(2, 1, HEAD_DIM).astype(F32)
        bf_col = jnp.zeros((F_ROWS, 1), F32).at[:FOX_HEADS, 0].set(b_f[l])
        w_gates = jnp.concatenate([w_ra[l], w_ri[l]], axis=-1).astype(BF16)

        x2d = x.reshape(T, D)

        qk, v_t, ug, f_t = _in_proj(x2d, row(g_mix[l]), w_qkug, w_vt, g_qk, w_f,
                                    tm=1024, tn=1024)
        o_fox = _fox_attention(qk.reshape(B, S, 2 * FOX_W), v_t, f_t, bf_col, batch=B, seq=S)
        y_lru = _rg_lru(ug.reshape(B, S, 2 * LRU_W), conv_w[l], row(conv_b[l]), w_gates,
                        row(b_ra[l]), row(b_ri[l]), row(lam[l]), batch=B, seq=S)
        x2d = _out_proj(o_fox.reshape(T, FOX_W), y_lru.reshape(T, LRU_W),
                        row(g_fox_out[l]), row(g_lru_out[l]), w_out[l].astype(BF16), x2d,
                        tm=512)

        ckv = _ckv_proj(mem.reshape(B * M, D), row(g_mem[l]), w_ckv[l].astype(BF16),
                        row(g_ck[l]))
        x3d = _cross_attention(x2d.reshape(B, S, D), row(g_xattn[l]), w_cq[l].astype(BF16),
                               row(g_cq[l]), ckv.reshape(B, M, 2 * XATT_W),
                               w_co[l].astype(BF16), tm=512)

        x2d = _ffn(x3d.reshape(T, D), row(g_ffn[l]), w_gate_up[l].astype(BF16),
                   w_down[l].astype(BF16), tm=1024, th=512)
        x = x2d.reshape(B, S, D)
    return x
```

```python
import functools
import math

import jax
import jax.numpy as jnp
from jax import lax
from jax.experimental import pallas as pl
from jax.experimental.pallas import tpu as pltpu

F32 = jnp.float32
BF16 = jnp.bfloat16

HEAD_DIM = 128
FOX_HEADS = 8
FOX_W = FOX_HEADS * HEAD_DIM
LRU_BLOCKS = 8
LRU_BLOCK = 128
LRU_W = LRU_BLOCKS * LRU_BLOCK
LRU_C = 8.0
CONV_W = 4
XATT_HEADS = 4
XATT_W = XATT_HEADS * HEAD_DIM
RMS_EPS = 1e-6
F_ROWS = 16
Q_BLOCK = 512
K_CHUNK = 256
SCORE_LOOKAHEAD = 3
V_PAD_ROWS = 16
PROJ_CHUNK = 256
SUBLANES = 8
BIAS_PIECES = 3

V7X_VMEM_LIMIT_BYTES = 60 * 1024 * 1024


def _rms(x, g):
    ms = jnp.mean(x * x, axis=-1, keepdims=True)
    return x * lax.rsqrt(ms + RMS_EPS) * g


def _dot(a, b):
    return jnp.dot(a, b, preferred_element_type=F32)


def _dot_nt(a, b):
    return lax.dot_general(a, b, (((1,), (1,)), ((), ())), preferred_element_type=F32)


def _softplus(x):
    return jnp.maximum(x, 0.0) + jnp.log1p(jnp.exp(-jnp.abs(x)))


def _params(*sem):
    return pltpu.CompilerParams(dimension_semantics=sem,
                                vmem_limit_bytes=V7X_VMEM_LIMIT_BYTES)


def _wstage_kernel(a_ref, b_ref, w_ref, wf_ref, *, f_tile):
    t = pl.program_id(0)

    @pl.when(t < f_tile)
    def _():
        w_ref[...] = a_ref[...].astype(BF16)

    @pl.when(t >= f_tile)
    def _():
        w_ref[...] = jnp.concatenate([a_ref[FOX_HEADS:, :], b_ref[:FOX_HEADS, :]],
                                     axis=0).astype(BF16)

    @pl.when(t == f_tile)
    def _():
        wf_ref[:FOX_HEADS, :] = a_ref[:FOX_HEADS, :].astype(BF16)
        wf_ref[FOX_HEADS:, :] = jnp.zeros((F_ROWS - FOX_HEADS, wf_ref.shape[1]), BF16)


def _stage_in_weights(w_t, layer, *, tr):
    _, N, D = w_t.shape
    n_rows = N - FOX_HEADS
    f_tile = 3 * FOX_W // tr
    return pl.pallas_call(
        functools.partial(_wstage_kernel, f_tile=f_tile),
        grid=(n_rows // tr,),
        in_specs=[pl.BlockSpec((None, tr, D), lambda t: (layer, t, 0)),
                  pl.BlockSpec((None, tr, D), lambda t: (layer, jnp.maximum(t, f_tile) + 1, 0))],
        out_specs=[pl.BlockSpec((tr, D), lambda t: (t, 0)),
                   pl.BlockSpec((F_ROWS, D), lambda t: (0, 0))],
        out_shape=[jax.ShapeDtypeStruct((n_rows, D), BF16),
                   jax.ShapeDtypeStruct((F_ROWS, D), BF16)],
        compiler_params=_params("arbitrary"),
    )(w_t, w_t)


def _inproj_kernel(x_ref, g_ref, w_ref, gh_ref, wf_ref,
                   qk_ref, vt_ref, ug_ref, f_ref, h_scr, *, tn, n_qk, n_v):
    j = pl.program_id(1)

    @pl.when(j == 0)
    def _():
        h = _rms(x_ref[...], g_ref[...]).astype(BF16)
        h_scr[...] = h
        f_ref[...] = _dot_nt(wf_ref[...], h)

    chunks = [slice(c * PROJ_CHUNK, (c + 1) * PROJ_CHUNK) for c in range(tn // PROJ_CHUNK)]

    @pl.when(j < n_qk)
    def _():
        for sl in chunks:
            acc = _dot_nt(h_scr[...], w_ref[sl, :])
            for hh in range(PROJ_CHUNK // HEAD_DIM):
                lo = sl.start + hh * HEAD_DIM
                qk_ref[:, lo:lo + HEAD_DIM] = _rms(
                    acc[:, hh * HEAD_DIM:(hh + 1) * HEAD_DIM], gh_ref[...]).astype(BF16)

    @pl.when((j >= n_qk) & (j < n_qk + n_v))
    def _():
        for sl in chunks:
            vt_ref[sl, :] = _dot_nt(w_ref[sl, :], h_scr[...]).astype(BF16)

    @pl.when(j >= n_qk + n_v)
    def _():
        for sl in chunks:
            ug_ref[:, sl] = _dot_nt(h_scr[...], w_ref[sl, :])


def _in_proj(x2d, g, w_t, g_qk, wf, *, tm, tn):
    T, D = x2d.shape
    n_qk = 2 * FOX_W // tn
    n_v = FOX_W // tn
    n_ug = 2 * LRU_W // tn
    tiles_per_gain = FOX_W // tn
    clip = lambda v, n: jnp.clip(v, 0, n - 1)
    return pl.pallas_call(
        functools.partial(_inproj_kernel, tn=tn, n_qk=n_qk, n_v=n_v),
        grid=(T // tm, n_qk + n_v + n_ug),
        in_specs=[
            pl.BlockSpec((tm, D), lambda i, j: (i, 0)),
            pl.BlockSpec((1, D), lambda i, j: (0, 0)),
            pl.BlockSpec((tn, D), lambda i, j: (j, 0)),
            pl.BlockSpec((None, 1, HEAD_DIM), lambda i, j: (clip(j // tiles_per_gain, 2), 0, 0)),
            pl.BlockSpec((F_ROWS, D), lambda i, j: (0, 0)),
        ],
        out_specs=[
            pl.BlockSpec((tm, tn), lambda i, j: (i, clip(j, n_qk))),
            pl.BlockSpec((tn, tm), lambda i, j: (clip(j - n_qk, n_v), i)),
            pl.BlockSpec((tm, tn), lambda i, j: (i, clip(j - n_qk - n_v, n_ug))),
            pl.BlockSpec((F_ROWS, tm), lambda i, j: (0, i)),
        ],
        out_shape=[
            jax.ShapeDtypeStruct((T, 2 * FOX_W), BF16),
            jax.ShapeDtypeStruct((FOX_W, T), BF16),
            jax.ShapeDtypeStruct((T, 2 * LRU_W), F32),
            jax.ShapeDtypeStruct((F_ROWS, T), F32),
        ],
        scratch_shapes=[pltpu.VMEM((tm, D), BF16)],
        compiler_params=_params("parallel", "arbitrary"),
    )(x2d, g, w_t, g_qk, wf)


def _ckv_kernel(m_ref, g_ref, w_ref, gh_ref, o_ref):
    h = _rms(m_ref[...], g_ref[...]).astype(BF16)
    acc = _dot(h, w_ref[...])
    for hh in range(XATT_HEADS):
        sl = slice(hh * HEAD_DIM, (hh + 1) * HEAD_DIM)
        o_ref[:, sl] = _rms(acc[:, sl], gh_ref[...]).astype(BF16)
    o_ref[:, XATT_W:] = acc[:, XATT_W:].astype(BF16)


def _ckv_proj(mem2d, g, w, gh):
    R, D = mem2d.shape
    full = lambda shape: pl.BlockSpec(shape, lambda i: (0,) * len(shape))
    return pl.pallas_call(
        _ckv_kernel,
        grid=(1,),
        in_specs=[full((R, D)), full((1, D)), full((D, 2 * XATT_W)), full((1, HEAD_DIM))],
        out_specs=full((R, 2 * XATT_W)),
        out_shape=jax.ShapeDtypeStruct((R, 2 * XATT_W), BF16),
        compiler_params=_params("arbitrary"),
    )(mem2d, g, w, gh)


def _fox_kernel(q_ref, k_ref, vt_ref, f_ref, bf_ref, o_ref, kaug_scr, vaug_scr, *, seq):
    h = pl.program_id(1)

    @pl.when(h == 0)
    def _():
        z = f_ref[...] + bf_ref[...]
        c = jnp.minimum(z, 0.0) - jnp.log1p(jnp.exp(-jnp.abs(z)))
        lane = lax.broadcasted_iota(jnp.int32, c.shape, 1)
        d = 1
        while d < seq:
            c = c + jnp.where(lane >= d, pltpu.roll(c, d, axis=1), 0.0)
            d *= 2
        bias = c[:FOX_HEADS] * (-math.sqrt(HEAD_DIM))
        pieces = []
        for _ in range(BIAS_PIECES):
            p = bias.astype(BF16).astype(F32)
            pieces.append(p)
            bias = bias - p
        pad = jnp.zeros((HEAD_DIM - BIAS_PIECES * FOX_HEADS, seq), F32)
        kaug_scr[:, HEAD_DIM:] = jnp.concatenate(pieces + [pad], axis=0).T.astype(BF16)

    kaug_scr[:, :HEAD_DIM] = k_ref[...]
    vaug_scr[:HEAD_DIM, :] = vt_ref[...]
    ones_row = lax.broadcasted_iota(jnp.int32, (V_PAD_ROWS, seq), 0) == 0
    vaug_scr[HEAD_DIM:, :] = jnp.where(ones_row, 1.0, 0.0).astype(BF16)

    tq, tk = Q_BLOCK, K_CHUNK
    lane = lax.broadcasted_iota(jnp.int32, (tq, HEAD_DIM), 1)
    mine = lane == h
    for p in range(1, BIAS_PIECES):
        mine = mine | (lane == h + p * FOX_HEADS)
    selector = jnp.where(mine, 1.0, 0.0).astype(BF16)
    key = lax.broadcasted_iota(jnp.int32, (tk, tq), 0)
    qry = lax.broadcasted_iota(jnp.int32, (tk, tq), 1)
    exp2_scale = math.log2(math.e) / math.sqrt(HEAD_DIM)

    def scores(lo, k0):
        q_aug = jnp.concatenate([q_ref[lo:lo + tq, :], selector], axis=1)
        return _dot_nt(kaug_scr[k0:k0 + tk, :], q_aug)

    steps = [(qi * tq, kc * tk) for qi in range(seq // tq) for kc in range((qi + 1) * tq // tk)]
    pending = [scores(*st) for st in steps[:SCORE_LOOKAHEAD]]
    for idx, (lo, k0) in enumerate(steps):
        t = pending.pop(0)
        if idx + SCORE_LOOKAHEAD < len(steps):
            pending.append(scores(*steps[idx + SCORE_LOOKAHEAD]))
        if k0 + tk > lo:
            t = jnp.where(key + (k0 - lo) <= qry, t, -jnp.inf)
        cm = jnp.max(t, axis=0, keepdims=True)
        m_new = cm if k0 == 0 else jnp.maximum(m, cm)
        p = jnp.exp2((t - m_new) * exp2_scale).astype(BF16)
        pv = _dot(vaug_scr[:, k0:k0 + tk], p)
        if k0 == 0:
            acc = pv
        else:
            acc = acc * jnp.exp2((m - m_new) * exp2_scale) + pv
        m = m_new
        if k0 + tk == lo + tq:
            inv_l = 1.0 / acc[HEAD_DIM:HEAD_DIM + 1, :]
            o_ref[lo:lo + tq, :] = (acc[:HEAD_DIM, :] * inv_l).T


def _fox_attention(qk, v_t, f_t, b_f, *, batch, seq):
    nh = FOX_HEADS
    return pl.pallas_call(
        functools.partial(_fox_kernel, seq=seq),
        grid=(batch, nh),
        in_specs=[
            pl.BlockSpec((None, seq, HEAD_DIM), lambda b, h: (b, 0, h)),
            pl.BlockSpec((None, seq, HEAD_DIM), lambda b, h: (b, 0, nh + h)),
            pl.BlockSpec((HEAD_DIM, seq), lambda b, h: (h, b)),
            pl.BlockSpec((F_ROWS, seq), lambda b, h: (0, b)),
            pl.BlockSpec((F_ROWS, 1), lambda b, h: (0, 0)),
        ],
        out_specs=pl.BlockSpec((None, seq, HEAD_DIM), lambda b, h: (b, 0, h)),
        out_shape=jax.ShapeDtypeStruct((batch, seq, FOX_W), F32),
        scratch_shapes=[pltpu.VMEM((seq, 2 * HEAD_DIM), BF16),
                        pltpu.VMEM((HEAD_DIM + V_PAD_ROWS, seq), BF16)],
        compiler_params=_params("parallel", "arbitrary"),
    )(qk, qk, v_t, f_t, b_f)


def _lru_kernel(u_ref, gate_ref, cw_ref, cb_ref, wg_ref, bra_ref, bri_ref, lam_ref, o_ref,
                at_scr, bt_scr, hp_scr, *, seq):
    row = lax.broadcasted_iota(jnp.int32, (SUBLANES, LRU_BLOCK), 0)

    def shifted(d):
        head = jnp.where(row >= d, pltpu.roll(u_ref[0:SUBLANES, :], d, axis=0), 0.0)
        return jnp.concatenate([head, u_ref[SUBLANES - d:seq - d, :]], axis=0)

    cw = cw_ref[...]
    uc = cw[0:1, :] * shifted(3)
    uc = uc + cw[1:2, :] * shifted(2)
    uc = uc + cw[2:3, :] * shifted(1)
    uc = uc + cw[3:4, :] * u_ref[...]
    uc = cb_ref[...] + uc

    gates = _dot(uc.astype(BF16), wg_ref[...])
    half_c = (-0.5 * LRU_C) * _softplus(-lam_ref[...])
    log_a = half_c * jnp.tanh(0.5 * (gates[:, :LRU_BLOCK] + bra_ref[...])) + half_c
    i = 0.5 * jnp.tanh(0.5 * (gates[:, LRU_BLOCK:] + bri_ref[...])) + 0.5
    a = jnp.exp(log_a)
    t = jnp.tanh(log_a)
    w = -2.0 * t
    sqrt_w = jnp.where(w > 0.0, w * lax.rsqrt(w), 0.0)
    b = (sqrt_w * lax.rsqrt(1.0 - t)) * (i * uc)

    groups = seq // SUBLANES
    gshape = (groups, SUBLANES, LRU_BLOCK)
    a3 = a.reshape(gshape)
    b3 = b.reshape(gshape)
    sub = lax.broadcasted_iota(jnp.int32, gshape, 1)
    d = 1
    while d < SUBLANES:
        keep = sub >= d
        b3 = a3 * jnp.where(keep, pltpu.roll(b3, d, axis=1), 0.0) + b3
        a3 = a3 * jnp.where(keep, pltpu.roll(a3, d, axis=1), 1.0)
        d *= 2
    at_scr[...] = jnp.broadcast_to(a3[:, SUBLANES - 1:, :], gshape).reshape(seq, LRU_BLOCK)
    bt_scr[...] = jnp.broadcast_to(b3[:, SUBLANES - 1:, :], gshape).reshape(seq, LRU_BLOCK)

    def carry(g, h):
        rows = pl.ds(pl.multiple_of(g * SUBLANES, SUBLANES), SUBLANES)
        hp_scr[rows, :] = h
        return at_scr[rows, :] * h + bt_scr[rows, :]

    lax.fori_loop(0, groups, carry, jnp.zeros((SUBLANES, LRU_BLOCK), F32), unroll=8)
    hs = b3.reshape(seq, LRU_BLOCK) + a3.reshape(seq, LRU_BLOCK) * hp_scr[...]
    o_ref[...] = hs * jax.nn.gelu(gate_ref[...])


def _rg_lru(ug, cw, cb, wg, bra, bri, lam, *, batch, seq):
    nb = LRU_BLOCKS
    vec = lambda: pl.BlockSpec((1, LRU_BLOCK), lambda b, n: (0, n))
    return pl.pallas_call(
        functools.partial(_lru_kernel, seq=seq),
        grid=(batch, nb),
        in_specs=[
            pl.BlockSpec((None, seq, LRU_BLOCK), lambda b, n: (b, 0, n)),
            pl.BlockSpec((None, seq, LRU_BLOCK), lambda b, n: (b, 0, nb + n)),
            pl.BlockSpec((CONV_W, LRU_BLOCK), lambda b, n: (0, n)),
            vec(),
            pl.BlockSpec((None, LRU_BLOCK, 2 * LRU_BLOCK), lambda b, n: (n, 0, 0)),
            vec(), vec(), vec(),
        ],
        out_specs=pl.BlockSpec((None, seq, LRU_BLOCK), lambda b, n: (b, 0, n)),
        out_shape=jax.ShapeDtypeStruct((batch, seq, LRU_W), F32),
        scratch_shapes=[pltpu.VMEM((seq, LRU_BLOCK), F32)] * 3,
        compiler_params=_params("parallel", "arbitrary"),
    )(ug, ug, cw, cb, wg, bra, bri, lam)


def _out_kernel(of_ref, yl_ref, gf_ref, gl_ref, w_ref, x_ref, o_ref, *, chunk):
    mf = _rms(of_ref[...], gf_ref[...]).astype(BF16)
    ml = _rms(yl_ref[...], gl_ref[...]).astype(BF16)
    for c in range(o_ref.shape[1] // chunk):
        sl = slice(c * chunk, (c + 1) * chunk)
        acc = _dot(mf, w_ref[:FOX_W, sl]) + _dot(ml, w_ref[FOX_W:, sl])
        o_ref[:, sl] = x_ref[:, sl] + acc


def _out_proj(o_fox, y_lru, gf, gl, w, x2d, *, tm):
    T, D = x2d.shape
    return pl.pallas_call(
        functools.partial(_out_kernel, chunk=2 * PROJ_CHUNK),
        grid=(T // tm,),
        in_specs=[
            pl.BlockSpec((tm, FOX_W), lambda i: (i, 0)),
            pl.BlockSpec((tm, LRU_W), lambda i: (i, 0)),
            pl.BlockSpec((1, FOX_W), lambda i: (0, 0)),
            pl.BlockSpec((1, LRU_W), lambda i: (0, 0)),
            pl.BlockSpec((FOX_W + LRU_W, D), lambda i: (0, 0)),
            pl.BlockSpec((tm, D), lambda i: (i, 0)),
        ],
        out_specs=pl.BlockSpec((tm, D), lambda i: (i, 0)),
        out_shape=jax.ShapeDtypeStruct((T, D), F32),
        compiler_params=_params("parallel"),
    )(o_fox, y_lru, gf, gl, w, x2d)


def _xattn_kernel(x_ref, g_ref, wq_ref, gq_ref, ckv_ref, wo_ref, o_ref):
    x = x_ref[...]
    cq = _dot(_rms(x, g_ref[...]).astype(BF16), wq_ref[...])
    scale = 1.0 / math.sqrt(HEAD_DIM)
    scores = []
    for hh in range(XATT_HEADS):
        sl = slice(hh * HEAD_DIM, (hh + 1) * HEAD_DIM)
        qh = _rms(cq[:, sl], gq_ref[...]).astype(BF16)
        scores.append(_dot_nt(qh, ckv_ref[:, sl]))
    heads = []
    for hh in range(XATT_HEADS):
        vh = ckv_ref[:, XATT_W + hh * HEAD_DIM:XATT_W + (hh + 1) * HEAD_DIM]
        s = scores[hh] * scale
        e = jnp.exp(s - jnp.max(s, axis=-1, keepdims=True))
        l = jnp.sum(e, axis=-1, keepdims=True)
        heads.append((_dot(e.astype(BF16), vh) / l).astype(BF16))
    ox = jnp.concatenate(heads, axis=-1)
    o_ref[...] = x + _dot(ox, wo_ref[...])


def _cross_attention(x3d, g, wq, gq, ckv, wo, *, tm):
    B, S, D = x3d.shape
    M = ckv.shape[1]
    return pl.pallas_call(
        _xattn_kernel,
        grid=(B, S // tm),
        in_specs=[
            pl.BlockSpec((None, tm, D), lambda b, i: (b, i, 0)),
            pl.BlockSpec((1, D), lambda b, i: (0, 0)),
            pl.BlockSpec((D, XATT_W), lambda b, i: (0, 0)),
            pl.BlockSpec((1, HEAD_DIM), lambda b, i: (0, 0)),
            pl.BlockSpec((None, M, 2 * XATT_W), lambda b, i: (b, 0, 0)),
            pl.BlockSpec((XATT_W, D), lambda b, i: (0, 0)),
        ],
        out_specs=pl.BlockSpec((None, tm, D), lambda b, i: (b, i, 0)),
        out_shape=jax.ShapeDtypeStruct((B, S, D), F32),
        compiler_params=_params("parallel", "parallel"),
    )(x3d, g, wq, gq, ckv, wo)


def _ffn_kernel(x_ref, g_ref, wg_ref, wu_ref, wd_ref, o_ref, h_scr):
    @pl.when(pl.program_id(1) == 0)
    def _():
        x = x_ref[...]
        h_scr[...] = _rms(x, g_ref[...]).astype(BF16)
        o_ref[...] = x

    h = h_scr[...]
    gate = _dot(h, wg_ref[...])
    up = _dot(h, wu_ref[...])
    act = (jax.nn.silu(gate) * up).astype(BF16)
    o_ref[...] += _dot(act, wd_ref[...])


def _ffn(x2d, g, w_gu, w_d, *, tm, th):
    T, D = x2d.shape
    H = w_d.shape[0]
    nk = H // th
    return pl.pallas_call(
        _ffn_kernel,
        grid=(T // tm, nk),
        in_specs=[
            pl.BlockSpec((tm, D), lambda i, k: (i, 0)),
            pl.BlockSpec((1, D), lambda i, k: (0, 0)),
            pl.BlockSpec((D, th), lambda i, k: (0, k)),
            pl.BlockSpec((D, th), lambda i, k: (0, nk + k)),
            pl.BlockSpec((th, D), lambda i, k: (k, 0)),
        ],
        out_specs=pl.BlockSpec((tm, D), lambda i, k: (i, 0)),
        out_shape=jax.ShapeDtypeStruct((T, D), F32),
        scratch_shapes=[pltpu.VMEM((tm, D), BF16)],
        compiler_params=_params("parallel", "arbitrary"),
    )(x2d, g, w_gu, w_gu, w_d)


def kernel(x, mem, g_mix, w_in, b_f, g_q, g_k, conv_w, conv_b, w_ra, b_ra, w_ri, b_ri, lam,
           g_fox_out, g_lru_out, w_out, g_xattn, g_mem, w_cq, w_ckv, g_cq, g_ck, w_co, g_ffn,
           w_gate_up, w_down):
    B, S, D = x.shape
    M = mem.shape[1]
    T = B * S
    depth = g_mix.shape[0]
    row = lambda v: v.reshape(1, -1).astype(F32)

    for l in range(depth):
        w_t, w_f = _stage_in_weights(jnp.swapaxes(w_in, 1, 2), l, tr=512)
        g_qk = jnp.stack([g_q[l], g_k[l]]).reshape(2, 1, HEAD_DIM).astype(F32)
        bf_col = jnp.zeros((F_ROWS, 1), F32).at[:FOX_HEADS, 0].set(b_f[l])
        w_gates = jnp.concatenate([w_ra[l], w_ri[l]], axis=-1).astype(BF16)

        x2d = x.reshape(T, D)

        qk, v_t, ug, f_t = _in_proj(x2d, row(g_mix[l]), w_t, g_qk, w_f, tm=1024, tn=1024)
        o_fox = _fox_attention(qk.reshape(B, S, 2 * FOX_W), v_t, f_t, bf_col, batch=B, seq=S)
        y_lru = _rg_lru(ug.reshape(B, S, 2 * LRU_W), conv_w[l], row(conv_b[l]), w_gates,
                        row(b_ra[l]), row(b_ri[l]), row(lam[l]), batch=B, seq=S)
        x2d = _out_proj(o_fox.reshape(T, FOX_W), y_lru.reshape(T, LRU_W),
                        row(g_fox_out[l]), row(g_lru_out[l]), w_out[l].astype(BF16), x2d,
                        tm=512)

        ckv = _ckv_proj(mem.reshape(B * M, D), row(g_mem[l]), w_ckv[l].astype(BF16),
                        row(g_ck[l]))
        x3d = _cross_attention(x2d.reshape(B, S, D), row(g_xattn[l]), w_cq[l].astype(BF16),
                               row(g_cq[l]), ckv.reshape(B, M, 2 * XATT_W),
                               w_co[l].astype(BF16), tm=512)

        x2d = _ffn(x3d.reshape(T, D), row(g_ffn[l]), w_gate_up[l].astype(BF16),
                   w_down[l].astype(BF16), tm=1024, th=512)
        x = x2d.reshape(B, S, D)
    return x
```

```python
import functools
import math

import jax
import jax.numpy as jnp
from jax import lax
from jax.experimental import pallas as pl
from jax.experimental.pallas import tpu as pltpu

F32 = jnp.float32
BF16 = jnp.bfloat16

HEAD_DIM = 128
FOX_HEADS = 8
FOX_W = FOX_HEADS * HEAD_DIM
LRU_BLOCKS = 8
LRU_BLOCK = 128
LRU_W = LRU_BLOCKS * LRU_BLOCK
LRU_C = 8.0
CONV_W = 4
XATT_HEADS = 4
XATT_W = XATT_HEADS * HEAD_DIM
RMS_EPS = 1e-6
F_ROWS = 16
Q_BLOCK = 512
K_CHUNK = 512
SCORE_LOOKAHEAD = 3
V_PAD_ROWS = 16
PROJ_CHUNK = 256
SUBLANES = 8
BIAS_PIECES = 3

V7X_VMEM_LIMIT_BYTES = 60 * 1024 * 1024


def _rms(x, g):
    ms = jnp.mean(x * x, axis=-1, keepdims=True)
    return x * lax.rsqrt(ms + RMS_EPS) * g


def _dot(a, b):
    return jnp.dot(a, b, preferred_element_type=F32)


def _dot_nt(a, b):
    return lax.dot_general(a, b, (((1,), (1,)), ((), ())), preferred_element_type=F32)


def _softplus(x):
    return jnp.maximum(x, 0.0) + jnp.log1p(jnp.exp(-jnp.abs(x)))


def _params(*sem):
    return pltpu.CompilerParams(dimension_semantics=sem,
                                vmem_limit_bytes=V7X_VMEM_LIMIT_BYTES)


def _wstage_kernel(a_ref, b_ref, w_ref, wf_ref, *, f_tile):
    t = pl.program_id(0)

    @pl.when(t < f_tile)
    def _():
        w_ref[...] = a_ref[...].astype(BF16)

    @pl.when(t >= f_tile)
    def _():
        w_ref[...] = jnp.concatenate([a_ref[FOX_HEADS:, :], b_ref[:FOX_HEADS, :]],
                                     axis=0).astype(BF16)

    @pl.when(t == f_tile)
    def _():
        wf_ref[:FOX_HEADS, :] = a_ref[:FOX_HEADS, :].astype(BF16)
        wf_ref[FOX_HEADS:, :] = jnp.zeros((F_ROWS - FOX_HEADS, wf_ref.shape[1]), BF16)


def _stage_in_weights(w_t, layer, *, tr):
    _, N, D = w_t.shape
    n_rows = N - FOX_HEADS
    f_tile = 3 * FOX_W // tr
    return pl.pallas_call(
        functools.partial(_wstage_kernel, f_tile=f_tile),
        grid=(n_rows // tr,),
        in_specs=[pl.BlockSpec((None, tr, D), lambda t: (layer, t, 0)),
                  pl.BlockSpec((None, tr, D), lambda t: (layer, jnp.maximum(t, f_tile) + 1, 0))],
        out_specs=[pl.BlockSpec((tr, D), lambda t: (t, 0)),
                   pl.BlockSpec((F_ROWS, D), lambda t: (0, 0))],
        out_shape=[jax.ShapeDtypeStruct((n_rows, D), BF16),
                   jax.ShapeDtypeStruct((F_ROWS, D), BF16)],
        compiler_params=_params("arbitrary"),
    )(w_t, w_t)


def _inproj_kernel(x_ref, g_ref, w_ref, gh_ref, wf_ref,
                   qk_ref, vt_ref, ug_ref, f_ref, h_scr, *, tn, n_qk, n_v):
    j = pl.program_id(1)

    @pl.when(j == 0)
    def _():
        h = _rms(x_ref[...], g_ref[...]).astype(BF16)
        h_scr[...] = h
        f_ref[...] = _dot_nt(wf_ref[...], h)

    chunks = [slice(c * PROJ_CHUNK, (c + 1) * PROJ_CHUNK) for c in range(tn // PROJ_CHUNK)]

    @pl.when(j < n_qk)
    def _():
        for sl in chunks:
            acc = _dot_nt(h_scr[...], w_ref[sl, :])
            for hh in range(PROJ_CHUNK // HEAD_DIM):
                lo = sl.start + hh * HEAD_DIM
                qk_ref[:, lo:lo + HEAD_DIM] = _rms(
                    acc[:, hh * HEAD_DIM:(hh + 1) * HEAD_DIM], gh_ref[...]).astype(BF16)

    @pl.when((j >= n_qk) & (j < n_qk + n_v))
    def _():
        for sl in chunks:
            vt_ref[sl, :] = _dot_nt(w_ref[sl, :], h_scr[...]).astype(BF16)

    @pl.when(j >= n_qk + n_v)
    def _():
        for sl in chunks:
            ug_ref[:, sl] = _dot_nt(h_scr[...], w_ref[sl, :])


def _in_proj(x2d, g, w_t, g_qk, wf, *, tm, tn):
    T, D = x2d.shape
    n_qk = 2 * FOX_W // tn
    n_v = FOX_W // tn
    n_ug = 2 * LRU_W // tn
    tiles_per_gain = FOX_W // tn
    clip = lambda v, n: jnp.clip(v, 0, n - 1)
    return pl.pallas_call(
        functools.partial(_inproj_kernel, tn=tn, n_qk=n_qk, n_v=n_v),
        grid=(T // tm, n_qk + n_v + n_ug),
        in_specs=[
            pl.BlockSpec((tm, D), lambda i, j: (i, 0)),
            pl.BlockSpec((1, D), lambda i, j: (0, 0)),
            pl.BlockSpec((tn, D), lambda i, j: (j, 0)),
            pl.BlockSpec((None, 1, HEAD_DIM), lambda i, j: (clip(j // tiles_per_gain, 2), 0, 0)),
            pl.BlockSpec((F_ROWS, D), lambda i, j: (0, 0)),
        ],
        out_specs=[
            pl.BlockSpec((tm, tn), lambda i, j: (i, clip(j, n_qk))),
            pl.BlockSpec((tn, tm), lambda i, j: (clip(j - n_qk, n_v), i)),
            pl.BlockSpec((tm, tn), lambda i, j: (i, clip(j - n_qk - n_v, n_ug))),
            pl.BlockSpec((F_ROWS, tm), lambda i, j: (0, i)),
        ],
        out_shape=[
            jax.ShapeDtypeStruct((T, 2 * FOX_W), BF16),
            jax.ShapeDtypeStruct((FOX_W, T), BF16),
            jax.ShapeDtypeStruct((T, 2 * LRU_W), F32),
            jax.ShapeDtypeStruct((F_ROWS, T), F32),
        ],
        scratch_shapes=[pltpu.VMEM((tm, D), BF16)],
        compiler_params=_params("parallel", "arbitrary"),
    )(x2d, g, w_t, g_qk, wf)


def _ckv_kernel(m_ref, g_ref, w_ref, gh_ref, o_ref):
    h = _rms(m_ref[...], g_ref[...]).astype(BF16)
    acc = _dot(h, w_ref[...].astype(BF16))
    for hh in range(XATT_HEADS):
        sl = slice(hh * HEAD_DIM, (hh + 1) * HEAD_DIM)
        o_ref[:, sl] = _rms(acc[:, sl], gh_ref[...]).astype(BF16)
    o_ref[:, XATT_W:] = acc[:, XATT_W:].astype(BF16)


def _ckv_proj(mem2d, g, w_ckv, layer, gh):
    R, D = mem2d.shape
    full = lambda shape: pl.BlockSpec(shape, lambda i: (0,) * len(shape))
    return pl.pallas_call(
        _ckv_kernel,
        grid=(1,),
        in_specs=[full((R, D)), full((1, D)),
                  pl.BlockSpec((None, D, 2 * XATT_W), lambda i: (layer, 0, 0)),
                  full((1, HEAD_DIM))],
        out_specs=full((R, 2 * XATT_W)),
        out_shape=jax.ShapeDtypeStruct((R, 2 * XATT_W), BF16),
        compiler_params=_params("arbitrary"),
    )(mem2d, g, w_ckv, gh)


def _fox_kernel(q_ref, k_ref, vt_ref, f_ref, bf_ref, o_ref, kaug_scr, vaug_scr, *, seq):
    h = pl.program_id(1)

    @pl.when(h == 0)
    def _():
        z = f_ref[...] + bf_ref[...]
        c = jnp.minimum(z, 0.0) - jnp.log1p(jnp.exp(-jnp.abs(z)))
        lane = lax.broadcasted_iota(jnp.int32, c.shape, 1)
        d = 1
        while d < seq:
            c = c + jnp.where(lane >= d, pltpu.roll(c, d, axis=1), 0.0)
            d *= 2
        bias = c[:FOX_HEADS] * (-math.sqrt(HEAD_DIM))
        pieces = []
        for _ in range(BIAS_PIECES):
            p = bias.astype(BF16).astype(F32)
            pieces.append(p)
            bias = bias - p
        pad = jnp.zeros((HEAD_DIM - BIAS_PIECES * FOX_HEADS, seq), F32)
        kaug_scr[:, HEAD_DIM:] = jnp.concatenate(pieces + [pad], axis=0).T.astype(BF16)

    kaug_scr[:, :HEAD_DIM] = k_ref[...]
    vaug_scr[:HEAD_DIM, :] = vt_ref[...]
    ones_row = lax.broadcasted_iota(jnp.int32, (V_PAD_ROWS, seq), 0) == 0
    vaug_scr[HEAD_DIM:, :] = jnp.where(ones_row, 1.0, 0.0).astype(BF16)

    tq, tk = Q_BLOCK, K_CHUNK
    lane = lax.broadcasted_iota(jnp.int32, (tq, HEAD_DIM), 1)
    mine = lane == h
    for p in range(1, BIAS_PIECES):
        mine = mine | (lane == h + p * FOX_HEADS)
    selector = jnp.where(mine, 1.0, 0.0).astype(BF16)
    key = lax.broadcasted_iota(jnp.int32, (tk, tq), 0)
    qry = lax.broadcasted_iota(jnp.int32, (tk, tq), 1)
    exp2_scale = math.log2(math.e) / math.sqrt(HEAD_DIM)

    def scores(lo, k0):
        q_aug = jnp.concatenate([q_ref[lo:lo + tq, :], selector], axis=1)
        return _dot_nt(kaug_scr[k0:k0 + tk, :], q_aug)

    steps = [(qi * tq, kc * tk) for qi in range(seq // tq) for kc in range((qi + 1) * tq // tk)]
    pending = [scores(*st) for st in steps[:SCORE_LOOKAHEAD]]
    for idx, (lo, k0) in enumerate(steps):
        t = pending.pop(0)
        if idx + SCORE_LOOKAHEAD < len(steps):
            pending.append(scores(*steps[idx + SCORE_LOOKAHEAD]))
        if k0 + tk > lo:
            t = jnp.where(key + (k0 - lo) <= qry, t, -jnp.inf)
        cm = jnp.max(t, axis=0, keepdims=True)
        m_new = cm if k0 == 0 else jnp.maximum(m, cm)
        p = jnp.exp2((t - m_new) * exp2_scale).astype(BF16)
        pv = _dot(vaug_scr[:, k0:k0 + tk], p)
        if k0 == 0:
            acc = pv
        else:
            acc = acc * jnp.exp2((m - m_new) * exp2_scale) + pv
        m = m_new
        if k0 + tk == lo + tq:
            inv_l = 1.0 / acc[HEAD_DIM:HEAD_DIM + 1, :]
            o_ref[lo:lo + tq, :] = (acc[:HEAD_DIM, :] * inv_l).T


def _fox_attention(qk, v_t, f_t, b_f, *, batch, seq):
    nh = FOX_HEADS
    return pl.pallas_call(
        functools.partial(_fox_kernel, seq=seq),
        grid=(batch, nh),
        in_specs=[
            pl.BlockSpec((None, seq, HEAD_DIM), lambda b, h: (b, 0, h)),
            pl.BlockSpec((None, seq, HEAD_DIM), lambda b, h: (b, 0, nh + h)),
            pl.BlockSpec((HEAD_DIM, seq), lambda b, h: (h, b)),
            pl.BlockSpec((F_ROWS, seq), lambda b, h: (0, b)),
            pl.BlockSpec((F_ROWS, 1), lambda b, h: (0, 0)),
        ],
        out_specs=pl.BlockSpec((None, seq, HEAD_DIM), lambda b, h: (b, 0, h)),
        out_shape=jax.ShapeDtypeStruct((batch, seq, FOX_W), F32),
        scratch_shapes=[pltpu.VMEM((seq, 2 * HEAD_DIM), BF16),
                        pltpu.VMEM((HEAD_DIM + V_PAD_ROWS, seq), BF16)],
        compiler_params=_params("parallel", "arbitrary"),
    )(qk, qk, v_t, f_t, b_f)


def _lru_kernel(u_ref, gate_ref, cw_ref, cb_ref, wg_ref, bra_ref, bri_ref, lam_ref, o_ref,
                at_scr, bt_scr, hp_scr, *, seq):
    row = lax.broadcasted_iota(jnp.int32, (SUBLANES, LRU_BLOCK), 0)

    def shifted(d):
        head = jnp.where(row >= d, pltpu.roll(u_ref[0:SUBLANES, :], d, axis=0), 0.0)
        return jnp.concatenate([head, u_ref[SUBLANES - d:seq - d, :]], axis=0)

    cw = cw_ref[...]
    uc = cw[0:1, :] * shifted(3)
    uc = uc + cw[1:2, :] * shifted(2)
    uc = uc + cw[2:3, :] * shifted(1)
    uc = uc + cw[3:4, :] * u_ref[...]
    uc = cb_ref[...] + uc

    gates = _dot(uc.astype(BF16), wg_ref[...])
    half_c = (-0.5 * LRU_C) * _softplus(-lam_ref[...])
    log_a = half_c * jnp.tanh(0.5 * (gates[:, :LRU_BLOCK] + bra_ref[...])) + half_c
    i = 0.5 * jnp.tanh(0.5 * (gates[:, LRU_BLOCK:] + bri_ref[...])) + 0.5
    a = jnp.exp(log_a)
    t = jnp.tanh(log_a)
    w = -2.0 * t
    sqrt_w = jnp.where(w > 0.0, w * lax.rsqrt(w), 0.0)
    b = (sqrt_w * lax.rsqrt(1.0 - t)) * (i * uc)

    groups = seq // SUBLANES
    gshape = (groups, SUBLANES, LRU_BLOCK)
    a3 = a.reshape(gshape)
    b3 = b.reshape(gshape)
    sub = lax.broadcasted_iota(jnp.int32, gshape, 1)
    d = 1
    while d < SUBLANES:
        keep = sub >= d
        b3 = a3 * jnp.where(keep, pltpu.roll(b3, d, axis=1), 0.0) + b3
        a3 = a3 * jnp.where(keep, pltpu.roll(a3, d, axis=1), 1.0)
        d *= 2
    at_scr[...] = jnp.broadcast_to(a3[:, SUBLANES - 1:, :], gshape).reshape(seq, LRU_BLOCK)
    bt_scr[...] = jnp.broadcast_to(b3[:, SUBLANES - 1:, :], gshape).reshape(seq, LRU_BLOCK)

    def carry(g, h):
        rows = pl.ds(pl.multiple_of(g * SUBLANES, SUBLANES), SUBLANES)
        hp_scr[rows, :] = h
        return at_scr[rows, :] * h + bt_scr[rows, :]

    lax.fori_loop(0, groups, carry, jnp.zeros((SUBLANES, LRU_BLOCK), F32), unroll=8)
    hs = b3.reshape(seq, LRU_BLOCK) + a3.reshape(seq, LRU_BLOCK) * hp_scr[...]
    o_ref[...] = hs * jax.nn.gelu(gate_ref[...])


def _rg_lru(ug, cw, cb, wg, bra, bri, lam, *, batch, seq):
    nb = LRU_BLOCKS
    vec = lambda: pl.BlockSpec((1, LRU_BLOCK), lambda b, n: (0, n))
    return pl.pallas_call(
        functools.partial(_lru_kernel, seq=seq),
        grid=(batch, nb),
        in_specs=[
            pl.BlockSpec((None, seq, LRU_BLOCK), lambda b, n: (b, 0, n)),
            pl.BlockSpec((None, seq, LRU_BLOCK), lambda b, n: (b, 0, nb + n)),
            pl.BlockSpec((CONV_W, LRU_BLOCK), lambda b, n: (0, n)),
            vec(),
            pl.BlockSpec((None, LRU_BLOCK, 2 * LRU_BLOCK), lambda b, n: (n, 0, 0)),
            vec(), vec(), vec(),
        ],
        out_specs=pl.BlockSpec((None, seq, LRU_BLOCK), lambda b, n: (b, 0, n)),
        out_shape=jax.ShapeDtypeStruct((batch, seq, LRU_W), F32),
        scratch_shapes=[pltpu.VMEM((seq, LRU_BLOCK), F32)] * 3,
        compiler_params=_params("parallel", "arbitrary"),
    )(ug, ug, cw, cb, wg, bra, bri, lam)


def _out_kernel(of_ref, yl_ref, gf_ref, gl_ref, w_ref, x_ref, o_ref, w_scr, *, chunk):
    @pl.when(pl.program_id(0) == 0)
    def _():
        w_scr[...] = w_ref[...].astype(BF16)

    mf = _rms(of_ref[...], gf_ref[...]).astype(BF16)
    ml = _rms(yl_ref[...], gl_ref[...]).astype(BF16)
    for c in range(o_ref.shape[1] // chunk):
        sl = slice(c * chunk, (c + 1) * chunk)
        acc = _dot(mf, w_scr[:FOX_W, sl]) + _dot(ml, w_scr[FOX_W:, sl])
        o_ref[:, sl] = x_ref[:, sl] + acc


def _out_proj(o_fox, y_lru, gf, gl, w_out, layer, x2d, *, tm):
    T, D = x2d.shape
    return pl.pallas_call(
        functools.partial(_out_kernel, chunk=2 * PROJ_CHUNK),
        grid=(T // tm,),
        in_specs=[
            pl.BlockSpec((tm, FOX_W), lambda i: (i, 0)),
            pl.BlockSpec((tm, LRU_W), lambda i: (i, 0)),
            pl.BlockSpec((1, FOX_W), lambda i: (0, 0)),
            pl.BlockSpec((1, LRU_W), lambda i: (0, 0)),
            pl.BlockSpec((None, FOX_W + LRU_W, D), lambda i: (layer, 0, 0),
                         pipeline_mode=pl.Buffered(1)),
            pl.BlockSpec((tm, D), lambda i: (i, 0)),
        ],
        out_specs=pl.BlockSpec((tm, D), lambda i: (i, 0)),
        out_shape=jax.ShapeDtypeStruct((T, D), F32),
        scratch_shapes=[pltpu.VMEM((FOX_W + LRU_W, D), BF16)],
        compiler_params=_params("arbitrary"),
    )(o_fox, y_lru, gf, gl, w_out, x2d)


def _xattn_kernel(x_ref, g_ref, wq_ref, gq_ref, ckv_ref, wo_ref, o_ref):
    x = x_ref[...]
    cq = _dot(_rms(x, g_ref[...]).astype(BF16), wq_ref[...].astype(BF16))
    scale = 1.0 / math.sqrt(HEAD_DIM)
    scores = []
    for hh in range(XATT_HEADS):
        sl = slice(hh * HEAD_DIM, (hh + 1) * HEAD_DIM)
        qh = _rms(cq[:, sl], gq_ref[...]).astype(BF16)
        scores.append(_dot_nt(qh, ckv_ref[:, sl]))
    heads = []
    for hh in range(XATT_HEADS):
        vh = ckv_ref[:, XATT_W + hh * HEAD_DIM:XATT_W + (hh + 1) * HEAD_DIM]
        s = scores[hh] * scale
        e = jnp.exp(s - jnp.max(s, axis=-1, keepdims=True))
        l = jnp.sum(e, axis=-1, keepdims=True)
        heads.append((_dot(e.astype(BF16), vh) / l).astype(BF16))
    ox = jnp.concatenate(heads, axis=-1)
    o_ref[...] = x + _dot(ox, wo_ref[...].astype(BF16))


def _cross_attention(x3d, g, w_cq, gq, ckv, w_co, layer, *, tm):
    B, S, D = x3d.shape
    M = ckv.shape[1]
    return pl.pallas_call(
        _xattn_kernel,
        grid=(B, S // tm),
        in_specs=[
            pl.BlockSpec((None, tm, D), lambda b, i: (b, i, 0)),
            pl.BlockSpec((1, D), lambda b, i: (0, 0)),
            pl.BlockSpec((None, D, XATT_W), lambda b, i: (layer, 0, 0)),
            pl.BlockSpec((1, HEAD_DIM), lambda b, i: (0, 0)),
            pl.BlockSpec((None, M, 2 * XATT_W), lambda b, i: (b, 0, 0)),
            pl.BlockSpec((None, XATT_W, D), lambda b, i: (layer, 0, 0)),
        ],
        out_specs=pl.BlockSpec((None, tm, D), lambda b, i: (b, i, 0)),
        out_shape=jax.ShapeDtypeStruct((B, S, D), F32),
        compiler_params=_params("parallel", "parallel"),
    )(x3d, g, w_cq, gq, ckv, w_co)


def _ffn_kernel(x_ref, g_ref, wg_ref, wu_ref, wd_ref, o_ref, h_scr):
    @pl.when(pl.program_id(1) == 0)
    def _():
        x = x_ref[...]
        h_scr[...] = _rms(x, g_ref[...]).astype(BF16)
        o_ref[...] = x

    h = h_scr[...]
    gate = _dot(h, wg_ref[...])
    up = _dot(h, wu_ref[...])
    act = (jax.nn.silu(gate) * up).astype(BF16)
    o_ref[...] += _dot(act, wd_ref[...])


def _ffn(x2d, g, w_gu, w_d, *, tm, th):
    T, D = x2d.shape
    H = w_d.shape[0]
    nk = H // th
    return pl.pallas_call(
        _ffn_kernel,
        grid=(T // tm, nk),
        in_specs=[
            pl.BlockSpec((tm, D), lambda i, k: (i, 0)),
            pl.BlockSpec((1, D), lambda i, k: (0, 0)),
            pl.BlockSpec((D, th), lambda i, k: (0, k)),
            pl.BlockSpec((D, th), lambda i, k: (0, nk + k)),
            pl.BlockSpec((th, D), lambda i, k: (k, 0)),
        ],
        out_specs=pl.BlockSpec((tm, D), lambda i, k: (i, 0)),
        out_shape=jax.ShapeDtypeStruct((T, D), F32),
        scratch_shapes=[pltpu.VMEM((tm, D), BF16)],
        compiler_params=_params("parallel", "arbitrary"),
    )(x2d, g, w_gu, w_gu, w_d)


def kernel(x, mem, g_mix, w_in, b_f, g_q, g_k, conv_w, conv_b, w_ra, b_ra, w_ri, b_ri, lam,
           g_fox_out, g_lru_out, w_out, g_xattn, g_mem, w_cq, w_ckv, g_cq, g_ck, w_co, g_ffn,
           w_gate_up, w_down):
    B, S, D = x.shape
    M = mem.shape[1]
    T = B * S
    depth = g_mix.shape[0]
    row = lambda v: v.reshape(1, -1).astype(F32)

    for l in range(depth):
        w_t, w_f = _stage_in_weights(jnp.swapaxes(w_in, 1, 2), l, tr=512)
        g_qk = jnp.stack([g_q[l], g_k[l]]).reshape(2, 1, HEAD_DIM).astype(F32)
        bf_col = jnp.zeros((F_ROWS, 1), F32).at[:FOX_HEADS, 0].set(b_f[l])
        w_gates = jnp.concatenate([w_ra[l], w_ri[l]], axis=-1).astype(BF16)

        x2d = x.reshape(T, D)

        qk, v_t, ug, f_t = _in_proj(x2d, row(g_mix[l]), w_t, g_qk, w_f, tm=1024, tn=1024)
        o_fox = _fox_attention(qk.reshape(B, S, 2 * FOX_W), v_t, f_t, bf_col, batch=B, seq=S)
        y_lru = _rg_lru(ug.reshape(B, S, 2 * LRU_W), conv_w[l], row(conv_b[l]), w_gates,
                        row(b_ra[l]), row(b_ri[l]), row(lam[l]), batch=B, seq=S)
        x2d = _out_proj(o_fox.reshape(T, FOX_W), y_lru.reshape(T, LRU_W),
                        row(g_fox_out[l]), row(g_lru_out[l]), w_out, l, x2d, tm=512)

        ckv = _ckv_proj(mem.reshape(B * M, D), row(g_mem[l]), w_ckv, l, row(g_ck[l]))
        x3d = _cross_attention(x2d.reshape(B, S, D), row(g_xattn[l]), w_cq, row(g_cq[l]),
                               ckv.reshape(B, M, 2 * XATT_W), w_co, l, tm=512)

        x2d = _ffn(x3d.reshape(T, D), row(g_ffn[l]), w_gate_up[l].astype(BF16),
                   w_down[l].astype(BF16), tm=1024, th=512)
        x = x2d.reshape(B, S, D)
    return x
```

```python
import functools
import math

import jax
import jax.numpy as jnp
from jax import lax
from jax.experimental import pallas as pl
from jax.experimental.pallas import tpu as pltpu

F32 = jnp.float32
BF16 = jnp.bfloat16

HEAD_DIM = 128
FOX_HEADS = 8
FOX_W = FOX_HEADS * HEAD_DIM
LRU_BLOCKS = 8
LRU_BLOCK = 128
LRU_W = LRU_BLOCKS * LRU_BLOCK
LRU_C = 8.0
CONV_W = 4
XATT_HEADS = 4
XATT_W = XATT_HEADS * HEAD_DIM
RMS_EPS = 1e-6
F_ROWS = 16
Q_BLOCK = 512
K_CHUNK = 512
SCORE_LOOKAHEAD = 3
V_PAD_ROWS = 16
PROJ_CHUNK = 256
SUBLANES = 8
BIAS_PIECES = 3

V7X_VMEM_LIMIT_BYTES = 60 * 1024 * 1024


def _rms(x, g):
    ms = jnp.mean(x * x, axis=-1, keepdims=True)
    return x * lax.rsqrt(ms + RMS_EPS) * g


def _dot(a, b):
    return jnp.dot(a, b, preferred_element_type=F32)


def _dot_nt(a, b):
    return lax.dot_general(a, b, (((1,), (1,)), ((), ())), preferred_element_type=F32)


def _softplus(x):
    return jnp.maximum(x, 0.0) + jnp.log1p(jnp.exp(-jnp.abs(x)))


def _params(*sem):
    return pltpu.CompilerParams(dimension_semantics=sem,
                                vmem_limit_bytes=V7X_VMEM_LIMIT_BYTES)


def _wstage_kernel(a_ref, b_ref, w_ref, wf_ref, *, f_tile):
    t = pl.program_id(0)

    @pl.when(t < f_tile)
    def _():
        w_ref[...] = a_ref[...].astype(BF16)

    @pl.when(t >= f_tile)
    def _():
        w_ref[...] = jnp.concatenate([a_ref[FOX_HEADS:, :], b_ref[:FOX_HEADS, :]],
                                     axis=0).astype(BF16)

    @pl.when(t == f_tile)
    def _():
        wf_ref[:FOX_HEADS, :] = a_ref[:FOX_HEADS, :].astype(BF16)
        wf_ref[FOX_HEADS:, :] = jnp.zeros((F_ROWS - FOX_HEADS, wf_ref.shape[1]), BF16)


def _stage_in_weights(w_t, layer, *, tr):
    _, N, D = w_t.shape
    n_rows = N - FOX_HEADS
    f_tile = 3 * FOX_W // tr
    return pl.pallas_call(
        functools.partial(_wstage_kernel, f_tile=f_tile),
        grid=(n_rows // tr,),
        in_specs=[pl.BlockSpec((None, tr, D), lambda t: (layer, t, 0)),
                  pl.BlockSpec((None, tr, D), lambda t: (layer, jnp.maximum(t, f_tile) + 1, 0))],
        out_specs=[pl.BlockSpec((tr, D), lambda t: (t, 0)),
                   pl.BlockSpec((F_ROWS, D), lambda t: (0, 0))],
        out_shape=[jax.ShapeDtypeStruct((n_rows, D), BF16),
                   jax.ShapeDtypeStruct((F_ROWS, D), BF16)],
        compiler_params=_params("arbitrary"),
    )(w_t, w_t)


def _inproj_kernel(x_ref, g_ref, w_ref, gh_ref, wf_ref,
                   qk_ref, vt_ref, ug_ref, f_ref, h_scr, *, tn, n_qk, n_v, n_steps):
    i = pl.program_id(0)
    j = pl.program_id(1)
    h_cur = h_scr.at[lax.rem(i, 2)]
    h_nxt = h_scr.at[lax.rem(i + 1, 2)]

    @pl.when((i == 0) & (j == 0))
    def _():
        h_cur[...] = _rms(x_ref[...], g_ref[...]).astype(BF16)

    chunks = [slice(c * PROJ_CHUNK, (c + 1) * PROJ_CHUNK) for c in range(tn // PROJ_CHUNK)]

    @pl.when(j < n_qk)
    def _():
        for sl in chunks:
            acc = _dot_nt(h_cur[...], w_ref[sl, :])
            for hh in range(PROJ_CHUNK // HEAD_DIM):
                lo = sl.start + hh * HEAD_DIM
                qk_ref[:, lo:lo + HEAD_DIM] = _rms(
                    acc[:, hh * HEAD_DIM:(hh + 1) * HEAD_DIM], gh_ref[...]).astype(BF16)

    @pl.when((j >= n_qk) & (j < n_qk + n_v))
    def _():
        for sl in chunks[:-1]:
            vt_ref[sl, :] = _dot_nt(w_ref[sl, :], h_cur[...]).astype(BF16)
        sl = chunks[-1]
        rows = jnp.concatenate([w_ref[sl, :], wf_ref[...]], axis=0)
        acc = _dot_nt(rows, h_cur[...])
        vt_ref[sl, :] = acc[:PROJ_CHUNK].astype(BF16)
        f_ref[...] = acc[PROJ_CHUNK:]

    def ug_tile():
        for sl in chunks:
            ug_ref[:, sl] = _dot_nt(h_cur[...], w_ref[sl, :])

    pl.when((j >= n_qk + n_v) & (j < n_steps - 1))(ug_tile)

    @pl.when(j == n_steps - 1)
    def _():
        h_nxt[...] = _rms(x_ref[...], g_ref[...]).astype(BF16)
        ug_tile()


def _in_proj(x2d, g, w_t, g_qk, wf, *, tm, tn):
    T, D = x2d.shape
    n_qk = 2 * FOX_W // tn
    n_v = FOX_W // tn
    assert n_v == 1, "the forget logits are produced by the single v step"
    n_ug = 2 * LRU_W // tn
    tiles_per_gain = FOX_W // tn
    clip = lambda v, n: jnp.clip(v, 0, n - 1)
    return pl.pallas_call(
        functools.partial(_inproj_kernel, tn=tn, n_qk=n_qk, n_v=n_v,
                          n_steps=n_qk + n_v + n_ug),
        grid=(T // tm, n_qk + n_v + n_ug),
        in_specs=[
            pl.BlockSpec((tm, D), lambda i, j: (jnp.minimum(i + (j > 0), T // tm - 1), 0)),
            pl.BlockSpec((1, D), lambda i, j: (0, 0)),
            pl.BlockSpec((tn, D), lambda i, j: (j, 0)),
            pl.BlockSpec((None, 1, HEAD_DIM), lambda i, j: (clip(j // tiles_per_gain, 2), 0, 0)),
            pl.BlockSpec((F_ROWS, D), lambda i, j: (0, 0)),
        ],
        out_specs=[
            pl.BlockSpec((tm, tn), lambda i, j: (i, clip(j, n_qk))),
            pl.BlockSpec((tn, tm), lambda i, j: (clip(j - n_qk, n_v), i)),
            pl.BlockSpec((tm, tn), lambda i, j: (i, clip(j - n_qk - n_v, n_ug))),
            pl.BlockSpec((F_ROWS, tm), lambda i, j: (0, i)),
        ],
        out_shape=[
            jax.ShapeDtypeStruct((T, 2 * FOX_W), BF16),
            jax.ShapeDtypeStruct((FOX_W, T), BF16),
            jax.ShapeDtypeStruct((T, 2 * LRU_W), F32),
            jax.ShapeDtypeStruct((F_ROWS, T), F32),
        ],
        scratch_shapes=[pltpu.VMEM((2, tm, D), BF16)],
        compiler_params=_params("arbitrary", "arbitrary"),
    )(x2d, g, w_t, g_qk, wf)


def _ckv_kernel(m_ref, g_ref, w_ref, gh_ref, o_ref):
    h = _rms(m_ref[...], g_ref[...]).astype(BF16)
    acc = _dot(h, w_ref[...].astype(BF16))
    for hh in range(XATT_HEADS):
        sl = slice(hh * HEAD_DIM, (hh + 1) * HEAD_DIM)
        o_ref[:, sl] = _rms(acc[:, sl], gh_ref[...]).astype(BF16)
    o_ref[:, XATT_W:] = acc[:, XATT_W:].astype(BF16)


def _ckv_proj(mem2d, g, w_ckv, layer, gh):
    R, D = mem2d.shape
    full = lambda shape: pl.BlockSpec(shape, lambda i: (0,) * len(shape))
    return pl.pallas_call(
        _ckv_kernel,
        grid=(1,),
        in_specs=[full((R, D)), full((1, D)),
                  pl.BlockSpec((None, D, 2 * XATT_W), lambda i: (layer, 0, 0)),
                  full((1, HEAD_DIM))],
        out_specs=full((R, 2 * XATT_W)),
        out_shape=jax.ShapeDtypeStruct((R, 2 * XATT_W), BF16),
        compiler_params=_params("arbitrary"),
    )(mem2d, g, w_ckv, gh)


def _fox_kernel(q_ref, k_ref, vt_ref, f_ref, bf_ref, o_ref, kaug_scr, vaug_scr, *, seq):
    h = pl.program_id(1)

    @pl.when(h == 0)
    def _():
        z = f_ref[...] + bf_ref[...]
        c = jnp.minimum(z, 0.0) - jnp.log1p(jnp.exp(-jnp.abs(z)))
        lane = lax.broadcasted_iota(jnp.int32, c.shape, 1)
        d = 1
        while d < seq:
            c = c + jnp.where(lane >= d, pltpu.roll(c, d, axis=1), 0.0)
            d *= 2
        bias = c[:FOX_HEADS] * (-math.sqrt(HEAD_DIM))
        pieces = []
        for _ in range(BIAS_PIECES):
            p = bias.astype(BF16).astype(F32)
            pieces.append(p)
            bias = bias - p
        pad = jnp.zeros((HEAD_DIM - BIAS_PIECES * FOX_HEADS, seq), F32)
        kaug_scr[:, HEAD_DIM:] = jnp.concatenate(pieces + [pad], axis=0).T.astype(BF16)

    kaug_scr[:, :HEAD_DIM] = k_ref[...]
    vaug_scr[:HEAD_DIM, :] = vt_ref[...]
    ones_row = lax.broadcasted_iota(jnp.int32, (V_PAD_ROWS, seq), 0) == 0
    vaug_scr[HEAD_DIM:, :] = jnp.where(ones_row, 1.0, 0.0).astype(BF16)

    tq, tk = Q_BLOCK, K_CHUNK
    lane = lax.broadcasted_iota(jnp.int32, (tq, HEAD_DIM), 1)
    mine = lane == h
    for p in range(1, BIAS_PIECES):
        mine = mine | (lane == h + p * FOX_HEADS)
    selector = jnp.where(mine, 1.0, 0.0).astype(BF16)
    key = lax.broadcasted_iota(jnp.int32, (tk, tq), 0)
    qry = lax.broadcasted_iota(jnp.int32, (tk, tq), 1)
    exp2_scale = math.log2(math.e) / math.sqrt(HEAD_DIM)

    def scores(lo, k0):
        q_aug = jnp.concatenate([q_ref[lo:lo + tq, :], selector], axis=1)
        return _dot_nt(kaug_scr[k0:k0 + tk, :], q_aug)

    steps = [(qi * tq, kc * tk) for qi in range(seq // tq) for kc in range((qi + 1) * tq // tk)]
    pending = [scores(*st) for st in steps[:SCORE_LOOKAHEAD]]
    for idx, (lo, k0) in enumerate(steps):
        t = pending.pop(0)
        if idx + SCORE_LOOKAHEAD < len(steps):
            pending.append(scores(*steps[idx + SCORE_LOOKAHEAD]))
        if k0 + tk > lo:
            t = jnp.where(key + (k0 - lo) <= qry, t, -jnp.inf)
        cm = jnp.max(t, axis=0, keepdims=True)
        m_new = cm if k0 == 0 else jnp.maximum(m, cm)
        p = jnp.exp2((t - m_new) * exp2_scale).astype(BF16)
        pv = _dot(vaug_scr[:, k0:k0 + tk], p)
        if k0 == 0:
            acc = pv
        else:
            acc = acc * jnp.exp2((m - m_new) * exp2_scale) + pv
        m = m_new
        if k0 + tk == lo + tq:
            inv_l = 1.0 / acc[HEAD_DIM:HEAD_DIM + 1, :]
            o_ref[lo:lo + tq, :] = (acc[:HEAD_DIM, :] * inv_l).T


def _fox_attention(qk, v_t, f_t, b_f, *, batch, seq):
    nh = FOX_HEADS
    return pl.pallas_call(
        functools.partial(_fox_kernel, seq=seq),
        grid=(batch, nh),
        in_specs=[
            pl.BlockSpec((None, seq, HEAD_DIM), lambda b, h: (b, 0, h)),
            pl.BlockSpec((None, seq, HEAD_DIM), lambda b, h: (b, 0, nh + h)),
            pl.BlockSpec((HEAD_DIM, seq), lambda b, h: (h, b)),
            pl.BlockSpec((F_ROWS, seq), lambda b, h: (0, b)),
            pl.BlockSpec((F_ROWS, 1), lambda b, h: (0, 0)),
        ],
        out_specs=pl.BlockSpec((None, seq, HEAD_DIM), lambda b, h: (b, 0, h)),
        out_shape=jax.ShapeDtypeStruct((batch, seq, FOX_W), F32),
        scratch_shapes=[pltpu.VMEM((seq, 2 * HEAD_DIM), BF16),
                        pltpu.VMEM((HEAD_DIM + V_PAD_ROWS, seq), BF16)],
        compiler_params=_params("parallel", "arbitrary"),
    )(qk, qk, v_t, f_t, b_f)


def _lru_kernel(u_ref, gate_ref, cw_ref, cb_ref, wg_ref, bra_ref, bri_ref, lam_ref, o_ref,
                at_scr, bt_scr, hp_scr, *, seq):
    row = lax.broadcasted_iota(jnp.int32, (SUBLANES, LRU_BLOCK), 0)

    def shifted(d):
        head = jnp.where(row >= d, pltpu.roll(u_ref[0:SUBLANES, :], d, axis=0), 0.0)
        return jnp.concatenate([head, u_ref[SUBLANES - d:seq - d, :]], axis=0)

    cw = cw_ref[...]
    uc = cw[0:1, :] * shifted(3)
    uc = uc + cw[1:2, :] * shifted(2)
    uc = uc + cw[2:3, :] * shifted(1)
    uc = uc + cw[3:4, :] * u_ref[...]
    uc = cb_ref[...] + uc

    gates = _dot(uc.astype(BF16), wg_ref[...])
    half_c = (-0.5 * LRU_C) * _softplus(-lam_ref[...])
    log_a = half_c * jnp.tanh(0.5 * (gates[:, :LRU_BLOCK] + bra_ref[...])) + half_c
    i = 0.5 * jnp.tanh(0.5 * (gates[:, LRU_BLOCK:] + bri_ref[...])) + 0.5
    a = jnp.exp(log_a)
    t = jnp.tanh(log_a)
    w = -2.0 * t
    sqrt_w = jnp.where(w > 0.0, w * lax.rsqrt(w), 0.0)
    b = (sqrt_w * lax.rsqrt(1.0 - t)) * (i * uc)

    groups = seq // SUBLANES
    gshape = (groups, SUBLANES, LRU_BLOCK)
    a3 = a.reshape(gshape)
    b3 = b.reshape(gshape)
    sub = lax.broadcasted_iota(jnp.int32, gshape, 1)
    d = 1
    while d < SUBLANES:
        keep = sub >= d
        b3 = a3 * jnp.where(keep, pltpu.roll(b3, d, axis=1), 0.0) + b3
        a3 = a3 * jnp.where(keep, pltpu.roll(a3, d, axis=1), 1.0)
        d *= 2
    at_scr[...] = jnp.broadcast_to(a3[:, SUBLANES - 1:, :], gshape).reshape(seq, LRU_BLOCK)
    bt_scr[...] = jnp.broadcast_to(b3[:, SUBLANES - 1:, :], gshape).reshape(seq, LRU_BLOCK)

    def carry(g, h):
        rows = pl.ds(pl.multiple_of(g * SUBLANES, SUBLANES), SUBLANES)
        hp_scr[rows, :] = h
        return at_scr[rows, :] * h + bt_scr[rows, :]

    lax.fori_loop(0, groups, carry, jnp.zeros((SUBLANES, LRU_BLOCK), F32), unroll=8)
    hs = b3.reshape(seq, LRU_BLOCK) + a3.reshape(seq, LRU_BLOCK) * hp_scr[...]
    o_ref[...] = hs * jax.nn.gelu(gate_ref[...])


def _rg_lru(ug, cw, cb, wg, bra, bri, lam, *, batch, seq):
    nb = LRU_BLOCKS
    vec = lambda: pl.BlockSpec((1, LRU_BLOCK), lambda b, n: (0, n))
    return pl.pallas_call(
        functools.partial(_lru_kernel, seq=seq),
        grid=(batch, nb),
        in_specs=[
            pl.BlockSpec((None, seq, LRU_BLOCK), lambda b, n: (b, 0, n)),
            pl.BlockSpec((None, seq, LRU_BLOCK), lambda b, n: (b, 0, nb + n)),
            pl.BlockSpec((CONV_W, LRU_BLOCK), lambda b, n: (0, n)),
            vec(),
            pl.BlockSpec((None, LRU_BLOCK, 2 * LRU_BLOCK), lambda b, n: (n, 0, 0)),
            vec(), vec(), vec(),
        ],
        out_specs=pl.BlockSpec((None, seq, LRU_BLOCK), lambda b, n: (b, 0, n)),
        out_shape=jax.ShapeDtypeStruct((batch, seq, LRU_W), F32),
        scratch_shapes=[pltpu.VMEM((seq, LRU_BLOCK), F32)] * 3,
        compiler_params=_params("parallel", "arbitrary"),
    )(ug, ug, cw, cb, wg, bra, bri, lam)


def _out_kernel(of_ref, yl_ref, gf_ref, gl_ref, w_ref, x_ref, o_ref, w_scr, *, chunk):
    @pl.when(pl.program_id(0) == 0)
    def _():
        w_scr[...] = w_ref[...].astype(BF16)

    mf = _rms(of_ref[...], gf_ref[...]).astype(BF16)
    ml = _rms(yl_ref[...], gl_ref[...]).astype(BF16)
    for c in range(o_ref.shape[1] // chunk):
        sl = slice(c * chunk, (c + 1) * chunk)
        acc = _dot(mf, w_scr[:FOX_W, sl]) + _dot(ml, w_scr[FOX_W:, sl])
        o_ref[:, sl] = x_ref[:, sl] + acc


def _out_proj(o_fox, y_lru, gf, gl, w_out, layer, x2d, *, tm):
    T, D = x2d.shape
    return pl.pallas_call(
        functools.partial(_out_kernel, chunk=2 * PROJ_CHUNK),
        grid=(T // tm,),
        in_specs=[
            pl.BlockSpec((tm, FOX_W), lambda i: (i, 0)),
            pl.BlockSpec((tm, LRU_W), lambda i: (i, 0)),
            pl.BlockSpec((1, FOX_W), lambda i: (0, 0)),
            pl.BlockSpec((1, LRU_W), lambda i: (0, 0)),
            pl.BlockSpec((None, FOX_W + LRU_W, D), lambda i: (layer, 0, 0),
                         pipeline_mode=pl.Buffered(1)),
            pl.BlockSpec((tm, D), lambda i: (i, 0)),
        ],
        out_specs=pl.BlockSpec((tm, D), lambda i: (i, 0)),
        out_shape=jax.ShapeDtypeStruct((T, D), F32),
        scratch_shapes=[pltpu.VMEM((FOX_W + LRU_W, D), BF16)],
        compiler_params=_params("arbitrary"),
    )(o_fox, y_lru, gf, gl, w_out, x2d)


def _xattn_kernel(x_ref, g_ref, wq_ref, gq_ref, ckv_ref, wo_ref, o_ref):
    x = x_ref[...]
    cq = _dot(_rms(x, g_ref[...]).astype(BF16), wq_ref[...].astype(BF16))
    scale = 1.0 / math.sqrt(HEAD_DIM)
    scores = []
    for hh in range(XATT_HEADS):
        sl = slice(hh * HEAD_DIM, (hh + 1) * HEAD_DIM)
        qh = _rms(cq[:, sl], gq_ref[...]).astype(BF16)
        scores.append(_dot_nt(qh, ckv_ref[:, sl]))
    heads = []
    for hh in range(XATT_HEADS):
        vh = ckv_ref[:, XATT_W + hh * HEAD_DIM:XATT_W + (hh + 1) * HEAD_DIM]
        s = scores[hh] * scale
        e = jnp.exp(s - jnp.max(s, axis=-1, keepdims=True))
        l = jnp.sum(e, axis=-1, keepdims=True)
        heads.append((_dot(e.astype(BF16), vh) / l).astype(BF16))
    ox = jnp.concatenate(heads, axis=-1)
    o_ref[...] = x + _dot(ox, wo_ref[...].astype(BF16))


def _cross_attention(x3d, g, w_cq, gq, ckv, w_co, layer, *, tm):
    B, S, D = x3d.shape
    M = ckv.shape[1]
    return pl.pallas_call(
        _xattn_kernel,
        grid=(B, S // tm),
        in_specs=[
            pl.BlockSpec((None, tm, D), lambda b, i: (b, i, 0)),
            pl.BlockSpec((1, D), lambda b, i: (0, 0)),
            pl.BlockSpec((None, D, XATT_W), lambda b, i: (layer, 0, 0)),
            pl.BlockSpec((1, HEAD_DIM), lambda b, i: (0, 0)),
            pl.BlockSpec((None, M, 2 * XATT_W), lambda b, i: (b, 0, 0)),
            pl.BlockSpec((None, XATT_W, D), lambda b, i: (layer, 0, 0)),
        ],
        out_specs=pl.BlockSpec((None, tm, D), lambda b, i: (b, i, 0)),
        out_shape=jax.ShapeDtypeStruct((B, S, D), F32),
        compiler_params=_params("parallel", "parallel"),
    )(x3d, g, w_cq, gq, ckv, w_co)


def _ffn_kernel(x_ref, g_ref, wg_ref, wu_ref, wd_ref, o_ref, h_scr):
    @pl.when(pl.program_id(1) == 0)
    def _():
        x = x_ref[...]
        h_scr[...] = _rms(x, g_ref[...]).astype(BF16)
        o_ref[...] = x

    h = h_scr[...]
    gate = _dot(h, wg_ref[...].astype(BF16))
    up = _dot(h, wu_ref[...].astype(BF16))
    act = (jax.nn.silu(gate) * up).astype(BF16)
    o_ref[...] += _dot(act, wd_ref[...].astype(BF16))


def _ffn(x2d, g, w_gu, w_d, layer, *, tm, th):
    T, D = x2d.shape
    H = w_d.shape[1]
    nk = H // th
    return pl.pallas_call(
        _ffn_kernel,
        grid=(T // tm, nk),
        in_specs=[
            pl.BlockSpec((tm, D), lambda i, k: (jnp.minimum(i + (k > 0), T // tm - 1), 0)),
            pl.BlockSpec((1, D), lambda i, k: (0, 0)),
            pl.BlockSpec((None, D, th), lambda i, k: (layer, 0, k)),
            pl.BlockSpec((None, D, th), lambda i, k: (layer, 0, nk + k)),
            pl.BlockSpec((None, th, D), lambda i, k: (layer, k, 0)),
        ],
        out_specs=pl.BlockSpec((tm, D), lambda i, k: (i, 0)),
        out_shape=jax.ShapeDtypeStruct((T, D), F32),
        scratch_shapes=[pltpu.VMEM((tm, D), BF16)],
        compiler_params=_params("parallel", "arbitrary"),
    )(x2d, g, w_gu, w_gu, w_d)


def kernel(x, mem, g_mix, w_in, b_f, g_q, g_k, conv_w, conv_b, w_ra, b_ra, w_ri, b_ri, lam,
           g_fox_out, g_lru_out, w_out, g_xattn, g_mem, w_cq, w_ckv, g_cq, g_ck, w_co, g_ffn,
           w_gate_up, w_down):
    B, S, D = x.shape
    M = mem.shape[1]
    T = B * S
    depth = g_mix.shape[0]
    row = lambda v: v.reshape(1, -1).astype(F32)

    for l in range(depth):
        w_t, w_f = _stage_in_weights(jnp.swapaxes(w_in, 1, 2), l, tr=512)
        g_qk = jnp.stack([g_q[l], g_k[l]]).reshape(2, 1, HEAD_DIM).astype(F32)
        bf_col = jnp.zeros((F_ROWS, 1), F32).at[:FOX_HEADS, 0].set(b_f[l])
        w_gates = jnp.concatenate([w_ra[l], w_ri[l]], axis=-1).astype(BF16)

        x2d = x.reshape(T, D)

        qk, v_t, ug, f_t = _in_proj(x2d, row(g_mix[l]), w_t, g_qk, w_f, tm=1024, tn=1024)
        o_fox = _fox_attention(qk.reshape(B, S, 2 * FOX_W), v_t, f_t, bf_col, batch=B, seq=S)
        y_lru = _rg_lru(ug.reshape(B, S, 2 * LRU_W), conv_w[l], row(conv_b[l]), w_gates,
                        row(b_ra[l]), row(b_ri[l]), row(lam[l]), batch=B, seq=S)
        x2d = _out_proj(o_fox.reshape(T, FOX_W), y_lru.reshape(T, LRU_W),
                        row(g_fox_out[l]), row(g_lru_out[l]), w_out, l, x2d, tm=512)

        ckv = _ckv_proj(mem.reshape(B * M, D), row(g_mem[l]), w_ckv, l, row(g_ck[l]))
        x3d = _cross_attention(x2d.reshape(B, S, D), row(g_xattn[l]), w_cq, row(g_cq[l]),
                               ckv.reshape(B, M, 2 * XATT_W), w_co, l, tm=512)

        x2d = _ffn(x3d.reshape(T, D), row(g_ffn[l]), w_gate_up, w_down, l, tm=1024, th=256)
        x = x2d.reshape(B, S, D)
    return x
```

```python
import functools
import math

import jax
import jax.numpy as jnp
from jax import lax
from jax.experimental import pallas as pl
from jax.experimental.pallas import tpu as pltpu

F32 = jnp.float32
BF16 = jnp.bfloat16

HEAD_DIM = 128
FOX_HEADS = 8
FOX_W = FOX_HEADS * HEAD_DIM
LRU_BLOCKS = 8
LRU_BLOCK = 128
LRU_W = LRU_BLOCKS * LRU_BLOCK
LRU_C = 8.0
CONV_W = 4
XATT_HEADS = 4
XATT_W = XATT_HEADS * HEAD_DIM
RMS_EPS = 1e-6
F_ROWS = 16
Q_BLOCK = 512
K_CHUNK = 512
SCORE_LOOKAHEAD = 3
V_PAD_ROWS = 16
PROJ_CHUNK = 256
SUBLANES = 8
FFN_TILE = 512
SIDE_CAST_SLABS = 32
BIAS_PIECES = 3

V7X_VMEM_LIMIT_BYTES = 60 * 1024 * 1024


def _rms(x, g):
    ms = jnp.mean(x * x, axis=-1, keepdims=True)
    return x * lax.rsqrt(ms + RMS_EPS) * g


def _dot(a, b):
    return jnp.dot(a, b, preferred_element_type=F32)


def _dot_nt(a, b):
    return lax.dot_general(a, b, (((1,), (1,)), ((), ())), preferred_element_type=F32)


def _softplus(x):
    return jnp.maximum(x, 0.0) + jnp.log1p(jnp.exp(-jnp.abs(x)))


def _cast_slab(step, n_slabs, src_ref, dst_ref):
    @pl.when(step < n_slabs)
    def _():
        n_tiles, _, tc = dst_ref.shape
        for t in range(n_tiles):
            dst_ref[t] = src_ref[:, t * tc:(t + 1) * tc].astype(BF16)


def _slab_specs(w, layer, n_slabs, tc, step_of):
    _, R, C = w.shape
    rows = R // n_slabs
    slab = lambda *ids: jnp.minimum(step_of(*ids), n_slabs - 1)
    return (pl.BlockSpec((None, rows, C), lambda *ids: (layer, slab(*ids), 0)),
            pl.BlockSpec((C // tc, rows, tc), lambda *ids: (0, slab(*ids), 0)),
            jax.ShapeDtypeStruct((C // tc, R, tc), BF16))


def _params(*sem):
    return pltpu.CompilerParams(dimension_semantics=sem,
                                vmem_limit_bytes=V7X_VMEM_LIMIT_BYTES)


def _wstage_kernel(a_ref, b_ref, w_ref, wf_ref, *, f_tile):
    t = pl.program_id(0)

    @pl.when(t < f_tile)
    def _():
        w_ref[...] = a_ref[...].astype(BF16)

    @pl.when(t >= f_tile)
    def _():
        w_ref[...] = jnp.concatenate([a_ref[FOX_HEADS:, :], b_ref[:FOX_HEADS, :]],
                                     axis=0).astype(BF16)

    @pl.when(t == f_tile)
    def _():
        wf_ref[:FOX_HEADS, :] = a_ref[:FOX_HEADS, :].astype(BF16)
        wf_ref[FOX_HEADS:, :] = jnp.zeros((F_ROWS - FOX_HEADS, wf_ref.shape[1]), BF16)


def _stage_in_weights(w_t, layer, *, tr):
    _, N, D = w_t.shape
    n_rows = N - FOX_HEADS
    f_tile = 3 * FOX_W // tr
    return pl.pallas_call(
        functools.partial(_wstage_kernel, f_tile=f_tile),
        grid=(n_rows // tr,),
        in_specs=[pl.BlockSpec((None, tr, D), lambda t: (layer, t, 0)),
                  pl.BlockSpec((None, tr, D), lambda t: (layer, jnp.maximum(t, f_tile) + 1, 0))],
        out_specs=[pl.BlockSpec((tr, D), lambda t: (t, 0)),
                   pl.BlockSpec((F_ROWS, D), lambda t: (0, 0))],
        out_shape=[jax.ShapeDtypeStruct((n_rows, D), BF16),
                   jax.ShapeDtypeStruct((F_ROWS, D), BF16)],
        compiler_params=_params("arbitrary"),
    )(w_t, w_t)


def _inproj_kernel(x_ref, g_ref, w_ref, gh_ref, wf_ref, side_ref,
                   qk_ref, vt_ref, ug_ref, f_ref, side_out_ref, h_scr,
                   *, tn, n_qk, n_v, n_steps, n_slabs):
    j = pl.program_id(1)
    _cast_slab(pl.program_id(0) * n_steps + j, n_slabs, side_ref, side_out_ref)

    @pl.when(j == 0)
    def _():
        h = _rms(x_ref[...], g_ref[...]).astype(BF16)
        h_scr[...] = h
        f_ref[...] = _dot_nt(wf_ref[...], h)

    chunks = [slice(c * PROJ_CHUNK, (c + 1) * PROJ_CHUNK) for c in range(tn // PROJ_CHUNK)]

    @pl.when(j < n_qk)
    def _():
        for sl in chunks:
            acc = _dot_nt(h_scr[...], w_ref[sl, :])
            for hh in range(PROJ_CHUNK // HEAD_DIM):
                lo = sl.start + hh * HEAD_DIM
                qk_ref[:, lo:lo + HEAD_DIM] = _rms(
                    acc[:, hh * HEAD_DIM:(hh + 1) * HEAD_DIM], gh_ref[...]).astype(BF16)

    @pl.when((j >= n_qk) & (j < n_qk + n_v))
    def _():
        for sl in chunks:
            vt_ref[sl, :] = _dot_nt(w_ref[sl, :], h_scr[...]).astype(BF16)

    @pl.when(j >= n_qk + n_v)
    def _():
        for sl in chunks:
            ug_ref[:, sl] = _dot_nt(h_scr[...], w_ref[sl, :])


def _in_proj(x2d, g, w_t, g_qk, wf, w_side, layer, *, tm, tn, n_slabs, side_tc):
    T, D = x2d.shape
    n_qk = 2 * FOX_W // tn
    n_v = FOX_W // tn
    n_ug = 2 * LRU_W // tn
    tiles_per_gain = FOX_W // tn
    clip = lambda v, n: jnp.clip(v, 0, n - 1)
    n_steps = n_qk + n_v + n_ug
    side_in, side_out, side_shape = _slab_specs(w_side, layer, n_slabs, side_tc,
                                                lambda i, j: i * n_steps + j)
    return pl.pallas_call(
        functools.partial(_inproj_kernel, tn=tn, n_qk=n_qk, n_v=n_v, n_steps=n_steps,
                          n_slabs=n_slabs),
        grid=(T // tm, n_steps),
        in_specs=[
            pl.BlockSpec((tm, D), lambda i, j: (jnp.minimum(i + (j > 0), T // tm - 1), 0)),
            pl.BlockSpec((1, D), lambda i, j: (0, 0)),
            pl.BlockSpec((tn, D), lambda i, j: (j, 0)),
            pl.BlockSpec((None, 1, HEAD_DIM), lambda i, j: (clip(j // tiles_per_gain, 2), 0, 0)),
            pl.BlockSpec((F_ROWS, D), lambda i, j: (0, 0)),
            side_in,
        ],
        out_specs=[
            pl.BlockSpec((tm, tn), lambda i, j: (i, clip(j, n_qk))),
            pl.BlockSpec((tn, tm), lambda i, j: (clip(j - n_qk, n_v), i)),
            pl.BlockSpec((tm, tn), lambda i, j: (i, clip(j - n_qk - n_v, n_ug))),
            pl.BlockSpec((F_ROWS, tm), lambda i, j: (0, i)),
            side_out,
        ],
        out_shape=[
            jax.ShapeDtypeStruct((T, 2 * FOX_W), BF16),
            jax.ShapeDtypeStruct((FOX_W, T), BF16),
            jax.ShapeDtypeStruct((T, 2 * LRU_W), F32),
            jax.ShapeDtypeStruct((F_ROWS, T), F32),
            side_shape,
        ],
        scratch_shapes=[pltpu.VMEM((tm, D), BF16)],
        compiler_params=_params("arbitrary", "arbitrary"),
    )(x2d, g, w_t, g_qk, wf, w_side)


def _ckv_kernel(m_ref, g_ref, w_ref, gh_ref, o_ref):
    h = _rms(m_ref[...], g_ref[...]).astype(BF16)
    acc = _dot(h, w_ref[...].astype(BF16))
    for hh in range(XATT_HEADS):
        sl = slice(hh * HEAD_DIM, (hh + 1) * HEAD_DIM)
        o_ref[:, sl] = _rms(acc[:, sl], gh_ref[...]).astype(BF16)
    o_ref[:, XATT_W:] = acc[:, XATT_W:].astype(BF16)


def _ckv_proj(mem2d, g, w_ckv, layer, gh):
    R, D = mem2d.shape
    full = lambda shape: pl.BlockSpec(shape, lambda i: (0,) * len(shape))
    return pl.pallas_call(
        _ckv_kernel,
        grid=(1,),
        in_specs=[full((R, D)), full((1, D)),
                  pl.BlockSpec((None, D, 2 * XATT_W), lambda i: (layer, 0, 0)),
                  full((1, HEAD_DIM))],
        out_specs=full((R, 2 * XATT_W)),
        out_shape=jax.ShapeDtypeStruct((R, 2 * XATT_W), BF16),
        compiler_params=_params("arbitrary"),
    )(mem2d, g, w_ckv, gh)


def _fox_kernel(q_ref, k_ref, vt_ref, f_ref, bf_ref, side_ref, o_ref, side_out_ref,
                kaug_scr, vaug_scr, *, seq, n_slabs):
    h = pl.program_id(1)
    _cast_slab(pl.program_id(0) * FOX_HEADS + h, n_slabs, side_ref, side_out_ref)

    @pl.when(h == 0)
    def _():
        z = f_ref[...] + bf_ref[...]
        c = jnp.minimum(z, 0.0) - jnp.log1p(jnp.exp(-jnp.abs(z)))
        lane = lax.broadcasted_iota(jnp.int32, c.shape, 1)
        d = 1
        while d < seq:
            c = c + jnp.where(lane >= d, pltpu.roll(c, d, axis=1), 0.0)
            d *= 2
        bias = c[:FOX_HEADS] * (-math.sqrt(HEAD_DIM))
        pieces = []
        for _ in range(BIAS_PIECES):
            p = bias.astype(BF16).astype(F32)
            pieces.append(p)
            bias = bias - p
        pad = jnp.zeros((HEAD_DIM - BIAS_PIECES * FOX_HEADS, seq), F32)
        kaug_scr[:, HEAD_DIM:] = jnp.concatenate(pieces + [pad], axis=0).T.astype(BF16)

    kaug_scr[:, :HEAD_DIM] = k_ref[...]
    vaug_scr[:HEAD_DIM, :] = vt_ref[...]
    ones_row = lax.broadcasted_iota(jnp.int32, (V_PAD_ROWS, seq), 0) == 0
    vaug_scr[HEAD_DIM:, :] = jnp.where(ones_row, 1.0, 0.0).astype(BF16)

    tq, tk = Q_BLOCK, K_CHUNK
    lane = lax.broadcasted_iota(jnp.int32, (tq, HEAD_DIM), 1)
    mine = lane == h
    for p in range(1, BIAS_PIECES):
        mine = mine | (lane == h + p * FOX_HEADS)
    selector = jnp.where(mine, 1.0, 0.0).astype(BF16)
    key = lax.broadcasted_iota(jnp.int32, (tk, tq), 0)
    qry = lax.broadcasted_iota(jnp.int32, (tk, tq), 1)
    exp2_scale = math.log2(math.e) / math.sqrt(HEAD_DIM)

    def scores(lo, k0):
        q_aug = jnp.concatenate([q_ref[lo:lo + tq, :], selector], axis=1)
        return _dot_nt(kaug_scr[k0:k0 + tk, :], q_aug)

    steps = [(qi * tq, kc * tk) for qi in range(seq // tq) for kc in range((qi + 1) * tq // tk)]
    pending = [scores(*st) for st in steps[:SCORE_LOOKAHEAD]]
    for idx, (lo, k0) in enumerate(steps):
        t = pending.pop(0)
        if idx + SCORE_LOOKAHEAD < len(steps):
            pending.append(scores(*steps[idx + SCORE_LOOKAHEAD]))
        if k0 + tk > lo:
            t = jnp.where(key + (k0 - lo) <= qry, t, -jnp.inf)
        cm = jnp.max(t, axis=0, keepdims=True)
        m_new = cm if k0 == 0 else jnp.maximum(m, cm)
        p = jnp.exp2((t - m_new) * exp2_scale).astype(BF16)
        pv = _dot(vaug_scr[:, k0:k0 + tk], p)
        if k0 == 0:
            acc = pv
        else:
            acc = acc * jnp.exp2((m - m_new) * exp2_scale) + pv
        m = m_new
        if k0 + tk == lo + tq:
            inv_l = 1.0 / acc[HEAD_DIM:HEAD_DIM + 1, :]
            o_ref[lo:lo + tq, :] = (acc[:HEAD_DIM, :] * inv_l).T


def _fox_attention(qk, v_t, f_t, b_f, w_side, layer, *, batch, seq, n_slabs, side_tc):
    nh = FOX_HEADS
    side_in, side_out, side_shape = _slab_specs(w_side, layer, n_slabs, side_tc,
                                                lambda b, h: b * nh + h)
    return pl.pallas_call(
        functools.partial(_fox_kernel, seq=seq, n_slabs=n_slabs),
        grid=(batch, nh),
        in_specs=[
            pl.BlockSpec((None, seq, HEAD_DIM), lambda b, h: (b, 0, h)),
            pl.BlockSpec((None, seq, HEAD_DIM), lambda b, h: (b, 0, nh + h)),
            pl.BlockSpec((HEAD_DIM, seq), lambda b, h: (h, b)),
            pl.BlockSpec((F_ROWS, seq), lambda b, h: (0, b)),
            pl.BlockSpec((F_ROWS, 1), lambda b, h: (0, 0)),
            side_in,
        ],
        out_specs=[pl.BlockSpec((None, seq, HEAD_DIM), lambda b, h: (b, 0, h)), side_out],
        out_shape=[jax.ShapeDtypeStruct((batch, seq, FOX_W), F32), side_shape],
        scratch_shapes=[pltpu.VMEM((seq, 2 * HEAD_DIM), BF16),
                        pltpu.VMEM((HEAD_DIM + V_PAD_ROWS, seq), BF16)],
        compiler_params=_params("arbitrary", "arbitrary"),
    )(qk, qk, v_t, f_t, b_f, w_side)


def _lru_kernel(u_ref, gate_ref, cw_ref, cb_ref, wg_ref, bra_ref, bri_ref, lam_ref, o_ref,
                at_scr, bt_scr, hp_scr, *, seq):
    row = lax.broadcasted_iota(jnp.int32, (SUBLANES, LRU_BLOCK), 0)

    def shifted(d):
        head = jnp.where(row >= d, pltpu.roll(u_ref[0:SUBLANES, :], d, axis=0), 0.0)
        return jnp.concatenate([head, u_ref[SUBLANES - d:seq - d, :]], axis=0)

    cw = cw_ref[...]
    uc = cw[0:1, :] * shifted(3)
    uc = uc + cw[1:2, :] * shifted(2)
    uc = uc + cw[2:3, :] * shifted(1)
    uc = uc + cw[3:4, :] * u_ref[...]
    uc = cb_ref[...] + uc

    gates = _dot(uc.astype(BF16), wg_ref[...])
    half_c = (-0.5 * LRU_C) * _softplus(-lam_ref[...])
    log_a = half_c * jnp.tanh(0.5 * (gates[:, :LRU_BLOCK] + bra_ref[...])) + half_c
    i = 0.5 * jnp.tanh(0.5 * (gates[:, LRU_BLOCK:] + bri_ref[...])) + 0.5
    a = jnp.exp(log_a)
    t = jnp.tanh(log_a)
    w = -2.0 * t
    sqrt_w = jnp.where(w > 0.0, w * lax.rsqrt(w), 0.0)
    b = (sqrt_w * lax.rsqrt(1.0 - t)) * (i * uc)

    groups = seq // SUBLANES
    gshape = (groups, SUBLANES, LRU_BLOCK)
    a3 = a.reshape(gshape)
    b3 = b.reshape(gshape)
    sub = lax.broadcasted_iota(jnp.int32, gshape, 1)
    d = 1
    while d < SUBLANES:
        keep = sub >= d
        b3 = a3 * jnp.where(keep, pltpu.roll(b3, d, axis=1), 0.0) + b3
        a3 = a3 * jnp.where(keep, pltpu.roll(a3, d, axis=1), 1.0)
        d *= 2
    at_scr[...] = jnp.broadcast_to(a3[:, SUBLANES - 1:, :], gshape).reshape(seq, LRU_BLOCK)
    bt_scr[...] = jnp.broadcast_to(b3[:, SUBLANES - 1:, :], gshape).reshape(seq, LRU_BLOCK)

    def carry(g, h):
        rows = pl.ds(pl.multiple_of(g * SUBLANES, SUBLANES), SUBLANES)
        hp_scr[rows, :] = h
        return at_scr[rows, :] * h + bt_scr[rows, :]

    lax.fori_loop(0, groups, carry, jnp.zeros((SUBLANES, LRU_BLOCK), F32), unroll=8)
    hs = b3.reshape(seq, LRU_BLOCK) + a3.reshape(seq, LRU_BLOCK) * hp_scr[...]
    o_ref[...] = hs * jax.nn.gelu(gate_ref[...])


def _rg_lru(ug, cw, cb, wg, bra, bri, lam, *, batch, seq):
    nb = LRU_BLOCKS
    vec = lambda: pl.BlockSpec((1, LRU_BLOCK), lambda b, n: (0, n))
    return pl.pallas_call(
        functools.partial(_lru_kernel, seq=seq),
        grid=(batch, nb),
        in_specs=[
            pl.BlockSpec((None, seq, LRU_BLOCK), lambda b, n: (b, 0, n)),
            pl.BlockSpec((None, seq, LRU_BLOCK), lambda b, n: (b, 0, nb + n)),
            pl.BlockSpec((CONV_W, LRU_BLOCK), lambda b, n: (0, n)),
            vec(),
            pl.BlockSpec((None, LRU_BLOCK, 2 * LRU_BLOCK), lambda b, n: (n, 0, 0)),
            vec(), vec(), vec(),
        ],
        out_specs=pl.BlockSpec((None, seq, LRU_BLOCK), lambda b, n: (b, 0, n)),
        out_shape=jax.ShapeDtypeStruct((batch, seq, LRU_W), F32),
        scratch_shapes=[pltpu.VMEM((seq, LRU_BLOCK), F32)] * 3,
        compiler_params=_params("parallel", "arbitrary"),
    )(ug, ug, cw, cb, wg, bra, bri, lam)


def _out_kernel(of_ref, yl_ref, gf_ref, gl_ref, w_ref, x_ref, o_ref, w_scr, *, chunk):
    @pl.when(pl.program_id(0) == 0)
    def _():
        w_scr[...] = w_ref[...].astype(BF16)

    mf = _rms(of_ref[...], gf_ref[...]).astype(BF16)
    ml = _rms(yl_ref[...], gl_ref[...]).astype(BF16)
    for c in range(o_ref.shape[1] // chunk):
        sl = slice(c * chunk, (c + 1) * chunk)
        acc = _dot(mf, w_scr[:FOX_W, sl]) + _dot(ml, w_scr[FOX_W:, sl])
        o_ref[:, sl] = x_ref[:, sl] + acc


def _out_proj(o_fox, y_lru, gf, gl, w_out, layer, x2d, *, tm):
    T, D = x2d.shape
    return pl.pallas_call(
        functools.partial(_out_kernel, chunk=2 * PROJ_CHUNK),
        grid=(T // tm,),
        in_specs=[
            pl.BlockSpec((tm, FOX_W), lambda i: (i, 0)),
            pl.BlockSpec((tm, LRU_W), lambda i: (i, 0)),
            pl.BlockSpec((1, FOX_W), lambda i: (0, 0)),
            pl.BlockSpec((1, LRU_W), lambda i: (0, 0)),
            pl.BlockSpec((None, FOX_W + LRU_W, D), lambda i: (layer, 0, 0),
                         pipeline_mode=pl.Buffered(1)),
            pl.BlockSpec((tm, D), lambda i: (i, 0)),
        ],
        out_specs=pl.BlockSpec((tm, D), lambda i: (i, 0)),
        out_shape=jax.ShapeDtypeStruct((T, D), F32),
        scratch_shapes=[pltpu.VMEM((FOX_W + LRU_W, D), BF16)],
        compiler_params=_params("arbitrary"),
    )(o_fox, y_lru, gf, gl, w_out, x2d)


def _xattn_kernel(x_ref, g_ref, wq_ref, gq_ref, ckv_ref, wo_ref, o_ref):
    x = x_ref[...]
    cq = _dot(_rms(x, g_ref[...]).astype(BF16), wq_ref[...].astype(BF16))
    scale = 1.0 / math.sqrt(HEAD_DIM)
    scores = []
    for hh in range(XATT_HEADS):
        sl = slice(hh * HEAD_DIM, (hh + 1) * HEAD_DIM)
        qh = _rms(cq[:, sl], gq_ref[...]).astype(BF16)
        scores.append(_dot_nt(qh, ckv_ref[:, sl]))
    heads = []
    for hh in range(XATT_HEADS):
        vh = ckv_ref[:, XATT_W + hh * HEAD_DIM:XATT_W + (hh + 1) * HEAD_DIM]
        s = scores[hh] * scale
        e = jnp.exp(s - jnp.max(s, axis=-1, keepdims=True))
        l = jnp.sum(e, axis=-1, keepdims=True)
        heads.append((_dot(e.astype(BF16), vh) / l).astype(BF16))
    ox = jnp.concatenate(heads, axis=-1)
    o_ref[...] = x + _dot(ox, wo_ref[...].astype(BF16))


def _cross_attention(x3d, g, w_cq, gq, ckv, w_co, layer, *, tm):
    B, S, D = x3d.shape
    M = ckv.shape[1]
    return pl.pallas_call(
        _xattn_kernel,
        grid=(B, S // tm),
        in_specs=[
            pl.BlockSpec((None, tm, D), lambda b, i: (b, i, 0)),
            pl.BlockSpec((1, D), lambda b, i: (0, 0)),
            pl.BlockSpec((None, D, XATT_W), lambda b, i: (layer, 0, 0)),
            pl.BlockSpec((1, HEAD_DIM), lambda b, i: (0, 0)),
            pl.BlockSpec((None, M, 2 * XATT_W), lambda b, i: (b, 0, 0)),
            pl.BlockSpec((None, XATT_W, D), lambda b, i: (layer, 0, 0)),
        ],
        out_specs=pl.BlockSpec((None, tm, D), lambda b, i: (b, i, 0)),
        out_shape=jax.ShapeDtypeStruct((B, S, D), F32),
        compiler_params=_params("parallel", "parallel"),
    )(x3d, g, w_cq, gq, ckv, w_co)


def _ffn_kernel(x_ref, g_ref, wg_ref, wu_ref, wd_ref, o_ref, h_scr):
    @pl.when(pl.program_id(1) == 0)
    def _():
        x = x_ref[...]
        h_scr[...] = _rms(x, g_ref[...]).astype(BF16)
        o_ref[...] = x

    h = h_scr[...]
    gate = _dot(h, wg_ref[...])
    up = _dot(h, wu_ref[...])
    act = (jax.nn.silu(gate) * up).astype(BF16)
    o_ref[...] += _dot(act, wd_ref[...])


def _ffn(x2d, g, w_gu, w_d, *, tm):
    T, D = x2d.shape
    H = w_d.shape[0]
    th = w_gu.shape[2]
    nk = H // th
    return pl.pallas_call(
        _ffn_kernel,
        grid=(T // tm, nk),
        in_specs=[
            pl.BlockSpec((tm, D), lambda i, k: (jnp.minimum(i + (k > 0), T // tm - 1), 0)),
            pl.BlockSpec((1, D), lambda i, k: (0, 0)),
            pl.BlockSpec((None, D, th), lambda i, k: (k, 0, 0)),
            pl.BlockSpec((None, D, th), lambda i, k: (nk + k, 0, 0)),
            pl.BlockSpec((th, D), lambda i, k: (k, 0)),
        ],
        out_specs=pl.BlockSpec((tm, D), lambda i, k: (i, 0)),
        out_shape=jax.ShapeDtypeStruct((T, D), F32),
        scratch_shapes=[pltpu.VMEM((tm, D), BF16)],
        compiler_params=_params("parallel", "arbitrary"),
    )(x2d, g, w_gu, w_gu, w_d)


def kernel(x, mem, g_mix, w_in, b_f, g_q, g_k, conv_w, conv_b, w_ra, b_ra, w_ri, b_ri, lam,
           g_fox_out, g_lru_out, w_out, g_xattn, g_mem, w_cq, w_ckv, g_cq, g_ck, w_co, g_ffn,
           w_gate_up, w_down):
    B, S, D = x.shape
    M = mem.shape[1]
    T = B * S
    depth = g_mix.shape[0]
    row = lambda v: v.reshape(1, -1).astype(F32)

    for l in range(depth):
        w_t, w_f = _stage_in_weights(jnp.swapaxes(w_in, 1, 2), l, tr=512)
        g_qk = jnp.stack([g_q[l], g_k[l]]).reshape(2, 1, HEAD_DIM).astype(F32)
        bf_col = jnp.zeros((F_ROWS, 1), F32).at[:FOX_HEADS, 0].set(b_f[l])
        w_gates = jnp.concatenate([w_ra[l], w_ri[l]], axis=-1).astype(BF16)

        x2d = x.reshape(T, D)

        qk, v_t, ug, f_t, w_down_bf = _in_proj(x2d, row(g_mix[l]), w_t, g_qk, w_f, w_down, l,
                                               tm=1024, tn=1024, n_slabs=SIDE_CAST_SLABS,
                                               side_tc=D)
        o_fox, w_gu_bf = _fox_attention(qk.reshape(B, S, 2 * FOX_W), v_t, f_t, bf_col,
                                        w_gate_up, l, batch=B, seq=S, n_slabs=SIDE_CAST_SLABS,
                                        side_tc=FFN_TILE)
        y_lru = _rg_lru(ug.reshape(B, S, 2 * LRU_W), conv_w[l], row(conv_b[l]), w_gates,
                        row(b_ra[l]), row(b_ri[l]), row(lam[l]), batch=B, seq=S)
        x2d = _out_proj(o_fox.reshape(T, FOX_W), y_lru.reshape(T, LRU_W),
                        row(g_fox_out[l]), row(g_lru_out[l]), w_out, l, x2d, tm=512)

        ckv = _ckv_proj(mem.reshape(B * M, D), row(g_mem[l]), w_ckv, l, row(g_ck[l]))
        x3d = _cross_attention(x2d.reshape(B, S, D), row(g_xattn[l]), w_cq, row(g_cq[l]),
                               ckv.reshape(B, M, 2 * XATT_W), w_co, l, tm=512)

        x2d = _ffn(x3d.reshape(T, D), row(g_ffn[l]), w_gu_bf, w_down_bf[0], tm=1024)
        x = x2d.reshape(B, S, D)
    return x
```

```python
import functools
import math

import jax
import jax.numpy as jnp
from jax import lax
from jax.experimental import pallas as pl
from jax.experimental.pallas import tpu as pltpu

F32 = jnp.float32
BF16 = jnp.bfloat16

HEAD_DIM = 128
FOX_HEADS = 8
FOX_W = FOX_HEADS * HEAD_DIM
LRU_BLOCKS = 8
LRU_BLOCK = 128
LRU_W = LRU_BLOCKS * LRU_BLOCK
LRU_C = 8.0
CONV_W = 4
XATT_HEADS = 4
XATT_W = XATT_HEADS * HEAD_DIM
RMS_EPS = 1e-6
F_ROWS = 16
Q_BLOCK = 512
K_CHUNK = 512
HEADS_PER_STEP = 2
SCORE_LOOKAHEAD = 4
V_PAD_ROWS = 16
PROJ_CHUNK = 256
SUBLANES = 8
FFN_TILE = 512
SIDE_CAST_SLABS = 32
BIAS_PIECES = 3

V7X_VMEM_LIMIT_BYTES = 60 * 1024 * 1024


def _rms(x, g):
    ms = jnp.mean(x * x, axis=-1, keepdims=True)
    return x * lax.rsqrt(ms + RMS_EPS) * g


def _dot(a, b):
    return jnp.dot(a, b, preferred_element_type=F32)


def _dot_nt(a, b):
    return lax.dot_general(a, b, (((1,), (1,)), ((), ())), preferred_element_type=F32)


def _softplus(x):
    return jnp.maximum(x, 0.0) + jnp.log1p(jnp.exp(-jnp.abs(x)))


def _cast_slab(step, n_slabs, src_ref, dst_ref):
    @pl.when(step < n_slabs)
    def _():
        n_tiles, _, tc = dst_ref.shape
        for t in range(n_tiles):
            dst_ref[t] = src_ref[:, t * tc:(t + 1) * tc].astype(BF16)


def _slab_specs(w, layer, n_slabs, tc, step_of):
    _, R, C = w.shape
    rows = R // n_slabs
    slab = lambda *ids: jnp.minimum(step_of(*ids), n_slabs - 1)
    return (pl.BlockSpec((None, rows, C), lambda *ids: (layer, slab(*ids), 0)),
            pl.BlockSpec((C // tc, rows, tc), lambda *ids: (0, slab(*ids), 0)),
            jax.ShapeDtypeStruct((C // tc, R, tc), BF16))


def _params(*sem):
    return pltpu.CompilerParams(dimension_semantics=sem,
                                vmem_limit_bytes=V7X_VMEM_LIMIT_BYTES)


def _wstage_kernel(a_ref, b_ref, w_ref, wf_ref, *, f_tile):
    t = pl.program_id(0)

    @pl.when(t < f_tile)
    def _():
        w_ref[...] = a_ref[...].astype(BF16)

    @pl.when(t >= f_tile)
    def _():
        w_ref[...] = jnp.concatenate([a_ref[FOX_HEADS:, :], b_ref[:FOX_HEADS, :]],
                                     axis=0).astype(BF16)

    @pl.when(t == f_tile)
    def _():
        wf_ref[:FOX_HEADS, :] = a_ref[:FOX_HEADS, :].astype(BF16)
        wf_ref[FOX_HEADS:, :] = jnp.zeros((F_ROWS - FOX_HEADS, wf_ref.shape[1]), BF16)


def _stage_in_weights(w_t, layer, *, tr):
    _, N, D = w_t.shape
    n_rows = N - FOX_HEADS
    f_tile = 3 * FOX_W // tr
    return pl.pallas_call(
        functools.partial(_wstage_kernel, f_tile=f_tile),
        grid=(n_rows // tr,),
        in_specs=[pl.BlockSpec((None, tr, D), lambda t: (layer, t, 0)),
                  pl.BlockSpec((None, tr, D), lambda t: (layer, jnp.maximum(t, f_tile) + 1, 0))],
        out_specs=[pl.BlockSpec((tr, D), lambda t: (t, 0)),
                   pl.BlockSpec((F_ROWS, D), lambda t: (0, 0))],
        out_shape=[jax.ShapeDtypeStruct((n_rows, D), BF16),
                   jax.ShapeDtypeStruct((F_ROWS, D), BF16)],
        compiler_params=_params("arbitrary"),
    )(w_t, w_t)


def _inproj_kernel(x_ref, g_ref, w_ref, gh_ref, wf_ref, side_ref,
                   qk_ref, vt_ref, ug_ref, f_ref, side_out_ref, h_scr,
                   *, tn, n_qk, n_v, n_steps, n_slabs):
    j = pl.program_id(1)
    _cast_slab(pl.program_id(0) * n_steps + j, n_slabs, side_ref, side_out_ref)

    @pl.when(j == 0)
    def _():
        h = _rms(x_ref[...], g_ref[...]).astype(BF16)
        h_scr[...] = h
        f_ref[...] = _dot_nt(wf_ref[...], h)

    chunks = [slice(c * PROJ_CHUNK, (c + 1) * PROJ_CHUNK) for c in range(tn // PROJ_CHUNK)]

    @pl.when(j < n_qk)
    def _():
        for sl in chunks:
            acc = _dot_nt(h_scr[...], w_ref[sl, :])
            for hh in range(PROJ_CHUNK // HEAD_DIM):
                lo = sl.start + hh * HEAD_DIM
                qk_ref[:, lo:lo + HEAD_DIM] = _rms(
                    acc[:, hh * HEAD_DIM:(hh + 1) * HEAD_DIM], gh_ref[...]).astype(BF16)

    @pl.when((j >= n_qk) & (j < n_qk + n_v))
    def _():
        for sl in chunks:
            vt_ref[sl, :] = _dot_nt(w_ref[sl, :], h_scr[...]).astype(BF16)

    @pl.when(j >= n_qk + n_v)
    def _():
        for sl in chunks:
            ug_ref[:, sl] = _dot_nt(h_scr[...], w_ref[sl, :])


def _in_proj(x2d, g, w_t, g_qk, wf, w_side, layer, *, tm, tn, n_slabs, side_tc):
    T, D = x2d.shape
    n_qk = 2 * FOX_W // tn
    n_v = FOX_W // tn
    n_ug = 2 * LRU_W // tn
    tiles_per_gain = FOX_W // tn
    clip = lambda v, n: jnp.clip(v, 0, n - 1)
    n_steps = n_qk + n_v + n_ug
    side_in, side_out, side_shape = _slab_specs(w_side, layer, n_slabs, side_tc,
                                                lambda i, j: i * n_steps + j)
    return pl.pallas_call(
        functools.partial(_inproj_kernel, tn=tn, n_qk=n_qk, n_v=n_v, n_steps=n_steps,
                          n_slabs=n_slabs),
        grid=(T // tm, n_steps),
        in_specs=[
            pl.BlockSpec((tm, D), lambda i, j: (jnp.minimum(i + (j > 0), T // tm - 1), 0)),
            pl.BlockSpec((1, D), lambda i, j: (0, 0)),
            pl.BlockSpec((tn, D), lambda i, j: (j, 0)),
            pl.BlockSpec((None, 1, HEAD_DIM), lambda i, j: (clip(j // tiles_per_gain, 2), 0, 0)),
            pl.BlockSpec((F_ROWS, D), lambda i, j: (0, 0)),
            side_in,
        ],
        out_specs=[
            pl.BlockSpec((tm, tn), lambda i, j: (i, clip(j, n_qk))),
            pl.BlockSpec((tn, tm), lambda i, j: (clip(j - n_qk, n_v), i)),
            pl.BlockSpec((tm, tn), lambda i, j: (i, clip(j - n_qk - n_v, n_ug))),
            pl.BlockSpec((F_ROWS, tm), lambda i, j: (0, i)),
            side_out,
        ],
        out_shape=[
            jax.ShapeDtypeStruct((T, 2 * FOX_W), BF16),
            jax.ShapeDtypeStruct((FOX_W, T), BF16),
            jax.ShapeDtypeStruct((T, 2 * LRU_W), F32),
            jax.ShapeDtypeStruct((F_ROWS, T), F32),
            side_shape,
        ],
        scratch_shapes=[pltpu.VMEM((tm, D), BF16)],
        compiler_params=_params("arbitrary", "arbitrary"),
    )(x2d, g, w_t, g_qk, wf, w_side)


def _ckv_kernel(m_ref, g_ref, w_ref, gh_ref, o_ref):
    h = _rms(m_ref[...], g_ref[...]).astype(BF16)
    acc = _dot(h, w_ref[...].astype(BF16))
    for hh in range(XATT_HEADS):
        sl = slice(hh * HEAD_DIM, (hh + 1) * HEAD_DIM)
        o_ref[:, sl] = _rms(acc[:, sl], gh_ref[...]).astype(BF16)
    o_ref[:, XATT_W:] = acc[:, XATT_W:].astype(BF16)


def _ckv_proj(mem2d, g, w_ckv, layer, gh):
    R, D = mem2d.shape
    full = lambda shape: pl.BlockSpec(shape, lambda i: (0,) * len(shape))
    return pl.pallas_call(
        _ckv_kernel,
        grid=(1,),
        in_specs=[full((R, D)), full((1, D)),
                  pl.BlockSpec((None, D, 2 * XATT_W), lambda i: (layer, 0, 0)),
                  full((1, HEAD_DIM))],
        out_specs=full((R, 2 * XATT_W)),
        out_shape=jax.ShapeDtypeStruct((R, 2 * XATT_W), BF16),
        compiler_params=_params("arbitrary"),
    )(mem2d, g, w_ckv, gh)


def _fox_kernel(q_ref, k_ref, vt_ref, f_ref, bf_ref, side_ref, o_ref, side_out_ref,
                kaug_scr, vaug_scr, *, seq, n_slabs):
    hp = pl.program_id(1)
    _cast_slab(pl.program_id(0) * (FOX_HEADS // HEADS_PER_STEP) + hp, n_slabs,
               side_ref, side_out_ref)

    @pl.when(hp == 0)
    def _():
        z = f_ref[...] + bf_ref[...]
        c = jnp.minimum(z, 0.0) - jnp.log1p(jnp.exp(-jnp.abs(z)))
        lane = lax.broadcasted_iota(jnp.int32, c.shape, 1)
        d = 1
        while d < seq:
            c = c + jnp.where(lane >= d, pltpu.roll(c, d, axis=1), 0.0)
            d *= 2
        bias = c[:FOX_HEADS] * (-math.sqrt(HEAD_DIM))
        pieces = []
        for _ in range(BIAS_PIECES):
            p = bias.astype(BF16).astype(F32)
            pieces.append(p)
            bias = bias - p
        pad = jnp.zeros((HEAD_DIM - BIAS_PIECES * FOX_HEADS, seq), F32)
        bias_cols = jnp.concatenate(pieces + [pad], axis=0).T.astype(BF16)
        for e in range(HEADS_PER_STEP):
            kaug_scr[e, :, HEAD_DIM:] = bias_cols

    ones_row = lax.broadcasted_iota(jnp.int32, (V_PAD_ROWS, seq), 0) == 0
    for e in range(HEADS_PER_STEP):
        lanes = slice(e * HEAD_DIM, (e + 1) * HEAD_DIM)
        kaug_scr[e, :, :HEAD_DIM] = k_ref[:, lanes]
        vaug_scr[e, :HEAD_DIM, :] = vt_ref[lanes, :]
        vaug_scr[e, HEAD_DIM:, :] = jnp.where(ones_row, 1.0, 0.0).astype(BF16)

    tq, tk = Q_BLOCK, K_CHUNK
    lane = lax.broadcasted_iota(jnp.int32, (tq, HEAD_DIM), 1)
    selectors = []
    for e in range(HEADS_PER_STEP):
        h = hp * HEADS_PER_STEP + e
        mine = lane == h
        for p in range(1, BIAS_PIECES):
            mine = mine | (lane == h + p * FOX_HEADS)
        selectors.append(jnp.where(mine, 1.0, 0.0).astype(BF16))
    key = lax.broadcasted_iota(jnp.int32, (tk, tq), 0)
    qry = lax.broadcasted_iota(jnp.int32, (tk, tq), 1)
    exp2_scale = math.log2(math.e) / math.sqrt(HEAD_DIM)

    def scores(e, lo, k0):
        q = q_ref[lo:lo + tq, e * HEAD_DIM:(e + 1) * HEAD_DIM]
        q_aug = jnp.concatenate([q, selectors[e]], axis=1)
        return _dot_nt(kaug_scr[e, k0:k0 + tk, :], q_aug)

    steps = [(e, qi * tq, kc * tk) for qi in range(seq // tq)
             for kc in range((qi + 1) * tq // tk) for e in range(HEADS_PER_STEP)]
    pending = [scores(*st) for st in steps[:SCORE_LOOKAHEAD]]
    m = [None] * HEADS_PER_STEP
    acc = [None] * HEADS_PER_STEP
    for idx, (e, lo, k0) in enumerate(steps):
        t = pending.pop(0)
        if idx + SCORE_LOOKAHEAD < len(steps):
            pending.append(scores(*steps[idx + SCORE_LOOKAHEAD]))
        if k0 + tk > lo:
            t = jnp.where(key + (k0 - lo) <= qry, t, -jnp.inf)
        cm = jnp.max(t, axis=0, keepdims=True)
        m_new = cm if k0 == 0 else jnp.maximum(m[e], cm)
        p = jnp.exp2((t - m_new) * exp2_scale).astype(BF16)
        pv = _dot(vaug_scr[e, :, k0:k0 + tk], p)
        if k0 == 0:
            acc[e] = pv
        else:
            acc[e] = acc[e] * jnp.exp2((m[e] - m_new) * exp2_scale) + pv
        m[e] = m_new
        if k0 + tk == lo + tq:
            inv_l = 1.0 / acc[e][HEAD_DIM:HEAD_DIM + 1, :]
            o_ref[lo:lo + tq, e * HEAD_DIM:(e + 1) * HEAD_DIM] = (acc[e][:HEAD_DIM, :] * inv_l).T


def _fox_attention(qk, v_t, f_t, b_f, w_side, layer, *, batch, seq, n_slabs, side_tc):
    n_pairs = FOX_HEADS // HEADS_PER_STEP
    wide = HEADS_PER_STEP * HEAD_DIM
    side_in, side_out, side_shape = _slab_specs(w_side, layer, n_slabs, side_tc,
                                                lambda b, hp: b * n_pairs + hp)
    return pl.pallas_call(
        functools.partial(_fox_kernel, seq=seq, n_slabs=n_slabs),
        grid=(batch, n_pairs),
        in_specs=[
            pl.BlockSpec((None, seq, wide), lambda b, hp: (b, 0, hp)),
            pl.BlockSpec((None, seq, wide), lambda b, hp: (b, 0, n_pairs + hp)),
            pl.BlockSpec((wide, seq), lambda b, hp: (hp, b)),
            pl.BlockSpec((F_ROWS, seq), lambda b, hp: (0, b)),
            pl.BlockSpec((F_ROWS, 1), lambda b, hp: (0, 0)),
            side_in,
        ],
        out_specs=[pl.BlockSpec((None, seq, wide), lambda b, hp: (b, 0, hp)), side_out],
        out_shape=[jax.ShapeDtypeStruct((batch, seq, FOX_W), F32), side_shape],
        scratch_shapes=[pltpu.VMEM((HEADS_PER_STEP, seq, 2 * HEAD_DIM), BF16),
                        pltpu.VMEM((HEADS_PER_STEP, HEAD_DIM + V_PAD_ROWS, seq), BF16)],
        compiler_params=_params("arbitrary", "arbitrary"),
    )(qk, qk, v_t, f_t, b_f, w_side)


def _lru_kernel(u_ref, gate_ref, cw_ref, cb_ref, wg_ref, bra_ref, bri_ref, lam_ref, o_ref,
                at_scr, bt_scr, hp_scr, *, seq):
    row = lax.broadcasted_iota(jnp.int32, (SUBLANES, LRU_BLOCK), 0)

    def shifted(d):
        head = jnp.where(row >= d, pltpu.roll(u_ref[0:SUBLANES, :], d, axis=0), 0.0)
        return jnp.concatenate([head, u_ref[SUBLANES - d:seq - d, :]], axis=0)

    cw = cw_ref[...]
    uc = cw[0:1, :] * shifted(3)
    uc = uc + cw[1:2, :] * shifted(2)
    uc = uc + cw[2:3, :] * shifted(1)
    uc = uc + cw[3:4, :] * u_ref[...]
    uc = cb_ref[...] + uc

    gates = _dot(uc.astype(BF16), wg_ref[...])
    half_c = (-0.5 * LRU_C) * _softplus(-lam_ref[...])
    log_a = half_c * jnp.tanh(0.5 * (gates[:, :LRU_BLOCK] + bra_ref[...])) + half_c
    i = 0.5 * jnp.tanh(0.5 * (gates[:, LRU_BLOCK:] + bri_ref[...])) + 0.5
    a = jnp.exp(log_a)
    t = jnp.tanh(log_a)
    w = -2.0 * t
    root = jnp.where(w > 0.0, w * lax.rsqrt(w * (1.0 - t)), 0.0)
    b = root * (i * uc)

    groups = seq // SUBLANES
    gshape = (groups, SUBLANES, LRU_BLOCK)
    a3 = a.reshape(gshape)
    b3 = b.reshape(gshape)
    sub = lax.broadcasted_iota(jnp.int32, gshape, 1)
    d = 1
    while d < SUBLANES:
        keep = sub >= d
        b3 = a3 * jnp.where(keep, pltpu.roll(b3, d, axis=1), 0.0) + b3
        a3 = a3 * jnp.where(keep, pltpu.roll(a3, d, axis=1), 1.0)
        d *= 2
    at_scr[...] = jnp.broadcast_to(a3[:, SUBLANES - 1:, :], gshape).reshape(seq, LRU_BLOCK)
    bt_scr[...] = jnp.broadcast_to(b3[:, SUBLANES - 1:, :], gshape).reshape(seq, LRU_BLOCK)

    def carry(g, h):
        rows = pl.ds(pl.multiple_of(g * SUBLANES, SUBLANES), SUBLANES)
        hp_scr[rows, :] = h
        return at_scr[rows, :] * h + bt_scr[rows, :]

    lax.fori_loop(0, groups, carry, jnp.zeros((SUBLANES, LRU_BLOCK), F32), unroll=8)
    hs = b3.reshape(seq, LRU_BLOCK) + a3.reshape(seq, LRU_BLOCK) * hp_scr[...]
    o_ref[...] = hs * jax.nn.gelu(gate_ref[...])


def _rg_lru(ug, cw, cb, wg, bra, bri, lam, *, batch, seq):
    nb = LRU_BLOCKS
    vec = lambda: pl.BlockSpec((1, LRU_BLOCK), lambda b, n: (0, n))
    return pl.pallas_call(
        functools.partial(_lru_kernel, seq=seq),
        grid=(batch, nb),
        in_specs=[
            pl.BlockSpec((None, seq, LRU_BLOCK), lambda b, n: (b, 0, n)),
            pl.BlockSpec((None, seq, LRU_BLOCK), lambda b, n: (b, 0, nb + n)),
            pl.BlockSpec((CONV_W, LRU_BLOCK), lambda b, n: (0, n)),
            vec(),
            pl.BlockSpec((None, LRU_BLOCK, 2 * LRU_BLOCK), lambda b, n: (n, 0, 0)),
            vec(), vec(), vec(),
        ],
        out_specs=pl.BlockSpec((None, seq, LRU_BLOCK), lambda b, n: (b, 0, n)),
        out_shape=jax.ShapeDtypeStruct((batch, seq, LRU_W), F32),
        scratch_shapes=[pltpu.VMEM((seq, LRU_BLOCK), F32)] * 3,
        compiler_params=_params("parallel", "arbitrary"),
    )(ug, ug, cw, cb, wg, bra, bri, lam)


def _out_kernel(of_ref, yl_ref, gf_ref, gl_ref, w_ref, x_ref, o_ref, w_scr, *, chunk):
    @pl.when(pl.program_id(0) == 0)
    def _():
        w_scr[...] = w_ref[...].astype(BF16)

    mf = _rms(of_ref[...], gf_ref[...]).astype(BF16)
    ml = _rms(yl_ref[...], gl_ref[...]).astype(BF16)
    for c in range(o_ref.shape[1] // chunk):
        sl = slice(c * chunk, (c + 1) * chunk)
        acc = _dot(mf, w_scr[:FOX_W, sl]) + _dot(ml, w_scr[FOX_W:, sl])
        o_ref[:, sl] = x_ref[:, sl] + acc


def _out_proj(o_fox, y_lru, gf, gl, w_out, layer, x2d, *, tm):
    T, D = x2d.shape
    return pl.pallas_call(
        functools.partial(_out_kernel, chunk=2 * PROJ_CHUNK),
        grid=(T // tm,),
        in_specs=[
            pl.BlockSpec((tm, FOX_W), lambda i: (i, 0)),
            pl.BlockSpec((tm, LRU_W), lambda i: (i, 0)),
            pl.BlockSpec((1, FOX_W), lambda i: (0, 0)),
            pl.BlockSpec((1, LRU_W), lambda i: (0, 0)),
            pl.BlockSpec((None, FOX_W + LRU_W, D), lambda i: (layer, 0, 0),
                         pipeline_mode=pl.Buffered(1)),
            pl.BlockSpec((tm, D), lambda i: (i, 0)),
        ],
        out_specs=pl.BlockSpec((tm, D), lambda i: (i, 0)),
        out_shape=jax.ShapeDtypeStruct((T, D), F32),
        scratch_shapes=[pltpu.VMEM((FOX_W + LRU_W, D), BF16)],
        compiler_params=_params("arbitrary"),
    )(o_fox, y_lru, gf, gl, w_out, x2d)


def _xattn_kernel(x_ref, g_ref, wq_ref, gq_ref, ckv_ref, wo_ref, o_ref):
    x = x_ref[...]
    cq = _dot(_rms(x, g_ref[...]).astype(BF16), wq_ref[...].astype(BF16))
    scale = 1.0 / math.sqrt(HEAD_DIM)
    scores = []
    for hh in range(XATT_HEADS):
        sl = slice(hh * HEAD_DIM, (hh + 1) * HEAD_DIM)
        qh = _rms(cq[:, sl], gq_ref[...]).astype(BF16)
        scores.append(_dot_nt(qh, ckv_ref[:, sl]))
    heads = []
    for hh in range(XATT_HEADS):
        vh = ckv_ref[:, XATT_W + hh * HEAD_DIM:XATT_W + (hh + 1) * HEAD_DIM]
        s = scores[hh] * scale
        e = jnp.exp(s - jnp.max(s, axis=-1, keepdims=True))
        l = jnp.sum(e, axis=-1, keepdims=True)
        heads.append((_dot(e.astype(BF16), vh) / l).astype(BF16))
    ox = jnp.concatenate(heads, axis=-1)
    o_ref[...] = x + _dot(ox, wo_ref[...].astype(BF16))


def _cross_attention(x3d, g, w_cq, gq, ckv, w_co, layer, *, tm):
    B, S, D = x3d.shape
    M = ckv.shape[1]
    return pl.pallas_call(
        _xattn_kernel,
        grid=(B, S // tm),
        in_specs=[
            pl.BlockSpec((None, tm, D), lambda b, i: (b, i, 0)),
            pl.BlockSpec((1, D), lambda b, i: (0, 0)),
            pl.BlockSpec((None, D, XATT_W), lambda b, i: (layer, 0, 0)),
            pl.BlockSpec((1, HEAD_DIM), lambda b, i: (0, 0)),
            pl.BlockSpec((None, M, 2 * XATT_W), lambda b, i: (b, 0, 0)),
            pl.BlockSpec((None, XATT_W, D), lambda b, i: (layer, 0, 0)),
        ],
        out_specs=pl.BlockSpec((None, tm, D), lambda b, i: (b, i, 0)),
        out_shape=jax.ShapeDtypeStruct((B, S, D), F32),
        compiler_params=_params("parallel", "parallel"),
    )(x3d, g, w_cq, gq, ckv, w_co)


def _ffn_kernel(x_ref, g_ref, wg_ref, wu_ref, wd_ref, o_ref, h_scr):
    @pl.when(pl.program_id(1) == 0)
    def _():
        x = x_ref[...]
        h_scr[...] = _rms(x, g_ref[...]).astype(BF16)
        o_ref[...] = x

    h = h_scr[...]
    gate = _dot(h, wg_ref[...])
    up = _dot(h, wu_ref[...])
    act = (jax.nn.silu(gate) * up).astype(BF16)
    o_ref[...] += _dot(act, wd_ref[...])


def _ffn(x2d, g, w_gu, w_d, *, tm):
    T, D = x2d.shape
    H = w_d.shape[0]
    th = w_gu.shape[2]
    nk = H // th
    return pl.pallas_call(
        _ffn_kernel,
        grid=(T // tm, nk),
        in_specs=[
            pl.BlockSpec((tm, D), lambda i, k: (jnp.minimum(i + (k > 0), T // tm - 1), 0)),
            pl.BlockSpec((1, D), lambda i, k: (0, 0)),
            pl.BlockSpec((None, D, th), lambda i, k: (k, 0, 0)),
            pl.BlockSpec((None, D, th), lambda i, k: (nk + k, 0, 0)),
            pl.BlockSpec((th, D), lambda i, k: (k, 0)),
        ],
        out_specs=pl.BlockSpec((tm, D), lambda i, k: (i, 0)),
        out_shape=jax.ShapeDtypeStruct((T, D), F32),
        scratch_shapes=[pltpu.VMEM((tm, D), BF16)],
        compiler_params=_params("parallel", "arbitrary"),
    )(x2d, g, w_gu, w_gu, w_d)


def kernel(x, mem, g_mix, w_in, b_f, g_q, g_k, conv_w, conv_b, w_ra, b_ra, w_ri, b_ri, lam,
           g_fox_out, g_lru_out, w_out, g_xattn, g_mem, w_cq, w_ckv, g_cq, g_ck, w_co, g_ffn,
           w_gate_up, w_down):
    B, S, D = x.shape
    M = mem.shape[1]
    T = B * S
    depth = g_mix.shape[0]
    row = lambda v: v.reshape(1, -1).astype(F32)

    for l in range(depth):
        w_t, w_f = _stage_in_weights(jnp.swapaxes(w_in, 1, 2), l, tr=512)
        g_qk = jnp.stack([g_q[l], g_k[l]]).reshape(2, 1, HEAD_DIM).astype(F32)
        bf_col = jnp.zeros((F_ROWS, 1), F32).at[:FOX_HEADS, 0].set(b_f[l])
        w_gates = jnp.concatenate([w_ra[l], w_ri[l]], axis=-1).astype(BF16)

        x2d = x.reshape(T, D)

        qk, v_t, ug, f_t, w_down_bf = _in_proj(x2d, row(g_mix[l]), w_t, g_qk, w_f, w_down, l,
                                               tm=1024, tn=1024, n_slabs=SIDE_CAST_SLABS,
                                               side_tc=D)
        o_fox, w_gu_bf = _fox_attention(qk.reshape(B, S, 2 * FOX_W), v_t, f_t, bf_col,
                                        w_gate_up, l, batch=B, seq=S,
                                        n_slabs=B * FOX_HEADS // HEADS_PER_STEP,
                                        side_tc=FFN_TILE)
        y_lru = _rg_lru(ug.reshape(B, S, 2 * LRU_W), conv_w[l], row(conv_b[l]), w_gates,
                        row(b_ra[l]), row(b_ri[l]), row(lam[l]), batch=B, seq=S)
        x2d = _out_proj(o_fox.reshape(T, FOX_W), y_lru.reshape(T, LRU_W),
                        row(g_fox_out[l]), row(g_lru_out[l]), w_out, l, x2d, tm=512)

        ckv = _ckv_proj(mem.reshape(B * M, D), row(g_mem[l]), w_ckv, l, row(g_ck[l]))
        x3d = _cross_attention(x2d.reshape(B, S, D), row(g_xattn[l]), w_cq, row(g_cq[l]),
                               ckv.reshape(B, M, 2 * XATT_W), w_co, l, tm=512)

        x2d = _ffn(x3d.reshape(T, D), row(g_ffn[l]), w_gu_bf, w_down_bf[0], tm=1024)
        x = x2d.reshape(B, S, D)
    return x
```

```python
import functools
import math

import jax
import jax.numpy as jnp
from jax import lax
from jax.experimental import pallas as pl
from jax.experimental.pallas import tpu as pltpu

F32 = jnp.float32
BF16 = jnp.bfloat16

HEAD_DIM = 128
FOX_HEADS = 8
FOX_W = FOX_HEADS * HEAD_DIM
LRU_BLOCKS = 8
LRU_BLOCK = 128
LRU_W = LRU_BLOCKS * LRU_BLOCK
LRU_C = 8.0
CONV_W = 4
XATT_HEADS = 4
XATT_W = XATT_HEADS * HEAD_DIM
RMS_EPS = 1e-6
F_ROWS = 16
Q_BLOCK = 512
K_CHUNK = 512
HEADS_PER_STEP = 2
SCORE_LOOKAHEAD = 4
V_PAD_ROWS = 16
PROJ_CHUNK = 256
SUBLANES = 8
LRU_BLOCKS_PER_STEP = 4
FFN_TILE = 512
SIDE_CAST_SLABS = 32
FOX_Q_SCALE = math.log2(math.e) / math.sqrt(HEAD_DIM)
BIAS_PIECES = 3

V7X_VMEM_LIMIT_BYTES = 60 * 1024 * 1024


def _rms(x, g):
    ms = jnp.mean(x * x, axis=-1, keepdims=True)
    return x * lax.rsqrt(ms + RMS_EPS) * g


def _dot(a, b):
    return jnp.dot(a, b, preferred_element_type=F32)


def _dot_nt(a, b):
    return lax.dot_general(a, b, (((1,), (1,)), ((), ())), preferred_element_type=F32)


def _softplus(x):
    return jnp.maximum(x, 0.0) + jnp.log1p(jnp.exp(-jnp.abs(x)))


def _cast_slab(step, n_slabs, src_ref, dst_ref):
    @pl.when(step < n_slabs)
    def _():
        n_tiles, _, tc = dst_ref.shape
        for t in range(n_tiles):
            dst_ref[t] = src_ref[:, t * tc:(t + 1) * tc].astype(BF16)


def _slab_specs(w, layer, n_slabs, tc, step_of):
    _, R, C = w.shape
    rows = R // n_slabs
    slab = lambda *ids: jnp.minimum(step_of(*ids), n_slabs - 1)
    return (pl.BlockSpec((None, rows, C), lambda *ids: (layer, slab(*ids), 0)),
            pl.BlockSpec((C // tc, rows, tc), lambda *ids: (0, slab(*ids), 0)),
            jax.ShapeDtypeStruct((C // tc, R, tc), BF16))


def _params(*sem):
    return pltpu.CompilerParams(dimension_semantics=sem,
                                vmem_limit_bytes=V7X_VMEM_LIMIT_BYTES)


def _wstage_kernel(a_ref, b_ref, w_ref, wf_ref, *, f_tile):
    t = pl.program_id(0)

    @pl.when(t < f_tile)
    def _():
        w_ref[...] = a_ref[...].astype(BF16)

    @pl.when(t >= f_tile)
    def _():
        w_ref[...] = jnp.concatenate([a_ref[FOX_HEADS:, :], b_ref[...]], axis=0).astype(BF16)

    @pl.when(t == f_tile)
    def _():
        wf_ref[:FOX_HEADS, :] = a_ref[:FOX_HEADS, :].astype(BF16)
        wf_ref[FOX_HEADS:, :] = jnp.zeros((F_ROWS - FOX_HEADS, wf_ref.shape[1]), BF16)


def _stage_in_weights(w_t, layer, *, tr):
    _, N, D = w_t.shape
    n_rows = N - FOX_HEADS
    f_tile = 3 * FOX_W // tr
    return pl.pallas_call(
        functools.partial(_wstage_kernel, f_tile=f_tile),
        grid=(n_rows // tr,),
        in_specs=[pl.BlockSpec((None, tr, D), lambda t: (layer, t, 0)),
                  pl.BlockSpec((None, FOX_HEADS, D),
                               lambda t: (layer, (jnp.maximum(t, f_tile) + 1) * (tr // FOX_HEADS), 0))],
        out_specs=[pl.BlockSpec((tr, D), lambda t: (t, 0)),
                   pl.BlockSpec((F_ROWS, D), lambda t: (0, 0))],
        out_shape=[jax.ShapeDtypeStruct((n_rows, D), BF16),
                   jax.ShapeDtypeStruct((F_ROWS, D), BF16)],
        compiler_params=_params("arbitrary"),
    )(w_t, w_t)


def _inproj_kernel(x_ref, g_ref, w_ref, gh_ref, wf_ref, side_ref,
                   qk_ref, vt_ref, ug_ref, f_ref, side_out_ref, h_scr,
                   *, tn, n_qk, n_v, n_steps, n_slabs):
    j = pl.program_id(1)
    _cast_slab(pl.program_id(0) * n_steps + j, n_slabs, side_ref, side_out_ref)

    @pl.when(j == 0)
    def _():
        h = _rms(x_ref[...], g_ref[...]).astype(BF16)
        h_scr[...] = h
        f_ref[...] = _dot_nt(wf_ref[...], h)

    chunks = [slice(c * PROJ_CHUNK, (c + 1) * PROJ_CHUNK) for c in range(tn // PROJ_CHUNK)]

    @pl.when(j < n_qk)
    def _():
        for sl in chunks:
            acc = _dot_nt(h_scr[...], w_ref[sl, :])
            for hh in range(PROJ_CHUNK // HEAD_DIM):
                lo = sl.start + hh * HEAD_DIM
                qk_ref[:, lo:lo + HEAD_DIM] = _rms(
                    acc[:, hh * HEAD_DIM:(hh + 1) * HEAD_DIM], gh_ref[...]).astype(BF16)

    @pl.when((j >= n_qk) & (j < n_qk + n_v))
    def _():
        for sl in chunks:
            vt_ref[sl, :] = _dot_nt(w_ref[sl, :], h_scr[...]).astype(BF16)

    @pl.when(j >= n_qk + n_v)
    def _():
        for sl in chunks:
            ug_ref[:, sl] = _dot_nt(h_scr[...], w_ref[sl, :])


def _in_proj(x2d, g, w_t, g_qk, wf, w_side, layer, *, tm, tn, n_slabs, side_tc):
    T, D = x2d.shape
    n_qk = 2 * FOX_W // tn
    n_v = FOX_W // tn
    n_ug = 2 * LRU_W // tn
    tiles_per_gain = FOX_W // tn
    clip = lambda v, n: jnp.clip(v, 0, n - 1)
    n_steps = n_qk + n_v + n_ug
    side_in, side_out, side_shape = _slab_specs(w_side, layer, n_slabs, side_tc,
                                                lambda i, j: i * n_steps + j)
    return pl.pallas_call(
        functools.partial(_inproj_kernel, tn=tn, n_qk=n_qk, n_v=n_v, n_steps=n_steps,
                          n_slabs=n_slabs),
        grid=(T // tm, n_steps),
        in_specs=[
            pl.BlockSpec((tm, D), lambda i, j: (jnp.minimum(i + (j > 0), T // tm - 1), 0)),
            pl.BlockSpec((1, D), lambda i, j: (0, 0)),
            pl.BlockSpec((tn, D), lambda i, j: (j, 0)),
            pl.BlockSpec((None, 1, HEAD_DIM), lambda i, j: (clip(j // tiles_per_gain, 2), 0, 0)),
            pl.BlockSpec((F_ROWS, D), lambda i, j: (0, 0)),
            side_in,
        ],
        out_specs=[
            pl.BlockSpec((tm, tn), lambda i, j: (i, clip(j, n_qk))),
            pl.BlockSpec((tn, tm), lambda i, j: (clip(j - n_qk, n_v), i)),
            pl.BlockSpec((tm, tn), lambda i, j: (i, clip(j - n_qk - n_v, n_ug))),
            pl.BlockSpec((F_ROWS, tm), lambda i, j: (0, i)),
            side_out,
        ],
        out_shape=[
            jax.ShapeDtypeStruct((T, 2 * FOX_W), BF16),
            jax.ShapeDtypeStruct((FOX_W, T), BF16),
            jax.ShapeDtypeStruct((T, 2 * LRU_W), F32),
            jax.ShapeDtypeStruct((F_ROWS, T), F32),
            side_shape,
        ],
        scratch_shapes=[pltpu.VMEM((tm, D), BF16)],
        compiler_params=_params("arbitrary", "arbitrary"),
    )(x2d, g, w_t, g_qk, wf, w_side)


def _fox_kernel(q_ref, k_ref, vt_ref, f_ref, bf_ref, side_ref, o_ref, side_out_ref,
                kaug_scr, vaug_scr, *, seq, n_slabs):
    hp = pl.program_id(1)
    _cast_slab(pl.program_id(0) * (FOX_HEADS // HEADS_PER_STEP) + hp, n_slabs,
               side_ref, side_out_ref)

    @pl.when(hp == 0)
    def _():
        z = f_ref[...] + bf_ref[...]
        c = jnp.minimum(z, 0.0) - jnp.log1p(jnp.exp(-jnp.abs(z)))
        lane = lax.broadcasted_iota(jnp.int32, c.shape, 1)
        d = 1
        while d < seq:
            c = c + jnp.where(lane >= d, pltpu.roll(c, d, axis=1), 0.0)
            d *= 2
        bias = c[:FOX_HEADS] * (-math.log2(math.e))
        pieces = []
        for _ in range(BIAS_PIECES):
            p = bias.astype(BF16).astype(F32)
            pieces.append(p)
            bias = bias - p
        pad = jnp.zeros((HEAD_DIM - BIAS_PIECES * FOX_HEADS, seq), F32)
        bias_cols = jnp.concatenate(pieces + [pad], axis=0).T.astype(BF16)
        for e in range(HEADS_PER_STEP):
            kaug_scr[e, :, HEAD_DIM:] = bias_cols

    ones_row = lax.broadcasted_iota(jnp.int32, (V_PAD_ROWS, seq), 0) == 0
    for e in range(HEADS_PER_STEP):
        lanes = slice(e * HEAD_DIM, (e + 1) * HEAD_DIM)
        kaug_scr[e, :, :HEAD_DIM] = k_ref[:, lanes]
        vaug_scr[e, :HEAD_DIM, :] = vt_ref[lanes, :]
        vaug_scr[e, HEAD_DIM:, :] = jnp.where(ones_row, 1.0, 0.0).astype(BF16)

    tq, tk = Q_BLOCK, K_CHUNK
    lane = lax.broadcasted_iota(jnp.int32, (tq, HEAD_DIM), 1)
    selectors = []
    for e in range(HEADS_PER_STEP):
        h = hp * HEADS_PER_STEP + e
        mine = lane == h
        for p in range(1, BIAS_PIECES):
            mine = mine | (lane == h + p * FOX_HEADS)
        selectors.append(jnp.where(mine, 1.0, 0.0).astype(BF16))
    key = lax.broadcasted_iota(jnp.int32, (tk, tq), 0)
    qry = lax.broadcasted_iota(jnp.int32, (tk, tq), 1)

    def scores(e, lo, k0):
        q = q_ref[lo:lo + tq, e * HEAD_DIM:(e + 1) * HEAD_DIM]
        q_aug = jnp.concatenate([q, selectors[e]], axis=1)
        return _dot_nt(kaug_scr[e, k0:k0 + tk, :], q_aug)

    steps = [(e, qi * tq, kc * tk) for qi in range(seq // tq)
             for kc in range((qi + 1) * tq // tk) for e in range(HEADS_PER_STEP)]
    pending = [scores(*st) for st in steps[:SCORE_LOOKAHEAD]]
    m = [None] * HEADS_PER_STEP
    acc = [None] * HEADS_PER_STEP
    for idx, (e, lo, k0) in enumerate(steps):
        t = pending.pop(0)
        if idx + SCORE_LOOKAHEAD < len(steps):
            pending.append(scores(*steps[idx + SCORE_LOOKAHEAD]))
        if k0 + tk > lo:
            t = jnp.where(key + (k0 - lo) <= qry, t, -jnp.inf)
        cm = jnp.max(t, axis=0, keepdims=True)
        m_new = cm if k0 == 0 else jnp.maximum(m[e], cm)
        p = jnp.exp2(t - m_new).astype(BF16)
        pv = _dot(vaug_scr[e, :, k0:k0 + tk], p)
        if k0 == 0:
            acc[e] = pv
        else:
            acc[e] = acc[e] * jnp.exp2(m[e] - m_new) + pv
        m[e] = m_new
        if k0 + tk == lo + tq:
            inv_l = 1.0 / acc[e][HEAD_DIM:HEAD_DIM + 1, :]
            o_ref[lo:lo + tq, e * HEAD_DIM:(e + 1) * HEAD_DIM] = (acc[e][:HEAD_DIM, :] * inv_l).T


def _fox_attention(qk, v_t, f_t, b_f, w_side, layer, *, batch, seq, n_slabs, side_tc):
    n_pairs = FOX_HEADS // HEADS_PER_STEP
    wide = HEADS_PER_STEP * HEAD_DIM
    side_in, side_out, side_shape = _slab_specs(w_side, layer, n_slabs, side_tc,
                                                lambda b, hp: b * n_pairs + hp)
    return pl.pallas_call(
        functools.partial(_fox_kernel, seq=seq, n_slabs=n_slabs),
        grid=(batch, n_pairs),
        in_specs=[
            pl.BlockSpec((None, seq, wide), lambda b, hp: (b, 0, hp)),
            pl.BlockSpec((None, seq, wide), lambda b, hp: (b, 0, n_pairs + hp)),
            pl.BlockSpec((wide, seq), lambda b, hp: (hp, b)),
            pl.BlockSpec((F_ROWS, seq), lambda b, hp: (0, b)),
            pl.BlockSpec((F_ROWS, 1), lambda b, hp: (0, 0)),
            side_in,
        ],
        out_specs=[pl.BlockSpec((None, seq, wide), lambda b, hp: (b, 0, hp)), side_out],
        out_shape=[jax.ShapeDtypeStruct((batch, seq, FOX_W), F32), side_shape],
        scratch_shapes=[pltpu.VMEM((HEADS_PER_STEP, seq, 2 * HEAD_DIM), BF16),
                        pltpu.VMEM((HEADS_PER_STEP, HEAD_DIM + V_PAD_ROWS, seq), BF16)],
        compiler_params=_params("arbitrary", "arbitrary"),
    )(qk, qk, v_t, f_t, b_f, w_side)


def _lru_kernel(*refs, seq, n_side, n_slabs):
    u_refs = refs[:LRU_BLOCKS_PER_STEP]
    refs = refs[LRU_BLOCKS_PER_STEP:]
    gate_ref, cw_ref, cb_ref, wg_ref, bra_ref, bri_ref, lam_ref = refs[:7]
    side_refs = refs[7:7 + n_side]
    o_ref = refs[7 + n_side]
    side_out_refs = refs[8 + n_side:8 + 2 * n_side]
    at_scr, bt_scr, hp_scr = refs[8 + 2 * n_side:]
    step = pl.program_id(0) * pl.num_programs(1) + pl.program_id(1)
    for src_ref, dst_ref in zip(side_refs, side_out_refs):
        _cast_slab(step, n_slabs, src_ref, dst_ref)
    width = o_ref.shape[1]
    row = lax.broadcasted_iota(jnp.int32, (SUBLANES, LRU_BLOCK), 0)

    def shifted(d):
        cols = []
        for u_ref in u_refs:
            head = jnp.where(row >= d, pltpu.roll(u_ref[0:SUBLANES, :], d, axis=0), 0.0)
            cols.append(jnp.concatenate([head, u_ref[SUBLANES - d:seq - d, :]], axis=0))
        return jnp.concatenate(cols, axis=1)

    cw = cw_ref[...]
    uc = cw[0:1, :] * shifted(3)
    uc = uc + cw[1:2, :] * shifted(2)
    uc = uc + cw[2:3, :] * shifted(1)
    uc = uc + cw[3:4, :] * jnp.concatenate([u_ref[...] for u_ref in u_refs], axis=1)
    uc = cb_ref[...] + uc

    ucb = uc.astype(BF16)
    gates = [_dot(ucb[:, n * LRU_BLOCK:(n + 1) * LRU_BLOCK], wg_ref[n])
             for n in range(width // LRU_BLOCK)]
    pre_r = jnp.concatenate([g[:, :LRU_BLOCK] for g in gates], axis=1)
    pre_i = jnp.concatenate([g[:, LRU_BLOCK:] for g in gates], axis=1)
    half_c = (-0.5 * LRU_C) * _softplus(-lam_ref[...])
    log_a = half_c * jnp.tanh(0.5 * (pre_r + bra_ref[...])) + half_c
    i = 0.5 * jnp.tanh(0.5 * (pre_i + bri_ref[...])) + 0.5
    a = jnp.exp(log_a)
    t = jnp.tanh(log_a)
    w = -2.0 * t
    root = jnp.where(w > 0.0, w * lax.rsqrt(w * (1.0 - t)), 0.0)
    b = root * (i * uc)

    groups = seq // SUBLANES
    gshape = (groups, SUBLANES, width)
    a3 = a.reshape(gshape)
    b3 = b.reshape(gshape)
    sub = lax.broadcasted_iota(jnp.int32, gshape, 1)
    d = 1
    while d < SUBLANES:
        keep = sub >= d
        b3 = a3 * jnp.where(keep, pltpu.roll(b3, d, axis=1), 0.0) + b3
        a3 = a3 * jnp.where(keep, pltpu.roll(a3, d, axis=1), 1.0)
        d *= 2
    at_scr[...] = jnp.broadcast_to(a3[:, SUBLANES - 1:, :], gshape).reshape(seq, width)
    bt_scr[...] = jnp.broadcast_to(b3[:, SUBLANES - 1:, :], gshape).reshape(seq, width)

    def carry(g, h):
        rows = pl.ds(pl.multiple_of(g * SUBLANES, SUBLANES), SUBLANES)
        hp_scr[rows, :] = h
        return at_scr[rows, :] * h + bt_scr[rows, :]

    lax.fori_loop(0, groups, carry, jnp.zeros((SUBLANES, width), F32), unroll=8)
    hs = b3.reshape(seq, width) + a3.reshape(seq, width) * hp_scr[...]
    o_ref[...] = hs * jax.nn.gelu(gate_ref[...])


def _rg_lru(ug, cw, cb, wg, bra, bri, lam, side_weights, layer, *, batch, seq):
    per = LRU_BLOCKS_PER_STEP
    nb = LRU_BLOCKS // per
    width = per * LRU_BLOCK
    vec = lambda: pl.BlockSpec((1, width), lambda b, n: (0, n))
    n_slabs = batch * nb
    sides = [_slab_specs(w, layer, n_slabs, w.shape[2], lambda b, n: b * nb + n)
             for w in side_weights]
    res = pl.pallas_call(
        functools.partial(_lru_kernel, seq=seq, n_side=len(sides), n_slabs=n_slabs),
        grid=(batch, nb),
        in_specs=[
            *[pl.BlockSpec((None, seq, LRU_BLOCK), lambda b, n, e=e: (b, 0, per * n + e))
              for e in range(per)],
            pl.BlockSpec((None, seq, width), lambda b, n: (b, 0, nb + n)),
            pl.BlockSpec((CONV_W, width), lambda b, n: (0, n)),
            vec(),
            pl.BlockSpec((per, LRU_BLOCK, 2 * LRU_BLOCK), lambda b, n: (n, 0, 0)),
            vec(), vec(), vec(),
            *[sd[0] for sd in sides],
        ],
        out_specs=[pl.BlockSpec((None, seq, width), lambda b, n: (b, 0, n)),
                   *[sd[1] for sd in sides]],
        out_shape=[jax.ShapeDtypeStruct((batch, seq, LRU_W), F32), *[sd[2] for sd in sides]],
        scratch_shapes=[pltpu.VMEM((seq, width), F32)] * 3,
        compiler_params=_params("arbitrary", "arbitrary"),
    )(*[ug] * per, ug, cw, cb, wg, bra, bri, lam, *side_weights)
    return res[0], [r[0] for r in res[1:]]


def _mix_xattn_kernel(of_ref, yl_ref, gf_ref, gl_ref, wout_ref, x_ref, gx_ref, wq_ref, gq_ref,
                      mem_ref, gm_ref, wkv_ref, gk_ref, wo_ref, o_ref, x1_scr, ckv_ref, *, chunk):
    @pl.when(pl.program_id(1) == 0)
    def _():
        kv = _dot(_rms(mem_ref[...], gm_ref[...]).astype(BF16), wkv_ref[...])
        for hh in range(XATT_HEADS):
            sl = slice(hh * HEAD_DIM, (hh + 1) * HEAD_DIM)
            ckv_ref[:, sl] = _rms(kv[:, sl], gk_ref[...]).astype(BF16)
        ckv_ref[:, XATT_W:] = kv[:, XATT_W:].astype(BF16)

    mf = _rms(of_ref[...], gf_ref[...]).astype(BF16)
    ml = _rms(yl_ref[...], gl_ref[...]).astype(BF16)
    for c in range(o_ref.shape[1] // chunk):
        sl = slice(c * chunk, (c + 1) * chunk)
        acc = _dot(mf, wout_ref[:FOX_W, sl]) + _dot(ml, wout_ref[FOX_W:, sl])
        x1_scr[:, sl] = x_ref[:, sl] + acc

    x1 = x1_scr[...]
    cq = _dot(_rms(x1, gx_ref[...]).astype(BF16), wq_ref[...])
    scale = 1.0 / math.sqrt(HEAD_DIM)
    scores = []
    for hh in range(XATT_HEADS):
        sl = slice(hh * HEAD_DIM, (hh + 1) * HEAD_DIM)
        qh = _rms(cq[:, sl], gq_ref[...]).astype(BF16)
        scores.append(_dot_nt(qh, ckv_ref[:, sl]))
    heads = []
    for hh in range(XATT_HEADS):
        vh = ckv_ref[:, XATT_W + hh * HEAD_DIM:XATT_W + (hh + 1) * HEAD_DIM]
        s = scores[hh] * scale
        e = jnp.exp(s - jnp.max(s, axis=-1, keepdims=True))
        l = jnp.sum(e, axis=-1, keepdims=True)
        heads.append((_dot(e.astype(BF16), vh) / l).astype(BF16))
    ox = jnp.concatenate(heads, axis=-1)
    o_ref[...] = x1 + _dot(ox, wo_ref[...])


def _mix_xattn(o_fox, y_lru, gf, gl, w_out, x3d, gx, w_cq, gq, mem, gm, w_ckv, gk, w_co, *, tm):
    B, S, D = x3d.shape
    M = mem.shape[1]
    const = lambda shape: pl.BlockSpec(shape, lambda b, i: (0,) * len(shape),
                                       pipeline_mode=pl.Buffered(1))
    tile = lambda w: pl.BlockSpec((None, tm, w), lambda b, i: (b, i, 0))
    return pl.pallas_call(
        functools.partial(_mix_xattn_kernel, chunk=2 * PROJ_CHUNK),
        grid=(B, S // tm),
        in_specs=[
            tile(FOX_W), tile(LRU_W), const((1, FOX_W)), const((1, LRU_W)),
            const((FOX_W + LRU_W, D)), tile(D), const((1, D)), const((D, XATT_W)),
            const((1, HEAD_DIM)),
            pl.BlockSpec((None, M, D), lambda b, i: (b, 0, 0)),
            const((1, D)), const((D, 2 * XATT_W)), const((1, HEAD_DIM)),
            const((XATT_W, D)),
        ],
        out_specs=tile(D),
        out_shape=jax.ShapeDtypeStruct((B, S, D), F32),
        scratch_shapes=[pltpu.VMEM((tm, D), F32), pltpu.VMEM((M, 2 * XATT_W), BF16)],
        compiler_params=_params("parallel", "arbitrary"),
    )(o_fox, y_lru, gf, gl, w_out, x3d, gx, w_cq, gq, mem, gm, w_ckv, gk, w_co)


def _ffn_kernel(x_ref, g_ref, wg_ref, wu_ref, wd_ref, o_ref, h_scr):
    @pl.when(pl.program_id(1) == 0)
    def _():
        x = x_ref[...]
        h_scr[...] = _rms(x, g_ref[...]).astype(BF16)
        o_ref[...] = x

    h = h_scr[...]
    gate = _dot(h, wg_ref[...])
    up = _dot(h, wu_ref[...])
    act = (jax.nn.silu(gate) * up).astype(BF16)
    o_ref[...] += _dot(act, wd_ref[...])


def _ffn(x2d, g, w_gu, w_d, *, tm):
    T, D = x2d.shape
    H = w_d.shape[0]
    th = w_gu.shape[2]
    nk = H // th
    return pl.pallas_call(
        _ffn_kernel,
        grid=(T // tm, nk),
        in_specs=[
            pl.BlockSpec((tm, D), lambda i, k: (jnp.minimum(i + (k > 0), T // tm - 1), 0)),
            pl.BlockSpec((1, D), lambda i, k: (0, 0)),
            pl.BlockSpec((None, D, th), lambda i, k: (k, 0, 0)),
            pl.BlockSpec((None, D, th), lambda i, k: (nk + k, 0, 0)),
            pl.BlockSpec((th, D), lambda i, k: (k, 0)),
        ],
        out_specs=pl.BlockSpec((tm, D), lambda i, k: (i, 0)),
        out_shape=jax.ShapeDtypeStruct((T, D), F32),
        scratch_shapes=[pltpu.VMEM((tm, D), BF16)],
        compiler_params=_params("parallel", "arbitrary"),
    )(x2d, g, w_gu, w_gu, w_d)


def kernel(x, mem, g_mix, w_in, b_f, g_q, g_k, conv_w, conv_b, w_ra, b_ra, w_ri, b_ri, lam,
           g_fox_out, g_lru_out, w_out, g_xattn, g_mem, w_cq, w_ckv, g_cq, g_ck, w_co, g_ffn,
           w_gate_up, w_down):
    B, S, D = x.shape
    M = mem.shape[1]
    T = B * S
    depth = g_mix.shape[0]
    row = lambda v: v.reshape(1, -1).astype(F32)

    for l in range(depth):
        w_t, w_f = _stage_in_weights(jnp.swapaxes(w_in, 1, 2), l, tr=512)
        g_qk = jnp.stack([g_q[l] * FOX_Q_SCALE, g_k[l]]).reshape(2, 1, HEAD_DIM).astype(F32)
        bf_col = jnp.zeros((F_ROWS, 1), F32).at[:FOX_HEADS, 0].set(b_f[l])
        w_gates = jnp.concatenate([w_ra[l], w_ri[l]], axis=-1).astype(BF16)

        x2d = x.reshape(T, D)

        qk, v_t, ug, f_t, w_down_bf = _in_proj(x2d, row(g_mix[l]), w_t, g_qk, w_f, w_down, l,
                                               tm=1024, tn=1024, n_slabs=SIDE_CAST_SLABS,
                                               side_tc=D)
        o_fox, w_gu_bf = _fox_attention(qk.reshape(B, S, 2 * FOX_W), v_t, f_t, bf_col,
                                        w_gate_up, l, batch=B, seq=S,
                                        n_slabs=B * FOX_HEADS // HEADS_PER_STEP,
                                        side_tc=FFN_TILE)
        y_lru, (w_out_bf, w_cq_bf, w_ckv_bf, w_co_bf) = _rg_lru(
            ug.reshape(B, S, 2 * LRU_W), conv_w[l], row(conv_b[l]), w_gates, row(b_ra[l]),
            row(b_ri[l]), row(lam[l]), [w_out, w_cq, w_ckv, w_co], l, batch=B, seq=S)

        x3d = _mix_xattn(o_fox, y_lru, row(g_fox_out[l]), row(g_lru_out[l]), w_out_bf,
                         x2d.reshape(B, S, D), row(g_xattn[l]), w_cq_bf, row(g_cq[l]),
                         mem, row(g_mem[l]), w_ckv_bf, row(g_ck[l]), w_co_bf, tm=512)

        x2d = _ffn(x3d.reshape(T, D), row(g_ffn[l]), w_gu_bf, w_down_bf[0], tm=1024)
        x = x2d.reshape(B, S, D)
    return x
```

```python
import functools
import math

import jax
import jax.numpy as jnp
from jax import lax
from jax.experimental import pallas as pl
from jax.experimental.pallas import tpu as pltpu

F32 = jnp.float32
BF16 = jnp.bfloat16

HEAD_DIM = 128
FOX_HEADS = 8
FOX_W = FOX_HEADS * HEAD_DIM
LRU_BLOCKS = 8
LRU_BLOCK = 128
LRU_W = LRU_BLOCKS * LRU_BLOCK
LRU_C = 8.0
CONV_W = 4
XATT_HEADS = 4
XATT_W = XATT_HEADS * HEAD_DIM
RMS_EPS = 1e-6
F_ROWS = 16
Q_BLOCK = 512
K_CHUNK = 512
HEADS_PER_STEP = 2
SCORE_LOOKAHEAD = 4
V_PAD_ROWS = 16
PROJ_CHUNK = 256
SUBLANES = 8
LRU_BLOCKS_PER_STEP = 4
FFN_TILE = 512
SIDE_CAST_SLABS = 32
FOX_Q_SCALE = math.log2(math.e) / math.sqrt(HEAD_DIM)
BIAS_PIECES = 3

V7X_VMEM_LIMIT_BYTES = 60 * 1024 * 1024


def _rms(x, g):
    ms = jnp.mean(x * x, axis=-1, keepdims=True)
    return x * lax.rsqrt(ms + RMS_EPS) * g


def _dot(a, b):
    return jnp.dot(a, b, preferred_element_type=F32)


def _dot_nt(a, b):
    return lax.dot_general(a, b, (((1,), (1,)), ((), ())), preferred_element_type=F32)


def _softplus(x):
    return jnp.maximum(x, 0.0) + jnp.log1p(jnp.exp(-jnp.abs(x)))


def _cast_slab(step, n_slabs, src_ref, dst_ref):
    @pl.when(step < n_slabs)
    def _():
        n_tiles, _, tc = dst_ref.shape
        for t in range(n_tiles):
            dst_ref[t] = src_ref[:, t * tc:(t + 1) * tc].astype(BF16)


def _slab_specs(w, layer, n_slabs, tc, step_of):
    _, R, C = w.shape
    rows = R // n_slabs
    slab = lambda *ids: jnp.minimum(step_of(*ids), n_slabs - 1)
    return (pl.BlockSpec((None, rows, C), lambda *ids: (layer, slab(*ids), 0)),
            pl.BlockSpec((C // tc, rows, tc), lambda *ids: (0, slab(*ids), 0)),
            jax.ShapeDtypeStruct((C // tc, R, tc), BF16))


def _params(*sem):
    return pltpu.CompilerParams(dimension_semantics=sem,
                                vmem_limit_bytes=V7X_VMEM_LIMIT_BYTES)


def _wstage_kernel(a_ref, b_ref, w_ref, wf_ref, *, f_tile):
    t = pl.program_id(0)

    @pl.when(t < f_tile)
    def _():
        w_ref[...] = a_ref[...].astype(BF16)

    @pl.when(t >= f_tile)
    def _():
        w_ref[...] = jnp.concatenate([a_ref[FOX_HEADS:, :], b_ref[...]], axis=0).astype(BF16)

    @pl.when(t == f_tile)
    def _():
        wf_ref[:FOX_HEADS, :] = a_ref[:FOX_HEADS, :].astype(BF16)
        wf_ref[FOX_HEADS:, :] = jnp.zeros((F_ROWS - FOX_HEADS, wf_ref.shape[1]), BF16)


def _stage_in_weights(w_t, layer, *, tr):
    _, N, D = w_t.shape
    n_rows = N - FOX_HEADS
    f_tile = 3 * FOX_W // tr
    return pl.pallas_call(
        functools.partial(_wstage_kernel, f_tile=f_tile),
        grid=(n_rows // tr,),
        in_specs=[pl.BlockSpec((None, tr, D), lambda t: (layer, t, 0)),
                  pl.BlockSpec((None, FOX_HEADS, D),
                               lambda t: (layer, (jnp.maximum(t, f_tile) + 1) * (tr // FOX_HEADS), 0))],
        out_specs=[pl.BlockSpec((tr, D), lambda t: (t, 0)),
                   pl.BlockSpec((F_ROWS, D), lambda t: (0, 0))],
        out_shape=[jax.ShapeDtypeStruct((n_rows, D), BF16),
                   jax.ShapeDtypeStruct((F_ROWS, D), BF16)],
        compiler_params=_params("arbitrary"),
    )(w_t, w_t)


def _inproj_kernel(x_ref, g_ref, w_ref, gh_ref, wf_ref, side_ref,
                   qk_ref, vt_ref, ug_ref, f_ref, side_out_ref, h_scr,
                   *, tn, n_qk, n_v, n_steps, n_slabs):
    j = pl.program_id(1)
    _cast_slab(pl.program_id(0) * n_steps + j, n_slabs, side_ref, side_out_ref)

    chunks = [slice(c * PROJ_CHUNK, (c + 1) * PROJ_CHUNK) for c in range(tn // PROJ_CHUNK)]

    def qk_tile():
        for sl in chunks:
            acc = _dot_nt(h_scr[...], w_ref[sl, :])
            for hh in range(PROJ_CHUNK // HEAD_DIM):
                lo = sl.start + hh * HEAD_DIM
                qk_ref[:, lo:lo + HEAD_DIM] = _rms(
                    acc[:, hh * HEAD_DIM:(hh + 1) * HEAD_DIM], gh_ref[...]).astype(BF16)

    @pl.when(j == 0)
    def _():
        h_scr[...] = _rms(x_ref[...], g_ref[...]).astype(BF16)
        qk_tile()

    pl.when((j > 0) & (j < n_qk))(qk_tile)

    @pl.when((j >= n_qk) & (j < n_qk + n_v))
    def _():
        for sl in chunks[:-1]:
            vt_ref[sl, :] = _dot_nt(w_ref[sl, :], h_scr[...]).astype(BF16)
        sl = chunks[-1]
        acc = _dot_nt(jnp.concatenate([w_ref[sl, :], wf_ref[...]], axis=0), h_scr[...])
        vt_ref[sl, :] = acc[:PROJ_CHUNK].astype(BF16)
        f_ref[...] = acc[PROJ_CHUNK:]

    @pl.when(j >= n_qk + n_v)
    def _():
        for sl in chunks:
            ug_ref[:, sl] = _dot_nt(h_scr[...], w_ref[sl, :])


def _in_proj(x2d, g, w_t, g_qk, wf, w_side, layer, *, tm, tn, n_slabs, side_tc):
    T, D = x2d.shape
    n_qk = 2 * FOX_W // tn
    n_v = FOX_W // tn
    assert n_v == 1, "the forget logits are produced by the single v step"
    n_ug = 2 * LRU_W // tn
    tiles_per_gain = FOX_W // tn
    clip = lambda v, n: jnp.clip(v, 0, n - 1)
    n_steps = n_qk + n_v + n_ug
    side_in, side_out, side_shape = _slab_specs(w_side, layer, n_slabs, side_tc,
                                                lambda i, j: i * n_steps + j)
    return pl.pallas_call(
        functools.partial(_inproj_kernel, tn=tn, n_qk=n_qk, n_v=n_v, n_steps=n_steps,
                          n_slabs=n_slabs),
        grid=(T // tm, n_steps),
        in_specs=[
            pl.BlockSpec((tm, D), lambda i, j: (jnp.minimum(i + (j > 0), T // tm - 1), 0)),
            pl.BlockSpec((1, D), lambda i, j: (0, 0)),
            pl.BlockSpec((tn, D), lambda i, j: (j, 0)),
            pl.BlockSpec((None, 1, HEAD_DIM), lambda i, j: (clip(j // tiles_per_gain, 2), 0, 0)),
            pl.BlockSpec((F_ROWS, D), lambda i, j: (0, 0)),
            side_in,
        ],
        out_specs=[
            pl.BlockSpec((tm, tn), lambda i, j: (i, clip(j, n_qk))),
            pl.BlockSpec((tn, tm), lambda i, j: (clip(j - n_qk, n_v), i)),
            pl.BlockSpec((tm, tn), lambda i, j: (i, clip(j - n_qk - n_v, n_ug))),
            pl.BlockSpec((F_ROWS, tm), lambda i, j: (0, i)),
            side_out,
        ],
        out_shape=[
            jax.ShapeDtypeStruct((T, 2 * FOX_W), BF16),
            jax.ShapeDtypeStruct((FOX_W, T), BF16),
            jax.ShapeDtypeStruct((T, 2 * LRU_W), F32),
            jax.ShapeDtypeStruct((F_ROWS, T), F32),
            side_shape,
        ],
        scratch_shapes=[pltpu.VMEM((tm, D), BF16)],
        compiler_params=_params("arbitrary", "arbitrary"),
    )(x2d, g, w_t, g_qk, wf, w_side)


def _fox_kernel(q_ref, k_ref, vt_ref, f_ref, bf_ref, side_ref, o_ref, side_out_ref,
                kaug_scr, vaug_scr, *, seq, n_slabs):
    hp = pl.program_id(1)
    _cast_slab(pl.program_id(0) * (FOX_HEADS // HEADS_PER_STEP) + hp, n_slabs,
               side_ref, side_out_ref)

    @pl.when(hp == 0)
    def _():
        z = f_ref[...] + bf_ref[...]
        c = jnp.minimum(z, 0.0) - jnp.log1p(jnp.exp(-jnp.abs(z)))
        lane = lax.broadcasted_iota(jnp.int32, c.shape, 1)
        d = 1
        while d < seq:
            c = c + jnp.where(lane >= d, pltpu.roll(c, d, axis=1), 0.0)
            d *= 2
        bias = c[:FOX_HEADS] * (-math.log2(math.e))
        pieces = []
        for _ in range(BIAS_PIECES):
            p = bias.astype(BF16).astype(F32)
            pieces.append(p)
            bias = bias - p
        pad = jnp.zeros((HEAD_DIM - BIAS_PIECES * FOX_HEADS, seq), F32)
        bias_cols = jnp.concatenate(pieces + [pad], axis=0).T.astype(BF16)
        for e in range(HEADS_PER_STEP):
            kaug_scr[e, :, HEAD_DIM:] = bias_cols

    ones_row = lax.broadcasted_iota(jnp.int32, (V_PAD_ROWS, seq), 0) == 0
    for e in range(HEADS_PER_STEP):
        lanes = slice(e * HEAD_DIM, (e + 1) * HEAD_DIM)
        kaug_scr[e, :, :HEAD_DIM] = k_ref[:, lanes]
        vaug_scr[e, :HEAD_DIM, :] = vt_ref[lanes, :]
        vaug_scr[e, HEAD_DIM:, :] = jnp.where(ones_row, 1.0, 0.0).astype(BF16)

    tq, tk = Q_BLOCK, K_CHUNK
    lane = lax.broadcasted_iota(jnp.int32, (tq, HEAD_DIM), 1)
    selectors = []
    for e in range(HEADS_PER_STEP):
        h = hp * HEADS_PER_STEP + e
        mine = lane == h
        for p in range(1, BIAS_PIECES):
            mine = mine | (lane == h + p * FOX_HEADS)
        selectors.append(jnp.where(mine, 1.0, 0.0).astype(BF16))
    key = lax.broadcasted_iota(jnp.int32, (tk, tq), 0)
    qry = lax.broadcasted_iota(jnp.int32, (tk, tq), 1)

    def scores(e, lo, k0):
        q = q_ref[lo:lo + tq, e * HEAD_DIM:(e + 1) * HEAD_DIM]
        q_aug = jnp.concatenate([q, selectors[e]], axis=1)
        return _dot_nt(kaug_scr[e, k0:k0 + tk, :], q_aug)

    steps = [(e, qi * tq, kc * tk) for qi in range(seq // tq)
             for kc in range((qi + 1) * tq // tk) for e in range(HEADS_PER_STEP)]
    pending = [scores(*st) for st in steps[:SCORE_LOOKAHEAD]]
    m = [None] * HEADS_PER_STEP
    acc = [None] * HEADS_PER_STEP
    for idx, (e, lo, k0) in enumerate(steps):
        t = pending.pop(0)
        if idx + SCORE_LOOKAHEAD < len(steps):
            pending.append(scores(*steps[idx + SCORE_LOOKAHEAD]))
        if k0 + tk > lo:
            t = jnp.where(key + (k0 - lo) <= qry, t, -jnp.inf)
        cm = jnp.max(t, axis=0, keepdims=True)
        m_new = cm if k0 == 0 else jnp.maximum(m[e], cm)
        p = jnp.exp2(t - m_new).astype(BF16)
        pv = _dot(vaug_scr[e, :, k0:k0 + tk], p)
        if k0 == 0:
            acc[e] = pv
        else:
            acc[e] = acc[e] * jnp.exp2(m[e] - m_new) + pv
        m[e] = m_new
        if k0 + tk == lo + tq:
            inv_l = 1.0 / acc[e][HEAD_DIM:HEAD_DIM + 1, :]
            o_ref[lo:lo + tq, e * HEAD_DIM:(e + 1) * HEAD_DIM] = (acc[e][:HEAD_DIM, :] * inv_l).T


def _fox_attention(qk, v_t, f_t, b_f, w_side, layer, *, batch, seq, n_slabs, side_tc):
    n_pairs = FOX_HEADS // HEADS_PER_STEP
    wide = HEADS_PER_STEP * HEAD_DIM
    side_in, side_out, side_shape = _slab_specs(w_side, layer, n_slabs, side_tc,
                                                lambda b, hp: b * n_pairs + hp)
    return pl.pallas_call(
        functools.partial(_fox_kernel, seq=seq, n_slabs=n_slabs),
        grid=(batch, n_pairs),
        in_specs=[
            pl.BlockSpec((None, seq, wide), lambda b, hp: (b, 0, hp)),
            pl.BlockSpec((None, seq, wide), lambda b, hp: (b, 0, n_pairs + hp)),
            pl.BlockSpec((wide, seq), lambda b, hp: (hp, b)),
            pl.BlockSpec((F_ROWS, seq), lambda b, hp: (0, b)),
            pl.BlockSpec((F_ROWS, 1), lambda b, hp: (0, 0)),
            side_in,
        ],
        out_specs=[pl.BlockSpec((None, seq, wide), lambda b, hp: (b, 0, hp)), side_out],
        out_shape=[jax.ShapeDtypeStruct((batch, seq, FOX_W), F32), side_shape],
        scratch_shapes=[pltpu.VMEM((HEADS_PER_STEP, seq, 2 * HEAD_DIM), BF16),
                        pltpu.VMEM((HEADS_PER_STEP, HEAD_DIM + V_PAD_ROWS, seq), BF16)],
        compiler_params=_params("arbitrary", "arbitrary"),
    )(qk, qk, v_t, f_t, b_f, w_side)


def _lru_kernel(*refs, seq, n_side, n_slabs):
    u_refs = refs[:LRU_BLOCKS_PER_STEP]
    refs = refs[LRU_BLOCKS_PER_STEP:]
    gate_ref, cw_ref, cb_ref, wg_ref, bra_ref, bri_ref, lam_ref = refs[:7]
    side_refs = refs[7:7 + n_side]
    o_ref = refs[7 + n_side]
    side_out_refs = refs[8 + n_side:8 + 2 * n_side]
    at_scr, bt_scr, hp_scr = refs[8 + 2 * n_side:]
    step = pl.program_id(0) * pl.num_programs(1) + pl.program_id(1)
    for src_ref, dst_ref in zip(side_refs, side_out_refs):
        _cast_slab(step, n_slabs, src_ref, dst_ref)
    width = o_ref.shape[1]
    row = lax.broadcasted_iota(jnp.int32, (SUBLANES, LRU_BLOCK), 0)

    def shifted(d):
        cols = []
        for u_ref in u_refs:
            head = jnp.where(row >= d, pltpu.roll(u_ref[0:SUBLANES, :], d, axis=0), 0.0)
            cols.append(jnp.concatenate([head, u_ref[SUBLANES - d:seq - d, :]], axis=0))
        return jnp.concatenate(cols, axis=1)

    cw = cw_ref[...]
    uc = cw[0:1, :] * shifted(3)
    uc = uc + cw[1:2, :] * shifted(2)
    uc = uc + cw[2:3, :] * shifted(1)
    uc = uc + cw[3:4, :] * jnp.concatenate([u_ref[...] for u_ref in u_refs], axis=1)
    uc = cb_ref[...] + uc

    ucb = uc.astype(BF16)
    gates = [_dot(ucb[:, n * LRU_BLOCK:(n + 1) * LRU_BLOCK], wg_ref[n])
             for n in range(width // LRU_BLOCK)]
    pre_r = jnp.concatenate([g[:, :LRU_BLOCK] for g in gates], axis=1)
    pre_i = jnp.concatenate([g[:, LRU_BLOCK:] for g in gates], axis=1)
    half_c = (-0.5 * LRU_C) * _softplus(-lam_ref[...])
    log_a = half_c * jnp.tanh(0.5 * (pre_r + bra_ref[...])) + half_c
    i = 0.5 * jnp.tanh(0.5 * (pre_i + bri_ref[...])) + 0.5
    a = jnp.exp(log_a)
    t = jnp.tanh(log_a)
    w = -2.0 * t
    root = jnp.where(w > 0.0, w * lax.rsqrt(w * (1.0 - t)), 0.0)
    b = root * (i * uc)

    groups = seq // SUBLANES
    gshape = (groups, SUBLANES, width)
    a3 = a.reshape(gshape)
    b3 = b.reshape(gshape)
    sub = lax.broadcasted_iota(jnp.int32, gshape, 1)
    d = 1
    while d < SUBLANES:
        keep = sub >= d
        b3 = a3 * jnp.where(keep, pltpu.roll(b3, d, axis=1), 0.0) + b3
        a3 = a3 * jnp.where(keep, pltpu.roll(a3, d, axis=1), 1.0)
        d *= 2
    at_scr[...] = jnp.broadcast_to(a3[:, SUBLANES - 1:, :], gshape).reshape(seq, width)
    bt_scr[...] = jnp.broadcast_to(b3[:, SUBLANES - 1:, :], gshape).reshape(seq, width)

    def carry(g, h):
        rows = pl.ds(pl.multiple_of(g * SUBLANES, SUBLANES), SUBLANES)
        hp_scr[rows, :] = h
        return at_scr[rows, :] * h + bt_scr[rows, :]

    lax.fori_loop(0, groups, carry, jnp.zeros((SUBLANES, width), F32), unroll=8)
    hs = b3.reshape(seq, width) + a3.reshape(seq, width) * hp_scr[...]
    o_ref[...] = hs * jax.nn.gelu(gate_ref[...])


def _rg_lru(ug, cw, cb, wg, bra, bri, lam, side_weights, layer, *, batch, seq):
    per = LRU_BLOCKS_PER_STEP
    nb = LRU_BLOCKS // per
    width = per * LRU_BLOCK
    vec = lambda: pl.BlockSpec((1, width), lambda b, n: (0, n))
    n_slabs = batch * nb
    sides = [_slab_specs(w, layer, n_slabs, w.shape[2], lambda b, n: b * nb + n)
             for w in side_weights]
    res = pl.pallas_call(
        functools.partial(_lru_kernel, seq=seq, n_side=len(sides), n_slabs=n_slabs),
        grid=(batch, nb),
        in_specs=[
            *[pl.BlockSpec((None, seq, LRU_BLOCK), lambda b, n, e=e: (b, 0, per * n + e))
              for e in range(per)],
            pl.BlockSpec((None, seq, width), lambda b, n: (b, 0, nb + n)),
            pl.BlockSpec((CONV_W, width), lambda b, n: (0, n)),
            vec(),
            pl.BlockSpec((per, LRU_BLOCK, 2 * LRU_BLOCK), lambda b, n: (n, 0, 0)),
            vec(), vec(), vec(),
            *[sd[0] for sd in sides],
        ],
        out_specs=[pl.BlockSpec((None, seq, width), lambda b, n: (b, 0, n)),
                   *[sd[1] for sd in sides]],
        out_shape=[jax.ShapeDtypeStruct((batch, seq, LRU_W), F32), *[sd[2] for sd in sides]],
        scratch_shapes=[pltpu.VMEM((seq, width), F32)] * 3,
        compiler_params=_params("arbitrary", "arbitrary"),
    )(*[ug] * per, ug, cw, cb, wg, bra, bri, lam, *side_weights)
    return res[0], [r[0] for r in res[1:]]


def _mix_xattn_kernel(of_ref, yl_ref, gf_ref, gl_ref, wout_ref, x_ref, gx_ref, wq_ref, gq_ref,
                      mem_ref, gm_ref, wkv_ref, gk_ref, wo_ref, o_ref, x1_scr, ckv_ref, *, chunk):
    @pl.when(pl.program_id(1) == 0)
    def _():
        kv = _dot(_rms(mem_ref[...], gm_ref[...]).astype(BF16), wkv_ref[...])
        for hh in range(XATT_HEADS):
            sl = slice(hh * HEAD_DIM, (hh + 1) * HEAD_DIM)
            ckv_ref[:, sl] = _rms(kv[:, sl], gk_ref[...]).astype(BF16)
        ckv_ref[:, XATT_W:] = kv[:, XATT_W:].astype(BF16)

    mf = _rms(of_ref[...], gf_ref[...]).astype(BF16)
    ml = _rms(yl_ref[...], gl_ref[...]).astype(BF16)
    for c in range(o_ref.shape[1] // chunk):
        sl = slice(c * chunk, (c + 1) * chunk)
        acc = _dot(mf, wout_ref[:FOX_W, sl]) + _dot(ml, wout_ref[FOX_W:, sl])
        x1_scr[:, sl] = x_ref[:, sl] + acc

    x1 = x1_scr[...]
    cq = _dot(_rms(x1, gx_ref[...]).astype(BF16), wq_ref[...])
    scale = 1.0 / math.sqrt(HEAD_DIM)
    scores = []
    for hh in range(XATT_HEADS):
        sl = slice(hh * HEAD_DIM, (hh + 1) * HEAD_DIM)
        qh = _rms(cq[:, sl], gq_ref[...]).astype(BF16)
        scores.append(_dot_nt(qh, ckv_ref[:, sl]))
    heads = []
    for hh in range(XATT_HEADS):
        vh = ckv_ref[:, XATT_W + hh * HEAD_DIM:XATT_W + (hh + 1) * HEAD_DIM]
        s = scores[hh] * scale
        e = jnp.exp(s - jnp.max(s, axis=-1, keepdims=True))
        l = jnp.sum(e, axis=-1, keepdims=True)
        heads.append((_dot(e.astype(BF16), vh) / l).astype(BF16))
    ox = jnp.concatenate(heads, axis=-1)
    o_ref[...] = x1 + _dot(ox, wo_ref[...])


def _mix_xattn(o_fox, y_lru, gf, gl, w_out, x3d, gx, w_cq, gq, mem, gm, w_ckv, gk, w_co, *, tm):
    B, S, D = x3d.shape
    M = mem.shape[1]
    const = lambda shape: pl.BlockSpec(shape, lambda b, i: (0,) * len(shape),
                                       pipeline_mode=pl.Buffered(1))
    tile = lambda w: pl.BlockSpec((None, tm, w), lambda b, i: (b, i, 0))
    return pl.pallas_call(
        functools.partial(_mix_xattn_kernel, chunk=2 * PROJ_CHUNK),
        grid=(B, S // tm),
        in_specs=[
            tile(FOX_W), tile(LRU_W), const((1, FOX_W)), const((1, LRU_W)),
            const((FOX_W + LRU_W, D)), tile(D), const((1, D)), const((D, XATT_W)),
            const((1, HEAD_DIM)),
            pl.BlockSpec((None, M, D), lambda b, i: (b, 0, 0)),
            const((1, D)), const((D, 2 * XATT_W)), const((1, HEAD_DIM)),
            const((XATT_W, D)),
        ],
        out_specs=tile(D),
        out_shape=jax.ShapeDtypeStruct((B, S, D), F32),
        scratch_shapes=[pltpu.VMEM((tm, D), F32), pltpu.VMEM((M, 2 * XATT_W), BF16)],
        compiler_params=_params("parallel", "arbitrary"),
    )(o_fox, y_lru, gf, gl, w_out, x3d, gx, w_cq, gq, mem, gm, w_ckv, gk, w_co)


def _ffn_kernel(x_ref, g_ref, wg_ref, wu_ref, wd_ref, o_ref, h_scr):
    k = pl.program_id(1)

    @pl.when(k == 0)
    def _():
        h_scr[...] = _rms(x_ref[...], g_ref[...]).astype(BF16)

    def hidden_tile(base_ref):
        h = h_scr[...]
        gate = _dot(h, wg_ref[...])
        up = _dot(h, wu_ref[...])
        act = (jax.nn.silu(gate) * up).astype(BF16)
        o_ref[...] = base_ref[...] + _dot(act, wd_ref[...])

    pl.when(k == 0)(lambda: hidden_tile(x_ref))
    pl.when(k > 0)(lambda: hidden_tile(o_ref))


def _ffn(x2d, g, w_gu, w_d, *, tm):
    T, D = x2d.shape
    H = w_d.shape[0]
    th = w_gu.shape[2]
    nk = H // th
    return pl.pallas_call(
        _ffn_kernel,
        grid=(T // tm, nk),
        in_specs=[
            pl.BlockSpec((tm, D), lambda i, k: (jnp.minimum(i + (k > 0), T // tm - 1), 0)),
            pl.BlockSpec((1, D), lambda i, k: (0, 0)),
            pl.BlockSpec((None, D, th), lambda i, k: (k, 0, 0)),
            pl.BlockSpec((None, D, th), lambda i, k: (nk + k, 0, 0)),
            pl.BlockSpec((th, D), lambda i, k: (k, 0)),
        ],
        out_specs=pl.BlockSpec((tm, D), lambda i, k: (i, 0)),
        out_shape=jax.ShapeDtypeStruct((T, D), F32),
        scratch_shapes=[pltpu.VMEM((tm, D), BF16)],
        compiler_params=_params("parallel", "arbitrary"),
    )(x2d, g, w_gu, w_gu, w_d)


def kernel(x, mem, g_mix, w_in, b_f, g_q, g_k, conv_w, conv_b, w_ra, b_ra, w_ri, b_ri, lam,
           g_fox_out, g_lru_out, w_out, g_xattn, g_mem, w_cq, w_ckv, g_cq, g_ck, w_co, g_ffn,
           w_gate_up, w_down):
    B, S, D = x.shape
    T = B * S
    depth = g_mix.shape[0]
    row = lambda v: v.reshape(1, -1).astype(F32)

    for l in range(depth):
        w_t, w_f = _stage_in_weights(jnp.swapaxes(w_in, 1, 2), l, tr=512)
        g_qk = jnp.stack([g_q[l] * FOX_Q_SCALE, g_k[l]]).reshape(2, 1, HEAD_DIM).astype(F32)
        bf_col = jnp.zeros((F_ROWS, 1), F32).at[:FOX_HEADS, 0].set(b_f[l])
        w_gates = jnp.concatenate([w_ra[l], w_ri[l]], axis=-1).astype(BF16)

        x2d = x.reshape(T, D)

        qk, v_t, ug, f_t, w_down_bf = _in_proj(x2d, row(g_mix[l]), w_t, g_qk, w_f, w_down, l,
                                               tm=1024, tn=1024, n_slabs=SIDE_CAST_SLABS,
                                               side_tc=D)
        o_fox, w_gu_bf = _fox_attention(qk.reshape(B, S, 2 * FOX_W), v_t, f_t, bf_col,
                                        w_gate_up, l, batch=B, seq=S,
                                        n_slabs=B * FOX_HEADS // HEADS_PER_STEP,
                                        side_tc=FFN_TILE)
        y_lru, (w_out_bf, w_cq_bf, w_ckv_bf, w_co_bf) = _rg_lru(
            ug.reshape(B, S, 2 * LRU_W), conv_w[l], row(conv_b[l]), w_gates, row(b_ra[l]),
            row(b_ri[l]), row(lam[l]), [w_out, w_cq, w_ckv, w_co], l, batch=B, seq=S)

        x3d = _mix_xattn(o_fox, y_lru, row(g_fox_out[l]), row(g_lru_out[l]), w_out_bf,
                         x2d.reshape(B, S, D), row(g_xattn[l]), w_cq_bf, row(g_cq[l]),
                         mem, row(g_mem[l]), w_ckv_bf, row(g_ck[l]), w_co_bf, tm=512)

        x2d = _ffn(x3d.reshape(T, D), row(g_ffn[l]), w_gu_bf, w_down_bf[0], tm=1024)
        x = x2d.reshape(B, S, D)
    return x
```

```python
import functools
import math

import jax
import jax.numpy as jnp
from jax import lax
from jax.experimental import pallas as pl
from jax.experimental.pallas import tpu as pltpu

F32 = jnp.float32
BF16 = jnp.bfloat16

HEAD_DIM = 128
FOX_HEADS = 8
FOX_W = FOX_HEADS * HEAD_DIM
LRU_BLOCKS = 8
LRU_BLOCK = 128
LRU_W = LRU_BLOCKS * LRU_BLOCK
LRU_C = 8.0
CONV_W = 4
XATT_HEADS = 4
XATT_W = XATT_HEADS * HEAD_DIM
RMS_EPS = 1e-6
F_ROWS = 16
Q_BLOCK = 512
K_CHUNK = 512
HEADS_PER_STEP = 2
SCORE_LOOKAHEAD = 4
V_PAD_ROWS = 16
PROJ_CHUNK = 256
SUBLANES = 8
LRU_BLOCKS_PER_STEP = 4
FFN_TILE = 512
SIDE_CAST_SLABS = 32
FOX_Q_SCALE = math.log2(math.e) / math.sqrt(HEAD_DIM)
BIAS_PIECES = 3

V7X_VMEM_LIMIT_BYTES = 60 * 1024 * 1024


def _rms(x, g):
    ms = jnp.mean(x * x, axis=-1, keepdims=True)
    return x * lax.rsqrt(ms + RMS_EPS) * g


def _dot(a, b):
    return jnp.dot(a, b, preferred_element_type=F32)


def _dot_nt(a, b):
    return lax.dot_general(a, b, (((1,), (1,)), ((), ())), preferred_element_type=F32)


def _softplus(x):
    return jnp.maximum(x, 0.0) + jnp.log1p(jnp.exp(-jnp.abs(x)))


def _cast_slab(src_ref, dst_ref):
    n_tiles, _, tc = dst_ref.shape
    for t in range(n_tiles):
        dst_ref[t] = src_ref[:, t * tc:(t + 1) * tc].astype(BF16)


def _slab_specs(w, layer, n_slabs, tc, step_of):
    _, R, C = w.shape
    rows = R // n_slabs
    slab = lambda *ids: jnp.minimum(step_of(*ids), n_slabs - 1)
    return (pl.BlockSpec((None, rows, C), lambda *ids: (layer, slab(*ids), 0)),
            pl.BlockSpec((C // tc, rows, tc), lambda *ids: (0, slab(*ids), 0)),
            jax.ShapeDtypeStruct((C // tc, R, tc), BF16))


def _params(*sem):
    return pltpu.CompilerParams(dimension_semantics=sem,
                                vmem_limit_bytes=V7X_VMEM_LIMIT_BYTES)


def _wstage_kernel(a_ref, b_ref, w_ref, wf_ref, *, f_tile):
    t = pl.program_id(0)

    @pl.when(t < f_tile)
    def _():
        w_ref[...] = a_ref[...].astype(BF16)

    @pl.when(t >= f_tile)
    def _():
        w_ref[...] = jnp.concatenate([a_ref[FOX_HEADS:, :], b_ref[...]], axis=0).astype(BF16)

    @pl.when(t == f_tile)
    def _():
        wf_ref[:FOX_HEADS, :] = a_ref[:FOX_HEADS, :].astype(BF16)
        wf_ref[FOX_HEADS:, :] = jnp.zeros((F_ROWS - FOX_HEADS, wf_ref.shape[1]), BF16)


def _stage_in_weights(w_t, layer, *, tr):
    _, N, D = w_t.shape
    n_rows = N - FOX_HEADS
    f_tile = 3 * FOX_W // tr
    return pl.pallas_call(
        functools.partial(_wstage_kernel, f_tile=f_tile),
        grid=(n_rows // tr,),
        in_specs=[pl.BlockSpec((None, tr, D), lambda t: (layer, t, 0)),
                  pl.BlockSpec((None, FOX_HEADS, D),
                               lambda t: (layer, (jnp.maximum(t, f_tile) + 1) * (tr // FOX_HEADS), 0))],
        out_specs=[pl.BlockSpec((tr, D), lambda t: (t, 0)),
                   pl.BlockSpec((F_ROWS, D), lambda t: (0, 0))],
        out_shape=[jax.ShapeDtypeStruct((n_rows, D), BF16),
                   jax.ShapeDtypeStruct((F_ROWS, D), BF16)],
        compiler_params=_params("arbitrary"),
    )(w_t, w_t)


def _inproj_kernel(x_ref, g_ref, w_ref, gh_ref, wf_ref, side_ref,
                   qk_ref, vt_ref, ug_ref, f_ref, side_out_ref, h_scr,
                   *, tn, n_qk, n_v):
    j = pl.program_id(1)

    chunks = [slice(c * PROJ_CHUNK, (c + 1) * PROJ_CHUNK) for c in range(tn // PROJ_CHUNK)]

    def qk_tile():
        _cast_slab(side_ref, side_out_ref)
        for sl in chunks:
            acc = _dot_nt(h_scr[...], w_ref[sl, :])
            for hh in range(PROJ_CHUNK // HEAD_DIM):
                lo = sl.start + hh * HEAD_DIM
                qk_ref[:, lo:lo + HEAD_DIM] = _rms(
                    acc[:, hh * HEAD_DIM:(hh + 1) * HEAD_DIM], gh_ref[...]).astype(BF16)

    @pl.when(j == 0)
    def _():
        h_scr[...] = _rms(x_ref[...], g_ref[...]).astype(BF16)
        qk_tile()

    pl.when((j > 0) & (j < n_qk))(qk_tile)

    @pl.when((j >= n_qk) & (j < n_qk + n_v))
    def _():
        _cast_slab(side_ref, side_out_ref)
        for sl in chunks[:-1]:
            vt_ref[sl, :] = _dot_nt(w_ref[sl, :], h_scr[...]).astype(BF16)
        sl = chunks[-1]
        acc = _dot_nt(jnp.concatenate([w_ref[sl, :], wf_ref[...]], axis=0), h_scr[...])
        vt_ref[sl, :] = acc[:PROJ_CHUNK].astype(BF16)
        f_ref[...] = acc[PROJ_CHUNK:]

    @pl.when(j >= n_qk + n_v)
    def _():
        _cast_slab(side_ref, side_out_ref)
        for sl in chunks:
            ug_ref[:, sl] = _dot_nt(h_scr[...], w_ref[sl, :])


def _in_proj(x2d, g, w_t, g_qk, wf, w_side, layer, *, tm, tn, n_slabs, side_tc):
    T, D = x2d.shape
    n_qk = 2 * FOX_W // tn
    n_v = FOX_W // tn
    assert n_v == 1, "the forget logits are produced by the single v step"
    n_ug = 2 * LRU_W // tn
    tiles_per_gain = FOX_W // tn
    clip = lambda v, n: jnp.clip(v, 0, n - 1)
    n_steps = n_qk + n_v + n_ug
    side_in, side_out, side_shape = _slab_specs(w_side, layer, n_slabs, side_tc,
                                                lambda i, j: i * n_steps + j)
    return pl.pallas_call(
        functools.partial(_inproj_kernel, tn=tn, n_qk=n_qk, n_v=n_v),
        grid=(T // tm, n_steps),
        in_specs=[
            pl.BlockSpec((tm, D), lambda i, j: (jnp.minimum(i + (j > 0), T // tm - 1), 0)),
            pl.BlockSpec((1, D), lambda i, j: (0, 0)),
            pl.BlockSpec((tn, D), lambda i, j: (j, 0)),
            pl.BlockSpec((None, 1, HEAD_DIM), lambda i, j: (clip(j // tiles_per_gain, 2), 0, 0)),
            pl.BlockSpec((F_ROWS, D), lambda i, j: (0, 0)),
            side_in,
        ],
        out_specs=[
            pl.BlockSpec((tm, tn), lambda i, j: (i, clip(j, n_qk))),
            pl.BlockSpec((tn, tm), lambda i, j: (clip(j - n_qk, n_v), i)),
            pl.BlockSpec((tm, tn), lambda i, j: (i, clip(j - n_qk - n_v, n_ug))),
            pl.BlockSpec((F_ROWS, tm), lambda i, j: (0, i)),
            side_out,
        ],
        out_shape=[
            jax.ShapeDtypeStruct((T, 2 * FOX_W), BF16),
            jax.ShapeDtypeStruct((FOX_W, T), BF16),
            jax.ShapeDtypeStruct((T, 2 * LRU_W), F32),
            jax.ShapeDtypeStruct((F_ROWS, T), F32),
            side_shape,
        ],
        scratch_shapes=[pltpu.VMEM((tm, D), BF16)],
        compiler_params=_params("arbitrary", "arbitrary"),
    )(x2d, g, w_t, g_qk, wf, w_side)


def _fox_kernel(q_ref, k_ref, vt_ref, f_ref, bf_ref, side_ref, o_ref, side_out_ref,
                bias_scr, vaug_scr, *, seq):
    hp = pl.program_id(1)

    @pl.when(hp == 0)
    def _():
        z = f_ref[...] + bf_ref[...]
        c = jnp.minimum(z, 0.0) - jnp.log1p(jnp.exp(-jnp.abs(z)))
        lane = lax.broadcasted_iota(jnp.int32, c.shape, 1)
        d = 1
        while d < seq:
            c = c + jnp.where(lane >= d, pltpu.roll(c, d, axis=1), 0.0)
            d *= 2
        bias = c[:FOX_HEADS] * (-math.log2(math.e))
        pieces = []
        for _ in range(BIAS_PIECES):
            p = bias.astype(BF16).astype(F32)
            pieces.append(p)
            bias = bias - p
        pad = jnp.zeros((HEAD_DIM - BIAS_PIECES * FOX_HEADS, seq), F32)
        bias_scr[...] = jnp.concatenate(pieces + [pad], axis=0).T.astype(BF16)

    _cast_slab(side_ref, side_out_ref)
    ones_row = lax.broadcasted_iota(jnp.int32, (V_PAD_ROWS, seq), 0) == 0
    for e in range(HEADS_PER_STEP):
        lanes = slice(e * HEAD_DIM, (e + 1) * HEAD_DIM)
        vaug_scr[e, :HEAD_DIM, :] = vt_ref[lanes, :]
        vaug_scr[e, HEAD_DIM:, :] = jnp.where(ones_row, 1.0, 0.0).astype(BF16)

    tq, tk = Q_BLOCK, K_CHUNK
    lane = lax.broadcasted_iota(jnp.int32, (tq, HEAD_DIM), 1)
    selectors = []
    for e in range(HEADS_PER_STEP):
        h = hp * HEADS_PER_STEP + e
        mine = lane == h
        for p in range(1, BIAS_PIECES):
            mine = mine | (lane == h + p * FOX_HEADS)
        selectors.append(jnp.where(mine, 1.0, 0.0).astype(BF16))
    key = lax.broadcasted_iota(jnp.int32, (tk, tq), 0)
    qry = lax.broadcasted_iota(jnp.int32, (tk, tq), 1)

    def scores(e, lo, k0):
        q = q_ref[lo:lo + tq, e * HEAD_DIM:(e + 1) * HEAD_DIM]
        q_aug = jnp.concatenate([q, selectors[e]], axis=1)
        k_aug = jnp.concatenate([k_ref[k0:k0 + tk, e * HEAD_DIM:(e + 1) * HEAD_DIM],
                                 bias_scr[k0:k0 + tk, :]], axis=1)
        return _dot_nt(k_aug, q_aug)

    steps = [(e, qi * tq, kc * tk) for qi in range(seq // tq)
             for kc in range((qi + 1) * tq // tk) for e in range(HEADS_PER_STEP)]
    pending = [scores(*st) for st in steps[:SCORE_LOOKAHEAD]]
    m = [None] * HEADS_PER_STEP
    acc = [None] * HEADS_PER_STEP
    for idx, (e, lo, k0) in enumerate(steps):
        t = pending.pop(0)
        if idx + SCORE_LOOKAHEAD < len(steps):
            pending.append(scores(*steps[idx + SCORE_LOOKAHEAD]))
        if k0 + tk > lo:
            t = jnp.where(key + (k0 - lo) <= qry, t, -jnp.inf)
        cm = jnp.max(t, axis=0, keepdims=True)
        m_new = cm if k0 == 0 else jnp.maximum(m[e], cm)
        p = jnp.exp2(t - m_new).astype(BF16)
        pv = _dot(vaug_scr[e, :, k0:k0 + tk], p)
        if k0 == 0:
            acc[e] = pv
        else:
            acc[e] = acc[e] * jnp.exp2(m[e] - m_new) + pv
        m[e] = m_new
        if k0 + tk == lo + tq:
            inv_l = 1.0 / acc[e][HEAD_DIM:HEAD_DIM + 1, :]
            o_ref[lo:lo + tq, e * HEAD_DIM:(e + 1) * HEAD_DIM] = (acc[e][:HEAD_DIM, :] * inv_l).T


def _fox_attention(qk, v_t, f_t, b_f, w_side, layer, *, batch, seq, n_slabs, side_tc):
    n_pairs = FOX_HEADS // HEADS_PER_STEP
    wide = HEADS_PER_STEP * HEAD_DIM
    side_in, side_out, side_shape = _slab_specs(w_side, layer, n_slabs, side_tc,
                                                lambda b, hp: b * n_pairs + hp)
    return pl.pallas_call(
        functools.partial(_fox_kernel, seq=seq),
        grid=(batch, n_pairs),
        in_specs=[
            pl.BlockSpec((None, seq, wide), lambda b, hp: (b, 0, hp)),
            pl.BlockSpec((None, seq, wide), lambda b, hp: (b, 0, n_pairs + hp)),
            pl.BlockSpec((wide, seq), lambda b, hp: (hp, b)),
            pl.BlockSpec((F_ROWS, seq), lambda b, hp: (0, b)),
            pl.BlockSpec((F_ROWS, 1), lambda b, hp: (0, 0)),
            side_in,
        ],
        out_specs=[pl.BlockSpec((None, seq, wide), lambda b, hp: (b, 0, hp)), side_out],
        out_shape=[jax.ShapeDtypeStruct((batch, seq, FOX_W), F32), side_shape],
        scratch_shapes=[pltpu.VMEM((seq, HEAD_DIM), BF16),
                        pltpu.VMEM((HEADS_PER_STEP, HEAD_DIM + V_PAD_ROWS, seq), BF16)],
        compiler_params=_params("arbitrary", "arbitrary"),
    )(qk, qk, v_t, f_t, b_f, w_side)


def _lru_kernel(*refs, seq, n_side):
    u_refs = refs[:LRU_BLOCKS_PER_STEP]
    refs = refs[LRU_BLOCKS_PER_STEP:]
    gate_ref, cw_ref, cb_ref, wg_ref, bra_ref, bri_ref, lam_ref = refs[:7]
    side_refs = refs[7:7 + n_side]
    o_ref = refs[7 + n_side]
    side_out_refs = refs[8 + n_side:8 + 2 * n_side]
    at_scr, bt_scr, hp_scr = refs[8 + 2 * n_side:]
    for src_ref, dst_ref in zip(side_refs, side_out_refs):
        _cast_slab(src_ref, dst_ref)
    width = o_ref.shape[1]
    row = lax.broadcasted_iota(jnp.int32, (SUBLANES, LRU_BLOCK), 0)

    def shifted(d):
        cols = []
        for u_ref in u_refs:
            head = jnp.where(row >= d, pltpu.roll(u_ref[0:SUBLANES, :], d, axis=0), 0.0)
            cols.append(jnp.concatenate([head, u_ref[SUBLANES - d:seq - d, :]], axis=0))
        return jnp.concatenate(cols, axis=1)

    cw = cw_ref[...]
    uc = cw[0:1, :] * shifted(3)
    uc = uc + cw[1:2, :] * shifted(2)
    uc = uc + cw[2:3, :] * shifted(1)
    uc = uc + cw[3:4, :] * jnp.concatenate([u_ref[...] for u_ref in u_refs], axis=1)
    uc = cb_ref[...] + uc

    ucb = uc.astype(BF16)
    gates = [_dot(ucb[:, n * LRU_BLOCK:(n + 1) * LRU_BLOCK], wg_ref[n])
             for n in range(width // LRU_BLOCK)]
    pre_r = jnp.concatenate([g[:, :LRU_BLOCK] for g in gates], axis=1)
    pre_i = jnp.concatenate([g[:, LRU_BLOCK:] for g in gates], axis=1)
    half_c = (-0.5 * LRU_C) * _softplus(-lam_ref[...])
    log_a = half_c * jnp.tanh(0.5 * (pre_r + bra_ref[...])) + half_c
    i = 0.5 * jnp.tanh(0.5 * (pre_i + bri_ref[...])) + 0.5
    a = jnp.exp(log_a)
    t = jnp.tanh(log_a)
    w = -2.0 * t
    root = jnp.where(w > 0.0, w * lax.rsqrt(w * (1.0 - t)), 0.0)
    b = root * (i * uc)

    groups = seq // SUBLANES
    gshape = (groups, SUBLANES, width)
    a3 = a.reshape(gshape)
    b3 = b.reshape(gshape)
    sub = lax.broadcasted_iota(jnp.int32, gshape, 1)
    d = 1
    while d < SUBLANES:
        keep = sub >= d
        b3 = a3 * jnp.where(keep, pltpu.roll(b3, d, axis=1), 0.0) + b3
        a3 = a3 * jnp.where(keep, pltpu.roll(a3, d, axis=1), 1.0)
        d *= 2
    at_scr[...] = jnp.broadcast_to(a3[:, SUBLANES - 1:, :], gshape).reshape(seq, width)
    bt_scr[...] = jnp.broadcast_to(b3[:, SUBLANES - 1:, :], gshape).reshape(seq, width)

    def carry(g, h):
        rows = pl.ds(pl.multiple_of(g * SUBLANES, SUBLANES), SUBLANES)
        hp_scr[rows, :] = h
        return at_scr[rows, :] * h + bt_scr[rows, :]

    lax.fori_loop(0, groups, carry, jnp.zeros((SUBLANES, width), F32), unroll=8)
    hs = b3.reshape(seq, width) + a3.reshape(seq, width) * hp_scr[...]
    o_ref[...] = hs * jax.nn.gelu(gate_ref[...])


def _rg_lru(ug, cw, cb, wg, bra, bri, lam, side_weights, layer, *, batch, seq):
    per = LRU_BLOCKS_PER_STEP
    nb = LRU_BLOCKS // per
    width = per * LRU_BLOCK
    vec = lambda: pl.BlockSpec((1, width), lambda b, n: (0, n))
    n_slabs = batch * nb
    sides = [_slab_specs(w, layer, n_slabs, w.shape[2], lambda b, n: b * nb + n)
             for w in side_weights]
    res = pl.pallas_call(
        functools.partial(_lru_kernel, seq=seq, n_side=len(sides)),
        grid=(batch, nb),
        in_specs=[
            *[pl.BlockSpec((None, seq, LRU_BLOCK), lambda b, n, e=e: (b, 0, per * n + e))
              for e in range(per)],
            pl.BlockSpec((None, seq, width), lambda b, n: (b, 0, nb + n)),
            pl.BlockSpec((CONV_W, width), lambda b, n: (0, n)),
            vec(),
            pl.BlockSpec((per, LRU_BLOCK, 2 * LRU_BLOCK), lambda b, n: (n, 0, 0)),
            vec(), vec(), vec(),
            *[sd[0] for sd in sides],
        ],
        out_specs=[pl.BlockSpec((None, seq, width), lambda b, n: (b, 0, n)),
                   *[sd[1] for sd in sides]],
        out_shape=[jax.ShapeDtypeStruct((batch, seq, LRU_W), F32), *[sd[2] for sd in sides]],
        scratch_shapes=[pltpu.VMEM((seq, width), F32)] * 3,
        compiler_params=_params("arbitrary", "arbitrary"),
    )(*[ug] * per, ug, cw, cb, wg, bra, bri, lam, *side_weights)
    return res[0], [r[0] for r in res[1:]]


def _mix_xattn_kernel(of_ref, yl_ref, gf_ref, gl_ref, wout_ref, x_ref, gx_ref, wq_ref, gq_ref,
                      mem_ref, gm_ref, wkv_ref, gk_ref, wo_ref, o_ref, x1_scr, ckv_ref, *, chunk):
    @pl.when(pl.program_id(1) == 0)
    def _():
        kv = _dot(_rms(mem_ref[...], gm_ref[...]).astype(BF16), wkv_ref[...])
        for hh in range(XATT_HEADS):
            sl = slice(hh * HEAD_DIM, (hh + 1) * HEAD_DIM)
            ckv_ref[:, sl] = _rms(kv[:, sl], gk_ref[...]).astype(BF16)
        ckv_ref[:, XATT_W:] = kv[:, XATT_W:].astype(BF16)

    mf = _rms(of_ref[...], gf_ref[...]).astype(BF16)
    ml = _rms(yl_ref[...], gl_ref[...]).astype(BF16)
    for c in range(o_ref.shape[1] // chunk):
        sl = slice(c * chunk, (c + 1) * chunk)
        acc = _dot(mf, wout_ref[:FOX_W, sl]) + _dot(ml, wout_ref[FOX_W:, sl])
        x1_scr[:, sl] = x_ref[:, sl] + acc

    x1 = x1_scr[...]
    cq = _dot(_rms(x1, gx_ref[...]).astype(BF16), wq_ref[...])
    scale = 1.0 / math.sqrt(HEAD_DIM)
    scores = []
    for hh in range(XATT_HEADS):
        sl = slice(hh * HEAD_DIM, (hh + 1) * HEAD_DIM)
        qh = _rms(cq[:, sl], gq_ref[...]).astype(BF16)
        scores.append(_dot_nt(qh, ckv_ref[:, sl]))
    heads = []
    for hh in range(XATT_HEADS):
        vh = ckv_ref[:, XATT_W + hh * HEAD_DIM:XATT_W + (hh + 1) * HEAD_DIM]
        s = scores[hh] * scale
        e = jnp.exp(s - jnp.max(s, axis=-1, keepdims=True))
        l = jnp.sum(e, axis=-1, keepdims=True)
        heads.append((_dot(e.astype(BF16), vh) / l).astype(BF16))
    ox = jnp.concatenate(heads, axis=-1)
    o_ref[...] = x1 + _dot(ox, wo_ref[...])


def _mix_xattn(o_fox, y_lru, gf, gl, w_out, x3d, gx, w_cq, gq, mem, gm, w_ckv, gk, w_co, *, tm):
    B, S, D = x3d.shape
    M = mem.shape[1]
    const = lambda shape: pl.BlockSpec(shape, lambda b, i: (0,) * len(shape),
                                       pipeline_mode=pl.Buffered(1))
    tile = lambda w: pl.BlockSpec((None, tm, w), lambda b, i: (b, i, 0))
    return pl.pallas_call(
        functools.partial(_mix_xattn_kernel, chunk=2 * PROJ_CHUNK),
        grid=(B, S // tm),
        in_specs=[
            tile(FOX_W), tile(LRU_W), const((1, FOX_W)), const((1, LRU_W)),
            const((FOX_W + LRU_W, D)), tile(D), const((1, D)), const((D, XATT_W)),
            const((1, HEAD_DIM)),
            pl.BlockSpec((None, M, D), lambda b, i: (b, 0, 0)),
            const((1, D)), const((D, 2 * XATT_W)), const((1, HEAD_DIM)),
            const((XATT_W, D)),
        ],
        out_specs=tile(D),
        out_shape=jax.ShapeDtypeStruct((B, S, D), F32),
        scratch_shapes=[pltpu.VMEM((tm, D), F32), pltpu.VMEM((M, 2 * XATT_W), BF16)],
        compiler_params=_params("parallel", "arbitrary"),
    )(o_fox, y_lru, gf, gl, w_out, x3d, gx, w_cq, gq, mem, gm, w_ckv, gk, w_co)


def _ffn_kernel(x_ref, g_ref, wg_ref, wu_ref, wd_ref, o_ref, h_scr):
    k = pl.program_id(1)

    @pl.when(k == 0)
    def _():
        h_scr[...] = _rms(x_ref[...], g_ref[...]).astype(BF16)

    def hidden_tile(base_ref):
        h = h_scr[...]
        gate = _dot(h, wg_ref[...])
        up = _dot(h, wu_ref[...])
        act = (jax.nn.silu(gate) * up).astype(BF16)
        o_ref[...] = base_ref[...] + _dot(act, wd_ref[...])

    pl.when(k == 0)(lambda: hidden_tile(x_ref))
    pl.when(k > 0)(lambda: hidden_tile(o_ref))


def _ffn(x2d, g, w_gu, w_d, *, tm):
    T, D = x2d.shape
    H = w_d.shape[0]
    th = w_gu.shape[2]
    nk = H // th
    return pl.pallas_call(
        _ffn_kernel,
        grid=(T // tm, nk),
        in_specs=[
            pl.BlockSpec((tm, D), lambda i, k: (jnp.minimum(i + (k > 0), T // tm - 1), 0)),
            pl.BlockSpec((1, D), lambda i, k: (0, 0)),
            pl.BlockSpec((None, D, th), lambda i, k: (k, 0, 0)),
            pl.BlockSpec((None, D, th), lambda i, k: (nk + k, 0, 0)),
            pl.BlockSpec((th, D), lambda i, k: (k, 0)),
        ],
        out_specs=pl.BlockSpec((tm, D), lambda i, k: (i, 0)),
        out_shape=jax.ShapeDtypeStruct((T, D), F32),
        scratch_shapes=[pltpu.VMEM((tm, D), BF16)],
        compiler_params=_params("parallel", "arbitrary"),
    )(x2d, g, w_gu, w_gu, w_d)


def kernel(x, mem, g_mix, w_in, b_f, g_q, g_k, conv_w, conv_b, w_ra, b_ra, w_ri, b_ri, lam,
           g_fox_out, g_lru_out, w_out, g_xattn, g_mem, w_cq, w_ckv, g_cq, g_ck, w_co, g_ffn,
           w_gate_up, w_down):
    B, S, D = x.shape
    T = B * S
    depth = g_mix.shape[0]
    row = lambda v: v.reshape(1, -1).astype(F32)

    for l in range(depth):
        w_t, w_f = _stage_in_weights(jnp.swapaxes(w_in, 1, 2), l, tr=512)
        g_qk = jnp.stack([g_q[l] * FOX_Q_SCALE, g_k[l]]).reshape(2, 1, HEAD_DIM).astype(F32)
        bf_col = jnp.zeros((F_ROWS, 1), F32).at[:FOX_HEADS, 0].set(b_f[l])
        w_gates = jnp.concatenate([w_ra[l], w_ri[l]], axis=-1).astype(BF16)

        x2d = x.reshape(T, D)

        qk, v_t, ug, f_t, w_down_bf = _in_proj(x2d, row(g_mix[l]), w_t, g_qk, w_f, w_down, l,
                                               tm=1024, tn=1024, n_slabs=SIDE_CAST_SLABS,
                                               side_tc=D)
        o_fox, w_gu_bf = _fox_attention(qk.reshape(B, S, 2 * FOX_W), v_t, f_t, bf_col,
                                        w_gate_up, l, batch=B, seq=S,
                                        n_slabs=B * FOX_HEADS // HEADS_PER_STEP,
                                        side_tc=FFN_TILE)
        y_lru, (w_out_bf, w_cq_bf, w_ckv_bf, w_co_bf) = _rg_lru(
            ug.reshape(B, S, 2 * LRU_W), conv_w[l], row(conv_b[l]), w_gates, row(b_ra[l]),
            row(b_ri[l]), row(lam[l]), [w_out, w_cq, w_ckv, w_co], l, batch=B, seq=S)

        x3d = _mix_xattn(o_fox, y_lru, row(g_fox_out[l]), row(g_lru_out[l]), w_out_bf,
                         x2d.reshape(B, S, D), row(g_xattn[l]), w_cq_bf, row(g_cq[l]),
                         mem, row(g_mem[l]), w_ckv_bf, row(g_ck[l]), w_co_bf, tm=512)

        x2d = _ffn(x3d.reshape(T, D), row(g_ffn[l]), w_gu_bf, w_down_bf[0], tm=1024)
        x = x2d.reshape(B, S, D)
    return x
```

```python
import functools
import math

import jax
import jax.numpy as jnp
from jax import lax
from jax.experimental import pallas as pl
from jax.experimental.pallas import tpu as pltpu

F32 = jnp.float32
BF16 = jnp.bfloat16

HEAD_DIM = 128
FOX_HEADS = 8
FOX_W = FOX_HEADS * HEAD_DIM
LRU_BLOCKS = 8
LRU_BLOCK = 128
LRU_W = LRU_BLOCKS * LRU_BLOCK
LRU_C = 8.0
CONV_W = 4
XATT_HEADS = 4
XATT_W = XATT_HEADS * HEAD_DIM
RMS_EPS = 1e-6
F_ROWS = 16
Q_BLOCK = 512
K_CHUNK = 512
HEADS_PER_STEP = 2
SCORE_LOOKAHEAD = 4
V_PAD_ROWS = 16
PROJ_CHUNK = 256
SUBLANES = 8
ROW_TILES_PER_WEIGHT = 2
FFN_ROW_TILES = 2
LRU_BLOCKS_PER_STEP = 4
FFN_TILE = 512
SIDE_CAST_SLABS = 32
FOX_Q_SCALE = math.log2(math.e) / math.sqrt(HEAD_DIM)
BIAS_PIECES = 3

V7X_VMEM_LIMIT_BYTES = 60 * 1024 * 1024


def _rms(x, g):
    ms = jnp.mean(x * x, axis=-1, keepdims=True)
    return x * lax.rsqrt(ms + RMS_EPS) * g


def _dot(a, b):
    return jnp.dot(a, b, preferred_element_type=F32)


def _dot_nt(a, b):
    return lax.dot_general(a, b, (((1,), (1,)), ((), ())), preferred_element_type=F32)


def _softplus(x):
    return jnp.maximum(x, 0.0) + jnp.log1p(jnp.exp(-jnp.abs(x)))


def _cast_slab(src_ref, dst_ref):
    n_tiles, _, tc = dst_ref.shape
    for t in range(n_tiles):
        dst_ref[t] = src_ref[:, t * tc:(t + 1) * tc].astype(BF16)


def _slab_specs(w, layer, n_slabs, tc, step_of):
    _, R, C = w.shape
    rows = R // n_slabs
    slab = lambda *ids: jnp.minimum(step_of(*ids), n_slabs - 1)
    return (pl.BlockSpec((None, rows, C), lambda *ids: (layer, slab(*ids), 0)),
            pl.BlockSpec((C // tc, rows, tc), lambda *ids: (0, slab(*ids), 0)),
            jax.ShapeDtypeStruct((C // tc, R, tc), BF16))


def _params(*sem):
    return pltpu.CompilerParams(dimension_semantics=sem,
                                vmem_limit_bytes=V7X_VMEM_LIMIT_BYTES)


def _wstage_kernel(a_ref, b_ref, w_ref, wf_ref, *, f_tile):
    t = pl.program_id(0)

    @pl.when(t < f_tile)
    def _():
        w_ref[...] = a_ref[...].astype(BF16)

    @pl.when(t >= f_tile)
    def _():
        w_ref[...] = jnp.concatenate([a_ref[FOX_HEADS:, :], b_ref[...]], axis=0).astype(BF16)

    @pl.when(t == f_tile)
    def _():
        wf_ref[:FOX_HEADS, :] = a_ref[:FOX_HEADS, :].astype(BF16)
        wf_ref[FOX_HEADS:, :] = jnp.zeros((F_ROWS - FOX_HEADS, wf_ref.shape[1]), BF16)


def _stage_in_weights(w_t, layer, *, tr):
    _, N, D = w_t.shape
    n_rows = N - FOX_HEADS
    f_tile = 3 * FOX_W // tr
    return pl.pallas_call(
        functools.partial(_wstage_kernel, f_tile=f_tile),
        grid=(n_rows // tr,),
        in_specs=[pl.BlockSpec((None, tr, D), lambda t: (layer, t, 0)),
                  pl.BlockSpec((None, FOX_HEADS, D),
                               lambda t: (layer, (jnp.maximum(t, f_tile) + 1) * (tr // FOX_HEADS), 0))],
        out_specs=[pl.BlockSpec((tr, D), lambda t: (t, 0)),
                   pl.BlockSpec((F_ROWS, D), lambda t: (0, 0))],
        out_shape=[jax.ShapeDtypeStruct((n_rows, D), BF16),
                   jax.ShapeDtypeStruct((F_ROWS, D), BF16)],
        compiler_params=_params("arbitrary"),
    )(w_t, w_t)


def _inproj_kernel(x_ref, g_ref, w_ref, gh_ref, wf_ref, side_ref,
                   qk_ref, vt_ref, ug_ref, f_ref, side_out_ref, h_scr,
                   *, tn, n_qk, n_v):
    j = pl.program_id(1)
    h_ref = h_scr.at[pl.program_id(2)]

    chunks = [slice(c * PROJ_CHUNK, (c + 1) * PROJ_CHUNK) for c in range(tn // PROJ_CHUNK)]

    def qk_tile():
        _cast_slab(side_ref, side_out_ref)
        for sl in chunks:
            acc = _dot_nt(h_ref[...], w_ref[sl, :])
            for hh in range(PROJ_CHUNK // HEAD_DIM):
                lo = sl.start + hh * HEAD_DIM
                qk_ref[:, lo:lo + HEAD_DIM] = _rms(
                    acc[:, hh * HEAD_DIM:(hh + 1) * HEAD_DIM], gh_ref[...]).astype(BF16)

    @pl.when(j == 0)
    def _():
        h_ref[...] = _rms(x_ref[...], g_ref[...]).astype(BF16)
        qk_tile()

    pl.when((j > 0) & (j < n_qk))(qk_tile)

    @pl.when((j >= n_qk) & (j < n_qk + n_v))
    def _():
        _cast_slab(side_ref, side_out_ref)
        for sl in chunks[:-1]:
            vt_ref[sl, :] = _dot_nt(w_ref[sl, :], h_ref[...]).astype(BF16)
        sl = chunks[-1]
        acc = _dot_nt(jnp.concatenate([w_ref[sl, :], wf_ref[...]], axis=0), h_ref[...])
        vt_ref[sl, :] = acc[:PROJ_CHUNK].astype(BF16)
        f_ref[...] = acc[PROJ_CHUNK:]

    @pl.when(j >= n_qk + n_v)
    def _():
        _cast_slab(side_ref, side_out_ref)
        for sl in chunks:
            ug_ref[:, sl] = _dot_nt(h_ref[...], w_ref[sl, :])


def _in_proj(x2d, g, w_t, g_qk, wf, w_side, layer, *, tm, tn, n_slabs, side_tc):
    T, D = x2d.shape
    n_qk = 2 * FOX_W // tn
    n_v = FOX_W // tn
    assert n_v == 1, "the forget logits are produced by the single v step"
    n_ug = 2 * LRU_W // tn
    tiles_per_gain = FOX_W // tn
    clip = lambda v, n: jnp.clip(v, 0, n - 1)
    n_steps = n_qk + n_v + n_ug
    per = ROW_TILES_PER_WEIGHT
    n_rows = T // tm
    first = lambda p: per * p
    last = lambda p: per * p + per - 1

    def row_of(p, j, r, j_lo, j_hi):
        return jnp.where(j < j_lo, first(p), jnp.where(j < j_hi, first(p) + r, last(p)))

    x_row = lambda p, j, r: jnp.where(j == 0, first(p) + r, jnp.minimum(first(p + 1), n_rows - 1))
    side_in, side_out, side_shape = _slab_specs(
        w_side, layer, n_slabs, side_tc, lambda p, j, r: (p * n_steps + j) * per + r)
    return pl.pallas_call(
        functools.partial(_inproj_kernel, tn=tn, n_qk=n_qk, n_v=n_v),
        grid=(n_rows // per, n_steps, per),
        in_specs=[
            pl.BlockSpec((tm, D), lambda p, j, r: (x_row(p, j, r), 0)),
            pl.BlockSpec((1, D), lambda p, j, r: (0, 0)),
            pl.BlockSpec((tn, D), lambda p, j, r: (j, 0)),
            pl.BlockSpec((None, 1, HEAD_DIM),
                         lambda p, j, r: (clip(j // tiles_per_gain, 2), 0, 0)),
            pl.BlockSpec((F_ROWS, D), lambda p, j, r: (0, 0)),
            side_in,
        ],
        out_specs=[
            pl.BlockSpec((tm, tn), lambda p, j, r: (row_of(p, j, r, 0, n_qk), clip(j, n_qk))),
            pl.BlockSpec((tn, tm), lambda p, j, r: (0, row_of(p, j, r, n_qk, n_qk + n_v))),
            pl.BlockSpec((tm, tn), lambda p, j, r: (row_of(p, j, r, n_qk + n_v, n_steps),
                                                    clip(j - n_qk - n_v, n_ug))),
            pl.BlockSpec((F_ROWS, tm), lambda p, j, r: (0, row_of(p, j, r, n_qk, n_qk + n_v))),
            side_out,
        ],
        out_shape=[
            jax.ShapeDtypeStruct((T, 2 * FOX_W), BF16),
            jax.ShapeDtypeStruct((FOX_W, T), BF16),
            jax.ShapeDtypeStruct((T, 2 * LRU_W), F32),
            jax.ShapeDtypeStruct((F_ROWS, T), F32),
            side_shape,
        ],
        scratch_shapes=[pltpu.VMEM((per, tm, D), BF16)],
        compiler_params=_params("arbitrary", "arbitrary", "arbitrary"),
    )(x2d, g, w_t, g_qk, wf, w_side)


def _fox_kernel(q_ref, k_ref, vt_ref, f_ref, bf_ref, side_ref, o_ref, side_out_ref,
                bias_scr, vaug_scr, *, seq):
    hp = pl.program_id(1)

    @pl.when(hp == 0)
    def _():
        z = f_ref[...] + bf_ref[...]
        c = jnp.minimum(z, 0.0) - jnp.log1p(jnp.exp(-jnp.abs(z)))
        lane = lax.broadcasted_iota(jnp.int32, c.shape, 1)
        d = 1
        while d < seq:
            c = c + jnp.where(lane >= d, pltpu.roll(c, d, axis=1), 0.0)
            d *= 2
        bias = c[:FOX_HEADS] * (-math.log2(math.e))
        pieces = []
        for _ in range(BIAS_PIECES):
            p = bias.astype(BF16).astype(F32)
            pieces.append(p)
            bias = bias - p
        pad = jnp.zeros((HEAD_DIM - BIAS_PIECES * FOX_HEADS, seq), F32)
        bias_scr[...] = jnp.concatenate(pieces + [pad], axis=0).T.astype(BF16)

    _cast_slab(side_ref, side_out_ref)
    ones_row = lax.broadcasted_iota(jnp.int32, (V_PAD_ROWS, seq), 0) == 0
    for e in range(HEADS_PER_STEP):
        lanes = slice(e * HEAD_DIM, (e + 1) * HEAD_DIM)
        vaug_scr[e, :HEAD_DIM, :] = vt_ref[lanes, :]
        vaug_scr[e, HEAD_DIM:, :] = jnp.where(ones_row, 1.0, 0.0).astype(BF16)

    tq, tk = Q_BLOCK, K_CHUNK
    lane = lax.broadcasted_iota(jnp.int32, (tq, HEAD_DIM), 1)
    selectors = []
    for e in range(HEADS_PER_STEP):
        h = hp * HEADS_PER_STEP + e
        mine = lane == h
        for p in range(1, BIAS_PIECES):
            mine = mine | (lane == h + p * FOX_HEADS)
        selectors.append(jnp.where(mine, 1.0, 0.0).astype(BF16))
    key = lax.broadcasted_iota(jnp.int32, (tk, tq), 0)
    qry = lax.broadcasted_iota(jnp.int32, (tk, tq), 1)

    def scores(e, lo, k0):
        q = q_ref[lo:lo + tq, e * HEAD_DIM:(e + 1) * HEAD_DIM]
        q_aug = jnp.concatenate([q, selectors[e]], axis=1)
        k_aug = jnp.concatenate([k_ref[k0:k0 + tk, e * HEAD_DIM:(e + 1) * HEAD_DIM],
                                 bias_scr[k0:k0 + tk, :]], axis=1)
        return _dot_nt(k_aug, q_aug)

    steps = [(e, qi * tq, kc * tk) for qi in range(seq // tq)
             for kc in range((qi + 1) * tq // tk) for e in range(HEADS_PER_STEP)]
    pending = [scores(*st) for st in steps[:SCORE_LOOKAHEAD]]
    m = [None] * HEADS_PER_STEP
    acc = [None] * HEADS_PER_STEP
    for idx, (e, lo, k0) in enumerate(steps):
        t = pending.pop(0)
        if idx + SCORE_LOOKAHEAD < len(steps):
            pending.append(scores(*steps[idx + SCORE_LOOKAHEAD]))
        if k0 + tk > lo:
            t = jnp.where(key + (k0 - lo) <= qry, t, -jnp.inf)
        cm = jnp.max(t, axis=0, keepdims=True)
        m_new = cm if k0 == 0 else jnp.maximum(m[e], cm)
        p = jnp.exp2(t - m_new).astype(BF16)
        pv = _dot(vaug_scr[e, :, k0:k0 + tk], p)
        if k0 == 0:
            acc[e] = pv
        else:
            acc[e] = acc[e] * jnp.exp2(m[e] - m_new) + pv
        m[e] = m_new
        if k0 + tk == lo + tq:
            inv_l = 1.0 / acc[e][HEAD_DIM:HEAD_DIM + 1, :]
            o_ref[lo:lo + tq, e * HEAD_DIM:(e + 1) * HEAD_DIM] = (acc[e][:HEAD_DIM, :] * inv_l).T


def _fox_attention(qk, v_t, f_t, b_f, w_side, layer, *, batch, seq, n_slabs, side_tc):
    n_pairs = FOX_HEADS // HEADS_PER_STEP
    wide = HEADS_PER_STEP * HEAD_DIM
    side_in, side_out, side_shape = _slab_specs(w_side, layer, n_slabs, side_tc,
                                                lambda b, hp: b * n_pairs + hp)
    return pl.pallas_call(
        functools.partial(_fox_kernel, seq=seq),
        grid=(batch, n_pairs),
        in_specs=[
            pl.BlockSpec((None, seq, wide), lambda b, hp: (b, 0, hp)),
            pl.BlockSpec((None, seq, wide), lambda b, hp: (b, 0, n_pairs + hp)),
            pl.BlockSpec((wide, seq), lambda b, hp: (hp, b)),
            pl.BlockSpec((F_ROWS, seq), lambda b, hp: (0, b)),
            pl.BlockSpec((F_ROWS, 1), lambda b, hp: (0, 0)),
            side_in,
        ],
        out_specs=[pl.BlockSpec((None, seq, wide), lambda b, hp: (b, 0, hp)), side_out],
        out_shape=[jax.ShapeDtypeStruct((batch, seq, FOX_W), F32), side_shape],
        scratch_shapes=[pltpu.VMEM((seq, HEAD_DIM), BF16),
                        pltpu.VMEM((HEADS_PER_STEP, HEAD_DIM + V_PAD_ROWS, seq), BF16)],
        compiler_params=_params("arbitrary", "arbitrary"),
    )(qk, qk, v_t, f_t, b_f, w_side)


def _lru_kernel(*refs, seq, n_side):
    u_refs = refs[:LRU_BLOCKS_PER_STEP]
    refs = refs[LRU_BLOCKS_PER_STEP:]
    gate_ref, cw_ref, cb_ref, wg_ref, bra_ref, bri_ref, lam_ref = refs[:7]
    side_refs = refs[7:7 + n_side]
    o_ref = refs[7 + n_side]
    side_out_refs = refs[8 + n_side:8 + 2 * n_side]
    at_scr, bt_scr, hp_scr = refs[8 + 2 * n_side:]
    for src_ref, dst_ref in zip(side_refs, side_out_refs):
        _cast_slab(src_ref, dst_ref)
    width = o_ref.shape[1]
    row = lax.broadcasted_iota(jnp.int32, (SUBLANES, LRU_BLOCK), 0)

    def shifted(d):
        cols = []
        for u_ref in u_refs:
            head = jnp.where(row >= d, pltpu.roll(u_ref[0:SUBLANES, :], d, axis=0), 0.0)
            cols.append(jnp.concatenate([head, u_ref[SUBLANES - d:seq - d, :]], axis=0))
        return jnp.concatenate(cols, axis=1)

    cw = cw_ref[...]
    uc = cw[0:1, :] * shifted(3)
    uc = uc + cw[1:2, :] * shifted(2)
    uc = uc + cw[2:3, :] * shifted(1)
    uc = uc + cw[3:4, :] * jnp.concatenate([u_ref[...] for u_ref in u_refs], axis=1)
    uc = cb_ref[...] + uc

    ucb = uc.astype(BF16)
    gates = [_dot(ucb[:, n * LRU_BLOCK:(n + 1) * LRU_BLOCK], wg_ref[n])
             for n in range(width // LRU_BLOCK)]
    pre_r = jnp.concatenate([g[:, :LRU_BLOCK] for g in gates], axis=1)
    pre_i = jnp.concatenate([g[:, LRU_BLOCK:] for g in gates], axis=1)
    half_c = (-0.5 * LRU_C) * _softplus(-lam_ref[...])
    log_a = half_c * jnp.tanh(0.5 * (pre_r + bra_ref[...])) + half_c
    i = 0.5 * jnp.tanh(0.5 * (pre_i + bri_ref[...])) + 0.5
    a = jnp.exp(log_a)
    t = jnp.tanh(log_a)
    w = -2.0 * t
    root = jnp.where(w > 0.0, w * lax.rsqrt(w * (1.0 - t)), 0.0)
    b = root * (i * uc)

    groups = seq // SUBLANES
    gshape = (groups, SUBLANES, width)
    a3 = a.reshape(gshape)
    b3 = b.reshape(gshape)
    sub = lax.broadcasted_iota(jnp.int32, gshape, 1)
    d = 1
    while d < SUBLANES:
        keep = sub >= d
        b3 = a3 * jnp.where(keep, pltpu.roll(b3, d, axis=1), 0.0) + b3
        a3 = a3 * jnp.where(keep, pltpu.roll(a3, d, axis=1), 1.0)
        d *= 2
    at_scr[...] = jnp.broadcast_to(a3[:, SUBLANES - 1:, :], gshape).reshape(seq, width)
    bt_scr[...] = jnp.broadcast_to(b3[:, SUBLANES - 1:, :], gshape).reshape(seq, width)

    def carry(g, h):
        rows = pl.ds(pl.multiple_of(g * SUBLANES, SUBLANES), SUBLANES)
        hp_scr[rows, :] = h
        return at_scr[rows, :] * h + bt_scr[rows, :]

    lax.fori_loop(0, groups, carry, jnp.zeros((SUBLANES, width), F32), unroll=8)
    hs = b3.reshape(seq, width) + a3.reshape(seq, width) * hp_scr[...]
    o_ref[...] = hs * jax.nn.gelu(gate_ref[...])


def _rg_lru(ug, cw, cb, wg, bra, bri, lam, side_weights, layer, *, batch, seq):
    per = LRU_BLOCKS_PER_STEP
    nb = LRU_BLOCKS // per
    width = per * LRU_BLOCK
    vec = lambda: pl.BlockSpec((1, width), lambda b, n: (0, n))
    n_slabs = batch * nb
    sides = [_slab_specs(w, layer, n_slabs, w.shape[2], lambda b, n: b * nb + n)
             for w in side_weights]
    res = pl.pallas_call(
        functools.partial(_lru_kernel, seq=seq, n_side=len(sides)),
        grid=(batch, nb),
        in_specs=[
            *[pl.BlockSpec((None, seq, LRU_BLOCK), lambda b, n, e=e: (b, 0, per * n + e))
              for e in range(per)],
            pl.BlockSpec((None, seq, width), lambda b, n: (b, 0, nb + n)),
            pl.BlockSpec((CONV_W, width), lambda b, n: (0, n)),
            vec(),
            pl.BlockSpec((per, LRU_BLOCK, 2 * LRU_BLOCK), lambda b, n: (n, 0, 0)),
            vec(), vec(), vec(),
            *[sd[0] for sd in sides],
        ],
        out_specs=[pl.BlockSpec((None, seq, width), lambda b, n: (b, 0, n)),
                   *[sd[1] for sd in sides]],
        out_shape=[jax.ShapeDtypeStruct((batch, seq, LRU_W), F32), *[sd[2] for sd in sides]],
        scratch_shapes=[pltpu.VMEM((seq, width), F32)] * 3,
        compiler_params=_params("arbitrary", "arbitrary"),
    )(*[ug] * per, ug, cw, cb, wg, bra, bri, lam, *side_weights)
    return res[0], [r[0] for r in res[1:]]


def _mix_xattn_kernel(of_ref, yl_ref, gf_ref, gl_ref, wout_ref, x_ref, gx_ref, wq_ref, gq_ref,
                      mem_ref, gm_ref, wkv_ref, gk_ref, wo_ref, o_ref, x1_scr, ckv_ref, *, chunk):
    @pl.when(pl.program_id(1) == 0)
    def _():
        kv = _dot(_rms(mem_ref[...], gm_ref[...]).astype(BF16), wkv_ref[...])
        for hh in range(XATT_HEADS):
            sl = slice(hh * HEAD_DIM, (hh + 1) * HEAD_DIM)
            ckv_ref[:, sl] = _rms(kv[:, sl], gk_ref[...]).astype(BF16)
        ckv_ref[:, XATT_W:] = kv[:, XATT_W:].astype(BF16)

    mf = _rms(of_ref[...], gf_ref[...]).astype(BF16)
    ml = _rms(yl_ref[...], gl_ref[...]).astype(BF16)
    for c in range(o_ref.shape[1] // chunk):
        sl = slice(c * chunk, (c + 1) * chunk)
        acc = _dot(mf, wout_ref[:FOX_W, sl]) + _dot(ml, wout_ref[FOX_W:, sl])
        x1_scr[:, sl] = x_ref[:, sl] + acc

    x1 = x1_scr[...]
    cq = _dot(_rms(x1, gx_ref[...]).astype(BF16), wq_ref[...])
    scale = 1.0 / math.sqrt(HEAD_DIM)
    scores = []
    for hh in range(XATT_HEADS):
        sl = slice(hh * HEAD_DIM, (hh + 1) * HEAD_DIM)
        qh = _rms(cq[:, sl], gq_ref[...]).astype(BF16)
        scores.append(_dot_nt(qh, ckv_ref[:, sl]))
    heads = []
    for hh in range(XATT_HEADS):
        vh = ckv_ref[:, XATT_W + hh * HEAD_DIM:XATT_W + (hh + 1) * HEAD_DIM]
        s = scores[hh] * scale
        e = jnp.exp(s - jnp.max(s, axis=-1, keepdims=True))
        l = jnp.sum(e, axis=-1, keepdims=True)
        heads.append((_dot(e.astype(BF16), vh) / l).astype(BF16))
    ox = jnp.concatenate(heads, axis=-1)
    o_ref[...] = x1 + _dot(ox, wo_ref[...])


def _mix_xattn(o_fox, y_lru, gf, gl, w_out, x3d, gx, w_cq, gq, mem, gm, w_ckv, gk, w_co, *, tm):
    B, S, D = x3d.shape
    M = mem.shape[1]
    const = lambda shape: pl.BlockSpec(shape, lambda b, i: (0,) * len(shape),
                                       pipeline_mode=pl.Buffered(1))
    tile = lambda w: pl.BlockSpec((None, tm, w), lambda b, i: (b, i, 0))
    return pl.pallas_call(
        functools.partial(_mix_xattn_kernel, chunk=2 * PROJ_CHUNK),
        grid=(B, S // tm),
        in_specs=[
            tile(FOX_W), tile(LRU_W), const((1, FOX_W)), const((1, LRU_W)),
            const((FOX_W + LRU_W, D)), tile(D), const((1, D)), const((D, XATT_W)),
            const((1, HEAD_DIM)),
            pl.BlockSpec((None, M, D), lambda b, i: (b, 0, 0)),
            const((1, D)), const((D, 2 * XATT_W)), const((1, HEAD_DIM)),
            const((XATT_W, D)),
        ],
        out_specs=tile(D),
        out_shape=jax.ShapeDtypeStruct((B, S, D), F32),
        scratch_shapes=[pltpu.VMEM((tm, D), F32), pltpu.VMEM((M, 2 * XATT_W), BF16)],
        compiler_params=_params("parallel", "arbitrary"),
    )(o_fox, y_lru, gf, gl, w_out, x3d, gx, w_cq, gq, mem, gm, w_ckv, gk, w_co)


def _ffn_kernel(x_ref, g_ref, wg_ref, wu_ref, wd_ref, o_hbm, acc_scr, h_scr, sems,
                *, tm, nk, n_groups):
    p, k, r = pl.program_id(0), pl.program_id(1), pl.program_id(2)
    acc_ref = acc_scr.at[r]
    h_ref = h_scr.at[r]
    row_tile = p * FFN_ROW_TILES + r

    def writeback(tile, slot):
        rows = pl.ds(pl.multiple_of(tile * tm, tm), tm)
        return pltpu.make_async_copy(acc_scr.at[slot], o_hbm.at[rows, :], sems.at[slot])

    def hidden_tile(base_ref):
        h = h_ref[...]
        gate = _dot(h, wg_ref[...])
        up = _dot(h, wu_ref[...])
        act = (jax.nn.silu(gate) * up).astype(BF16)
        acc_ref[...] = base_ref[...] + _dot(act, wd_ref[...])

    @pl.when((k == 0) & (p > 0))
    def _():
        writeback(row_tile - FFN_ROW_TILES, r).wait()

    @pl.when(k == 0)
    def _():
        h_ref[...] = _rms(x_ref[...], g_ref[...]).astype(BF16)
        hidden_tile(x_ref)

    pl.when(k > 0)(lambda: hidden_tile(acc_ref))

    @pl.when(k == nk - 1)
    def _():
        writeback(row_tile, r).start()

    @pl.when((k == nk - 1) & (p == n_groups - 1) & (r == FFN_ROW_TILES - 1))
    def _():
        for slot in range(FFN_ROW_TILES):
            writeback((n_groups - 1) * FFN_ROW_TILES + slot, slot).wait()


def _ffn(x2d, g, w_gu, w_d, *, tm):
    T, D = x2d.shape
    H = w_d.shape[0]
    th = w_gu.shape[2]
    nk = H // th
    per = FFN_ROW_TILES
    n_rows = T // tm
    n_groups = n_rows // per
    x_row = lambda p, k, r: jnp.where(k == 0, per * p + r,
                                      jnp.minimum(per * (p + 1), n_rows - 1))
    return pl.pallas_call(
        functools.partial(_ffn_kernel, tm=tm, nk=nk, n_groups=n_groups),
        grid=(n_groups, nk, per),
        in_specs=[
            pl.BlockSpec((tm, D), lambda p, k, r: (x_row(p, k, r), 0)),
            pl.BlockSpec((1, D), lambda p, k, r: (0, 0)),
            pl.BlockSpec((None, D, th), lambda p, k, r: (k, 0, 0)),
            pl.BlockSpec((None, D, th), lambda p, k, r: (nk + k, 0, 0)),
            pl.BlockSpec((th, D), lambda p, k, r: (k, 0)),
        ],
        out_specs=pl.BlockSpec(memory_space=pl.ANY),
        out_shape=jax.ShapeDtypeStruct((T, D), F32),
        scratch_shapes=[pltpu.VMEM((per, tm, D), F32), pltpu.VMEM((per, tm, D), BF16),
                        pltpu.SemaphoreType.DMA((per,))],
        compiler_params=_params("arbitrary", "arbitrary", "arbitrary"),
    )(x2d, g, w_gu, w_gu, w_d)


def kernel(x, mem, g_mix, w_in, b_f, g_q, g_k, conv_w, conv_b, w_ra, b_ra, w_ri, b_ri, lam,
           g_fox_out, g_lru_out, w_out, g_xattn, g_mem, w_cq, w_ckv, g_cq, g_ck, w_co, g_ffn,
           w_gate_up, w_down):
    B, S, D = x.shape
    T = B * S
    depth = g_mix.shape[0]
    row = lambda v: v.reshape(1, -1).astype(F32)

    for l in range(depth):
        w_t, w_f = _stage_in_weights(jnp.swapaxes(w_in, 1, 2), l, tr=512)
        g_qk = jnp.stack([g_q[l] * FOX_Q_SCALE, g_k[l]]).reshape(2, 1, HEAD_DIM).astype(F32)
        bf_col = jnp.zeros((F_ROWS, 1), F32).at[:FOX_HEADS, 0].set(b_f[l])
        w_gates = jnp.concatenate([w_ra[l], w_ri[l]], axis=-1).astype(BF16)

        x2d = x.reshape(T, D)

        qk, v_t, ug, f_t, w_down_bf = _in_proj(x2d, row(g_mix[l]), w_t, g_qk, w_f, w_down, l,
                                               tm=1024, tn=1024, n_slabs=SIDE_CAST_SLABS,
                                               side_tc=D)
        o_fox, w_gu_bf = _fox_attention(qk.reshape(B, S, 2 * FOX_W), v_t, f_t, bf_col,
                                        w_gate_up, l, batch=B, seq=S,
                                        n_slabs=B * FOX_HEADS // HEADS_PER_STEP,
                                        side_tc=FFN_TILE)
        y_lru, (w_out_bf, w_cq_bf, w_ckv_bf, w_co_bf) = _rg_lru(
            ug.reshape(B, S, 2 * LRU_W), conv_w[l], row(conv_b[l]), w_gates, row(b_ra[l]),
            row(b_ri[l]), row(lam[l]), [w_out, w_cq, w_ckv, w_co], l, batch=B, seq=S)

        x3d = _mix_xattn(o_fox, y_lru, row(g_fox_out[l]), row(g_lru_out[l]), w_out_bf,
                         x2d.reshape(B, S, D), row(g_xattn[l]), w_cq_bf, row(g_cq[l]),
                         mem, row(g_mem[l]), w_ckv_bf, row(g_ck[l]), w_co_bf, tm=512)

        x2d = _ffn(x3d.reshape(T, D), row(g_ffn[l]), w_gu_bf, w_down_bf[0], tm=1024)
        x = x2d.reshape(B, S, D)
    return x
```

```python
import functools
import math

import jax
import jax.numpy as jnp
from jax import lax
from jax.experimental import pallas as pl
from jax.experimental.pallas import tpu as pltpu

F32 = jnp.float32
BF16 = jnp.bfloat16

HEAD_DIM = 128
FOX_HEADS = 8
FOX_W = FOX_HEADS * HEAD_DIM
LRU_BLOCKS = 8
LRU_BLOCK = 128
LRU_W = LRU_BLOCKS * LRU_BLOCK
LRU_C = 8.0
CONV_W = 4
XATT_HEADS = 4
XATT_W = XATT_HEADS * HEAD_DIM
RMS_EPS = 1e-6
F_ROWS = 16
Q_BLOCK = 512
K_CHUNK = 512
HEADS_PER_STEP = 2
SCORE_LOOKAHEAD = 4
V_PAD_ROWS = 16
PROJ_CHUNK = 256
SUBLANES = 8
ROW_TILES_PER_WEIGHT = 2
FFN_ROW_TILES = 2
LRU_BLOCKS_PER_STEP = 4
FFN_TILE = 512
FOX_Q_SCALE = math.log2(math.e) / math.sqrt(HEAD_DIM)
BIAS_PIECES = 3

V7X_VMEM_LIMIT_BYTES = 60 * 1024 * 1024


def _rms(x, g):
    ms = jnp.mean(x * x, axis=-1, keepdims=True)
    return x * lax.rsqrt(ms + RMS_EPS) * g


def _dot(a, b):
    return jnp.dot(a, b, preferred_element_type=F32)


def _dot_nt(a, b):
    return lax.dot_general(a, b, (((1,), (1,)), ((), ())), preferred_element_type=F32)


def _softplus(x):
    return jnp.maximum(x, 0.0) + jnp.log1p(jnp.exp(-jnp.abs(x)))


def _cast_slab(src_ref, dst_ref):
    n_tiles, _, tc = dst_ref.shape
    for t in range(n_tiles):
        dst_ref[t] = src_ref[:, t * tc:(t + 1) * tc].astype(BF16)


def _slab_specs(w, layer, n_slabs, tc, step_of):
    _, R, C = w.shape
    rows = R // n_slabs
    slab = lambda *ids: jnp.minimum(step_of(*ids), n_slabs - 1)
    return (pl.BlockSpec((None, rows, C), lambda *ids: (layer, slab(*ids), 0)),
            pl.BlockSpec((C // tc, rows, tc), lambda *ids: (0, slab(*ids), 0)),
            jax.ShapeDtypeStruct((C // tc, R, tc), BF16))


def _params(*sem):
    return pltpu.CompilerParams(dimension_semantics=sem,
                                vmem_limit_bytes=V7X_VMEM_LIMIT_BYTES)


def _wstage_kernel(a_ref, b_ref, w_ref, wf_ref, *, f_tile):
    t = pl.program_id(0)

    @pl.when(t < f_tile)
    def _():
        w_ref[...] = a_ref[...].astype(BF16)

    @pl.when(t >= f_tile)
    def _():
        w_ref[...] = jnp.concatenate([a_ref[FOX_HEADS:, :], b_ref[...]], axis=0).astype(BF16)

    @pl.when(t == f_tile)
    def _():
        wf_ref[:FOX_HEADS, :] = a_ref[:FOX_HEADS, :].astype(BF16)
        wf_ref[FOX_HEADS:, :] = jnp.zeros((F_ROWS - FOX_HEADS, wf_ref.shape[1]), BF16)


def _stage_in_weights(w_t, layer, *, tr):
    _, N, D = w_t.shape
    n_rows = N - FOX_HEADS
    f_tile = 3 * FOX_W // tr
    return pl.pallas_call(
        functools.partial(_wstage_kernel, f_tile=f_tile),
        grid=(n_rows // tr,),
        in_specs=[pl.BlockSpec((None, tr, D), lambda t: (layer, t, 0)),
                  pl.BlockSpec((None, FOX_HEADS, D),
                               lambda t: (layer, (jnp.maximum(t, f_tile) + 1) * (tr // FOX_HEADS), 0))],
        out_specs=[pl.BlockSpec((tr, D), lambda t: (t, 0)),
                   pl.BlockSpec((F_ROWS, D), lambda t: (0, 0))],
        out_shape=[jax.ShapeDtypeStruct((n_rows, D), BF16),
                   jax.ShapeDtypeStruct((F_ROWS, D), BF16)],
        compiler_params=_params("arbitrary"),
    )(w_t, w_t)


def _inproj_kernel(x_ref, g_ref, w_ref, gh_ref, wf_ref,
                   qk_ref, vt_ref, ug_ref, f_ref, h_scr, *, tn, n_qk, n_v):
    j = pl.program_id(1)
    h_ref = h_scr.at[pl.program_id(2)]

    chunks = [slice(c * PROJ_CHUNK, (c + 1) * PROJ_CHUNK) for c in range(tn // PROJ_CHUNK)]

    def qk_tile():
        for sl in chunks:
            acc = _dot_nt(h_ref[...], w_ref[sl, :])
            for hh in range(PROJ_CHUNK // HEAD_DIM):
                lo = sl.start + hh * HEAD_DIM
                qk_ref[:, lo:lo + HEAD_DIM] = _rms(
                    acc[:, hh * HEAD_DIM:(hh + 1) * HEAD_DIM], gh_ref[...]).astype(BF16)

    @pl.when(j == 0)
    def _():
        h_ref[...] = _rms(x_ref[...], g_ref[...]).astype(BF16)
        qk_tile()

    pl.when((j > 0) & (j < n_qk))(qk_tile)

    @pl.when((j >= n_qk) & (j < n_qk + n_v))
    def _():
        for sl in chunks[:-1]:
            vt_ref[sl, :] = _dot_nt(w_ref[sl, :], h_ref[...]).astype(BF16)
        sl = chunks[-1]
        acc = _dot_nt(jnp.concatenate([w_ref[sl, :], wf_ref[...]], axis=0), h_ref[...])
        vt_ref[sl, :] = acc[:PROJ_CHUNK].astype(BF16)
        f_ref[...] = acc[PROJ_CHUNK:]

    @pl.when(j >= n_qk + n_v)
    def _():
        for sl in chunks:
            ug_ref[:, sl] = _dot_nt(h_ref[...], w_ref[sl, :])


def _in_proj(x2d, g, w_t, g_qk, wf, *, tm, tn):
    T, D = x2d.shape
    n_qk = 2 * FOX_W // tn
    n_v = FOX_W // tn
    assert n_v == 1, "the forget logits are produced by the single v step"
    n_ug = 2 * LRU_W // tn
    tiles_per_gain = FOX_W // tn
    clip = lambda v, n: jnp.clip(v, 0, n - 1)
    n_steps = n_qk + n_v + n_ug
    per = ROW_TILES_PER_WEIGHT
    n_rows = T // tm
    first = lambda p: per * p
    last = lambda p: per * p + per - 1

    def row_of(p, j, r, j_lo, j_hi):
        return jnp.where(j < j_lo, first(p), jnp.where(j < j_hi, first(p) + r, last(p)))

    x_row = lambda p, j, r: jnp.where(j == 0, first(p) + r, jnp.minimum(first(p + 1), n_rows - 1))
    return pl.pallas_call(
        functools.partial(_inproj_kernel, tn=tn, n_qk=n_qk, n_v=n_v),
        grid=(n_rows // per, n_steps, per),
        in_specs=[
            pl.BlockSpec((tm, D), lambda p, j, r: (x_row(p, j, r), 0)),
            pl.BlockSpec((1, D), lambda p, j, r: (0, 0)),
            pl.BlockSpec((tn, D), lambda p, j, r: (j, 0)),
            pl.BlockSpec((None, 1, HEAD_DIM),
                         lambda p, j, r: (clip(j // tiles_per_gain, 2), 0, 0)),
            pl.BlockSpec((F_ROWS, D), lambda p, j, r: (0, 0)),
        ],
        out_specs=[
            pl.BlockSpec((tm, tn), lambda p, j, r: (row_of(p, j, r, 0, n_qk), clip(j, n_qk))),
            pl.BlockSpec((tn, tm), lambda p, j, r: (0, row_of(p, j, r, n_qk, n_qk + n_v))),
            pl.BlockSpec((tm, tn), lambda p, j, r: (row_of(p, j, r, n_qk + n_v, n_steps),
                                                    clip(j - n_qk - n_v, n_ug))),
            pl.BlockSpec((F_ROWS, tm), lambda p, j, r: (0, row_of(p, j, r, n_qk, n_qk + n_v))),
        ],
        out_shape=[
            jax.ShapeDtypeStruct((T, 2 * FOX_W), BF16),
            jax.ShapeDtypeStruct((FOX_W, T), BF16),
            jax.ShapeDtypeStruct((T, 2 * LRU_W), F32),
            jax.ShapeDtypeStruct((F_ROWS, T), F32),
        ],
        scratch_shapes=[pltpu.VMEM((per, tm, D), BF16)],
        compiler_params=_params("arbitrary", "arbitrary", "arbitrary"),
    )(x2d, g, w_t, g_qk, wf)


def _fox_kernel(q_ref, k_ref, vt_ref, f_ref, bf_ref, side_ref, o_ref, side_out_ref,
                bias_scr, vaug_scr, *, seq):
    hp = pl.program_id(1)

    @pl.when(hp == 0)
    def _():
        z = f_ref[...] + bf_ref[...]
        c = jnp.minimum(z, 0.0) - jnp.log1p(jnp.exp(-jnp.abs(z)))
        lane = lax.broadcasted_iota(jnp.int32, c.shape, 1)
        d = 1
        while d < seq:
            c = c + jnp.where(lane >= d, pltpu.roll(c, d, axis=1), 0.0)
            d *= 2
        bias = c[:FOX_HEADS] * (-math.log2(math.e))
        pieces = []
        for _ in range(BIAS_PIECES):
            p = bias.astype(BF16).astype(F32)
            pieces.append(p)
            bias = bias - p
        pad = jnp.zeros((HEAD_DIM - BIAS_PIECES * FOX_HEADS, seq), F32)
        bias_scr[...] = jnp.concatenate(pieces + [pad], axis=0).T.astype(BF16)

    _cast_slab(side_ref, side_out_ref)
    ones_row = lax.broadcasted_iota(jnp.int32, (V_PAD_ROWS, seq), 0) == 0
    for e in range(HEADS_PER_STEP):
        lanes = slice(e * HEAD_DIM, (e + 1) * HEAD_DIM)
        vaug_scr[e, :HEAD_DIM, :] = vt_ref[lanes, :]
        vaug_scr[e, HEAD_DIM:, :] = jnp.where(ones_row, 1.0, 0.0).astype(BF16)

    tq, tk = Q_BLOCK, K_CHUNK
    lane = lax.broadcasted_iota(jnp.int32, (tq, HEAD_DIM), 1)
    selectors = []
    for e in range(HEADS_PER_STEP):
        h = hp * HEADS_PER_STEP + e
        mine = lane == h
        for p in range(1, BIAS_PIECES):
            mine = mine | (lane == h + p * FOX_HEADS)
        selectors.append(jnp.where(mine, 1.0, 0.0).astype(BF16))
    key = lax.broadcasted_iota(jnp.int32, (tk, tq), 0)
    qry = lax.broadcasted_iota(jnp.int32, (tk, tq), 1)

    def scores(e, lo, k0):
        q = q_ref[lo:lo + tq, e * HEAD_DIM:(e + 1) * HEAD_DIM]
        q_aug = jnp.concatenate([q, selectors[e]], axis=1)
        k_aug = jnp.concatenate([k_ref[k0:k0 + tk, e * HEAD_DIM:(e + 1) * HEAD_DIM],
                                 bias_scr[k0:k0 + tk, :]], axis=1)
        return _dot_nt(k_aug, q_aug)

    steps = [(e, qi * tq, kc * tk) for qi in range(seq // tq)
             for kc in range((qi + 1) * tq // tk) for e in range(HEADS_PER_STEP)]
    pending = [scores(*st) for st in steps[:SCORE_LOOKAHEAD]]
    m = [None] * HEADS_PER_STEP
    acc = [None] * HEADS_PER_STEP
    for idx, (e, lo, k0) in enumerate(steps):
        t = pending.pop(0)
        if idx + SCORE_LOOKAHEAD < len(steps):
            pending.append(scores(*steps[idx + SCORE_LOOKAHEAD]))
        if k0 + tk > lo:
            t = jnp.where(key + (k0 - lo) <= qry, t, -jnp.inf)
        cm = jnp.max(t, axis=0, keepdims=True)
        m_new = cm if k0 == 0 else jnp.maximum(m[e], cm)
        p = jnp.exp2(t - m_new).astype(BF16)
        pv = _dot(vaug_scr[e, :, k0:k0 + tk], p)
        if k0 == 0:
            acc[e] = pv
        else:
            acc[e] = acc[e] * jnp.exp2(m[e] - m_new) + pv
        m[e] = m_new
        if k0 + tk == lo + tq:
            inv_l = 1.0 / acc[e][HEAD_DIM:HEAD_DIM + 1, :]
            o_ref[lo:lo + tq, e * HEAD_DIM:(e + 1) * HEAD_DIM] = (acc[e][:HEAD_DIM, :] * inv_l).T


def _fox_attention(qk, v_t, f_t, b_f, w_side, layer, *, batch, seq, n_slabs, side_tc):
    n_pairs = FOX_HEADS // HEADS_PER_STEP
    wide = HEADS_PER_STEP * HEAD_DIM
    side_in, side_out, side_shape = _slab_specs(w_side, layer, n_slabs, side_tc,
                                                lambda b, hp: b * n_pairs + hp)
    return pl.pallas_call(
        functools.partial(_fox_kernel, seq=seq),
        grid=(batch, n_pairs),
        in_specs=[
            pl.BlockSpec((None, seq, wide), lambda b, hp: (b, 0, hp)),
            pl.BlockSpec((None, seq, wide), lambda b, hp: (b, 0, n_pairs + hp)),
            pl.BlockSpec((wide, seq), lambda b, hp: (hp, b)),
            pl.BlockSpec((F_ROWS, seq), lambda b, hp: (0, b)),
            pl.BlockSpec((F_ROWS, 1), lambda b, hp: (0, 0)),
            side_in,
        ],
        out_specs=[pl.BlockSpec((None, seq, wide), lambda b, hp: (b, 0, hp)), side_out],
        out_shape=[jax.ShapeDtypeStruct((batch, seq, FOX_W), F32), side_shape],
        scratch_shapes=[pltpu.VMEM((seq, HEAD_DIM), BF16),
                        pltpu.VMEM((HEADS_PER_STEP, HEAD_DIM + V_PAD_ROWS, seq), BF16)],
        compiler_params=_params("arbitrary", "arbitrary"),
    )(qk, qk, v_t, f_t, b_f, w_side)


def _lru_kernel(*refs, seq, n_side):
    u_refs = refs[:LRU_BLOCKS_PER_STEP]
    refs = refs[LRU_BLOCKS_PER_STEP:]
    gate_ref, cw_ref, cb_ref, wg_ref, bra_ref, bri_ref, lam_ref = refs[:7]
    side_refs = refs[7:7 + n_side]
    o_ref = refs[7 + n_side]
    side_out_refs = refs[8 + n_side:8 + 2 * n_side]
    at_scr, bt_scr, hp_scr = refs[8 + 2 * n_side:]
    for src_ref, dst_ref in zip(side_refs, side_out_refs):
        _cast_slab(src_ref, dst_ref)
    width = o_ref.shape[1]
    row = lax.broadcasted_iota(jnp.int32, (SUBLANES, LRU_BLOCK), 0)

    def shifted(d):
        cols = []
        for u_ref in u_refs:
            head = jnp.where(row >= d, pltpu.roll(u_ref[0:SUBLANES, :], d, axis=0), 0.0)
            cols.append(jnp.concatenate([head, u_ref[SUBLANES - d:seq - d, :]], axis=0))
        return jnp.concatenate(cols, axis=1)

    cw = cw_ref[...]
    uc = cw[0:1, :] * shifted(3)
    uc = uc + cw[1:2, :] * shifted(2)
    uc = uc + cw[2:3, :] * shifted(1)
    uc = uc + cw[3:4, :] * jnp.concatenate([u_ref[...] for u_ref in u_refs], axis=1)
    uc = cb_ref[...] + uc

    ucb = uc.astype(BF16)
    gates = [_dot(ucb[:, n * LRU_BLOCK:(n + 1) * LRU_BLOCK], wg_ref[n])
             for n in range(width // LRU_BLOCK)]
    pre_r = jnp.concatenate([g[:, :LRU_BLOCK] for g in gates], axis=1)
    pre_i = jnp.concatenate([g[:, LRU_BLOCK:] for g in gates], axis=1)
    half_c = (-0.5 * LRU_C) * _softplus(-lam_ref[...])
    log_a = half_c * jnp.tanh(0.5 * (pre_r + bra_ref[...])) + half_c
    i = 0.5 * jnp.tanh(0.5 * (pre_i + bri_ref[...])) + 0.5
    a = jnp.exp(log_a)
    t = jnp.tanh(log_a)
    w = -2.0 * t
    root = jnp.where(w > 0.0, w * lax.rsqrt(w * (1.0 - t)), 0.0)
    b = root * (i * uc)

    groups = seq // SUBLANES
    gshape = (groups, SUBLANES, width)
    a3 = a.reshape(gshape)
    b3 = b.reshape(gshape)
    sub = lax.broadcasted_iota(jnp.int32, gshape, 1)
    d = 1
    while d < SUBLANES:
        keep = sub >= d
        b3 = a3 * jnp.where(keep, pltpu.roll(b3, d, axis=1), 0.0) + b3
        a3 = a3 * jnp.where(keep, pltpu.roll(a3, d, axis=1), 1.0)
        d *= 2
    at_scr[...] = jnp.broadcast_to(a3[:, SUBLANES - 1:, :], gshape).reshape(seq, width)
    bt_scr[...] = jnp.broadcast_to(b3[:, SUBLANES - 1:, :], gshape).reshape(seq, width)

    def carry(g, h):
        rows = pl.ds(pl.multiple_of(g * SUBLANES, SUBLANES), SUBLANES)
        hp_scr[rows, :] = h
        return at_scr[rows, :] * h + bt_scr[rows, :]

    lax.fori_loop(0, groups, carry, jnp.zeros((SUBLANES, width), F32), unroll=8)
    hs = b3.reshape(seq, width) + a3.reshape(seq, width) * hp_scr[...]
    o_ref[...] = hs * jax.nn.gelu(gate_ref[...])


def _rg_lru(ug, cw, cb, wg, bra, bri, lam, side_weights, layer, *, batch, seq):
    per = LRU_BLOCKS_PER_STEP
    nb = LRU_BLOCKS // per
    width = per * LRU_BLOCK
    vec = lambda: pl.BlockSpec((1, width), lambda b, n: (0, n))
    n_slabs = batch * nb
    sides = [_slab_specs(w, layer, n_slabs, w.shape[2], lambda b, n: b * nb + n)
             for w in side_weights]
    res = pl.pallas_call(
        functools.partial(_lru_kernel, seq=seq, n_side=len(sides)),
        grid=(batch, nb),
        in_specs=[
            *[pl.BlockSpec((None, seq, LRU_BLOCK), lambda b, n, e=e: (b, 0, per * n + e))
              for e in range(per)],
            pl.BlockSpec((None, seq, width), lambda b, n: (b, 0, nb + n)),
            pl.BlockSpec((CONV_W, width), lambda b, n: (0, n)),
            vec(),
            pl.BlockSpec((per, LRU_BLOCK, 2 * LRU_BLOCK), lambda b, n: (n, 0, 0)),
            vec(), vec(), vec(),
            *[sd[0] for sd in sides],
        ],
        out_specs=[pl.BlockSpec((None, seq, width), lambda b, n: (b, 0, n)),
                   *[sd[1] for sd in sides]],
        out_shape=[jax.ShapeDtypeStruct((batch, seq, LRU_W), F32), *[sd[2] for sd in sides]],
        scratch_shapes=[pltpu.VMEM((seq, width), F32)] * 3,
        compiler_params=_params("arbitrary", "arbitrary"),
    )(*[ug] * per, ug, cw, cb, wg, bra, bri, lam, *side_weights)
    return res[0], [r[0] for r in res[1:]]


def _mix_xattn_kernel(of_ref, yl_ref, gf_ref, gl_ref, wout_ref, x_ref, gx_ref, wq_ref, gq_ref,
                      mem_ref, gm_ref, wkv_ref, gk_ref, wo_ref, side_ref, o_ref, side_out_ref,
                      ckv_ref, *, chunk):
    @pl.when(pl.program_id(1) == 0)
    def _():
        kv = _dot(_rms(mem_ref[...], gm_ref[...]).astype(BF16), wkv_ref[...])
        for hh in range(XATT_HEADS):
            sl = slice(hh * HEAD_DIM, (hh + 1) * HEAD_DIM)
            ckv_ref[:, sl] = _rms(kv[:, sl], gk_ref[...]).astype(BF16)
        ckv_ref[:, XATT_W:] = kv[:, XATT_W:].astype(BF16)

    _cast_slab(side_ref, side_out_ref)
    mf = _rms(of_ref[...], gf_ref[...]).astype(BF16)
    ml = _rms(yl_ref[...], gl_ref[...]).astype(BF16)
    for c in range(o_ref.shape[1] // chunk):
        sl = slice(c * chunk, (c + 1) * chunk)
        acc = _dot(mf, wout_ref[:FOX_W, sl]) + _dot(ml, wout_ref[FOX_W:, sl])
        o_ref[:, sl] = x_ref[:, sl] + acc

    x1 = o_ref[...]
    cq = _dot(_rms(x1, gx_ref[...]).astype(BF16), wq_ref[...])
    scale = 1.0 / math.sqrt(HEAD_DIM)
    scores = []
    for hh in range(XATT_HEADS):
        sl = slice(hh * HEAD_DIM, (hh + 1) * HEAD_DIM)
        qh = _rms(cq[:, sl], gq_ref[...]).astype(BF16)
        scores.append(_dot_nt(qh, ckv_ref[:, sl]))
    heads = []
    for hh in range(XATT_HEADS):
        vh = ckv_ref[:, XATT_W + hh * HEAD_DIM:XATT_W + (hh + 1) * HEAD_DIM]
        s = scores[hh] * scale
        e = jnp.exp(s - jnp.max(s, axis=-1, keepdims=True))
        l = jnp.sum(e, axis=-1, keepdims=True)
        heads.append((_dot(e.astype(BF16), vh) / l).astype(BF16))
    ox = jnp.concatenate(heads, axis=-1)
    o_ref[...] = x1 + _dot(ox, wo_ref[...])


def _mix_xattn(o_fox, y_lru, gf, gl, w_out, x3d, gx, w_cq, gq, mem, gm, w_ckv, gk, w_co,
               w_side, layer, *, tm):
    B, S, D = x3d.shape
    M = mem.shape[1]
    n_tiles = S // tm
    side_in, side_out, side_shape = _slab_specs(w_side, layer, B * n_tiles, w_side.shape[2],
                                                lambda b, i: b * n_tiles + i)
    const = lambda shape: pl.BlockSpec(shape, lambda b, i: (0,) * len(shape),
                                       pipeline_mode=pl.Buffered(1))
    tile = lambda w: pl.BlockSpec((None, tm, w), lambda b, i: (b, i, 0))
    return pl.pallas_call(
        functools.partial(_mix_xattn_kernel, chunk=2 * PROJ_CHUNK),
        grid=(B, S // tm),
        in_specs=[
            tile(FOX_W), tile(LRU_W), const((1, FOX_W)), const((1, LRU_W)),
            const((FOX_W + LRU_W, D)), tile(D), const((1, D)), const((D, XATT_W)),
            const((1, HEAD_DIM)),
            pl.BlockSpec((None, M, D), lambda b, i: (b, 0, 0)),
            const((1, D)), const((D, 2 * XATT_W)), const((1, HEAD_DIM)),
            const((XATT_W, D)),
            side_in,
        ],
        out_specs=[tile(D), side_out],
        out_shape=[jax.ShapeDtypeStruct((B, S, D), F32), side_shape],
        scratch_shapes=[pltpu.VMEM((M, 2 * XATT_W), BF16)],
        compiler_params=_params("arbitrary", "arbitrary"),
    )(o_fox, y_lru, gf, gl, w_out, x3d, gx, w_cq, gq, mem, gm, w_ckv, gk, w_co, w_side)


def _ffn_kernel(x_ref, g_ref, wg_ref, wu_ref, wd_ref, o_hbm, acc_scr, h_scr, sems,
                *, tm, nk, n_groups):
    p, k, r = pl.program_id(0), pl.program_id(1), pl.program_id(2)
    acc_ref = acc_scr.at[r]
    h_ref = h_scr.at[r]
    row_tile = p * FFN_ROW_TILES + r

    def writeback(tile, slot):
        rows = pl.ds(pl.multiple_of(tile * tm, tm), tm)
        return pltpu.make_async_copy(acc_scr.at[slot], o_hbm.at[rows, :], sems.at[slot])

    def hidden_tile(base_ref):
        h = h_ref[...]
        gate = _dot(h, wg_ref[...])
        up = _dot(h, wu_ref[...])
        act = (jax.nn.silu(gate) * up).astype(BF16)
        acc_ref[...] = base_ref[...] + _dot(act, wd_ref[...])

    @pl.when((k == 0) & (p > 0))
    def _():
        writeback(row_tile - FFN_ROW_TILES, r).wait()

    @pl.when(k == 0)
    def _():
        h_ref[...] = _rms(x_ref[...], g_ref[...]).astype(BF16)
        hidden_tile(x_ref)

    pl.when(k > 0)(lambda: hidden_tile(acc_ref))

    @pl.when(k == nk - 1)
    def _():
        writeback(row_tile, r).start()

    @pl.when((k == nk - 1) & (p == n_groups - 1) & (r == FFN_ROW_TILES - 1))
    def _():
        for slot in range(FFN_ROW_TILES):
            writeback((n_groups - 1) * FFN_ROW_TILES + slot, slot).wait()


def _ffn(x2d, g, w_gu, w_d, *, tm):
    T, D = x2d.shape
    H = w_d.shape[0]
    th = w_gu.shape[2]
    nk = H // th
    per = FFN_ROW_TILES
    n_rows = T // tm
    n_groups = n_rows // per
    x_row = lambda p, k, r: jnp.where(k == 0, per * p + r,
                                      jnp.minimum(per * (p + 1), n_rows - 1))
    return pl.pallas_call(
        functools.partial(_ffn_kernel, tm=tm, nk=nk, n_groups=n_groups),
        grid=(n_groups, nk, per),
        in_specs=[
            pl.BlockSpec((tm, D), lambda p, k, r: (x_row(p, k, r), 0)),
            pl.BlockSpec((1, D), lambda p, k, r: (0, 0)),
            pl.BlockSpec((None, D, th), lambda p, k, r: (k, 0, 0)),
            pl.BlockSpec((None, D, th), lambda p, k, r: (nk + k, 0, 0)),
            pl.BlockSpec((th, D), lambda p, k, r: (k, 0)),
        ],
        out_specs=pl.BlockSpec(memory_space=pl.ANY),
        out_shape=jax.ShapeDtypeStruct((T, D), F32),
        scratch_shapes=[pltpu.VMEM((per, tm, D), F32), pltpu.VMEM((per, tm, D), BF16),
                        pltpu.SemaphoreType.DMA((per,))],
        compiler_params=_params("arbitrary", "arbitrary", "arbitrary"),
    )(x2d, g, w_gu, w_gu, w_d)


def kernel(x, mem, g_mix, w_in, b_f, g_q, g_k, conv_w, conv_b, w_ra, b_ra, w_ri, b_ri, lam,
           g_fox_out, g_lru_out, w_out, g_xattn, g_mem, w_cq, w_ckv, g_cq, g_ck, w_co, g_ffn,
           w_gate_up, w_down):
    B, S, D = x.shape
    T = B * S
    depth = g_mix.shape[0]
    row = lambda v: v.reshape(1, -1).astype(F32)

    for l in range(depth):
        w_t, w_f = _stage_in_weights(jnp.swapaxes(w_in, 1, 2), l, tr=512)
        g_qk = jnp.stack([g_q[l] * FOX_Q_SCALE, g_k[l]]).reshape(2, 1, HEAD_DIM).astype(F32)
        bf_col = jnp.zeros((F_ROWS, 1), F32).at[:FOX_HEADS, 0].set(b_f[l])
        w_gates = jnp.concatenate([w_ra[l], w_ri[l]], axis=-1).astype(BF16)

        x2d = x.reshape(T, D)

        qk, v_t, ug, f_t = _in_proj(x2d, row(g_mix[l]), w_t, g_qk, w_f, tm=1024, tn=1024)
        o_fox, w_gu_bf = _fox_attention(qk.reshape(B, S, 2 * FOX_W), v_t, f_t, bf_col,
                                        w_gate_up, l, batch=B, seq=S,
                                        n_slabs=B * FOX_HEADS // HEADS_PER_STEP,
                                        side_tc=FFN_TILE)
        y_lru, (w_out_bf, w_cq_bf, w_ckv_bf, w_co_bf) = _rg_lru(
            ug.reshape(B, S, 2 * LRU_W), conv_w[l], row(conv_b[l]), w_gates, row(b_ra[l]),
            row(b_ri[l]), row(lam[l]), [w_out, w_cq, w_ckv, w_co], l, batch=B, seq=S)

        x3d, w_down_bf = _mix_xattn(
            o_fox, y_lru, row(g_fox_out[l]), row(g_lru_out[l]), w_out_bf, x2d.reshape(B, S, D),
            row(g_xattn[l]), w_cq_bf, row(g_cq[l]), mem, row(g_mem[l]), w_ckv_bf, row(g_ck[l]),
            w_co_bf, w_down, l, tm=512)

        x2d = _ffn(x3d.reshape(T, D), row(g_ffn[l]), w_gu_bf, w_down_bf[0], tm=1024)
        x = x2d.reshape(B, S, D)
    return x
```

```python
import functools
import math

import jax
import jax.numpy as jnp
from jax import lax
from jax.experimental import pallas as pl
from jax.experimental.pallas import tpu as pltpu

F32 = jnp.float32
BF16 = jnp.bfloat16

HEAD_DIM = 128
FOX_HEADS = 8
FOX_W = FOX_HEADS * HEAD_DIM
LRU_BLOCKS = 8
LRU_BLOCK = 128
LRU_W = LRU_BLOCKS * LRU_BLOCK
LRU_C = 8.0
CONV_W = 4
XATT_HEADS = 4
XATT_W = XATT_HEADS * HEAD_DIM
RMS_EPS = 1e-6
F_ROWS = 16
Q_BLOCK = 512
K_CHUNK = 512
DIAG_CHUNK = 256
HEADS_PER_STEP = 2
SCORE_LOOKAHEAD = 4
V_PAD_ROWS = 16
PROJ_CHUNK = 256
SUBLANES = 8
ROW_TILES_PER_WEIGHT = 2
FFN_ROW_TILES = 2
LRU_BLOCKS_PER_STEP = 4
FFN_TILE = 512
FOX_Q_SCALE = math.log2(math.e) / math.sqrt(HEAD_DIM)
BIAS_PIECES = 3

V7X_VMEM_LIMIT_BYTES = 60 * 1024 * 1024


def _rms(x, g):
    ms = jnp.mean(x * x, axis=-1, keepdims=True)
    return x * lax.rsqrt(ms + RMS_EPS) * g


def _dot(a, b):
    return jnp.dot(a, b, preferred_element_type=F32)


def _dot_nt(a, b):
    return lax.dot_general(a, b, (((1,), (1,)), ((), ())), preferred_element_type=F32)


def _softplus(x):
    return jnp.maximum(x, 0.0) + jnp.log1p(jnp.exp(-jnp.abs(x)))


def _cast_slab(src_ref, dst_ref):
    n_tiles, _, tc = dst_ref.shape
    for t in range(n_tiles):
        dst_ref[t] = src_ref[:, t * tc:(t + 1) * tc].astype(BF16)


def _slab_specs(w, layer, n_slabs, tc, step_of):
    _, R, C = w.shape
    rows = R // n_slabs
    slab = lambda *ids: jnp.minimum(step_of(*ids), n_slabs - 1)
    return (pl.BlockSpec((None, rows, C), lambda *ids: (layer, slab(*ids), 0)),
            pl.BlockSpec((C // tc, rows, tc), lambda *ids: (0, slab(*ids), 0)),
            jax.ShapeDtypeStruct((C // tc, R, tc), BF16))


def _params(*sem):
    return pltpu.CompilerParams(dimension_semantics=sem,
                                vmem_limit_bytes=V7X_VMEM_LIMIT_BYTES)


def _wstage_kernel(a_ref, b_ref, w_ref, wf_ref, *, f_tile):
    t = pl.program_id(0)

    @pl.when(t < f_tile)
    def _():
        w_ref[...] = a_ref[...].astype(BF16)

    @pl.when(t >= f_tile)
    def _():
        w_ref[...] = jnp.concatenate([a_ref[FOX_HEADS:, :], b_ref[...]], axis=0).astype(BF16)

    @pl.when(t == f_tile)
    def _():
        wf_ref[:FOX_HEADS, :] = a_ref[:FOX_HEADS, :].astype(BF16)
        wf_ref[FOX_HEADS:, :] = jnp.zeros((F_ROWS - FOX_HEADS, wf_ref.shape[1]), BF16)


def _stage_in_weights(w_t, layer, *, tr):
    _, N, D = w_t.shape
    n_rows = N - FOX_HEADS
    f_tile = 3 * FOX_W // tr
    return pl.pallas_call(
        functools.partial(_wstage_kernel, f_tile=f_tile),
        grid=(n_rows // tr,),
        in_specs=[pl.BlockSpec((None, tr, D), lambda t: (layer, t, 0)),
                  pl.BlockSpec((None, FOX_HEADS, D),
                               lambda t: (layer, (jnp.maximum(t, f_tile) + 1) * (tr // FOX_HEADS), 0))],
        out_specs=[pl.BlockSpec((tr, D), lambda t: (t, 0)),
                   pl.BlockSpec((F_ROWS, D), lambda t: (0, 0))],
        out_shape=[jax.ShapeDtypeStruct((n_rows, D), BF16),
                   jax.ShapeDtypeStruct((F_ROWS, D), BF16)],
        compiler_params=_params("arbitrary"),
    )(w_t, w_t)


def _inproj_kernel(x_ref, g_ref, w_ref, gh_ref, wf_ref,
                   qk_ref, vt_ref, ug_ref, f_ref, h_scr, *, tn, n_qk, n_v):
    j = pl.program_id(1)
    h_ref = h_scr.at[pl.program_id(2)]

    chunks = [slice(c * PROJ_CHUNK, (c + 1) * PROJ_CHUNK) for c in range(tn // PROJ_CHUNK)]

    def qk_tile():
        for sl in chunks:
            acc = _dot_nt(h_ref[...], w_ref[sl, :])
            for hh in range(PROJ_CHUNK // HEAD_DIM):
                lo = sl.start + hh * HEAD_DIM
                qk_ref[:, lo:lo + HEAD_DIM] = _rms(
                    acc[:, hh * HEAD_DIM:(hh + 1) * HEAD_DIM], gh_ref[...]).astype(BF16)

    @pl.when(j == 0)
    def _():
        h_ref[...] = _rms(x_ref[...], g_ref[...]).astype(BF16)
        qk_tile()

    pl.when((j > 0) & (j < n_qk))(qk_tile)

    @pl.when((j >= n_qk) & (j < n_qk + n_v))
    def _():
        for sl in chunks[:-1]:
            vt_ref[sl, :] = _dot_nt(w_ref[sl, :], h_ref[...]).astype(BF16)
        sl = chunks[-1]
        acc = _dot_nt(jnp.concatenate([w_ref[sl, :], wf_ref[...]], axis=0), h_ref[...])
        vt_ref[sl, :] = acc[:PROJ_CHUNK].astype(BF16)
        f_ref[...] = acc[PROJ_CHUNK:]

    @pl.when(j >= n_qk + n_v)
    def _():
        for sl in chunks:
            ug_ref[:, sl] = _dot_nt(h_ref[...], w_ref[sl, :])


def _in_proj(x2d, g, w_t, g_qk, wf, *, tm, tn):
    T, D = x2d.shape
    n_qk = 2 * FOX_W // tn
    n_v = FOX_W // tn
    assert n_v == 1, "the forget logits are produced by the single v step"
    n_ug = 2 * LRU_W // tn
    tiles_per_gain = FOX_W // tn
    clip = lambda v, n: jnp.clip(v, 0, n - 1)
    n_steps = n_qk + n_v + n_ug
    per = ROW_TILES_PER_WEIGHT
    n_rows = T // tm
    first = lambda p: per * p
    last = lambda p: per * p + per - 1

    def row_of(p, j, r, j_lo, j_hi):
        return jnp.where(j < j_lo, first(p), jnp.where(j < j_hi, first(p) + r, last(p)))

    x_row = lambda p, j, r: jnp.where(j == 0, first(p) + r, jnp.minimum(first(p + 1), n_rows - 1))
    return pl.pallas_call(
        functools.partial(_inproj_kernel, tn=tn, n_qk=n_qk, n_v=n_v),
        grid=(n_rows // per, n_steps, per),
        in_specs=[
            pl.BlockSpec((tm, D), lambda p, j, r: (x_row(p, j, r), 0)),
            pl.BlockSpec((1, D), lambda p, j, r: (0, 0)),
            pl.BlockSpec((tn, D), lambda p, j, r: (j, 0)),
            pl.BlockSpec((None, 1, HEAD_DIM),
                         lambda p, j, r: (clip(j // tiles_per_gain, 2), 0, 0)),
            pl.BlockSpec((F_ROWS, D), lambda p, j, r: (0, 0)),
        ],
        out_specs=[
            pl.BlockSpec((tm, tn), lambda p, j, r: (row_of(p, j, r, 0, n_qk), clip(j, n_qk))),
            pl.BlockSpec((tn, tm), lambda p, j, r: (0, row_of(p, j, r, n_qk, n_qk + n_v))),
            pl.BlockSpec((tm, tn), lambda p, j, r: (row_of(p, j, r, n_qk + n_v, n_steps),
                                                    clip(j - n_qk - n_v, n_ug))),
            pl.BlockSpec((F_ROWS, tm), lambda p, j, r: (0, row_of(p, j, r, n_qk, n_qk + n_v))),
        ],
        out_shape=[
            jax.ShapeDtypeStruct((T, 2 * FOX_W), BF16),
            jax.ShapeDtypeStruct((FOX_W, T), BF16),
            jax.ShapeDtypeStruct((T, 2 * LRU_W), F32),
            jax.ShapeDtypeStruct((F_ROWS, T), F32),
        ],
        scratch_shapes=[pltpu.VMEM((per, tm, D), BF16)],
        compiler_params=_params("arbitrary", "arbitrary", "arbitrary"),
    )(x2d, g, w_t, g_qk, wf)


def _fox_kernel(q_ref, k_ref, vt_ref, f_ref, bf_ref, side_ref, o_ref, side_out_ref,
                bias_scr, vaug_scr, *, seq):
    hp = pl.program_id(1)

    @pl.when(hp == 0)
    def _():
        z = f_ref[...] + bf_ref[...]
        c = jnp.minimum(z, 0.0) - jnp.log1p(jnp.exp(-jnp.abs(z)))
        lane = lax.broadcasted_iota(jnp.int32, c.shape, 1)
        d = 1
        while d < seq:
            c = c + jnp.where(lane >= d, pltpu.roll(c, d, axis=1), 0.0)
            d *= 2
        bias = c[:FOX_HEADS] * (-math.log2(math.e))
        pieces = []
        for _ in range(BIAS_PIECES):
            p = bias.astype(BF16).astype(F32)
            pieces.append(p)
            bias = bias - p
        pad = jnp.zeros((HEAD_DIM - BIAS_PIECES * FOX_HEADS, seq), F32)
        bias_scr[...] = jnp.concatenate(pieces + [pad], axis=0).T.astype(BF16)

    _cast_slab(side_ref, side_out_ref)
    ones_row = lax.broadcasted_iota(jnp.int32, (V_PAD_ROWS, seq), 0) == 0
    for e in range(HEADS_PER_STEP):
        lanes = slice(e * HEAD_DIM, (e + 1) * HEAD_DIM)
        vaug_scr[e, :HEAD_DIM, :] = vt_ref[lanes, :]
        vaug_scr[e, HEAD_DIM:, :] = jnp.where(ones_row, 1.0, 0.0).astype(BF16)

    tq, tk = Q_BLOCK, K_CHUNK
    dc = DIAG_CHUNK

    def selector(e, n):
        lane = lax.broadcasted_iota(jnp.int32, (n, HEAD_DIM), 1)
        h = hp * HEADS_PER_STEP + e
        mine = lane == h
        for p in range(1, BIAS_PIECES):
            mine = mine | (lane == h + p * FOX_HEADS)
        return jnp.where(mine, 1.0, 0.0).astype(BF16)

    selectors = {(e, n): selector(e, n) for e in range(HEADS_PER_STEP)
                 for n in range(dc, tq + 1, dc)}
    below_diag = (lax.broadcasted_iota(jnp.int32, (dc, dc), 0)
                  <= lax.broadcasted_iota(jnp.int32, (dc, dc), 1))

    def scores(e, lo, q0, k0, nk):
        q = q_ref[lo + q0:lo + tq, e * HEAD_DIM:(e + 1) * HEAD_DIM]
        q_aug = jnp.concatenate([q, selectors[e, tq - q0]], axis=1)
        k_aug = jnp.concatenate([k_ref[k0:k0 + nk, e * HEAD_DIM:(e + 1) * HEAD_DIM],
                                 bias_scr[k0:k0 + nk, :]], axis=1)
        return _dot_nt(k_aug, q_aug)

    steps = []
    for qi in range(seq // tq):
        lo = qi * tq
        pieces = [(0, kc * tk, tk) for kc in range(lo // tk)]
        pieces += [(j * dc, lo + j * dc, dc) for j in range(tq // dc)]
        steps += [(e, lo) + piece for piece in pieces for e in range(HEADS_PER_STEP)]
    pending = [scores(*st) for st in steps[:SCORE_LOOKAHEAD]]
    m = [None] * HEADS_PER_STEP
    acc = [None] * HEADS_PER_STEP
    for idx, (e, lo, q0, k0, nk) in enumerate(steps):
        t = pending.pop(0)
        if idx + SCORE_LOOKAHEAD < len(steps):
            pending.append(scores(*steps[idx + SCORE_LOOKAHEAD]))
        if k0 >= lo:
            masked = jnp.where(below_diag, t[:, :dc], -jnp.inf)
            t = jnp.concatenate([masked, t[:, dc:]], axis=1) if q0 + dc < tq else masked
        cm = jnp.max(t, axis=0, keepdims=True)
        if k0 == 0:
            m_new = cm
        else:
            m_old = m[e][:, q0:]
            m_new = jnp.maximum(m_old, cm)
        p = jnp.exp2(t - m_new).astype(BF16)
        pv = _dot(vaug_scr[e, :, k0:k0 + nk], p)
        if k0 == 0:
            acc[e], m[e] = pv, m_new
        else:
            upd = acc[e][:, q0:] * jnp.exp2(m_old - m_new) + pv
            acc[e] = jnp.concatenate([acc[e][:, :q0], upd], axis=1) if q0 else upd
            m[e] = jnp.concatenate([m[e][:, :q0], m_new], axis=1) if q0 else m_new
        if k0 + nk == lo + tq:
            inv_l = 1.0 / acc[e][HEAD_DIM:HEAD_DIM + 1, :]
            o_ref[lo:lo + tq, e * HEAD_DIM:(e + 1) * HEAD_DIM] = (acc[e][:HEAD_DIM, :] * inv_l).T


def _fox_attention(qk, v_t, f_t, b_f, w_side, layer, *, batch, seq, n_slabs, side_tc):
    n_pairs = FOX_HEADS // HEADS_PER_STEP
    wide = HEADS_PER_STEP * HEAD_DIM
    side_in, side_out, side_shape = _slab_specs(w_side, layer, n_slabs, side_tc,
                                                lambda b, hp: b * n_pairs + hp)
    return pl.pallas_call(
        functools.partial(_fox_kernel, seq=seq),
        grid=(batch, n_pairs),
        in_specs=[
            pl.BlockSpec((None, seq, wide), lambda b, hp: (b, 0, hp)),
            pl.BlockSpec((None, seq, wide), lambda b, hp: (b, 0, n_pairs + hp)),
            pl.BlockSpec((wide, seq), lambda b, hp: (hp, b)),
            pl.BlockSpec((F_ROWS, seq), lambda b, hp: (0, b)),
            pl.BlockSpec((F_ROWS, 1), lambda b, hp: (0, 0)),
            side_in,
        ],
        out_specs=[pl.BlockSpec((None, seq, wide), lambda b, hp: (b, 0, hp)), side_out],
        out_shape=[jax.ShapeDtypeStruct((batch, seq, FOX_W), F32), side_shape],
        scratch_shapes=[pltpu.VMEM((seq, HEAD_DIM), BF16),
                        pltpu.VMEM((HEADS_PER_STEP, HEAD_DIM + V_PAD_ROWS, seq), BF16)],
        compiler_params=_params("arbitrary", "arbitrary"),
    )(qk, qk, v_t, f_t, b_f, w_side)


def _lru_kernel(*refs, seq, n_side):
    u_refs = refs[:LRU_BLOCKS_PER_STEP]
    refs = refs[LRU_BLOCKS_PER_STEP:]
    gate_ref, cw_ref, cb_ref, wg_ref, bra_ref, bri_ref, lam_ref = refs[:7]
    side_refs = refs[7:7 + n_side]
    o_ref = refs[7 + n_side]
    side_out_refs = refs[8 + n_side:8 + 2 * n_side]
    at_scr, bt_scr, hp_scr = refs[8 + 2 * n_side:]
    for src_ref, dst_ref in zip(side_refs, side_out_refs):
        _cast_slab(src_ref, dst_ref)
    width = o_ref.shape[1]
    row = lax.broadcasted_iota(jnp.int32, (SUBLANES, LRU_BLOCK), 0)

    def shifted(d):
        cols = []
        for u_ref in u_refs:
            head = jnp.where(row >= d, pltpu.roll(u_ref[0:SUBLANES, :], d, axis=0), 0.0)
            cols.append(jnp.concatenate([head, u_ref[SUBLANES - d:seq - d, :]], axis=0))
        return jnp.concatenate(cols, axis=1)

    cw = cw_ref[...]
    uc = cw[0:1, :] * shifted(3)
    uc = uc + cw[1:2, :] * shifted(2)
    uc = uc + cw[2:3, :] * shifted(1)
    uc = uc + cw[3:4, :] * jnp.concatenate([u_ref[...] for u_ref in u_refs], axis=1)
    uc = cb_ref[...] + uc

    ucb = uc.astype(BF16)
    gates = [_dot(ucb[:, n * LRU_BLOCK:(n + 1) * LRU_BLOCK], wg_ref[n])
             for n in range(width // LRU_BLOCK)]
    pre_r = jnp.concatenate([g[:, :LRU_BLOCK] for g in gates], axis=1)
    pre_i = jnp.concatenate([g[:, LRU_BLOCK:] for g in gates], axis=1)
    half_c = (-0.5 * LRU_C) * _softplus(-lam_ref[...])
    log_a = half_c * jnp.tanh(0.5 * (pre_r + bra_ref[...])) + half_c
    i = 0.5 * jnp.tanh(0.5 * (pre_i + bri_ref[...])) + 0.5
    a = jnp.exp(log_a)
    t = jnp.tanh(log_a)
    w = -2.0 * t
    root = jnp.where(w > 0.0, w * lax.rsqrt(w * (1.0 - t)), 0.0)
    b = root * (i * uc)

    groups = seq // SUBLANES
    gshape = (groups, SUBLANES, width)
    a3 = a.reshape(gshape)
    b3 = b.reshape(gshape)
    sub = lax.broadcasted_iota(jnp.int32, gshape, 1)
    d = 1
    while d < SUBLANES:
        keep = sub >= d
        b3 = a3 * jnp.where(keep, pltpu.roll(b3, d, axis=1), 0.0) + b3
        a3 = a3 * jnp.where(keep, pltpu.roll(a3, d, axis=1), 1.0)
        d *= 2
    at_scr[...] = jnp.broadcast_to(a3[:, SUBLANES - 1:, :], gshape).reshape(seq, width)
    bt_scr[...] = jnp.broadcast_to(b3[:, SUBLANES - 1:, :], gshape).reshape(seq, width)

    def carry(g, h):
        rows = pl.ds(pl.multiple_of(g * SUBLANES, SUBLANES), SUBLANES)
        hp_scr[rows, :] = h
        return at_scr[rows, :] * h + bt_scr[rows, :]

    lax.fori_loop(0, groups, carry, jnp.zeros((SUBLANES, width), F32), unroll=8)
    hs = b3.reshape(seq, width) + a3.reshape(seq, width) * hp_scr[...]
    o_ref[...] = hs * jax.nn.gelu(gate_ref[...])


def _rg_lru(ug, cw, cb, wg, bra, bri, lam, side_weights, layer, *, batch, seq):
    per = LRU_BLOCKS_PER_STEP
    nb = LRU_BLOCKS // per
    width = per * LRU_BLOCK
    vec = lambda: pl.BlockSpec((1, width), lambda b, n: (0, n))
    n_slabs = batch * nb
    sides = [_slab_specs(w, layer, n_slabs, w.shape[2], lambda b, n: b * nb + n)
             for w in side_weights]
    res = pl.pallas_call(
        functools.partial(_lru_kernel, seq=seq, n_side=len(sides)),
        grid=(batch, nb),
        in_specs=[
            *[pl.BlockSpec((None, seq, LRU_BLOCK), lambda b, n, e=e: (b, 0, per * n + e))
              for e in range(per)],
            pl.BlockSpec((None, seq, width), lambda b, n: (b, 0, nb + n)),
            pl.BlockSpec((CONV_W, width), lambda b, n: (0, n)),
            vec(),
            pl.BlockSpec((per, LRU_BLOCK, 2 * LRU_BLOCK), lambda b, n: (n, 0, 0)),
            vec(), vec(), vec(),
            *[sd[0] for sd in sides],
        ],
        out_specs=[pl.BlockSpec((None, seq, width), lambda b, n: (b, 0, n)),
                   *[sd[1] for sd in sides]],
        out_shape=[jax.ShapeDtypeStruct((batch, seq, LRU_W), F32), *[sd[2] for sd in sides]],
        scratch_shapes=[pltpu.VMEM((seq, width), F32)] * 3,
        compiler_params=_params("arbitrary", "arbitrary"),
    )(*[ug] * per, ug, cw, cb, wg, bra, bri, lam, *side_weights)
    return res[0], [r[0] for r in res[1:]]


def _mix_xattn_kernel(of_ref, yl_ref, gf_ref, gl_ref, wout_ref, x_ref, gx_ref, wq_ref, gq_ref,
                      mem_ref, gm_ref, wkv_ref, gk_ref, wo_ref, side_ref, o_ref, side_out_ref,
                      ckv_ref, *, chunk):
    @pl.when(pl.program_id(1) == 0)
    def _():
        kv = _dot(_rms(mem_ref[...], gm_ref[...]).astype(BF16), wkv_ref[...])
        for hh in range(XATT_HEADS):
            sl = slice(hh * HEAD_DIM, (hh + 1) * HEAD_DIM)
            ckv_ref[:, sl] = _rms(kv[:, sl], gk_ref[...]).astype(BF16)
        ckv_ref[:, XATT_W:] = kv[:, XATT_W:].astype(BF16)

    _cast_slab(side_ref, side_out_ref)
    mf = _rms(of_ref[...], gf_ref[...]).astype(BF16)
    ml = _rms(yl_ref[...], gl_ref[...]).astype(BF16)
    for c in range(o_ref.shape[1] // chunk):
        sl = slice(c * chunk, (c + 1) * chunk)
        acc = _dot(mf, wout_ref[:FOX_W, sl]) + _dot(ml, wout_ref[FOX_W:, sl])
        o_ref[:, sl] = x_ref[:, sl] + acc

    x1 = o_ref[...]
    cq = _dot(_rms(x1, gx_ref[...]).astype(BF16), wq_ref[...])
    scale = 1.0 / math.sqrt(HEAD_DIM)
    scores = []
    for hh in range(XATT_HEADS):
        sl = slice(hh * HEAD_DIM, (hh + 1) * HEAD_DIM)
        qh = _rms(cq[:, sl], gq_ref[...]).astype(BF16)
        scores.append(_dot_nt(qh, ckv_ref[:, sl]))
    heads = []
    for hh in range(XATT_HEADS):
        vh = ckv_ref[:, XATT_W + hh * HEAD_DIM:XATT_W + (hh + 1) * HEAD_DIM]
        s = scores[hh] * scale
        e = jnp.exp(s - jnp.max(s, axis=-1, keepdims=True))
        l = jnp.sum(e, axis=-1, keepdims=True)
        heads.append((_dot(e.astype(BF16), vh) / l).astype(BF16))
    ox = jnp.concatenate(heads, axis=-1)
    o_ref[...] = x1 + _dot(ox, wo_ref[...])


def _mix_xattn(o_fox, y_lru, gf, gl, w_out, x3d, gx, w_cq, gq, mem, gm, w_ckv, gk, w_co,
               w_side, layer, *, tm):
    B, S, D = x3d.shape
    M = mem.shape[1]
    n_tiles = S // tm
    side_in, side_out, side_shape = _slab_specs(w_side, layer, B * n_tiles, w_side.shape[2],
                                                lambda b, i: b * n_tiles + i)
    const = lambda shape: pl.BlockSpec(shape, lambda b, i: (0,) * len(shape),
                                       pipeline_mode=pl.Buffered(1))
    tile = lambda w: pl.BlockSpec((None, tm, w), lambda b, i: (b, i, 0))
    return pl.pallas_call(
        functools.partial(_mix_xattn_kernel, chunk=2 * PROJ_CHUNK),
        grid=(B, S // tm),
        in_specs=[
            tile(FOX_W), tile(LRU_W), const((1, FOX_W)), const((1, LRU_W)),
            const((FOX_W + LRU_W, D)), tile(D), const((1, D)), const((D, XATT_W)),
            const((1, HEAD_DIM)),
            pl.BlockSpec((None, M, D), lambda b, i: (b, 0, 0)),
            const((1, D)), const((D, 2 * XATT_W)), const((1, HEAD_DIM)),
            const((XATT_W, D)),
            side_in,
        ],
        out_specs=[tile(D), side_out],
        out_shape=[jax.ShapeDtypeStruct((B, S, D), F32), side_shape],
        scratch_shapes=[pltpu.VMEM((M, 2 * XATT_W), BF16)],
        compiler_params=_params("arbitrary", "arbitrary"),
    )(o_fox, y_lru, gf, gl, w_out, x3d, gx, w_cq, gq, mem, gm, w_ckv, gk, w_co, w_side)


def _ffn_kernel(x_ref, g_ref, wg_ref, wu_ref, wd_ref, o_hbm, acc_scr, h_scr, sems,
                *, tm, nk, n_groups):
    p, k, r = pl.program_id(0), pl.program_id(1), pl.program_id(2)
    acc_ref = acc_scr.at[r]
    h_ref = h_scr.at[r]
    row_tile = p * FFN_ROW_TILES + r

    def writeback(tile, slot):
        rows = pl.ds(pl.multiple_of(tile * tm, tm), tm)
        return pltpu.make_async_copy(acc_scr.at[slot], o_hbm.at[rows, :], sems.at[slot])

    def hidden_tile(base_ref):
        h = h_ref[...]
        gate = _dot(h, wg_ref[...])
        up = _dot(h, wu_ref[...])
        act = (jax.nn.silu(gate) * up).astype(BF16)
        acc_ref[...] = base_ref[...] + _dot(act, wd_ref[...])

    @pl.when((k == 0) & (p > 0))
    def _():
        writeback(row_tile - FFN_ROW_TILES, r).wait()

    @pl.when(k == 0)
    def _():
        h_ref[...] = _rms(x_ref[...], g_ref[...]).astype(BF16)
        hidden_tile(x_ref)

    pl.when(k > 0)(lambda: hidden_tile(acc_ref))

    @pl.when(k == nk - 1)
    def _():
        writeback(row_tile, r).start()

    @pl.when((k == nk - 1) & (p == n_groups - 1) & (r == FFN_ROW_TILES - 1))
    def _():
        for slot in range(FFN_ROW_TILES):
            writeback((n_groups - 1) * FFN_ROW_TILES + slot, slot).wait()


def _ffn(x2d, g, w_gu, w_d, *, tm):
    T, D = x2d.shape
    H = w_d.shape[0]
    th = w_gu.shape[2]
    nk = H // th
    per = FFN_ROW_TILES
    n_rows = T // tm
    n_groups = n_rows // per
    x_row = lambda p, k, r: jnp.where(k == 0, per * p + r,
                                      jnp.minimum(per * (p + 1), n_rows - 1))
    return pl.pallas_call(
        functools.partial(_ffn_kernel, tm=tm, nk=nk, n_groups=n_groups),
        grid=(n_groups, nk, per),
        in_specs=[
            pl.BlockSpec((tm, D), lambda p, k, r: (x_row(p, k, r), 0)),
            pl.BlockSpec((1, D), lambda p, k, r: (0, 0)),
            pl.BlockSpec((None, D, th), lambda p, k, r: (k, 0, 0)),
            pl.BlockSpec((None, D, th), lambda p, k, r: (nk + k, 0, 0)),
            pl.BlockSpec((th, D), lambda p, k, r: (k, 0)),
        ],
        out_specs=pl.BlockSpec(memory_space=pl.ANY),
        out_shape=jax.ShapeDtypeStruct((T, D), F32),
        scratch_shapes=[pltpu.VMEM((per, tm, D), F32), pltpu.VMEM((per, tm, D), BF16),
                        pltpu.SemaphoreType.DMA((per,))],
        compiler_params=_params("arbitrary", "arbitrary", "arbitrary"),
    )(x2d, g, w_gu, w_gu, w_d)


def kernel(x, mem, g_mix, w_in, b_f, g_q, g_k, conv_w, conv_b, w_ra, b_ra, w_ri, b_ri, lam,
           g_fox_out, g_lru_out, w_out, g_xattn, g_mem, w_cq, w_ckv, g_cq, g_ck, w_co, g_ffn,
           w_gate_up, w_down):
    B, S, D = x.shape
    T = B * S
    depth = g_mix.shape[0]
    row = lambda v: v.reshape(1, -1).astype(F32)

    for l in range(depth):
        w_t, w_f = _stage_in_weights(jnp.swapaxes(w_in, 1, 2), l, tr=512)
        g_qk = jnp.stack([g_q[l] * FOX_Q_SCALE, g_k[l]]).reshape(2, 1, HEAD_DIM).astype(F32)
        bf_col = jnp.zeros((F_ROWS, 1), F32).at[:FOX_HEADS, 0].set(b_f[l])
        w_gates = jnp.concatenate([w_ra[l], w_ri[l]], axis=-1).astype(BF16)

        x2d = x.reshape(T, D)

        qk, v_t, ug, f_t = _in_proj(x2d, row(g_mix[l]), w_t, g_qk, w_f, tm=1024, tn=1024)
        o_fox, w_gu_bf = _fox_attention(qk.reshape(B, S, 2 * FOX_W), v_t, f_t, bf_col,
                                        w_gate_up, l, batch=B, seq=S,
                                        n_slabs=B * FOX_HEADS // HEADS_PER_STEP,
                                        side_tc=FFN_TILE)
        y_lru, (w_out_bf, w_cq_bf, w_ckv_bf, w_co_bf) = _rg_lru(
            ug.reshape(B, S, 2 * LRU_W), conv_w[l], row(conv_b[l]), w_gates, row(b_ra[l]),
            row(b_ri[l]), row(lam[l]), [w_out, w_cq, w_ckv, w_co], l, batch=B, seq=S)

        x3d, w_down_bf = _mix_xattn(
            o_fox, y_lru, row(g_fox_out[l]), row(g_lru_out[l]), w_out_bf, x2d.reshape(B, S, D),
            row(g_xattn[l]), w_cq_bf, row(g_cq[l]), mem, row(g_mem[l]), w_ckv_bf, row(g_ck[l]),
            w_co_bf, w_down, l, tm=512)

        x2d = _ffn(x3d.reshape(T, D), row(g_ffn[l]), w_gu_bf, w_down_bf[0], tm=1024)
        x = x2d.reshape(B, S, D)
    return x
```

```python
import functools
import math

import jax
import jax.numpy as jnp
from jax import lax
from jax.experimental import pallas as pl
from jax.experimental.pallas import tpu as pltpu

F32 = jnp.float32
BF16 = jnp.bfloat16

HEAD_DIM = 128
FOX_HEADS = 8
FOX_W = FOX_HEADS * HEAD_DIM
LRU_BLOCKS = 8
LRU_BLOCK = 128
LRU_W = LRU_BLOCKS * LRU_BLOCK
LRU_C = 8.0
CONV_W = 4
XATT_HEADS = 4
XATT_W = XATT_HEADS * HEAD_DIM
RMS_EPS = 1e-6
F_ROWS = 16
Q_BLOCK = 512
K_CHUNK = 512
DIAG_CHUNK = 256
HEADS_PER_STEP = 2
SCORE_LOOKAHEAD = 4
V_PAD_ROWS = 16
PROJ_CHUNK = 256
SUBLANES = 8
ROW_TILES_PER_WEIGHT = 2
FFN_ROW_TILES = 2
LRU_BLOCKS_PER_STEP = 4
FFN_TILE = 512
FOX_Q_SCALE = math.log2(math.e) / math.sqrt(HEAD_DIM)
BIAS_PIECES = 3

V7X_VMEM_LIMIT_BYTES = 60 * 1024 * 1024


def _rms(x, g):
    ms = jnp.mean(x * x, axis=-1, keepdims=True)
    return x * lax.rsqrt(ms + RMS_EPS) * g


def _dot(a, b):
    return jnp.dot(a, b, preferred_element_type=F32)


def _dot_nt(a, b):
    return lax.dot_general(a, b, (((1,), (1,)), ((), ())), preferred_element_type=F32)


def _softplus(x):
    return jnp.maximum(x, 0.0) + jnp.log1p(jnp.exp(-jnp.abs(x)))


def _cast_slab(src_ref, dst_ref):
    n_tiles, _, tc = dst_ref.shape
    for t in range(n_tiles):
        dst_ref[t] = src_ref[:, t * tc:(t + 1) * tc].astype(BF16)


def _slab_specs(w, layer, n_slabs, tc, step_of):
    _, R, C = w.shape
    rows = R // n_slabs
    slab = lambda *ids: jnp.minimum(step_of(*ids), n_slabs - 1)
    return (pl.BlockSpec((None, rows, C), lambda *ids: (layer, slab(*ids), 0)),
            pl.BlockSpec((C // tc, rows, tc), lambda *ids: (0, slab(*ids), 0)),
            jax.ShapeDtypeStruct((C // tc, R, tc), BF16))


def _params(*sem):
    return pltpu.CompilerParams(dimension_semantics=sem,
                                vmem_limit_bytes=V7X_VMEM_LIMIT_BYTES)


def _wstage_kernel(a_ref, b_ref, w_ref, wf_ref, *, f_tile):
    t = pl.program_id(0)

    @pl.when(t < f_tile)
    def _():
        w_ref[...] = a_ref[...].astype(BF16)

    @pl.when(t >= f_tile)
    def _():
        w_ref[...] = jnp.concatenate([a_ref[FOX_HEADS:, :], b_ref[...]], axis=0).astype(BF16)

    @pl.when(t == f_tile)
    def _():
        wf_ref[:FOX_HEADS, :] = a_ref[:FOX_HEADS, :].astype(BF16)
        wf_ref[FOX_HEADS:, :] = jnp.zeros((F_ROWS - FOX_HEADS, wf_ref.shape[1]), BF16)


def _stage_in_weights(w_t, layer, *, tr):
    _, N, D = w_t.shape
    n_rows = N - FOX_HEADS
    f_tile = 3 * FOX_W // tr
    return pl.pallas_call(
        functools.partial(_wstage_kernel, f_tile=f_tile),
        grid=(n_rows // tr,),
        in_specs=[pl.BlockSpec((None, tr, D), lambda t: (layer, t, 0)),
                  pl.BlockSpec((None, FOX_HEADS, D),
                               lambda t: (layer, (jnp.maximum(t, f_tile) + 1) * (tr // FOX_HEADS), 0))],
        out_specs=[pl.BlockSpec((tr, D), lambda t: (t, 0)),
                   pl.BlockSpec((F_ROWS, D), lambda t: (0, 0))],
        out_shape=[jax.ShapeDtypeStruct((n_rows, D), BF16),
                   jax.ShapeDtypeStruct((F_ROWS, D), BF16)],
        compiler_params=_params("arbitrary"),
    )(w_t, w_t)


def _inproj_kernel(x_ref, g_ref, w_ref, gh_ref, wf_ref,
                   qk_ref, vt_ref, ug_ref, f_ref, h_scr, *, tn, n_qk, n_v):
    j = pl.program_id(1)
    h_ref = h_scr.at[pl.program_id(2)]

    chunks = [slice(c * PROJ_CHUNK, (c + 1) * PROJ_CHUNK) for c in range(tn // PROJ_CHUNK)]

    def qk_tile():
        for sl in chunks:
            acc = _dot_nt(h_ref[...], w_ref[sl, :])
            for hh in range(PROJ_CHUNK // HEAD_DIM):
                lo = sl.start + hh * HEAD_DIM
                qk_ref[:, lo:lo + HEAD_DIM] = _rms(
                    acc[:, hh * HEAD_DIM:(hh + 1) * HEAD_DIM], gh_ref[...]).astype(BF16)

    @pl.when(j == 0)
    def _():
        h_ref[...] = _rms(x_ref[...], g_ref[...]).astype(BF16)
        qk_tile()

    pl.when((j > 0) & (j < n_qk))(qk_tile)

    @pl.when((j >= n_qk) & (j < n_qk + n_v))
    def _():
        for sl in chunks[:-1]:
            vt_ref[sl, :] = _dot_nt(w_ref[sl, :], h_ref[...]).astype(BF16)
        sl = chunks[-1]
        acc = _dot_nt(jnp.concatenate([w_ref[sl, :], wf_ref[...]], axis=0), h_ref[...])
        vt_ref[sl, :] = acc[:PROJ_CHUNK].astype(BF16)
        f_ref[...] = acc[PROJ_CHUNK:]

    @pl.when(j >= n_qk + n_v)
    def _():
        for sl in chunks:
            ug_ref[:, sl] = _dot_nt(h_ref[...], w_ref[sl, :])


def _in_proj(x2d, g, w_t, g_qk, wf, *, tm, tn):
    T, D = x2d.shape
    n_qk = 2 * FOX_W // tn
    n_v = FOX_W // tn
    assert n_v == 1, "the forget logits are produced by the single v step"
    n_ug = 2 * LRU_W // tn
    tiles_per_gain = FOX_W // tn
    clip = lambda v, n: jnp.clip(v, 0, n - 1)
    n_steps = n_qk + n_v + n_ug
    per = ROW_TILES_PER_WEIGHT
    n_rows = T // tm
    first = lambda p: per * p
    last = lambda p: per * p + per - 1

    def row_of(p, j, r, j_lo, j_hi):
        return jnp.where(j < j_lo, first(p), jnp.where(j < j_hi, first(p) + r, last(p)))

    x_row = lambda p, j, r: jnp.where(j == 0, first(p) + r, jnp.minimum(first(p + 1), n_rows - 1))
    return pl.pallas_call(
        functools.partial(_inproj_kernel, tn=tn, n_qk=n_qk, n_v=n_v),
        grid=(n_rows // per, n_steps, per),
        in_specs=[
            pl.BlockSpec((tm, D), lambda p, j, r: (x_row(p, j, r), 0)),
            pl.BlockSpec((1, D), lambda p, j, r: (0, 0)),
            pl.BlockSpec((tn, D), lambda p, j, r: (j, 0)),
            pl.BlockSpec((None, 1, HEAD_DIM),
                         lambda p, j, r: (clip(j // tiles_per_gain, 2), 0, 0)),
            pl.BlockSpec((F_ROWS, D), lambda p, j, r: (0, 0)),
        ],
        out_specs=[
            pl.BlockSpec((tm, tn), lambda p, j, r: (row_of(p, j, r, 0, n_qk), clip(j, n_qk))),
            pl.BlockSpec((tn, tm), lambda p, j, r: (0, row_of(p, j, r, n_qk, n_qk + n_v))),
            pl.BlockSpec((tm, tn), lambda p, j, r: (row_of(p, j, r, n_qk + n_v, n_steps),
                                                    clip(j - n_qk - n_v, n_ug))),
            pl.BlockSpec((F_ROWS, tm), lambda p, j, r: (0, row_of(p, j, r, n_qk, n_qk + n_v))),
        ],
        out_shape=[
            jax.ShapeDtypeStruct((T, 2 * FOX_W), BF16),
            jax.ShapeDtypeStruct((FOX_W, T), BF16),
            jax.ShapeDtypeStruct((T, 2 * LRU_W), F32),
            jax.ShapeDtypeStruct((F_ROWS, T), F32),
        ],
        scratch_shapes=[pltpu.VMEM((per, tm, D), BF16)],
        compiler_params=_params("arbitrary", "arbitrary", "arbitrary"),
    )(x2d, g, w_t, g_qk, wf)


def _fox_kernel(q_ref, k_ref, vt_ref, f_ref, bf_ref, side_ref, o_ref, side_out_ref,
                bias_scr, vaug_scr, *, seq):
    hp = pl.program_id(1)

    @pl.when(hp == 0)
    def _():
        z = f_ref[...] + bf_ref[...]
        c = jnp.minimum(z, 0.0) - jnp.log1p(jnp.exp(-jnp.abs(z)))
        lane = lax.broadcasted_iota(jnp.int32, c.shape, 1)
        d = 1
        while d < seq:
            c = c + jnp.where(lane >= d, pltpu.roll(c, d, axis=1), 0.0)
            d *= 2
        bias = c[:FOX_HEADS] * (-math.log2(math.e))
        pieces = []
        for _ in range(BIAS_PIECES):
            p = bias.astype(BF16).astype(F32)
            pieces.append(p)
            bias = bias - p
        pad = jnp.zeros((HEAD_DIM - BIAS_PIECES * FOX_HEADS, seq), F32)
        bias_scr[...] = jnp.concatenate(pieces + [pad], axis=0).T.astype(BF16)

    _cast_slab(side_ref, side_out_ref)
    ones_row = lax.broadcasted_iota(jnp.int32, (V_PAD_ROWS, seq), 0) == 0
    for e in range(HEADS_PER_STEP):
        lanes = slice(e * HEAD_DIM, (e + 1) * HEAD_DIM)
        vaug_scr[e, :HEAD_DIM, :] = vt_ref[lanes, :]
        vaug_scr[e, HEAD_DIM:, :] = jnp.where(ones_row, 1.0, 0.0).astype(BF16)

    tq, tk = Q_BLOCK, K_CHUNK
    dc = DIAG_CHUNK

    def selector(e, n):
        lane = lax.broadcasted_iota(jnp.int32, (n, HEAD_DIM), 1)
        h = hp * HEADS_PER_STEP + e
        mine = lane == h
        for p in range(1, BIAS_PIECES):
            mine = mine | (lane == h + p * FOX_HEADS)
        return jnp.where(mine, 1.0, 0.0).astype(BF16)

    selectors = {(e, n): selector(e, n) for e in range(HEADS_PER_STEP)
                 for n in range(dc, tq + 1, dc)}
    below_diag = (lax.broadcasted_iota(jnp.int32, (dc, dc), 0)
                  <= lax.broadcasted_iota(jnp.int32, (dc, dc), 1))

    def scores(e, lo, q0, k0, nk):
        q = q_ref[lo + q0:lo + tq, e * HEAD_DIM:(e + 1) * HEAD_DIM]
        q_aug = jnp.concatenate([q, selectors[e, tq - q0]], axis=1)
        k_aug = jnp.concatenate([k_ref[k0:k0 + nk, e * HEAD_DIM:(e + 1) * HEAD_DIM],
                                 bias_scr[k0:k0 + nk, :]], axis=1)
        return _dot_nt(k_aug, q_aug)

    steps = []
    for qi in range(seq // tq):
        lo = qi * tq
        pieces = [(0, kc * tk, tk) for kc in range(lo // tk)]
        pieces += [(j * dc, lo + j * dc, dc) for j in range(tq // dc)]
        steps += [(e, lo) + piece for piece in pieces for e in range(HEADS_PER_STEP)]
    pending = [scores(*st) for st in steps[:SCORE_LOOKAHEAD]]
    m = [None] * HEADS_PER_STEP
    acc = [None] * HEADS_PER_STEP
    for idx, (e, lo, q0, k0, nk) in enumerate(steps):
        t = pending.pop(0)
        if idx + SCORE_LOOKAHEAD < len(steps):
            pending.append(scores(*steps[idx + SCORE_LOOKAHEAD]))
        if k0 >= lo:
            masked = jnp.where(below_diag, t[:, :dc], -jnp.inf)
            t = jnp.concatenate([masked, t[:, dc:]], axis=1) if q0 + dc < tq else masked
        cm = jnp.max(t, axis=0, keepdims=True)
        if k0 == 0:
            m_new = cm
        else:
            m_old = m[e][:, q0:]
            m_new = jnp.maximum(m_old, cm)
        p = jnp.exp2(t - m_new).astype(BF16)
        pv = _dot(vaug_scr[e, :, k0:k0 + nk], p)
        if k0 == 0:
            acc[e], m[e] = pv, m_new
        else:
            upd = acc[e][:, q0:] * jnp.exp2(m_old - m_new) + pv
            acc[e] = jnp.concatenate([acc[e][:, :q0], upd], axis=1) if q0 else upd
            m[e] = jnp.concatenate([m[e][:, :q0], m_new], axis=1) if q0 else m_new
        if k0 + nk == lo + tq:
            inv_l = 1.0 / acc[e][HEAD_DIM:HEAD_DIM + 1, :]
            o_ref[lo:lo + tq, e * HEAD_DIM:(e + 1) * HEAD_DIM] = (acc[e][:HEAD_DIM, :] * inv_l).T


def _fox_attention(qk, v_t, f_t, b_f, w_side, layer, *, batch, seq, n_slabs, side_tc):
    n_pairs = FOX_HEADS // HEADS_PER_STEP
    wide = HEADS_PER_STEP * HEAD_DIM
    side_in, side_out, side_shape = _slab_specs(w_side, layer, n_slabs, side_tc,
                                                lambda b, hp: b * n_pairs + hp)
    return pl.pallas_call(
        functools.partial(_fox_kernel, seq=seq),
        grid=(batch, n_pairs),
        in_specs=[
            pl.BlockSpec((None, seq, wide), lambda b, hp: (b, 0, hp)),
            pl.BlockSpec((None, seq, wide), lambda b, hp: (b, 0, n_pairs + hp)),
            pl.BlockSpec((wide, seq), lambda b, hp: (hp, b)),
            pl.BlockSpec((F_ROWS, seq), lambda b, hp: (0, b)),
            pl.BlockSpec((F_ROWS, 1), lambda b, hp: (0, 0)),
            side_in,
        ],
        out_specs=[pl.BlockSpec((None, seq, wide), lambda b, hp: (b, 0, hp)), side_out],
        out_shape=[jax.ShapeDtypeStruct((batch, seq, FOX_W), F32), side_shape],
        scratch_shapes=[pltpu.VMEM((seq, HEAD_DIM), BF16),
                        pltpu.VMEM((HEADS_PER_STEP, HEAD_DIM + V_PAD_ROWS, seq), BF16)],
        compiler_params=_params("arbitrary", "arbitrary"),
    )(qk, qk, v_t, f_t, b_f, w_side)


def _lru_kernel(*refs, seq, n_side):
    u_refs = refs[:LRU_BLOCKS_PER_STEP]
    refs = refs[LRU_BLOCKS_PER_STEP:]
    gate_ref, cw_ref, cb_ref, wg_ref, bra_ref, bri_ref, lam_ref = refs[:7]
    side_refs = refs[7:7 + n_side]
    o_ref = refs[7 + n_side]
    side_out_refs = refs[8 + n_side:8 + 2 * n_side]
    at_scr, bt_scr = refs[8 + 2 * n_side:]
    for src_ref, dst_ref in zip(side_refs, side_out_refs):
        _cast_slab(src_ref, dst_ref)
    width = o_ref.shape[1]
    row = lax.broadcasted_iota(jnp.int32, (SUBLANES, LRU_BLOCK), 0)

    def shifted(d):
        cols = []
        for u_ref in u_refs:
            head = jnp.where(row >= d, pltpu.roll(u_ref[0:SUBLANES, :], d, axis=0), 0.0)
            cols.append(jnp.concatenate([head, u_ref[SUBLANES - d:seq - d, :]], axis=0))
        return jnp.concatenate(cols, axis=1)

    cw = cw_ref[...]
    uc = cw[0:1, :] * shifted(3)
    uc = uc + cw[1:2, :] * shifted(2)
    uc = uc + cw[2:3, :] * shifted(1)
    uc = uc + cw[3:4, :] * jnp.concatenate([u_ref[...] for u_ref in u_refs], axis=1)
    uc = cb_ref[...] + uc

    ucb = uc.astype(BF16)
    gates = [_dot(ucb[:, n * LRU_BLOCK:(n + 1) * LRU_BLOCK], wg_ref[n])
             for n in range(width // LRU_BLOCK)]
    pre_r = jnp.concatenate([g[:, :LRU_BLOCK] for g in gates], axis=1)
    pre_i = jnp.concatenate([g[:, LRU_BLOCK:] for g in gates], axis=1)
    half_c = (-0.5 * LRU_C) * _softplus(-lam_ref[...])
    log_a = half_c * jnp.tanh(0.5 * (pre_r + bra_ref[...])) + half_c
    i = 0.5 * jnp.tanh(0.5 * (pre_i + bri_ref[...])) + 0.5
    a = jnp.exp(log_a)
    t = jnp.tanh(log_a)
    w = -2.0 * t
    root = jnp.where(w > 0.0, w * lax.rsqrt(w * (1.0 - t)), 0.0)
    b = root * (i * uc)

    n_blocks = width // LRU_BLOCK
    seg = seq // SUBLANES
    pad = SUBLANES
    blank = jnp.zeros((pad, LRU_BLOCK), F32)
    for n in range(n_blocks):
        for scr, val in ((at_scr, a), (bt_scr, b)):
            scr[n, :pad, :] = blank
            scr[n, pad:pad + seq, :] = val[:, n * LRU_BLOCK:(n + 1) * LRU_BLOCK]
            scr[n, pad + seq:, :] = blank

    def local(g, state, edge):
        steps = pl.ds(pad + g, SUBLANES, stride=seg + 1)
        inside = (row + g >= 0) & (row + g < seg)
        out = []
        for n, (h, p) in enumerate(state):
            a_g, b_g = at_scr[n, steps, :], bt_scr[n, steps, :]
            if edge:
                h_new = jnp.where(inside, a_g * h + b_g, h)
                p_new = jnp.where(inside, a_g * p, p)
                bt_scr[n, steps, :] = jnp.where(inside, h_new, b_g)
                at_scr[n, steps, :] = jnp.where(inside, p_new, a_g)
            else:
                h_new, p_new = a_g * h + b_g, a_g * p
                bt_scr[n, steps, :] = h_new
                at_scr[n, steps, :] = p_new
            out.append((h_new, p_new))
        return tuple(out)

    zero = jnp.zeros((SUBLANES, LRU_BLOCK), F32)
    state = ((zero, zero + 1.0),) * n_blocks
    full = seg - SUBLANES
    for g in range(1 - SUBLANES, 0):
        state = local(g, state, True)
    state = lax.fori_loop(0, full, functools.partial(local, edge=False), state, unroll=8)
    for g in range(full, seg):
        state = local(g, state, True)
    cols = []
    for n, (h, p) in enumerate(state):
        d = 1
        while d < SUBLANES:
            keep = row >= d
            h = p * jnp.where(keep, pltpu.roll(h, d, axis=0), 0.0) + h
            p = p * jnp.where(keep, pltpu.roll(p, d, axis=0), 1.0)
            d *= 2
        h_in = jnp.where(row >= 1, pltpu.roll(h, 1, axis=0), 0.0)
        cols.append(jnp.concatenate(
            [bt_scr[n, pad + j * seg:pad + (j + 1) * seg, :]
             + at_scr[n, pad + j * seg:pad + (j + 1) * seg, :] * h_in[j:j + 1, :]
             for j in range(SUBLANES)], axis=0))
    o_ref[...] = jnp.concatenate(cols, axis=1) * jax.nn.gelu(gate_ref[...])


def _rg_lru(ug, cw, cb, wg, bra, bri, lam, side_weights, layer, *, batch, seq):
    per = LRU_BLOCKS_PER_STEP
    nb = LRU_BLOCKS // per
    width = per * LRU_BLOCK
    vec = lambda: pl.BlockSpec((1, width), lambda b, n: (0, n))
    n_slabs = batch * nb
    sides = [_slab_specs(w, layer, n_slabs, w.shape[2], lambda b, n: b * nb + n)
             for w in side_weights]
    res = pl.pallas_call(
        functools.partial(_lru_kernel, seq=seq, n_side=len(sides)),
        grid=(batch, nb),
        in_specs=[
            *[pl.BlockSpec((None, seq, LRU_BLOCK), lambda b, n, e=e: (b, 0, per * n + e))
              for e in range(per)],
            pl.BlockSpec((None, seq, width), lambda b, n: (b, 0, nb + n)),
            pl.BlockSpec((CONV_W, width), lambda b, n: (0, n)),
            vec(),
            pl.BlockSpec((per, LRU_BLOCK, 2 * LRU_BLOCK), lambda b, n: (n, 0, 0)),
            vec(), vec(), vec(),
            *[sd[0] for sd in sides],
        ],
        out_specs=[pl.BlockSpec((None, seq, width), lambda b, n: (b, 0, n)),
                   *[sd[1] for sd in sides]],
        out_shape=[jax.ShapeDtypeStruct((batch, seq, LRU_W), F32), *[sd[2] for sd in sides]],
        scratch_shapes=[pltpu.VMEM((per, seq + 2 * SUBLANES, LRU_BLOCK), F32)] * 2,
        compiler_params=_params("arbitrary", "arbitrary"),
    )(*[ug] * per, ug, cw, cb, wg, bra, bri, lam, *side_weights)
    return res[0], [r[0] for r in res[1:]]


def _mix_xattn_kernel(of_ref, yl_ref, gf_ref, gl_ref, wout_ref, x_ref, gx_ref, wq_ref, gq_ref,
                      mem_ref, gm_ref, wkv_ref, gk_ref, wo_ref, side_ref, o_ref, side_out_ref,
                      ckv_ref, *, chunk):
    @pl.when(pl.program_id(1) == 0)
    def _():
        kv = _dot(_rms(mem_ref[...], gm_ref[...]).astype(BF16), wkv_ref[...])
        for hh in range(XATT_HEADS):
            sl = slice(hh * HEAD_DIM, (hh + 1) * HEAD_DIM)
            ckv_ref[:, sl] = _rms(kv[:, sl], gk_ref[...]).astype(BF16)
        ckv_ref[:, XATT_W:] = kv[:, XATT_W:].astype(BF16)

    _cast_slab(side_ref, side_out_ref)
    mf = _rms(of_ref[...], gf_ref[...]).astype(BF16)
    ml = _rms(yl_ref[...], gl_ref[...]).astype(BF16)
    for c in range(o_ref.shape[1] // chunk):
        sl = slice(c * chunk, (c + 1) * chunk)
        acc = _dot(mf, wout_ref[:FOX_W, sl]) + _dot(ml, wout_ref[FOX_W:, sl])
        o_ref[:, sl] = x_ref[:, sl] + acc

    x1 = o_ref[...]
    cq = _dot(_rms(x1, gx_ref[...]).astype(BF16), wq_ref[...])
    scale = 1.0 / math.sqrt(HEAD_DIM)
    scores = []
    for hh in range(XATT_HEADS):
        sl = slice(hh * HEAD_DIM, (hh + 1) * HEAD_DIM)
        qh = _rms(cq[:, sl], gq_ref[...]).astype(BF16)
        scores.append(_dot_nt(qh, ckv_ref[:, sl]))
    heads = []
    for hh in range(XATT_HEADS):
        vh = ckv_ref[:, XATT_W + hh * HEAD_DIM:XATT_W + (hh + 1) * HEAD_DIM]
        s = scores[hh] * scale
        e = jnp.exp(s - jnp.max(s, axis=-1, keepdims=True))
        l = jnp.sum(e, axis=-1, keepdims=True)
        heads.append((_dot(e.astype(BF16), vh) / l).astype(BF16))
    ox = jnp.concatenate(heads, axis=-1)
    o_ref[...] = x1 + _dot(ox, wo_ref[...])


def _mix_xattn(o_fox, y_lru, gf, gl, w_out, x3d, gx, w_cq, gq, mem, gm, w_ckv, gk, w_co,
               w_side, layer, *, tm):
    B, S, D = x3d.shape
    M = mem.shape[1]
    n_tiles = S // tm
    side_in, side_out, side_shape = _slab_specs(w_side, layer, B * n_tiles, w_side.shape[2],
                                                lambda b, i: b * n_tiles + i)
    const = lambda shape: pl.BlockSpec(shape, lambda b, i: (0,) * len(shape),
                                       pipeline_mode=pl.Buffered(1))
    tile = lambda w: pl.BlockSpec((None, tm, w), lambda b, i: (b, i, 0))
    return pl.pallas_call(
        functools.partial(_mix_xattn_kernel, chunk=2 * PROJ_CHUNK),
        grid=(B, S // tm),
        in_specs=[
            tile(FOX_W), tile(LRU_W), const((1, FOX_W)), const((1, LRU_W)),
            const((FOX_W + LRU_W, D)), tile(D), const((1, D)), const((D, XATT_W)),
            const((1, HEAD_DIM)),
            pl.BlockSpec((None, M, D), lambda b, i: (b, 0, 0)),
            const((1, D)), const((D, 2 * XATT_W)), const((1, HEAD_DIM)),
            const((XATT_W, D)),
            side_in,
        ],
        out_specs=[tile(D), side_out],
        out_shape=[jax.ShapeDtypeStruct((B, S, D), F32), side_shape],
        scratch_shapes=[pltpu.VMEM((M, 2 * XATT_W), BF16)],
        compiler_params=_params("arbitrary", "arbitrary"),
    )(o_fox, y_lru, gf, gl, w_out, x3d, gx, w_cq, gq, mem, gm, w_ckv, gk, w_co, w_side)


def _ffn_kernel(x_ref, g_ref, wg_ref, wu_ref, wd_ref, o_hbm, acc_scr, h_scr, sems,
                *, tm, nk, n_groups):
    p, k, r = pl.program_id(0), pl.program_id(1), pl.program_id(2)
    acc_ref = acc_scr.at[r]
    h_ref = h_scr.at[r]
    row_tile = p * FFN_ROW_TILES + r

    def writeback(tile, slot):
        rows = pl.ds(pl.multiple_of(tile * tm, tm), tm)
        return pltpu.make_async_copy(acc_scr.at[slot], o_hbm.at[rows, :], sems.at[slot])

    def hidden_tile(base_ref):
        h = h_ref[...]
        gate = _dot(h, wg_ref[...])
        up = _dot(h, wu_ref[...])
        act = (jax.nn.silu(gate) * up).astype(BF16)
        acc_ref[...] = base_ref[...] + _dot(act, wd_ref[...])

    @pl.when((k == 0) & (p > 0))
    def _():
        writeback(row_tile - FFN_ROW_TILES, r).wait()

    @pl.when(k == 0)
    def _():
        h_ref[...] = _rms(x_ref[...], g_ref[...]).astype(BF16)
        hidden_tile(x_ref)

    pl.when(k > 0)(lambda: hidden_tile(acc_ref))

    @pl.when(k == nk - 1)
    def _():
        writeback(row_tile, r).start()

    @pl.when((k == nk - 1) & (p == n_groups - 1) & (r == FFN_ROW_TILES - 1))
    def _():
        for slot in range(FFN_ROW_TILES):
            writeback((n_groups - 1) * FFN_ROW_TILES + slot, slot).wait()


def _ffn(x2d, g, w_gu, w_d, *, tm):
    T, D = x2d.shape
    H = w_d.shape[0]
    th = w_gu.shape[2]
    nk = H // th
    per = FFN_ROW_TILES
    n_rows = T // tm
    n_groups = n_rows // per
    x_row = lambda p, k, r: jnp.where(k == 0, per * p + r,
                                      jnp.minimum(per * (p + 1), n_rows - 1))
    return pl.pallas_call(
        functools.partial(_ffn_kernel, tm=tm, nk=nk, n_groups=n_groups),
        grid=(n_groups, nk, per),
        in_specs=[
            pl.BlockSpec((tm, D), lambda p, k, r: (x_row(p, k, r), 0)),
            pl.BlockSpec((1, D), lambda p, k, r: (0, 0)),
            pl.BlockSpec((None, D, th), lambda p, k, r: (k, 0, 0)),
            pl.BlockSpec((None, D, th), lambda p, k, r: (nk + k, 0, 0)),
            pl.BlockSpec((th, D), lambda p, k, r: (k, 0)),
        ],
        out_specs=pl.BlockSpec(memory_space=pl.ANY),
        out_shape=jax.ShapeDtypeStruct((T, D), F32),
        scratch_shapes=[pltpu.VMEM((per, tm, D), F32), pltpu.VMEM((per, tm, D), BF16),
                        pltpu.SemaphoreType.DMA((per,))],
        compiler_params=_params("arbitrary", "arbitrary", "arbitrary"),
    )(x2d, g, w_gu, w_gu, w_d)


def kernel(x, mem, g_mix, w_in, b_f, g_q, g_k, conv_w, conv_b, w_ra, b_ra, w_ri, b_ri, lam,
           g_fox_out, g_lru_out, w_out, g_xattn, g_mem, w_cq, w_ckv, g_cq, g_ck, w_co, g_ffn,
           w_gate_up, w_down):
    B, S, D = x.shape
    T = B * S
    depth = g_mix.shape[0]
    row = lambda v: v.reshape(1, -1).astype(F32)

    for l in range(depth):
        w_t, w_f = _stage_in_weights(jnp.swapaxes(w_in, 1, 2), l, tr=512)
        g_qk = jnp.stack([g_q[l] * FOX_Q_SCALE, g_k[l]]).reshape(2, 1, HEAD_DIM).astype(F32)
        bf_col = jnp.zeros((F_ROWS, 1), F32).at[:FOX_HEADS, 0].set(b_f[l])
        w_gates = jnp.concatenate([w_ra[l], w_ri[l]], axis=-1).astype(BF16)

        x2d = x.reshape(T, D)

        qk, v_t, ug, f_t = _in_proj(x2d, row(g_mix[l]), w_t, g_qk, w_f, tm=1024, tn=1024)
        o_fox, w_gu_bf = _fox_attention(qk.reshape(B, S, 2 * FOX_W), v_t, f_t, bf_col,
                                        w_gate_up, l, batch=B, seq=S,
                                        n_slabs=B * FOX_HEADS // HEADS_PER_STEP,
                                        side_tc=FFN_TILE)
        y_lru, (w_out_bf, w_cq_bf, w_ckv_bf, w_co_bf) = _rg_lru(
            ug.reshape(B, S, 2 * LRU_W), conv_w[l], row(conv_b[l]), w_gates, row(b_ra[l]),
            row(b_ri[l]), row(lam[l]), [w_out, w_cq, w_ckv, w_co], l, batch=B, seq=S)

        x3d, w_down_bf = _mix_xattn(
            o_fox, y_lru, row(g_fox_out[l]), row(g_lru_out[l]), w_out_bf, x2d.reshape(B, S, D),
            row(g_xattn[l]), w_cq_bf, row(g_cq[l]), mem, row(g_mem[l]), w_ckv_bf, row(g_ck[l]),
            w_co_bf, w_down, l, tm=512)

        x2d = _ffn(x3d.reshape(T, D), row(g_ffn[l]), w_gu_bf, w_down_bf[0], tm=1024)
        x = x2d.reshape(B, S, D)
    return x
```

```python
import functools
import math

import jax
import jax.numpy as jnp
from jax import lax
from jax.experimental import pallas as pl
from jax.experimental.pallas import tpu as pltpu

F32 = jnp.float32
BF16 = jnp.bfloat16

HEAD_DIM = 128
FOX_HEADS = 8
FOX_W = FOX_HEADS * HEAD_DIM
LRU_BLOCKS = 8
LRU_BLOCK = 128
LRU_W = LRU_BLOCKS * LRU_BLOCK
LRU_C = 8.0
CONV_W = 4
XATT_HEADS = 4
XATT_W = XATT_HEADS * HEAD_DIM
RMS_EPS = 1e-6
F_ROWS = 16
Q_BLOCK = 512
K_CHUNK = 512
DIAG_CHUNK = 256
HEADS_PER_STEP = 2
SCORE_LOOKAHEAD = 4
V_PAD_ROWS = 16
PROJ_CHUNK = 256
SUBLANES = 8
ROW_TILES_PER_WEIGHT = 2
FFN_ROW_TILES = 2
LRU_BLOCKS_PER_STEP = 4
FFN_TILE = 512
FOX_Q_SCALE = math.log2(math.e) / math.sqrt(HEAD_DIM)
BIAS_PIECES = 3

V7X_VMEM_LIMIT_BYTES = 60 * 1024 * 1024


def _rms(x, g):
    ms = jnp.mean(x * x, axis=-1, keepdims=True)
    return x * lax.rsqrt(ms + RMS_EPS) * g


def _dot(a, b):
    return jnp.dot(a, b, preferred_element_type=F32)


def _dot_nt(a, b):
    return lax.dot_general(a, b, (((1,), (1,)), ((), ())), preferred_element_type=F32)


def _softplus(x):
    return jnp.maximum(x, 0.0) + jnp.log1p(jnp.exp(-jnp.abs(x)))


def _cast_slab(src_ref, dst_ref):
    n_tiles, _, tc = dst_ref.shape
    for t in range(n_tiles):
        dst_ref[t] = src_ref[:, t * tc:(t + 1) * tc].astype(BF16)


def _slab_specs(w, layer, n_slabs, tc, step_of):
    _, R, C = w.shape
    rows = R // n_slabs
    slab = lambda *ids: jnp.minimum(step_of(*ids), n_slabs - 1)
    return (pl.BlockSpec((None, rows, C), lambda *ids: (layer, slab(*ids), 0)),
            pl.BlockSpec((C // tc, rows, tc), lambda *ids: (0, slab(*ids), 0)),
            jax.ShapeDtypeStruct((C // tc, R, tc), BF16))


def _params(*sem):
    return pltpu.CompilerParams(dimension_semantics=sem,
                                vmem_limit_bytes=V7X_VMEM_LIMIT_BYTES)


def _wstage_kernel(a_ref, b_ref, w_ref, wf_ref, *, f_tile):
    t = pl.program_id(0)

    @pl.when(t < f_tile)
    def _():
        w_ref[...] = a_ref[...].astype(BF16)

    @pl.when(t >= f_tile)
    def _():
        w_ref[...] = jnp.concatenate([a_ref[FOX_HEADS:, :], b_ref[...]], axis=0).astype(BF16)

    @pl.when(t == f_tile)
    def _():
        wf_ref[:FOX_HEADS, :] = a_ref[:FOX_HEADS, :].astype(BF16)
        wf_ref[FOX_HEADS:, :] = jnp.zeros((F_ROWS - FOX_HEADS, wf_ref.shape[1]), BF16)


def _stage_in_weights(w_t, layer, *, tr):
    _, N, D = w_t.shape
    n_rows = N - FOX_HEADS
    f_tile = 3 * FOX_W // tr
    return pl.pallas_call(
        functools.partial(_wstage_kernel, f_tile=f_tile),
        grid=(n_rows // tr,),
        in_specs=[pl.BlockSpec((None, tr, D), lambda t: (layer, t, 0)),
                  pl.BlockSpec((None, FOX_HEADS, D),
                               lambda t: (layer, (jnp.maximum(t, f_tile) + 1) * (tr // FOX_HEADS), 0))],
        out_specs=[pl.BlockSpec((tr, D), lambda t: (t, 0)),
                   pl.BlockSpec((F_ROWS, D), lambda t: (0, 0))],
        out_shape=[jax.ShapeDtypeStruct((n_rows, D), BF16),
                   jax.ShapeDtypeStruct((F_ROWS, D), BF16)],
        compiler_params=_params("arbitrary"),
    )(w_t, w_t)


def _inproj_kernel(x_ref, g_ref, w_ref, gh_ref, wf_ref,
                   qk_ref, vt_ref, ug_ref, f_ref, h_scr, *, tn, n_qk, n_v):
    j = pl.program_id(1)
    h_ref = h_scr.at[pl.program_id(2)]

    chunks = [slice(c * PROJ_CHUNK, (c + 1) * PROJ_CHUNK) for c in range(tn // PROJ_CHUNK)]

    def qk_tile():
        for sl in chunks:
            acc = _dot_nt(h_ref[...], w_ref[sl, :])
            for hh in range(PROJ_CHUNK // HEAD_DIM):
                lo = sl.start + hh * HEAD_DIM
                qk_ref[:, lo:lo + HEAD_DIM] = _rms(
                    acc[:, hh * HEAD_DIM:(hh + 1) * HEAD_DIM], gh_ref[...]).astype(BF16)

    @pl.when(j == 0)
    def _():
        h_ref[...] = _rms(x_ref[...], g_ref[...]).astype(BF16)
        qk_tile()

    pl.when((j > 0) & (j < n_qk))(qk_tile)

    @pl.when((j >= n_qk) & (j < n_qk + n_v))
    def _():
        for sl in chunks[:-1]:
            vt_ref[sl, :] = _dot_nt(w_ref[sl, :], h_ref[...]).astype(BF16)
        sl = chunks[-1]
        acc = _dot_nt(jnp.concatenate([w_ref[sl, :], wf_ref[...]], axis=0), h_ref[...])
        vt_ref[sl, :] = acc[:PROJ_CHUNK].astype(BF16)
        f_ref[...] = acc[PROJ_CHUNK:]

    @pl.when(j >= n_qk + n_v)
    def _():
        for sl in chunks:
            ug_ref[:, sl] = _dot_nt(h_ref[...], w_ref[sl, :])


def _in_proj(x2d, g, w_t, g_qk, wf, *, tm, tn):
    T, D = x2d.shape
    n_qk = 2 * FOX_W // tn
    n_v = FOX_W // tn
    assert n_v == 1, "the forget logits are produced by the single v step"
    n_ug = 2 * LRU_W // tn
    tiles_per_gain = FOX_W // tn
    clip = lambda v, n: jnp.clip(v, 0, n - 1)
    n_steps = n_qk + n_v + n_ug
    per = ROW_TILES_PER_WEIGHT
    n_rows = T // tm
    first = lambda p: per * p
    last = lambda p: per * p + per - 1

    def row_of(p, j, r, j_lo, j_hi):
        return jnp.where(j < j_lo, first(p), jnp.where(j < j_hi, first(p) + r, last(p)))

    x_row = lambda p, j, r: jnp.where(j == 0, first(p) + r, jnp.minimum(first(p + 1), n_rows - 1))
    return pl.pallas_call(
        functools.partial(_inproj_kernel, tn=tn, n_qk=n_qk, n_v=n_v),
        grid=(n_rows // per, n_steps, per),
        in_specs=[
            pl.BlockSpec((tm, D), lambda p, j, r: (x_row(p, j, r), 0)),
            pl.BlockSpec((1, D), lambda p, j, r: (0, 0)),
            pl.BlockSpec((tn, D), lambda p, j, r: (j, 0)),
            pl.BlockSpec((None, 1, HEAD_DIM),
                         lambda p, j, r: (clip(j // tiles_per_gain, 2), 0, 0)),
            pl.BlockSpec((F_ROWS, D), lambda p, j, r: (0, 0)),
        ],
        out_specs=[
            pl.BlockSpec((tm, tn), lambda p, j, r: (row_of(p, j, r, 0, n_qk), clip(j, n_qk))),
            pl.BlockSpec((tn, tm), lambda p, j, r: (0, row_of(p, j, r, n_qk, n_qk + n_v))),
            pl.BlockSpec((tm, tn), lambda p, j, r: (row_of(p, j, r, n_qk + n_v, n_steps),
                                                    clip(j - n_qk - n_v, n_ug))),
            pl.BlockSpec((F_ROWS, tm), lambda p, j, r: (0, row_of(p, j, r, n_qk, n_qk + n_v))),
        ],
        out_shape=[
            jax.ShapeDtypeStruct((T, 2 * FOX_W), BF16),
            jax.ShapeDtypeStruct((FOX_W, T), BF16),
            jax.ShapeDtypeStruct((T, 2 * LRU_W), F32),
            jax.ShapeDtypeStruct((F_ROWS, T), F32),
        ],
        scratch_shapes=[pltpu.VMEM((per, tm, D), BF16)],
        compiler_params=_params("arbitrary", "arbitrary", "arbitrary"),
    )(x2d, g, w_t, g_qk, wf)


def _fox_kernel(q_ref, k_ref, vt_ref, f_ref, bf_ref, side_ref, o_ref, side_out_ref,
                bias_scr, vaug_scr, *, seq):
    hp = pl.program_id(1)

    @pl.when(hp == 0)
    def _():
        z = f_ref[...] + bf_ref[...]
        c = jnp.minimum(z, 0.0) - jnp.log1p(jnp.exp(-jnp.abs(z)))
        lane = lax.broadcasted_iota(jnp.int32, c.shape, 1)
        d = 1
        while d < seq:
            c = c + jnp.where(lane >= d, pltpu.roll(c, d, axis=1), 0.0)
            d *= 2
        bias = c[:FOX_HEADS] * (-math.log2(math.e))
        pieces = []
        for _ in range(BIAS_PIECES):
            p = bias.astype(BF16).astype(F32)
            pieces.append(p)
            bias = bias - p
        pad = jnp.zeros((HEAD_DIM - BIAS_PIECES * FOX_HEADS, seq), F32)
        bias_scr[...] = jnp.concatenate(pieces + [pad], axis=0).T.astype(BF16)

    _cast_slab(side_ref, side_out_ref)
    ones_row = lax.broadcasted_iota(jnp.int32, (V_PAD_ROWS, seq), 0) == 0
    for e in range(HEADS_PER_STEP):
        lanes = slice(e * HEAD_DIM, (e + 1) * HEAD_DIM)
        vaug_scr[e, :HEAD_DIM, :] = vt_ref[lanes, :]
        vaug_scr[e, HEAD_DIM:, :] = jnp.where(ones_row, 1.0, 0.0).astype(BF16)

    tq, tk = Q_BLOCK, K_CHUNK
    dc = DIAG_CHUNK

    def selector(e, n):
        lane = lax.broadcasted_iota(jnp.int32, (n, HEAD_DIM), 1)
        h = hp * HEADS_PER_STEP + e
        mine = lane == h
        for p in range(1, BIAS_PIECES):
            mine = mine | (lane == h + p * FOX_HEADS)
        return jnp.where(mine, 1.0, 0.0).astype(BF16)

    selectors = {(e, n): selector(e, n) for e in range(HEADS_PER_STEP)
                 for n in range(dc, tq + 1, dc)}
    below_diag = (lax.broadcasted_iota(jnp.int32, (dc, dc), 0)
                  <= lax.broadcasted_iota(jnp.int32, (dc, dc), 1))

    def scores(e, lo, q0, k0, nk):
        q = q_ref[lo + q0:lo + tq, e * HEAD_DIM:(e + 1) * HEAD_DIM]
        q_aug = jnp.concatenate([q, selectors[e, tq - q0]], axis=1)
        k_aug = jnp.concatenate([k_ref[k0:k0 + nk, e * HEAD_DIM:(e + 1) * HEAD_DIM],
                                 bias_scr[k0:k0 + nk, :]], axis=1)
        return _dot_nt(k_aug, q_aug)

    steps = []
    for qi in range(seq // tq):
        lo = qi * tq
        pieces = [(0, kc * tk, tk) for kc in range(lo // tk)]
        pieces += [(j * dc, lo + j * dc, dc) for j in range(tq // dc)]
        steps += [(e, lo) + piece for piece in pieces for e in range(HEADS_PER_STEP)]
    pending = [scores(*st) for st in steps[:SCORE_LOOKAHEAD]]
    m = [None] * HEADS_PER_STEP
    acc = [None] * HEADS_PER_STEP
    for idx, (e, lo, q0, k0, nk) in enumerate(steps):
        t = pending.pop(0)
        if idx + SCORE_LOOKAHEAD < len(steps):
            pending.append(scores(*steps[idx + SCORE_LOOKAHEAD]))
        if k0 >= lo:
            masked = jnp.where(below_diag, t[:, :dc], -jnp.inf)
            t = jnp.concatenate([masked, t[:, dc:]], axis=1) if q0 + dc < tq else masked
        cm = jnp.max(t, axis=0, keepdims=True)
        if k0 == 0:
            m_new = cm
        else:
            m_old = m[e][:, q0:]
            m_new = jnp.maximum(m_old, cm)
        p = jnp.exp2(t - m_new).astype(BF16)
        pv = _dot(vaug_scr[e, :, k0:k0 + nk], p)
        if k0 == 0:
            acc[e], m[e] = pv, m_new
        else:
            upd = acc[e][:, q0:] * jnp.exp2(m_old - m_new) + pv
            acc[e] = jnp.concatenate([acc[e][:, :q0], upd], axis=1) if q0 else upd
            m[e] = jnp.concatenate([m[e][:, :q0], m_new], axis=1) if q0 else m_new
        if k0 + nk == lo + tq:
            inv_l = 1.0 / acc[e][HEAD_DIM:HEAD_DIM + 1, :]
            o_ref[lo:lo + tq, e * HEAD_DIM:(e + 1) * HEAD_DIM] = (acc[e][:HEAD_DIM, :] * inv_l).T


def _fox_attention(qk, v_t, f_t, b_f, w_side, layer, *, batch, seq, n_slabs, side_tc):
    n_pairs = FOX_HEADS // HEADS_PER_STEP
    wide = HEADS_PER_STEP * HEAD_DIM
    side_in, side_out, side_shape = _slab_specs(w_side, layer, n_slabs, side_tc,
                                                lambda b, hp: b * n_pairs + hp)
    return pl.pallas_call(
        functools.partial(_fox_kernel, seq=seq),
        grid=(batch, n_pairs),
        in_specs=[
            pl.BlockSpec((None, seq, wide), lambda b, hp: (b, 0, hp)),
            pl.BlockSpec((None, seq, wide), lambda b, hp: (b, 0, n_pairs + hp)),
            pl.BlockSpec((wide, seq), lambda b, hp: (hp, b)),
            pl.BlockSpec((F_ROWS, seq), lambda b, hp: (0, b)),
            pl.BlockSpec((F_ROWS, 1), lambda b, hp: (0, 0)),
            side_in,
        ],
        out_specs=[pl.BlockSpec((None, seq, wide), lambda b, hp: (b, 0, hp)), side_out],
        out_shape=[jax.ShapeDtypeStruct((batch, seq, FOX_W), F32), side_shape],
        scratch_shapes=[pltpu.VMEM((seq, HEAD_DIM), BF16),
                        pltpu.VMEM((HEADS_PER_STEP, HEAD_DIM + V_PAD_ROWS, seq), BF16)],
        compiler_params=_params("arbitrary", "arbitrary"),
    )(qk, qk, v_t, f_t, b_f, w_side)


def _lru_kernel(*refs, seq, n_side):
    u_refs = refs[:LRU_BLOCKS_PER_STEP]
    refs = refs[LRU_BLOCKS_PER_STEP:]
    gate_ref, cw_ref, cb_ref, wg_ref, bra_ref, bri_ref, lam_ref = refs[:7]
    side_refs = refs[7:7 + n_side]
    o_ref = refs[7 + n_side]
    side_out_refs = refs[8 + n_side:8 + 2 * n_side]
    at_scr, bt_scr = refs[8 + 2 * n_side:]
    for src_ref, dst_ref in zip(side_refs, side_out_refs):
        _cast_slab(src_ref, dst_ref)
    width = o_ref.shape[1]
    row = lax.broadcasted_iota(jnp.int32, (SUBLANES, LRU_BLOCK), 0)

    def shifted(d):
        cols = []
        for u_ref in u_refs:
            head = jnp.where(row >= d, pltpu.roll(u_ref[0:SUBLANES, :], d, axis=0), 0.0)
            cols.append(jnp.concatenate([head, u_ref[SUBLANES - d:seq - d, :]], axis=0))
        return jnp.concatenate(cols, axis=1)

    cw = cw_ref[...]
    uc = cw[0:1, :] * shifted(3)
    uc = uc + cw[1:2, :] * shifted(2)
    uc = uc + cw[2:3, :] * shifted(1)
    uc = uc + cw[3:4, :] * jnp.concatenate([u_ref[...] for u_ref in u_refs], axis=1)
    uc = cb_ref[...] + uc

    ucb = uc.astype(BF16)
    gates = [_dot(ucb[:, n * LRU_BLOCK:(n + 1) * LRU_BLOCK], wg_ref[n])
             for n in range(width // LRU_BLOCK)]
    pre_r = jnp.concatenate([g[:, :LRU_BLOCK] for g in gates], axis=1)
    pre_i = jnp.concatenate([g[:, LRU_BLOCK:] for g in gates], axis=1)
    half_c = (-0.5 * LRU_C) * _softplus(-lam_ref[...])
    log_a = half_c * jnp.tanh(pre_r + 0.5 * bra_ref[...]) + half_c
    neg_2i = -1.0 - jnp.tanh(pre_i + 0.5 * bri_ref[...])
    a = jnp.exp(log_a)
    t = jnp.tanh(log_a)
    neg_half_root = jnp.where(t < 0.0, t * lax.rsqrt((-2.0 * t) * (1.0 - t)), 0.0)
    b = neg_half_root * (neg_2i * uc)

    n_blocks = width // LRU_BLOCK
    seg = seq // SUBLANES
    pad = SUBLANES
    blank = jnp.zeros((pad, LRU_BLOCK), F32)
    for n in range(n_blocks):
        for scr, val in ((at_scr, a), (bt_scr, b)):
            scr[n, :pad, :] = blank
            scr[n, pad:pad + seq, :] = val[:, n * LRU_BLOCK:(n + 1) * LRU_BLOCK]
            scr[n, pad + seq:, :] = blank

    def local(g, state, edge):
        steps = pl.ds(pad + g, SUBLANES, stride=seg + 1)
        inside = (row + g >= 0) & (row + g < seg)
        out = []
        for n, (h, p) in enumerate(state):
            a_g, b_g = at_scr[n, steps, :], bt_scr[n, steps, :]
            if edge:
                h_new = jnp.where(inside, a_g * h + b_g, h)
                p_new = jnp.where(inside, a_g * p, p)
                bt_scr[n, steps, :] = jnp.where(inside, h_new, b_g)
                at_scr[n, steps, :] = jnp.where(inside, p_new, a_g)
            else:
                h_new, p_new = a_g * h + b_g, a_g * p
                bt_scr[n, steps, :] = h_new
                at_scr[n, steps, :] = p_new
            out.append((h_new, p_new))
        return tuple(out)

    zero = jnp.zeros((SUBLANES, LRU_BLOCK), F32)
    state = ((zero, zero + 1.0),) * n_blocks
    full = seg - SUBLANES
    for g in range(1 - SUBLANES, 0):
        state = local(g, state, True)
    state = lax.fori_loop(0, full, functools.partial(local, edge=False), state, unroll=8)
    for g in range(full, seg):
        state = local(g, state, True)
    cols = []
    for n, (h, p) in enumerate(state):
        d = 1
        while d < SUBLANES:
            keep = row >= d
            h = p * jnp.where(keep, pltpu.roll(h, d, axis=0), 0.0) + h
            p = p * jnp.where(keep, pltpu.roll(p, d, axis=0), 1.0)
            d *= 2
        h_in = jnp.where(row >= 1, pltpu.roll(h, 1, axis=0), 0.0)
        cols.append(jnp.concatenate(
            [bt_scr[n, pad + j * seg:pad + (j + 1) * seg, :]
             + at_scr[n, pad + j * seg:pad + (j + 1) * seg, :] * h_in[j:j + 1, :]
             for j in range(SUBLANES)], axis=0))
    o_ref[...] = jnp.concatenate(cols, axis=1) * jax.nn.gelu(gate_ref[...])


def _rg_lru(ug, cw, cb, wg, bra, bri, lam, side_weights, layer, *, batch, seq):
    per = LRU_BLOCKS_PER_STEP
    nb = LRU_BLOCKS // per
    width = per * LRU_BLOCK
    vec = lambda: pl.BlockSpec((1, width), lambda b, n: (0, n))
    n_slabs = batch * nb
    sides = [_slab_specs(w, layer, n_slabs, w.shape[2], lambda b, n: b * nb + n)
             for w in side_weights]
    res = pl.pallas_call(
        functools.partial(_lru_kernel, seq=seq, n_side=len(sides)),
        grid=(batch, nb),
        in_specs=[
            *[pl.BlockSpec((None, seq, LRU_BLOCK), lambda b, n, e=e: (b, 0, per * n + e))
              for e in range(per)],
            pl.BlockSpec((None, seq, width), lambda b, n: (b, 0, nb + n)),
            pl.BlockSpec((CONV_W, width), lambda b, n: (0, n)),
            vec(),
            pl.BlockSpec((per, LRU_BLOCK, 2 * LRU_BLOCK), lambda b, n: (n, 0, 0)),
            vec(), vec(), vec(),
            *[sd[0] for sd in sides],
        ],
        out_specs=[pl.BlockSpec((None, seq, width), lambda b, n: (b, 0, n)),
                   *[sd[1] for sd in sides]],
        out_shape=[jax.ShapeDtypeStruct((batch, seq, LRU_W), F32), *[sd[2] for sd in sides]],
        scratch_shapes=[pltpu.VMEM((per, seq + 2 * SUBLANES, LRU_BLOCK), F32)] * 2,
        compiler_params=_params("arbitrary", "arbitrary"),
    )(*[ug] * per, ug, cw, cb, wg, bra, bri, lam, *side_weights)
    return res[0], [r[0] for r in res[1:]]


def _mix_xattn_kernel(of_ref, yl_ref, gf_ref, gl_ref, wout_ref, x_ref, gx_ref, wq_ref, gq_ref,
                      mem_ref, gm_ref, wkv_ref, gk_ref, wo_ref, side_ref, o_ref, side_out_ref,
                      ckv_ref, *, chunk):
    @pl.when(pl.program_id(1) == 0)
    def _():
        kv = _dot(_rms(mem_ref[...], gm_ref[...]).astype(BF16), wkv_ref[...])
        for hh in range(XATT_HEADS):
            sl = slice(hh * HEAD_DIM, (hh + 1) * HEAD_DIM)
            ckv_ref[:, sl] = _rms(kv[:, sl], gk_ref[...]).astype(BF16)
        ckv_ref[:, XATT_W:] = kv[:, XATT_W:].astype(BF16)

    _cast_slab(side_ref, side_out_ref)
    mf = _rms(of_ref[...], gf_ref[...]).astype(BF16)
    ml = _rms(yl_ref[...], gl_ref[...]).astype(BF16)
    for c in range(o_ref.shape[1] // chunk):
        sl = slice(c * chunk, (c + 1) * chunk)
        acc = _dot(mf, wout_ref[:FOX_W, sl]) + _dot(ml, wout_ref[FOX_W:, sl])
        o_ref[:, sl] = x_ref[:, sl] + acc

    x1 = o_ref[...]
    cq = _dot(_rms(x1, gx_ref[...]).astype(BF16), wq_ref[...])
    scale = 1.0 / math.sqrt(HEAD_DIM)
    scores = []
    for hh in range(XATT_HEADS):
        sl = slice(hh * HEAD_DIM, (hh + 1) * HEAD_DIM)
        qh = _rms(cq[:, sl], gq_ref[...]).astype(BF16)
        scores.append(_dot_nt(qh, ckv_ref[:, sl]))
    heads = []
    for hh in range(XATT_HEADS):
        vh = ckv_ref[:, XATT_W + hh * HEAD_DIM:XATT_W + (hh + 1) * HEAD_DIM]
        s = scores[hh] * scale
        e = jnp.exp(s - jnp.max(s, axis=-1, keepdims=True))
        l = jnp.sum(e, axis=-1, keepdims=True)
        heads.append((_dot(e.astype(BF16), vh) / l).astype(BF16))
    ox = jnp.concatenate(heads, axis=-1)
    o_ref[...] = x1 + _dot(ox, wo_ref[...])


def _mix_xattn(o_fox, y_lru, gf, gl, w_out, x3d, gx, w_cq, gq, mem, gm, w_ckv, gk, w_co,
               w_side, layer, *, tm):
    B, S, D = x3d.shape
    M = mem.shape[1]
    n_tiles = S // tm
    side_in, side_out, side_shape = _slab_specs(w_side, layer, B * n_tiles, w_side.shape[2],
                                                lambda b, i: b * n_tiles + i)
    const = lambda shape: pl.BlockSpec(shape, lambda b, i: (0,) * len(shape),
                                       pipeline_mode=pl.Buffered(1))
    tile = lambda w: pl.BlockSpec((None, tm, w), lambda b, i: (b, i, 0))
    return pl.pallas_call(
        functools.partial(_mix_xattn_kernel, chunk=2 * PROJ_CHUNK),
        grid=(B, S // tm),
        in_specs=[
            tile(FOX_W), tile(LRU_W), const((1, FOX_W)), const((1, LRU_W)),
            const((FOX_W + LRU_W, D)), tile(D), const((1, D)), const((D, XATT_W)),
            const((1, HEAD_DIM)),
            pl.BlockSpec((None, M, D), lambda b, i: (b, 0, 0)),
            const((1, D)), const((D, 2 * XATT_W)), const((1, HEAD_DIM)),
            const((XATT_W, D)),
            side_in,
        ],
        out_specs=[tile(D), side_out],
        out_shape=[jax.ShapeDtypeStruct((B, S, D), F32), side_shape],
        scratch_shapes=[pltpu.VMEM((M, 2 * XATT_W), BF16)],
        compiler_params=_params("arbitrary", "arbitrary"),
    )(o_fox, y_lru, gf, gl, w_out, x3d, gx, w_cq, gq, mem, gm, w_ckv, gk, w_co, w_side)


def _ffn_kernel(x_ref, g_ref, wg_ref, wu_ref, wd_ref, o_hbm, acc_scr, h_scr, sems,
                *, tm, nk, n_groups):
    p, k, r = pl.program_id(0), pl.program_id(1), pl.program_id(2)
    acc_ref = acc_scr.at[r]
    h_ref = h_scr.at[r]
    row_tile = p * FFN_ROW_TILES + r

    def writeback(tile, slot):
        rows = pl.ds(pl.multiple_of(tile * tm, tm), tm)
        return pltpu.make_async_copy(acc_scr.at[slot], o_hbm.at[rows, :], sems.at[slot])

    def hidden_tile(base_ref):
        h = h_ref[...]
        gate = _dot(h, wg_ref[...])
        up = _dot(h, wu_ref[...])
        act = (jax.nn.silu(gate) * up).astype(BF16)
        acc_ref[...] = base_ref[...] + _dot(act, wd_ref[...])

    @pl.when((k == 0) & (p > 0))
    def _():
        writeback(row_tile - FFN_ROW_TILES, r).wait()

    @pl.when(k == 0)
    def _():
        h_ref[...] = _rms(x_ref[...], g_ref[...]).astype(BF16)
        hidden_tile(x_ref)

    pl.when(k > 0)(lambda: hidden_tile(acc_ref))

    @pl.when(k == nk - 1)
    def _():
        writeback(row_tile, r).start()

    @pl.when((k == nk - 1) & (p == n_groups - 1) & (r == FFN_ROW_TILES - 1))
    def _():
        for slot in range(FFN_ROW_TILES):
            writeback((n_groups - 1) * FFN_ROW_TILES + slot, slot).wait()


def _ffn(x2d, g, w_gu, w_d, *, tm):
    T, D = x2d.shape
    H = w_d.shape[0]
    th = w_gu.shape[2]
    nk = H // th
    per = FFN_ROW_TILES
    n_rows = T // tm
    n_groups = n_rows // per
    x_row = lambda p, k, r: jnp.where(k == 0, per * p + r,
                                      jnp.minimum(per * (p + 1), n_rows - 1))
    return pl.pallas_call(
        functools.partial(_ffn_kernel, tm=tm, nk=nk, n_groups=n_groups),
        grid=(n_groups, nk, per),
        in_specs=[
            pl.BlockSpec((tm, D), lambda p, k, r: (x_row(p, k, r), 0)),
            pl.BlockSpec((1, D), lambda p, k, r: (0, 0)),
            pl.BlockSpec((None, D, th), lambda p, k, r: (k, 0, 0)),
            pl.BlockSpec((None, D, th), lambda p, k, r: (nk + k, 0, 0)),
            pl.BlockSpec((th, D), lambda p, k, r: (k, 0)),
        ],
        out_specs=pl.BlockSpec(memory_space=pl.ANY),
        out_shape=jax.ShapeDtypeStruct((T, D), F32),
        scratch_shapes=[pltpu.VMEM((per, tm, D), F32), pltpu.VMEM((per, tm, D), BF16),
                        pltpu.SemaphoreType.DMA((per,))],
        compiler_params=_params("arbitrary", "arbitrary", "arbitrary"),
    )(x2d, g, w_gu, w_gu, w_d)


def kernel(x, mem, g_mix, w_in, b_f, g_q, g_k, conv_w, conv_b, w_ra, b_ra, w_ri, b_ri, lam,
           g_fox_out, g_lru_out, w_out, g_xattn, g_mem, w_cq, w_ckv, g_cq, g_ck, w_co, g_ffn,
           w_gate_up, w_down):
    B, S, D = x.shape
    T = B * S
    depth = g_mix.shape[0]
    row = lambda v: v.reshape(1, -1).astype(F32)

    for l in range(depth):
        w_t, w_f = _stage_in_weights(jnp.swapaxes(w_in, 1, 2), l, tr=512)
        g_qk = jnp.stack([g_q[l] * FOX_Q_SCALE, g_k[l]]).reshape(2, 1, HEAD_DIM).astype(F32)
        bf_col = jnp.zeros((F_ROWS, 1), F32).at[:FOX_HEADS, 0].set(b_f[l])
        w_gates = (0.5 * jnp.concatenate([w_ra[l], w_ri[l]], axis=-1)).astype(BF16)

        x2d = x.reshape(T, D)

        qk, v_t, ug, f_t = _in_proj(x2d, row(g_mix[l]), w_t, g_qk, w_f, tm=1024, tn=1024)
        o_fox, w_gu_bf = _fox_attention(qk.reshape(B, S, 2 * FOX_W), v_t, f_t, bf_col,
                                        w_gate_up, l, batch=B, seq=S,
                                        n_slabs=B * FOX_HEADS // HEADS_PER_STEP,
                                        side_tc=FFN_TILE)
        y_lru, (w_out_bf, w_cq_bf, w_ckv_bf, w_co_bf) = _rg_lru(
            ug.reshape(B, S, 2 * LRU_W), conv_w[l], row(conv_b[l]), w_gates, row(b_ra[l]),
            row(b_ri[l]), row(lam[l]), [w_out, w_cq, w_ckv, w_co], l, batch=B, seq=S)

        x3d, w_down_bf = _mix_xattn(
            o_fox, y_lru, row(g_fox_out[l]), row(g_lru_out[l]), w_out_bf, x2d.reshape(B, S, D),
            row(g_xattn[l]), w_cq_bf, row(g_cq[l]), mem, row(g_mem[l]), w_ckv_bf, row(g_ck[l]),
            w_co_bf, w_down, l, tm=512)

        x2d = _ffn(x3d.reshape(T, D), row(g_ffn[l]), w_gu_bf, w_down_bf[0], tm=1024)
        x = x2d.reshape(B, S, D)
    return x
```

```python
import functools
import math

import jax
import jax.numpy as jnp
from jax import lax
from jax.experimental import pallas as pl
from jax.experimental.pallas import tpu as pltpu

F32 = jnp.float32
BF16 = jnp.bfloat16

HEAD_DIM = 128
FOX_HEADS = 8
FOX_W = FOX_HEADS * HEAD_DIM
LRU_BLOCKS = 8
LRU_BLOCK = 128
LRU_W = LRU_BLOCKS * LRU_BLOCK
LRU_C = 8.0
CONV_W = 4
XATT_HEADS = 4
XATT_W = XATT_HEADS * HEAD_DIM
RMS_EPS = 1e-6
F_ROWS = 16
Q_BLOCK = 512
K_CHUNK = 512
GELU_C1 = math.sqrt(2.0 / math.pi)
GELU_C3 = 0.044715
DIAG_CHUNK = 256
HEADS_PER_STEP = 2
SCORE_LOOKAHEAD = 8
V_PAD_ROWS = 16
PROJ_CHUNK = 256
SUBLANES = 8
ROW_TILES_PER_WEIGHT = 2
FFN_ROW_TILES = 2
LRU_BLOCKS_PER_STEP = 4
FFN_TILE = 512
FOX_Q_SCALE = math.log2(math.e) / math.sqrt(HEAD_DIM)
BIAS_PIECES = 3

V7X_VMEM_LIMIT_BYTES = 60 * 1024 * 1024


def _rms(x, g):
    ms = jnp.mean(x * x, axis=-1, keepdims=True)
    return x * lax.rsqrt(ms + RMS_EPS) * g


def _dot(a, b):
    return jnp.dot(a, b, preferred_element_type=F32)


def _dot_nt(a, b):
    return lax.dot_general(a, b, (((1,), (1,)), ((), ())), preferred_element_type=F32)


def _softplus(x):
    return jnp.maximum(x, 0.0) + jnp.log1p(jnp.exp(-jnp.abs(x)))


def _cast_slab(src_ref, dst_ref):
    n_tiles, _, tc = dst_ref.shape
    for t in range(n_tiles):
        dst_ref[t] = src_ref[:, t * tc:(t + 1) * tc].astype(BF16)


def _slab_specs(w, layer, n_slabs, tc, step_of):
    _, R, C = w.shape
    rows = R // n_slabs
    slab = lambda *ids: jnp.minimum(step_of(*ids), n_slabs - 1)
    return (pl.BlockSpec((None, rows, C), lambda *ids: (layer, slab(*ids), 0)),
            pl.BlockSpec((C // tc, rows, tc), lambda *ids: (0, slab(*ids), 0)),
            jax.ShapeDtypeStruct((C // tc, R, tc), BF16))


def _params(*sem):
    return pltpu.CompilerParams(dimension_semantics=sem,
                                vmem_limit_bytes=V7X_VMEM_LIMIT_BYTES)


def _wstage_kernel(a_ref, b_ref, w_ref, wf_ref, *, f_tile):
    t = pl.program_id(0)

    @pl.when(t < f_tile)
    def _():
        w_ref[...] = a_ref[...].astype(BF16)

    @pl.when(t >= f_tile)
    def _():
        w_ref[...] = jnp.concatenate([a_ref[FOX_HEADS:, :], b_ref[...]], axis=0).astype(BF16)

    @pl.when(t == f_tile)
    def _():
        wf_ref[:FOX_HEADS, :] = a_ref[:FOX_HEADS, :].astype(BF16)
        wf_ref[FOX_HEADS:, :] = jnp.zeros((F_ROWS - FOX_HEADS, wf_ref.shape[1]), BF16)


def _stage_in_weights(w_t, layer, *, tr):
    _, N, D = w_t.shape
    n_rows = N - FOX_HEADS
    f_tile = 3 * FOX_W // tr
    return pl.pallas_call(
        functools.partial(_wstage_kernel, f_tile=f_tile),
        grid=(n_rows // tr,),
        in_specs=[pl.BlockSpec((None, tr, D), lambda t: (layer, t, 0)),
                  pl.BlockSpec((None, FOX_HEADS, D),
                               lambda t: (layer, (jnp.maximum(t, f_tile) + 1) * (tr // FOX_HEADS), 0))],
        out_specs=[pl.BlockSpec((tr, D), lambda t: (t, 0)),
                   pl.BlockSpec((F_ROWS, D), lambda t: (0, 0))],
        out_shape=[jax.ShapeDtypeStruct((n_rows, D), BF16),
                   jax.ShapeDtypeStruct((F_ROWS, D), BF16)],
        compiler_params=_params("arbitrary"),
    )(w_t, w_t)


def _inproj_kernel(x_ref, g_ref, w_ref, gh_ref, wf_ref,
                   qk_ref, vt_ref, ug_ref, f_ref, h_scr, *, tn, n_qk, n_v):
    j = pl.program_id(1)
    h_ref = h_scr.at[pl.program_id(2)]

    chunks = [slice(c * PROJ_CHUNK, (c + 1) * PROJ_CHUNK) for c in range(tn // PROJ_CHUNK)]

    def qk_tile():
        for sl in chunks:
            acc = _dot_nt(h_ref[...], w_ref[sl, :])
            for hh in range(PROJ_CHUNK // HEAD_DIM):
                lo = sl.start + hh * HEAD_DIM
                qk_ref[:, lo:lo + HEAD_DIM] = _rms(
                    acc[:, hh * HEAD_DIM:(hh + 1) * HEAD_DIM], gh_ref[...]).astype(BF16)

    @pl.when(j == 0)
    def _():
        h_ref[...] = _rms(x_ref[...], g_ref[...]).astype(BF16)
        qk_tile()

    pl.when((j > 0) & (j < n_qk))(qk_tile)

    @pl.when((j >= n_qk) & (j < n_qk + n_v))
    def _():
        for sl in chunks[:-1]:
            vt_ref[sl, :] = _dot_nt(w_ref[sl, :], h_ref[...]).astype(BF16)
        sl = chunks[-1]
        acc = _dot_nt(jnp.concatenate([w_ref[sl, :], wf_ref[...]], axis=0), h_ref[...])
        vt_ref[sl, :] = acc[:PROJ_CHUNK].astype(BF16)
        f_ref[...] = acc[PROJ_CHUNK:]

    @pl.when(j >= n_qk + n_v)
    def _():
        for sl in chunks:
            ug_ref[:, sl] = _dot_nt(h_ref[...], w_ref[sl, :])


def _in_proj(x2d, g, w_t, g_qk, wf, *, tm, tn):
    T, D = x2d.shape
    n_qk = 2 * FOX_W // tn
    n_v = FOX_W // tn
    assert n_v == 1, "the forget logits are produced by the single v step"
    n_ug = 2 * LRU_W // tn
    tiles_per_gain = FOX_W // tn
    clip = lambda v, n: jnp.clip(v, 0, n - 1)
    n_steps = n_qk + n_v + n_ug
    per = ROW_TILES_PER_WEIGHT
    n_rows = T // tm
    first = lambda p: per * p
    last = lambda p: per * p + per - 1

    def row_of(p, j, r, j_lo, j_hi):
        return jnp.where(j < j_lo, first(p), jnp.where(j < j_hi, first(p) + r, last(p)))

    x_row = lambda p, j, r: jnp.where(j == 0, first(p) + r, jnp.minimum(first(p + 1), n_rows - 1))
    return pl.pallas_call(
        functools.partial(_inproj_kernel, tn=tn, n_qk=n_qk, n_v=n_v),
        grid=(n_rows // per, n_steps, per),
        in_specs=[
            pl.BlockSpec((tm, D), lambda p, j, r: (x_row(p, j, r), 0)),
            pl.BlockSpec((1, D), lambda p, j, r: (0, 0)),
            pl.BlockSpec((tn, D), lambda p, j, r: (j, 0)),
            pl.BlockSpec((None, 1, HEAD_DIM),
                         lambda p, j, r: (clip(j // tiles_per_gain, 2), 0, 0)),
            pl.BlockSpec((F_ROWS, D), lambda p, j, r: (0, 0)),
        ],
        out_specs=[
            pl.BlockSpec((tm, tn), lambda p, j, r: (row_of(p, j, r, 0, n_qk), clip(j, n_qk))),
            pl.BlockSpec((tn, tm), lambda p, j, r: (0, row_of(p, j, r, n_qk, n_qk + n_v))),
            pl.BlockSpec((tm, tn), lambda p, j, r: (row_of(p, j, r, n_qk + n_v, n_steps),
                                                    clip(j - n_qk - n_v, n_ug))),
            pl.BlockSpec((F_ROWS, tm), lambda p, j, r: (0, row_of(p, j, r, n_qk, n_qk + n_v))),
        ],
        out_shape=[
            jax.ShapeDtypeStruct((T, 2 * FOX_W), BF16),
            jax.ShapeDtypeStruct((FOX_W, T), BF16),
            jax.ShapeDtypeStruct((T, 2 * LRU_W), F32),
            jax.ShapeDtypeStruct((F_ROWS, T), F32),
        ],
        scratch_shapes=[pltpu.VMEM((per, tm, D), BF16)],
        compiler_params=_params("arbitrary", "arbitrary", "arbitrary"),
    )(x2d, g, w_t, g_qk, wf)


def _fox_kernel(q_ref, k_ref, vt_ref, f_ref, bf_ref, side_ref, o_ref, side_out_ref,
                bias_scr, vaug_scr, *, seq):
    hp = pl.program_id(1)

    @pl.when(hp == 0)
    def _():
        z = f_ref[...] + bf_ref[...]
        c = jnp.minimum(z, 0.0) - jnp.log1p(jnp.exp(-jnp.abs(z)))
        lane = lax.broadcasted_iota(jnp.int32, c.shape, 1)
        d = 1
        while d < seq:
            c = c + jnp.where(lane >= d, pltpu.roll(c, d, axis=1), 0.0)
            d *= 2
        bias = c[:FOX_HEADS] * (-math.log2(math.e))
        pieces = []
        for _ in range(BIAS_PIECES):
            p = bias.astype(BF16).astype(F32)
            pieces.append(p)
            bias = bias - p
        pad = jnp.zeros((HEAD_DIM - BIAS_PIECES * FOX_HEADS, seq), F32)
        bias_scr[...] = jnp.concatenate(pieces + [pad], axis=0).T.astype(BF16)

    _cast_slab(side_ref, side_out_ref)
    ones_row = lax.broadcasted_iota(jnp.int32, (V_PAD_ROWS, seq), 0) == 0
    for e in range(HEADS_PER_STEP):
        lanes = slice(e * HEAD_DIM, (e + 1) * HEAD_DIM)
        vaug_scr[e, :HEAD_DIM, :] = vt_ref[lanes, :]
        vaug_scr[e, HEAD_DIM:, :] = jnp.where(ones_row, 1.0, 0.0).astype(BF16)

    tq, tk = Q_BLOCK, K_CHUNK
    dc = DIAG_CHUNK

    def selector(e, n):
        lane = lax.broadcasted_iota(jnp.int32, (n, HEAD_DIM), 1)
        h = hp * HEADS_PER_STEP + e
        mine = lane == h
        for p in range(1, BIAS_PIECES):
            mine = mine | (lane == h + p * FOX_HEADS)
        return jnp.where(mine, 1.0, 0.0).astype(BF16)

    selectors = {(e, n): selector(e, n) for e in range(HEADS_PER_STEP)
                 for n in range(dc, tq + 1, dc)}
    below_diag = (lax.broadcasted_iota(jnp.int32, (dc, dc), 0)
                  <= lax.broadcasted_iota(jnp.int32, (dc, dc), 1))

    def scores(e, lo, q0, k0, nk):
        q = q_ref[lo + q0:lo + tq, e * HEAD_DIM:(e + 1) * HEAD_DIM]
        q_aug = jnp.concatenate([q, selectors[e, tq - q0]], axis=1)
        k_aug = jnp.concatenate([k_ref[k0:k0 + nk, e * HEAD_DIM:(e + 1) * HEAD_DIM],
                                 bias_scr[k0:k0 + nk, :]], axis=1)
        return _dot_nt(k_aug, q_aug)

    steps = []
    for qi in range(seq // tq):
        lo = qi * tq
        pieces = [(0, kc * tk, tk) for kc in range(lo // tk)]
        pieces += [(j * dc, lo + j * dc, dc) for j in range(tq // dc)]
        steps += [(e, lo) + piece for piece in pieces for e in range(HEADS_PER_STEP)]
    pending = [scores(*st) for st in steps[:SCORE_LOOKAHEAD]]
    m = [None] * HEADS_PER_STEP
    acc = [None] * HEADS_PER_STEP
    for idx, (e, lo, q0, k0, nk) in enumerate(steps):
        t = pending.pop(0)
        if idx + SCORE_LOOKAHEAD < len(steps):
            pending.append(scores(*steps[idx + SCORE_LOOKAHEAD]))
        if k0 >= lo:
            masked = jnp.where(below_diag, t[:, :dc], -jnp.inf)
            t = jnp.concatenate([masked, t[:, dc:]], axis=1) if q0 + dc < tq else masked
        cm = jnp.max(t, axis=0, keepdims=True)
        if k0 == 0:
            m_new = cm
        else:
            m_old = m[e][:, q0:]
            m_new = jnp.maximum(m_old, cm)
        p = jnp.exp2(t - m_new).astype(BF16)
        pv = _dot(vaug_scr[e, :, k0:k0 + nk], p)
        if k0 == 0:
            acc[e], m[e] = pv, m_new
        else:
            upd = acc[e][:, q0:] * jnp.exp2(m_old - m_new) + pv
            acc[e] = jnp.concatenate([acc[e][:, :q0], upd], axis=1) if q0 else upd
            m[e] = jnp.concatenate([m[e][:, :q0], m_new], axis=1) if q0 else m_new
        if k0 + nk == lo + tq:
            inv_l = 1.0 / acc[e][HEAD_DIM:HEAD_DIM + 1, :]
            o_ref[lo:lo + tq, e * HEAD_DIM:(e + 1) * HEAD_DIM] = (acc[e][:HEAD_DIM, :] * inv_l).T


def _fox_attention(qk, v_t, f_t, b_f, w_side, layer, *, batch, seq, n_slabs, side_tc):
    n_pairs = FOX_HEADS // HEADS_PER_STEP
    wide = HEADS_PER_STEP * HEAD_DIM
    side_in, side_out, side_shape = _slab_specs(w_side, layer, n_slabs, side_tc,
                                                lambda b, hp: b * n_pairs + hp)
    return pl.pallas_call(
        functools.partial(_fox_kernel, seq=seq),
        grid=(batch, n_pairs),
        in_specs=[
            pl.BlockSpec((None, seq, wide), lambda b, hp: (b, 0, hp)),
            pl.BlockSpec((None, seq, wide), lambda b, hp: (b, 0, n_pairs + hp)),
            pl.BlockSpec((wide, seq), lambda b, hp: (hp, b)),
            pl.BlockSpec((F_ROWS, seq), lambda b, hp: (0, b)),
            pl.BlockSpec((F_ROWS, 1), lambda b, hp: (0, 0)),
            side_in,
        ],
        out_specs=[pl.BlockSpec((None, seq, wide), lambda b, hp: (b, 0, hp)), side_out],
        out_shape=[jax.ShapeDtypeStruct((batch, seq, FOX_W), F32), side_shape],
        scratch_shapes=[pltpu.VMEM((seq, HEAD_DIM), BF16),
                        pltpu.VMEM((HEADS_PER_STEP, HEAD_DIM + V_PAD_ROWS, seq), BF16)],
        compiler_params=_params("arbitrary", "arbitrary"),
    )(qk, qk, v_t, f_t, b_f, w_side)


def _lru_kernel(*refs, seq, n_side):
    u_refs = refs[:LRU_BLOCKS_PER_STEP]
    refs = refs[LRU_BLOCKS_PER_STEP:]
    gate_ref, cw_ref, cb_ref, wg_ref, bra_ref, bri_ref, lam_ref = refs[:7]
    side_refs = refs[7:7 + n_side]
    o_ref = refs[7 + n_side]
    side_out_refs = refs[8 + n_side:8 + 2 * n_side]
    at_scr, bt_scr = refs[8 + 2 * n_side:]
    for src_ref, dst_ref in zip(side_refs, side_out_refs):
        _cast_slab(src_ref, dst_ref)
    width = o_ref.shape[1]
    row = lax.broadcasted_iota(jnp.int32, (SUBLANES, LRU_BLOCK), 0)

    def shifted(d):
        cols = []
        for u_ref in u_refs:
            head = jnp.where(row >= d, pltpu.roll(u_ref[0:SUBLANES, :], d, axis=0), 0.0)
            cols.append(jnp.concatenate([head, u_ref[SUBLANES - d:seq - d, :]], axis=0))
        return jnp.concatenate(cols, axis=1)

    cw = cw_ref[...]
    uc = cw[0:1, :] * shifted(3)
    uc = uc + cw[1:2, :] * shifted(2)
    uc = uc + cw[2:3, :] * shifted(1)
    uc = uc + cw[3:4, :] * jnp.concatenate([u_ref[...] for u_ref in u_refs], axis=1)
    uc = cb_ref[...] + uc

    ucb = uc.astype(BF16)
    gates = [_dot(ucb[:, n * LRU_BLOCK:(n + 1) * LRU_BLOCK], wg_ref[n] * 0.5)
             for n in range(width // LRU_BLOCK)]
    pre_r = jnp.concatenate([g[:, :LRU_BLOCK] for g in gates], axis=1)
    pre_i = jnp.concatenate([g[:, LRU_BLOCK:] for g in gates], axis=1)
    half_c = (-0.5 * LRU_C) * _softplus(-lam_ref[...])
    log_a = half_c * jnp.tanh(pre_r + 0.5 * bra_ref[...]) + half_c
    neg_2i = -1.0 - jnp.tanh(pre_i + 0.5 * bri_ref[...])
    a = jnp.exp(log_a)
    t = jnp.tanh(log_a)
    neg_quarter_root = jnp.where(t < 0.0, t * lax.rsqrt((-8.0 * t) * (1.0 - t)), 0.0)
    b = neg_quarter_root * (neg_2i * uc)

    n_blocks = width // LRU_BLOCK
    seg = seq // SUBLANES
    pad = SUBLANES
    blank = jnp.zeros((pad, LRU_BLOCK), F32)
    for n in range(n_blocks):
        for scr, val in ((at_scr, a), (bt_scr, b)):
            scr[n, :pad, :] = blank
            scr[n, pad:pad + seq, :] = val[:, n * LRU_BLOCK:(n + 1) * LRU_BLOCK]
            scr[n, pad + seq:, :] = blank

    def local(g, state, edge):
        steps = pl.ds(pad + g, SUBLANES, stride=seg + 1)
        inside = (row + g >= 0) & (row + g < seg)
        out = []
        for n, (h, p) in enumerate(state):
            a_g, b_g = at_scr[n, steps, :], bt_scr[n, steps, :]
            if edge:
                h_new = jnp.where(inside, a_g * h + b_g, h)
                p_new = jnp.where(inside, a_g * p, p)
                bt_scr[n, steps, :] = jnp.where(inside, h_new, b_g)
                at_scr[n, steps, :] = jnp.where(inside, p_new, a_g)
            else:
                h_new, p_new = a_g * h + b_g, a_g * p
                bt_scr[n, steps, :] = h_new
                at_scr[n, steps, :] = p_new
            out.append((h_new, p_new))
        return tuple(out)

    zero = jnp.zeros((SUBLANES, LRU_BLOCK), F32)
    state = ((zero, zero + 1.0),) * n_blocks
    full = seg - SUBLANES
    for g in range(1 - SUBLANES, 0):
        state = local(g, state, True)
    state = lax.fori_loop(0, full, functools.partial(local, edge=False), state, unroll=8)
    for g in range(full, seg):
        state = local(g, state, True)
    cols = []
    for n, (h, p) in enumerate(state):
        d = 1
        while d < SUBLANES:
            keep = row >= d
            h = p * jnp.where(keep, pltpu.roll(h, d, axis=0), 0.0) + h
            p = p * jnp.where(keep, pltpu.roll(p, d, axis=0), 1.0)
            d *= 2
        h_in = jnp.where(row >= 1, pltpu.roll(h, 1, axis=0), 0.0)
        cols.append(jnp.concatenate(
            [bt_scr[n, pad + j * seg:pad + (j + 1) * seg, :]
             + at_scr[n, pad + j * seg:pad + (j + 1) * seg, :] * h_in[j:j + 1, :]
             for j in range(SUBLANES)], axis=0))
    gate = gate_ref[...]
    inner = gate * (GELU_C1 + (GELU_C1 * GELU_C3) * (gate * gate))
    o_ref[...] = jnp.concatenate(cols, axis=1) * (gate * (1.0 + jnp.tanh(inner)))


def _rg_lru(ug, cw, cb, wg, bra, bri, lam, side_weights, layer, *, batch, seq):
    per = LRU_BLOCKS_PER_STEP
    nb = LRU_BLOCKS // per
    width = per * LRU_BLOCK
    vec = lambda: pl.BlockSpec((1, width), lambda b, n: (0, n))
    n_slabs = batch * nb
    sides = [_slab_specs(w, layer, n_slabs, w.shape[2], lambda b, n: b * nb + n)
             for w in side_weights]
    res = pl.pallas_call(
        functools.partial(_lru_kernel, seq=seq, n_side=len(sides)),
        grid=(batch, nb),
        in_specs=[
            *[pl.BlockSpec((None, seq, LRU_BLOCK), lambda b, n, e=e: (b, 0, per * n + e))
              for e in range(per)],
            pl.BlockSpec((None, seq, width), lambda b, n: (b, 0, nb + n)),
            pl.BlockSpec((CONV_W, width), lambda b, n: (0, n)),
            vec(),
            pl.BlockSpec((per, LRU_BLOCK, 2 * LRU_BLOCK), lambda b, n: (n, 0, 0)),
            vec(), vec(), vec(),
            *[sd[0] for sd in sides],
        ],
        out_specs=[pl.BlockSpec((None, seq, width), lambda b, n: (b, 0, n)),
                   *[sd[1] for sd in sides]],
        out_shape=[jax.ShapeDtypeStruct((batch, seq, LRU_W), F32), *[sd[2] for sd in sides]],
        scratch_shapes=[pltpu.VMEM((per, seq + 2 * SUBLANES, LRU_BLOCK), F32)] * 2,
        compiler_params=_params("arbitrary", "arbitrary"),
    )(*[ug] * per, ug, cw, cb, wg, bra, bri, lam, *side_weights)
    return res[0], [r[0] for r in res[1:]]


def _mix_xattn_kernel(of_ref, yl_ref, gf_ref, gl_ref, wout_ref, x_ref, gx_ref, wq_ref, gq_ref,
                      mem_ref, gm_ref, wkv_ref, gk_ref, wo_ref, side_ref, o_ref, side_out_ref,
                      ckv_ref, *, chunk):
    @pl.when(pl.program_id(1) == 0)
    def _():
        kv = _dot(_rms(mem_ref[...], gm_ref[...]).astype(BF16), wkv_ref[...])
        for hh in range(XATT_HEADS):
            sl = slice(hh * HEAD_DIM, (hh + 1) * HEAD_DIM)
            ckv_ref[:, sl] = _rms(kv[:, sl], gk_ref[...]).astype(BF16)
        ckv_ref[:, XATT_W:] = kv[:, XATT_W:].astype(BF16)

    _cast_slab(side_ref, side_out_ref)
    mf = _rms(of_ref[...], gf_ref[...]).astype(BF16)
    ml = _rms(yl_ref[...], gl_ref[...]).astype(BF16)
    for c in range(o_ref.shape[1] // chunk):
        sl = slice(c * chunk, (c + 1) * chunk)
        acc = _dot(mf, wout_ref[:FOX_W, sl]) + _dot(ml, wout_ref[FOX_W:, sl])
        o_ref[:, sl] = x_ref[:, sl] + acc

    x1 = o_ref[...]
    cq = _dot(_rms(x1, gx_ref[...]).astype(BF16), wq_ref[...])
    scale = 1.0 / math.sqrt(HEAD_DIM)
    scores = []
    for hh in range(XATT_HEADS):
        sl = slice(hh * HEAD_DIM, (hh + 1) * HEAD_DIM)
        qh = _rms(cq[:, sl], gq_ref[...]).astype(BF16)
        scores.append(_dot_nt(qh, ckv_ref[:, sl]))
    heads = []
    for hh in range(XATT_HEADS):
        vh = ckv_ref[:, XATT_W + hh * HEAD_DIM:XATT_W + (hh + 1) * HEAD_DIM]
        s = scores[hh] * scale
        e = jnp.exp(s - jnp.max(s, axis=-1, keepdims=True))
        l = jnp.sum(e, axis=-1, keepdims=True)
        heads.append((_dot(e.astype(BF16), vh) / l).astype(BF16))
    ox = jnp.concatenate(heads, axis=-1)
    o_ref[...] = x1 + _dot(ox, wo_ref[...])


def _mix_xattn(o_fox, y_lru, gf, gl, w_out, x3d, gx, w_cq, gq, mem, gm, w_ckv, gk, w_co,
               w_side, layer, *, tm):
    B, S, D = x3d.shape
    M = mem.shape[1]
    n_tiles = S // tm
    side_in, side_out, side_shape = _slab_specs(w_side, layer, B * n_tiles, w_side.shape[2],
                                                lambda b, i: b * n_tiles + i)
    const = lambda shape: pl.BlockSpec(shape, lambda b, i: (0,) * len(shape),
                                       pipeline_mode=pl.Buffered(1))
    tile = lambda w: pl.BlockSpec((None, tm, w), lambda b, i: (b, i, 0))
    return pl.pallas_call(
        functools.partial(_mix_xattn_kernel, chunk=2 * PROJ_CHUNK),
        grid=(B, S // tm),
        in_specs=[
            tile(FOX_W), tile(LRU_W), const((1, FOX_W)), const((1, LRU_W)),
            const((FOX_W + LRU_W, D)), tile(D), const((1, D)), const((D, XATT_W)),
            const((1, HEAD_DIM)),
            pl.BlockSpec((None, M, D), lambda b, i: (b, 0, 0)),
            const((1, D)), const((D, 2 * XATT_W)), const((1, HEAD_DIM)),
            const((XATT_W, D)),
            side_in,
        ],
        out_specs=[tile(D), side_out],
        out_shape=[jax.ShapeDtypeStruct((B, S, D), F32), side_shape],
        scratch_shapes=[pltpu.VMEM((M, 2 * XATT_W), BF16)],
        compiler_params=_params("arbitrary", "arbitrary"),
    )(o_fox, y_lru, gf, gl, w_out, x3d, gx, w_cq, gq, mem, gm, w_ckv, gk, w_co, w_side)


def _ffn_kernel(x_ref, g_ref, wg_ref, wu_ref, wd_ref, o_hbm, acc_scr, h_scr, sems,
                *, tm, nk, n_groups):
    p, k, r = pl.program_id(0), pl.program_id(1), pl.program_id(2)
    acc_ref = acc_scr.at[r]
    h_ref = h_scr.at[r]
    row_tile = p * FFN_ROW_TILES + r

    def writeback(tile, slot):
        rows = pl.ds(pl.multiple_of(tile * tm, tm), tm)
        return pltpu.make_async_copy(acc_scr.at[slot], o_hbm.at[rows, :], sems.at[slot])

    def hidden_tile(base_ref):
        h = h_ref[...]
        gate = _dot(h, wg_ref[...])
        up = _dot(h, wu_ref[...])
        act = (jax.nn.silu(gate) * up).astype(BF16)
        acc_ref[...] = base_ref[...] + _dot(act, wd_ref[...])

    @pl.when((k == 0) & (p > 0))
    def _():
        writeback(row_tile - FFN_ROW_TILES, r).wait()

    @pl.when(k == 0)
    def _():
        h_ref[...] = _rms(x_ref[...], g_ref[...]).astype(BF16)
        hidden_tile(x_ref)

    pl.when(k > 0)(lambda: hidden_tile(acc_ref))

    @pl.when(k == nk - 1)
    def _():
        writeback(row_tile, r).start()

    @pl.when((k == nk - 1) & (p == n_groups - 1) & (r == FFN_ROW_TILES - 1))
    def _():
        for slot in range(FFN_ROW_TILES):
            writeback((n_groups - 1) * FFN_ROW_TILES + slot, slot).wait()


def _ffn(x2d, g, w_gu, w_d, *, tm):
    T, D = x2d.shape
    H = w_d.shape[0]
    th = w_gu.shape[2]
    nk = H // th
    per = FFN_ROW_TILES
    n_rows = T // tm
    n_groups = n_rows // per
    x_row = lambda p, k, r: jnp.where(k == 0, per * p + r,
                                      jnp.minimum(per * (p + 1), n_rows - 1))
    return pl.pallas_call(
        functools.partial(_ffn_kernel, tm=tm, nk=nk, n_groups=n_groups),
        grid=(n_groups, nk, per),
        in_specs=[
            pl.BlockSpec((tm, D), lambda p, k, r: (x_row(p, k, r), 0)),
            pl.BlockSpec((1, D), lambda p, k, r: (0, 0)),
            pl.BlockSpec((None, D, th), lambda p, k, r: (k, 0, 0)),
            pl.BlockSpec((None, D, th), lambda p, k, r: (nk + k, 0, 0)),
            pl.BlockSpec((th, D), lambda p, k, r: (k, 0)),
        ],
        out_specs=pl.BlockSpec(memory_space=pl.ANY),
        out_shape=jax.ShapeDtypeStruct((T, D), F32),
        scratch_shapes=[pltpu.VMEM((per, tm, D), F32), pltpu.VMEM((per, tm, D), BF16),
                        pltpu.SemaphoreType.DMA((per,))],
        compiler_params=_params("arbitrary", "arbitrary", "arbitrary"),
    )(x2d, g, w_gu, w_gu, w_d)


def kernel(x, mem, g_mix, w_in, b_f, g_q, g_k, conv_w, conv_b, w_ra, b_ra, w_ri, b_ri, lam,
           g_fox_out, g_lru_out, w_out, g_xattn, g_mem, w_cq, w_ckv, g_cq, g_ck, w_co, g_ffn,
           w_gate_up, w_down):
    B, S, D = x.shape
    T = B * S
    depth = g_mix.shape[0]
    row = lambda v: v.reshape(1, -1).astype(F32)

    for l in range(depth):
        w_t, w_f = _stage_in_weights(jnp.swapaxes(w_in, 1, 2), l, tr=512)
        g_qk = jnp.stack([g_q[l] * FOX_Q_SCALE, g_k[l]]).reshape(2, 1, HEAD_DIM).astype(F32)
        bf_col = jnp.zeros((F_ROWS, 1), F32).at[:FOX_HEADS, 0].set(b_f[l])
        w_gates = jnp.concatenate([w_ra[l], w_ri[l]], axis=-1).astype(BF16)

        x2d = x.reshape(T, D)

        qk, v_t, ug, f_t = _in_proj(x2d, row(g_mix[l]), w_t, g_qk, w_f, tm=1024, tn=1024)
        o_fox, w_gu_bf = _fox_attention(qk.reshape(B, S, 2 * FOX_W), v_t, f_t, bf_col,
                                        w_gate_up, l, batch=B, seq=S,
                                        n_slabs=B * FOX_HEADS // HEADS_PER_STEP,
                                        side_tc=FFN_TILE)
        y_lru, (w_out_bf, w_cq_bf, w_ckv_bf, w_co_bf) = _rg_lru(
            ug.reshape(B, S, 2 * LRU_W), conv_w[l], row(conv_b[l]), w_gates, row(b_ra[l]),
            row(b_ri[l]), row(lam[l]), [w_out, w_cq, w_ckv, w_co], l, batch=B, seq=S)

        x3d, w_down_bf = _mix_xattn(
            o_fox, y_lru, row(g_fox_out[l]), row(g_lru_out[l]), w_out_bf, x2d.reshape(B, S, D),
            row(g_xattn[l]), w_cq_bf, row(g_cq[l]), mem, row(g_mem[l]), w_ckv_bf, row(g_ck[l]),
            w_co_bf, w_down, l, tm=512)

        x2d = _ffn(x3d.reshape(T, D), row(g_ffn[l]), w_gu_bf, w_down_bf[0], tm=1024)
        x = x2d.reshape(B, S, D)
    return x
```

```python
import functools
import math

import jax
import jax.numpy as jnp
from jax import lax
from jax.experimental import pallas as pl
from jax.experimental.pallas import tpu as pltpu

F32 = jnp.float32
BF16 = jnp.bfloat16

HEAD_DIM = 128
FOX_HEADS = 8
FOX_W = FOX_HEADS * HEAD_DIM
LRU_BLOCKS = 8
LRU_BLOCK = 128
LRU_W = LRU_BLOCKS * LRU_BLOCK
LRU_C = 8.0
CONV_W = 4
XATT_HEADS = 4
XATT_W = XATT_HEADS * HEAD_DIM
RMS_EPS = 1e-6
F_ROWS = 16
Q_BLOCK = 512
K_CHUNK = 256
GELU_C1 = math.sqrt(2.0 / math.pi)
GELU_C3 = 0.044715
DIAG_CHUNK = 256
HEADS_PER_STEP = 2
SCORE_LOOKAHEAD = 10
V_PAD_ROWS = 16
PROJ_CHUNK = 256
SUBLANES = 8
ROW_TILES_PER_WEIGHT = 2
FFN_ROW_TILES = 2
LRU_BLOCKS_PER_STEP = 4
FFN_TILE = 512
FOX_Q_SCALE = math.log2(math.e) / math.sqrt(HEAD_DIM)
BIAS_PIECES = 3

V7X_VMEM_LIMIT_BYTES = 60 * 1024 * 1024


def _rms(x, g):
    ms = jnp.mean(x * x, axis=-1, keepdims=True)
    return x * lax.rsqrt(ms + RMS_EPS) * g


def _dot(a, b):
    return jnp.dot(a, b, preferred_element_type=F32)


def _dot_nt(a, b):
    return lax.dot_general(a, b, (((1,), (1,)), ((), ())), preferred_element_type=F32)


def _softplus(x):
    return jnp.maximum(x, 0.0) + jnp.log1p(jnp.exp(-jnp.abs(x)))


def _cast_slab(src_ref, dst_ref):
    n_tiles, _, tc = dst_ref.shape
    for t in range(n_tiles):
        dst_ref[t] = src_ref[:, t * tc:(t + 1) * tc].astype(BF16)


def _slab_specs(w, layer, n_slabs, tc, step_of):
    _, R, C = w.shape
    rows = R // n_slabs
    slab = lambda *ids: jnp.minimum(step_of(*ids), n_slabs - 1)
    return (pl.BlockSpec((None, rows, C), lambda *ids: (layer, slab(*ids), 0)),
            pl.BlockSpec((C // tc, rows, tc), lambda *ids: (0, slab(*ids), 0)),
            jax.ShapeDtypeStruct((C // tc, R, tc), BF16))


def _params(*sem):
    return pltpu.CompilerParams(dimension_semantics=sem,
                                vmem_limit_bytes=V7X_VMEM_LIMIT_BYTES)


def _wstage_kernel(a_ref, b_ref, w_ref, wf_ref, *, f_tile):
    t = pl.program_id(0)

    @pl.when(t < f_tile)
    def _():
        w_ref[...] = a_ref[...].astype(BF16)

    @pl.when(t >= f_tile)
    def _():
        w_ref[...] = jnp.concatenate([a_ref[FOX_HEADS:, :], b_ref[...]], axis=0).astype(BF16)

    @pl.when(t == f_tile)
    def _():
        wf_ref[:FOX_HEADS, :] = a_ref[:FOX_HEADS, :].astype(BF16)
        wf_ref[FOX_HEADS:, :] = jnp.zeros((F_ROWS - FOX_HEADS, wf_ref.shape[1]), BF16)


def _stage_in_weights(w_t, layer, *, tr):
    _, N, D = w_t.shape
    n_rows = N - FOX_HEADS
    f_tile = 3 * FOX_W // tr
    return pl.pallas_call(
        functools.partial(_wstage_kernel, f_tile=f_tile),
        grid=(n_rows // tr,),
        in_specs=[pl.BlockSpec((None, tr, D), lambda t: (layer, t, 0)),
                  pl.BlockSpec((None, FOX_HEADS, D),
                               lambda t: (layer, (jnp.maximum(t, f_tile) + 1) * (tr // FOX_HEADS), 0))],
        out_specs=[pl.BlockSpec((tr, D), lambda t: (t, 0)),
                   pl.BlockSpec((F_ROWS, D), lambda t: (0, 0))],
        out_shape=[jax.ShapeDtypeStruct((n_rows, D), BF16),
                   jax.ShapeDtypeStruct((F_ROWS, D), BF16)],
        compiler_params=_params("arbitrary"),
    )(w_t, w_t)


def _inproj_kernel(x_ref, g_ref, w_ref, gh_ref, wf_ref,
                   qk_ref, vt_ref, ug_ref, f_ref, h_scr, *, tn, n_qk, n_v):
    j = pl.program_id(1)
    h_ref = h_scr.at[pl.program_id(2)]

    chunks = [slice(c * PROJ_CHUNK, (c + 1) * PROJ_CHUNK) for c in range(tn // PROJ_CHUNK)]

    def qk_tile():
        for sl in chunks:
            acc = _dot_nt(h_ref[...], w_ref[sl, :])
            for hh in range(PROJ_CHUNK // HEAD_DIM):
                lo = sl.start + hh * HEAD_DIM
                qk_ref[:, lo:lo + HEAD_DIM] = _rms(
                    acc[:, hh * HEAD_DIM:(hh + 1) * HEAD_DIM], gh_ref[...]).astype(BF16)

    @pl.when(j == 0)
    def _():
        h_ref[...] = _rms(x_ref[...], g_ref[...]).astype(BF16)
        qk_tile()

    pl.when((j > 0) & (j < n_qk))(qk_tile)

    @pl.when((j >= n_qk) & (j < n_qk + n_v))
    def _():
        for sl in chunks[:-1]:
            vt_ref[sl, :] = _dot_nt(w_ref[sl, :], h_ref[...]).astype(BF16)
        sl = chunks[-1]
        acc = _dot_nt(jnp.concatenate([w_ref[sl, :], wf_ref[...]], axis=0), h_ref[...])
        vt_ref[sl, :] = acc[:PROJ_CHUNK].astype(BF16)
        f_ref[...] = acc[PROJ_CHUNK:]

    @pl.when(j >= n_qk + n_v)
    def _():
        for sl in chunks:
            ug_ref[:, sl] = _dot_nt(h_ref[...], w_ref[sl, :])


def _in_proj(x2d, g, w_t, g_qk, wf, *, tm, tn):
    T, D = x2d.shape
    n_qk = 2 * FOX_W // tn
    n_v = FOX_W // tn
    assert n_v == 1, "the forget logits are produced by the single v step"
    n_ug = 2 * LRU_W // tn
    tiles_per_gain = FOX_W // tn
    clip = lambda v, n: jnp.clip(v, 0, n - 1)
    n_steps = n_qk + n_v + n_ug
    per = ROW_TILES_PER_WEIGHT
    n_rows = T // tm
    first = lambda p: per * p
    last = lambda p: per * p + per - 1

    def row_of(p, j, r, j_lo, j_hi):
        return jnp.where(j < j_lo, first(p), jnp.where(j < j_hi, first(p) + r, last(p)))

    x_row = lambda p, j, r: jnp.where(j == 0, first(p) + r, jnp.minimum(first(p + 1), n_rows - 1))
    return pl.pallas_call(
        functools.partial(_inproj_kernel, tn=tn, n_qk=n_qk, n_v=n_v),
        grid=(n_rows // per, n_steps, per),
        in_specs=[
            pl.BlockSpec((tm, D), lambda p, j, r: (x_row(p, j, r), 0)),
            pl.BlockSpec((1, D), lambda p, j, r: (0, 0)),
            pl.BlockSpec((tn, D), lambda p, j, r: (j, 0)),
            pl.BlockSpec((None, 1, HEAD_DIM),
                         lambda p, j, r: (clip(j // tiles_per_gain, 2), 0, 0)),
            pl.BlockSpec((F_ROWS, D), lambda p, j, r: (0, 0)),
        ],
        out_specs=[
            pl.BlockSpec((tm, tn), lambda p, j, r: (row_of(p, j, r, 0, n_qk), clip(j, n_qk))),
            pl.BlockSpec((tn, tm), lambda p, j, r: (0, row_of(p, j, r, n_qk, n_qk + n_v))),
            pl.BlockSpec((tm, tn), lambda p, j, r: (row_of(p, j, r, n_qk + n_v, n_steps),
                                                    clip(j - n_qk - n_v, n_ug))),
            pl.BlockSpec((F_ROWS, tm), lambda p, j, r: (0, row_of(p, j, r, n_qk, n_qk + n_v))),
        ],
        out_shape=[
            jax.ShapeDtypeStruct((T, 2 * FOX_W), BF16),
            jax.ShapeDtypeStruct((FOX_W, T), BF16),
            jax.ShapeDtypeStruct((T, 2 * LRU_W), F32),
            jax.ShapeDtypeStruct((F_ROWS, T), F32),
        ],
        scratch_shapes=[pltpu.VMEM((per, tm, D), BF16)],
        compiler_params=_params("arbitrary", "arbitrary", "arbitrary"),
    )(x2d, g, w_t, g_qk, wf)


def _fox_kernel(q_ref, k_ref, vt_ref, f_ref, bf_ref, side_ref, o_ref, side_out_ref,
                bias_scr, vaug_scr, *, seq):
    hp = pl.program_id(1)

    @pl.when(hp == 0)
    def _():
        z = f_ref[...] + bf_ref[...]
        c = jnp.minimum(z, 0.0) - jnp.log1p(jnp.exp(-jnp.abs(z)))
        lane = lax.broadcasted_iota(jnp.int32, c.shape, 1)
        d = 1
        while d < seq:
            c = c + jnp.where(lane >= d, pltpu.roll(c, d, axis=1), 0.0)
            d *= 2
        bias = c[:FOX_HEADS] * (-math.log2(math.e))
        pieces = []
        for _ in range(BIAS_PIECES):
            p = bias.astype(BF16).astype(F32)
            pieces.append(p)
            bias = bias - p
        pad = jnp.zeros((HEAD_DIM - BIAS_PIECES * FOX_HEADS, seq), F32)
        bias_scr[...] = jnp.concatenate(pieces + [pad], axis=0).T.astype(BF16)

    _cast_slab(side_ref, side_out_ref)
    ones_row = lax.broadcasted_iota(jnp.int32, (V_PAD_ROWS, seq), 0) == 0
    for e in range(HEADS_PER_STEP):
        lanes = slice(e * HEAD_DIM, (e + 1) * HEAD_DIM)
        vaug_scr[e, :HEAD_DIM, :] = vt_ref[lanes, :]
        vaug_scr[e, HEAD_DIM:, :] = jnp.where(ones_row, 1.0, 0.0).astype(BF16)

    tq, tk = Q_BLOCK, K_CHUNK
    dc = DIAG_CHUNK

    def selector(e, n):
        lane = lax.broadcasted_iota(jnp.int32, (n, HEAD_DIM), 1)
        h = hp * HEADS_PER_STEP + e
        mine = lane == h
        for p in range(1, BIAS_PIECES):
            mine = mine | (lane == h + p * FOX_HEADS)
        return jnp.where(mine, 1.0, 0.0).astype(BF16)

    selectors = {(e, n): selector(e, n) for e in range(HEADS_PER_STEP)
                 for n in range(dc, tq + 1, dc)}
    below_diag = (lax.broadcasted_iota(jnp.int32, (dc, dc), 0)
                  <= lax.broadcasted_iota(jnp.int32, (dc, dc), 1))

    def scores(e, lo, q0, k0, nk):
        q = q_ref[lo + q0:lo + tq, e * HEAD_DIM:(e + 1) * HEAD_DIM]
        q_aug = jnp.concatenate([q, selectors[e, tq - q0]], axis=1)
        k_aug = jnp.concatenate([k_ref[k0:k0 + nk, e * HEAD_DIM:(e + 1) * HEAD_DIM],
                                 bias_scr[k0:k0 + nk, :]], axis=1)
        return _dot_nt(k_aug, q_aug)

    steps = []
    for qi in range(seq // tq):
        lo = qi * tq
        pieces = [(0, kc * tk, tk) for kc in range(lo // tk)]
        pieces += [(j * dc, lo + j * dc, dc) for j in range(tq // dc)]
        steps += [(e, lo) + piece for piece in pieces for e in range(HEADS_PER_STEP)]
    pending = [scores(*st) for st in steps[:SCORE_LOOKAHEAD]]
    m = [None] * HEADS_PER_STEP
    acc = [None] * HEADS_PER_STEP
    for idx, (e, lo, q0, k0, nk) in enumerate(steps):
        t = pending.pop(0)
        if idx + SCORE_LOOKAHEAD < len(steps):
            pending.append(scores(*steps[idx + SCORE_LOOKAHEAD]))
        if k0 >= lo:
            masked = jnp.where(below_diag, t[:, :dc], -jnp.inf)
            t = jnp.concatenate([masked, t[:, dc:]], axis=1) if q0 + dc < tq else masked
        cm = jnp.max(t, axis=0, keepdims=True)
        if k0 == 0:
            m_new = cm
        else:
            m_old = m[e][:, q0:]
            m_new = jnp.maximum(m_old, cm)
        p = jnp.exp2(t - m_new).astype(BF16)
        pv = _dot(vaug_scr[e, :, k0:k0 + nk], p)
        if k0 == 0:
            acc[e], m[e] = pv, m_new
        else:
            upd = acc[e][:, q0:] * jnp.exp2(m_old - m_new) + pv
            acc[e] = jnp.concatenate([acc[e][:, :q0], upd], axis=1) if q0 else upd
            m[e] = jnp.concatenate([m[e][:, :q0], m_new], axis=1) if q0 else m_new
        if k0 + nk == lo + tq:
            inv_l = 1.0 / acc[e][HEAD_DIM:HEAD_DIM + 1, :]
            o_ref[lo:lo + tq, e * HEAD_DIM:(e + 1) * HEAD_DIM] = (acc[e][:HEAD_DIM, :] * inv_l).T


def _fox_attention(qk, v_t, f_t, b_f, w_side, layer, *, batch, seq, n_slabs, side_tc):
    n_pairs = FOX_HEADS // HEADS_PER_STEP
    wide = HEADS_PER_STEP * HEAD_DIM
    side_in, side_out, side_shape = _slab_specs(w_side, layer, n_slabs, side_tc,
                                                lambda b, hp: b * n_pairs + hp)
    return pl.pallas_call(
        functools.partial(_fox_kernel, seq=seq),
        grid=(batch, n_pairs),
        in_specs=[
            pl.BlockSpec((None, seq, wide), lambda b, hp: (b, 0, hp)),
            pl.BlockSpec((None, seq, wide), lambda b, hp: (b, 0, n_pairs + hp)),
            pl.BlockSpec((wide, seq), lambda b, hp: (hp, b)),
            pl.BlockSpec((F_ROWS, seq), lambda b, hp: (0, b)),
            pl.BlockSpec((F_ROWS, 1), lambda b, hp: (0, 0)),
            side_in,
        ],
        out_specs=[pl.BlockSpec((None, seq, wide), lambda b, hp: (b, 0, hp)), side_out],
        out_shape=[jax.ShapeDtypeStruct((batch, seq, FOX_W), F32), side_shape],
        scratch_shapes=[pltpu.VMEM((seq, HEAD_DIM), BF16),
                        pltpu.VMEM((HEADS_PER_STEP, HEAD_DIM + V_PAD_ROWS, seq), BF16)],
        compiler_params=_params("arbitrary", "arbitrary"),
    )(qk, qk, v_t, f_t, b_f, w_side)


def _lru_kernel(*refs, seq, n_side):
    u_refs = refs[:LRU_BLOCKS_PER_STEP]
    refs = refs[LRU_BLOCKS_PER_STEP:]
    gate_ref, cw_ref, cb_ref, wg_ref, bra_ref, bri_ref, lam_ref = refs[:7]
    side_refs = refs[7:7 + n_side]
    o_ref = refs[7 + n_side]
    side_out_refs = refs[8 + n_side:8 + 2 * n_side]
    at_scr, bt_scr = refs[8 + 2 * n_side:]
    for src_ref, dst_ref in zip(side_refs, side_out_refs):
        _cast_slab(src_ref, dst_ref)
    width = o_ref.shape[1]
    row = lax.broadcasted_iota(jnp.int32, (SUBLANES, LRU_BLOCK), 0)

    def shifted(d):
        cols = []
        for u_ref in u_refs:
            head = jnp.where(row >= d, pltpu.roll(u_ref[0:SUBLANES, :], d, axis=0), 0.0)
            cols.append(jnp.concatenate([head, u_ref[SUBLANES - d:seq - d, :]], axis=0))
        return jnp.concatenate(cols, axis=1)

    cw = cw_ref[...]
    uc = cw[0:1, :] * shifted(3)
    uc = uc + cw[1:2, :] * shifted(2)
    uc = uc + cw[2:3, :] * shifted(1)
    uc = uc + cw[3:4, :] * jnp.concatenate([u_ref[...] for u_ref in u_refs], axis=1)
    uc = cb_ref[...] + uc

    ucb = uc.astype(BF16)
    gates = [_dot(ucb[:, n * LRU_BLOCK:(n + 1) * LRU_BLOCK], wg_ref[n] * 0.5)
             for n in range(width // LRU_BLOCK)]
    pre_r = jnp.concatenate([g[:, :LRU_BLOCK] for g in gates], axis=1)
    pre_i = jnp.concatenate([g[:, LRU_BLOCK:] for g in gates], axis=1)
    half_c = (-0.5 * LRU_C) * _softplus(-lam_ref[...])
    log_a = half_c * jnp.tanh(pre_r + 0.5 * bra_ref[...]) + half_c
    neg_2i = -1.0 - jnp.tanh(pre_i + 0.5 * bri_ref[...])
    a = jnp.exp(log_a)
    t = jnp.tanh(log_a)
    neg_quarter_root = jnp.where(t < 0.0, t * lax.rsqrt((-8.0 * t) * (1.0 - t)), 0.0)
    b = neg_quarter_root * (neg_2i * uc)

    n_blocks = width // LRU_BLOCK
    seg = seq // SUBLANES
    pad = SUBLANES
    blank = jnp.zeros((pad, LRU_BLOCK), F32)
    for n in range(n_blocks):
        for scr, val in ((at_scr, a), (bt_scr, b)):
            scr[n, :pad, :] = blank
            scr[n, pad:pad + seq, :] = val[:, n * LRU_BLOCK:(n + 1) * LRU_BLOCK]
            scr[n, pad + seq:, :] = blank

    def local(g, state, edge):
        steps = pl.ds(pad + g, SUBLANES, stride=seg + 1)
        inside = (row + g >= 0) & (row + g < seg)
        out = []
        for n, (h, p) in enumerate(state):
            a_g, b_g = at_scr[n, steps, :], bt_scr[n, steps, :]
            if edge:
                h_new = jnp.where(inside, a_g * h + b_g, h)
                p_new = jnp.where(inside, a_g * p, p)
                bt_scr[n, steps, :] = jnp.where(inside, h_new, b_g)
                at_scr[n, steps, :] = jnp.where(inside, p_new, a_g)
            else:
                h_new, p_new = a_g * h + b_g, a_g * p
                bt_scr[n, steps, :] = h_new
                at_scr[n, steps, :] = p_new
            out.append((h_new, p_new))
        return tuple(out)

    zero = jnp.zeros((SUBLANES, LRU_BLOCK), F32)
    state = ((zero, zero + 1.0),) * n_blocks
    full = seg - SUBLANES
    for g in range(1 - SUBLANES, 0):
        state = local(g, state, True)
    state = lax.fori_loop(0, full, functools.partial(local, edge=False), state, unroll=8)
    for g in range(full, seg):
        state = local(g, state, True)
    cols = []
    for n, (h, p) in enumerate(state):
        d = 1
        while d < SUBLANES:
            keep = row >= d
            h = p * jnp.where(keep, pltpu.roll(h, d, axis=0), 0.0) + h
            p = p * jnp.where(keep, pltpu.roll(p, d, axis=0), 1.0)
            d *= 2
        h_in = jnp.where(row >= 1, pltpu.roll(h, 1, axis=0), 0.0)
        cols.append(jnp.concatenate(
            [bt_scr[n, pad + j * seg:pad + (j + 1) * seg, :]
             + at_scr[n, pad + j * seg:pad + (j + 1) * seg, :] * h_in[j:j + 1, :]
             for j in range(SUBLANES)], axis=0))
    gate = gate_ref[...]
    inner = gate * (GELU_C1 + (GELU_C1 * GELU_C3) * (gate * gate))
    o_ref[...] = jnp.concatenate(cols, axis=1) * (gate * (1.0 + jnp.tanh(inner)))


def _rg_lru(ug, cw, cb, wg, bra, bri, lam, side_weights, layer, *, batch, seq):
    per = LRU_BLOCKS_PER_STEP
    nb = LRU_BLOCKS // per
    width = per * LRU_BLOCK
    vec = lambda: pl.BlockSpec((1, width), lambda b, n: (0, n))
    n_slabs = batch * nb
    sides = [_slab_specs(w, layer, n_slabs, w.shape[2], lambda b, n: b * nb + n)
             for w in side_weights]
    res = pl.pallas_call(
        functools.partial(_lru_kernel, seq=seq, n_side=len(sides)),
        grid=(batch, nb),
        in_specs=[
            *[pl.BlockSpec((None, seq, LRU_BLOCK), lambda b, n, e=e: (b, 0, per * n + e))
              for e in range(per)],
            pl.BlockSpec((None, seq, width), lambda b, n: (b, 0, nb + n)),
            pl.BlockSpec((CONV_W, width), lambda b, n: (0, n)),
            vec(),
            pl.BlockSpec((per, LRU_BLOCK, 2 * LRU_BLOCK), lambda b, n: (n, 0, 0)),
            vec(), vec(), vec(),
            *[sd[0] for sd in sides],
        ],
        out_specs=[pl.BlockSpec((None, seq, width), lambda b, n: (b, 0, n)),
                   *[sd[1] for sd in sides]],
        out_shape=[jax.ShapeDtypeStruct((batch, seq, LRU_W), F32), *[sd[2] for sd in sides]],
        scratch_shapes=[pltpu.VMEM((per, seq + 2 * SUBLANES, LRU_BLOCK), F32)] * 2,
        compiler_params=_params("arbitrary", "arbitrary"),
    )(*[ug] * per, ug, cw, cb, wg, bra, bri, lam, *side_weights)
    return res[0], [r[0] for r in res[1:]]


def _mix_xattn_kernel(of_ref, yl_ref, gf_ref, gl_ref, wout_ref, x_ref, gx_ref, wq_ref, gq_ref,
                      mem_ref, gm_ref, wkv_ref, gk_ref, wo_ref, side_ref, o_ref, side_out_ref,
                      ckv_ref, *, chunk):
    @pl.when(pl.program_id(1) == 0)
    def _():
        kv = _dot(_rms(mem_ref[...], gm_ref[...]).astype(BF16), wkv_ref[...])
        for hh in range(XATT_HEADS):
            sl = slice(hh * HEAD_DIM, (hh + 1) * HEAD_DIM)
            ckv_ref[:, sl] = _rms(kv[:, sl], gk_ref[...]).astype(BF16)
        ckv_ref[:, XATT_W:] = kv[:, XATT_W:].astype(BF16)

    _cast_slab(side_ref, side_out_ref)
    mf = _rms(of_ref[...], gf_ref[...]).astype(BF16)
    ml = _rms(yl_ref[...], gl_ref[...]).astype(BF16)
    for c in range(o_ref.shape[1] // chunk):
        sl = slice(c * chunk, (c + 1) * chunk)
        acc = _dot(mf, wout_ref[:FOX_W, sl]) + _dot(ml, wout_ref[FOX_W:, sl])
        o_ref[:, sl] = x_ref[:, sl] + acc

    x1 = o_ref[...]
    cq = _dot(_rms(x1, gx_ref[...]).astype(BF16), wq_ref[...])
    scale = 1.0 / math.sqrt(HEAD_DIM)
    scores = []
    for hh in range(XATT_HEADS):
        sl = slice(hh * HEAD_DIM, (hh + 1) * HEAD_DIM)
        qh = _rms(cq[:, sl], gq_ref[...]).astype(BF16)
        scores.append(_dot_nt(qh, ckv_ref[:, sl]))
    heads = []
    for hh in range(XATT_HEADS):
        vh = ckv_ref[:, XATT_W + hh * HEAD_DIM:XATT_W + (hh + 1) * HEAD_DIM]
        s = scores[hh] * scale
        e = jnp.exp(s - jnp.max(s, axis=-1, keepdims=True))
        l = jnp.sum(e, axis=-1, keepdims=True)
        heads.append((_dot(e.astype(BF16), vh) / l).astype(BF16))
    ox = jnp.concatenate(heads, axis=-1)
    o_ref[...] = x1 + _dot(ox, wo_ref[...])


def _mix_xattn(o_fox, y_lru, gf, gl, w_out, x3d, gx, w_cq, gq, mem, gm, w_ckv, gk, w_co,
               w_side, layer, *, tm):
    B, S, D = x3d.shape
    M = mem.shape[1]
    n_tiles = S // tm
    side_in, side_out, side_shape = _slab_specs(w_side, layer, B * n_tiles, w_side.shape[2],
                                                lambda b, i: b * n_tiles + i)
    const = lambda shape: pl.BlockSpec(shape, lambda b, i: (0,) * len(shape),
                                       pipeline_mode=pl.Buffered(1))
    tile = lambda w: pl.BlockSpec((None, tm, w), lambda b, i: (b, i, 0))
    return pl.pallas_call(
        functools.partial(_mix_xattn_kernel, chunk=2 * PROJ_CHUNK),
        grid=(B, S // tm),
        in_specs=[
            tile(FOX_W), tile(LRU_W), const((1, FOX_W)), const((1, LRU_W)),
            const((FOX_W + LRU_W, D)), tile(D), const((1, D)), const((D, XATT_W)),
            const((1, HEAD_DIM)),
            pl.BlockSpec((None, M, D), lambda b, i: (b, 0, 0)),
            const((1, D)), const((D, 2 * XATT_W)), const((1, HEAD_DIM)),
            const((XATT_W, D)),
            side_in,
        ],
        out_specs=[tile(D), side_out],
        out_shape=[jax.ShapeDtypeStruct((B, S, D), F32), side_shape],
        scratch_shapes=[pltpu.VMEM((M, 2 * XATT_W), BF16)],
        compiler_params=_params("arbitrary", "arbitrary"),
    )(o_fox, y_lru, gf, gl, w_out, x3d, gx, w_cq, gq, mem, gm, w_ckv, gk, w_co, w_side)


def _ffn_kernel(x_ref, g_ref, wg_ref, wu_ref, wd_ref, o_hbm, acc_scr, h_scr, sems,
                *, tm, nk, n_groups):
    p, k, r = pl.program_id(0), pl.program_id(1), pl.program_id(2)
    acc_ref = acc_scr.at[r]
    h_ref = h_scr.at[r]
    row_tile = p * FFN_ROW_TILES + r

    def writeback(tile, slot):
        rows = pl.ds(pl.multiple_of(tile * tm, tm), tm)
        return pltpu.make_async_copy(acc_scr.at[slot], o_hbm.at[rows, :], sems.at[slot])

    def hidden_tile(base_ref):
        h = h_ref[...]
        gate = _dot(h, wg_ref[...])
        up = _dot(h, wu_ref[...])
        act = (jax.nn.silu(gate) * up).astype(BF16)
        acc_ref[...] = base_ref[...] + _dot(act, wd_ref[...])

    @pl.when((k == 0) & (p > 0))
    def _():
        writeback(row_tile - FFN_ROW_TILES, r).wait()

    @pl.when(k == 0)
    def _():
        h_ref[...] = _rms(x_ref[...], g_ref[...]).astype(BF16)
        hidden_tile(x_ref)

    pl.when(k > 0)(lambda: hidden_tile(acc_ref))

    @pl.when(k == nk - 1)
    def _():
        writeback(row_tile, r).start()

    @pl.when((k == nk - 1) & (p == n_groups - 1) & (r == FFN_ROW_TILES - 1))
    def _():
        for slot in range(FFN_ROW_TILES):
            writeback((n_groups - 1) * FFN_ROW_TILES + slot, slot).wait()


def _ffn(x2d, g, w_gu, w_d, *, tm):
    T, D = x2d.shape
    H = w_d.shape[0]
    th = w_gu.shape[2]
    nk = H // th
    per = FFN_ROW_TILES
    n_rows = T // tm
    n_groups = n_rows // per
    x_row = lambda p, k, r: jnp.where(k == 0, per * p + r,
                                      jnp.minimum(per * (p + 1), n_rows - 1))
    return pl.pallas_call(
        functools.partial(_ffn_kernel, tm=tm, nk=nk, n_groups=n_groups),
        grid=(n_groups, nk, per),
        in_specs=[
            pl.BlockSpec((tm, D), lambda p, k, r: (x_row(p, k, r), 0)),
            pl.BlockSpec((1, D), lambda p, k, r: (0, 0)),
            pl.BlockSpec((None, D, th), lambda p, k, r: (k, 0, 0)),
            pl.BlockSpec((None, D, th), lambda p, k, r: (nk + k, 0, 0)),
            pl.BlockSpec((th, D), lambda p, k, r: (k, 0)),
        ],
        out_specs=pl.BlockSpec(memory_space=pl.ANY),
        out_shape=jax.ShapeDtypeStruct((T, D), F32),
        scratch_shapes=[pltpu.VMEM((per, tm, D), F32), pltpu.VMEM((per, tm, D), BF16),
                        pltpu.SemaphoreType.DMA((per,))],
        compiler_params=_params("arbitrary", "arbitrary", "arbitrary"),
    )(x2d, g, w_gu, w_gu, w_d)


def kernel(x, mem, g_mix, w_in, b_f, g_q, g_k, conv_w, conv_b, w_ra, b_ra, w_ri, b_ri, lam,
           g_fox_out, g_lru_out, w_out, g_xattn, g_mem, w_cq, w_ckv, g_cq, g_ck, w_co, g_ffn,
           w_gate_up, w_down):
    B, S, D = x.shape
    T = B * S
    depth = g_mix.shape[0]
    row = lambda v: v.reshape(1, -1).astype(F32)

    for l in range(depth):
        w_t, w_f = _stage_in_weights(jnp.swapaxes(w_in, 1, 2), l, tr=512)
        g_qk = jnp.stack([g_q[l] * FOX_Q_SCALE, g_k[l]]).reshape(2, 1, HEAD_DIM).astype(F32)
        bf_col = jnp.zeros((F_ROWS, 1), F32).at[:FOX_HEADS, 0].set(b_f[l])
        w_gates = jnp.concatenate([w_ra[l], w_ri[l]], axis=-1).astype(BF16)

        x2d = x.reshape(T, D)

        qk, v_t, ug, f_t = _in_proj(x2d, row(g_mix[l]), w_t, g_qk, w_f, tm=1024, tn=1024)
        o_fox, w_gu_bf = _fox_attention(qk.reshape(B, S, 2 * FOX_W), v_t, f_t, bf_col,
                                        w_gate_up, l, batch=B, seq=S,
                                        n_slabs=B * FOX_HEADS // HEADS_PER_STEP,
                                        side_tc=FFN_TILE)
        y_lru, (w_out_bf, w_cq_bf, w_ckv_bf, w_co_bf) = _rg_lru(
            ug.reshape(B, S, 2 * LRU_W), conv_w[l], row(conv_b[l]), w_gates, row(b_ra[l]),
            row(b_ri[l]), row(lam[l]), [w_out, w_cq, w_ckv, w_co], l, batch=B, seq=S)

        x3d, w_down_bf = _mix_xattn(
            o_fox, y_lru, row(g_fox_out[l]), row(g_lru_out[l]), w_out_bf, x2d.reshape(B, S, D),
            row(g_xattn[l]), w_cq_bf, row(g_cq[l]), mem, row(g_mem[l]), w_ckv_bf, row(g_ck[l]),
            w_co_bf, w_down, l, tm=512)

        x2d = _ffn(x3d.reshape(T, D), row(g_ffn[l]), w_gu_bf, w_down_bf[0], tm=1024)
        x = x2d.reshape(B, S, D)
    return x
```

```python
import functools
import math

import jax
import jax.numpy as jnp
from jax import lax
from jax.experimental import pallas as pl
from jax.experimental.pallas import tpu as pltpu

F32 = jnp.float32
BF16 = jnp.bfloat16

HEAD_DIM = 128
FOX_HEADS = 8
FOX_W = FOX_HEADS * HEAD_DIM
LRU_BLOCKS = 8
LRU_BLOCK = 128
LRU_W = LRU_BLOCKS * LRU_BLOCK
LRU_C = 8.0
CONV_W = 4
XATT_HEADS = 4
XATT_W = XATT_HEADS * HEAD_DIM
RMS_EPS = 1e-6
F_ROWS = 16
Q_BLOCK = 512
K_CHUNK = 256
GELU_C1 = math.sqrt(2.0 / math.pi)
GELU_C3 = 0.044715
DIAG_CHUNK = 256
HEADS_PER_STEP = 2
SCORE_LOOKAHEAD = 10
V_PAD_ROWS = 16
PROJ_CHUNK = 256
SUBLANES = 8
ROW_TILES_PER_WEIGHT = 4
FFN_ROW_TILES = 2
LRU_BLOCKS_PER_STEP = 4
FFN_TILE = 512
FOX_Q_SCALE = math.log2(math.e) / math.sqrt(HEAD_DIM)
BIAS_PIECES = 3

V7X_VMEM_LIMIT_BYTES = 60 * 1024 * 1024


def _rms(x, g):
    ms = jnp.mean(x * x, axis=-1, keepdims=True)
    return x * lax.rsqrt(ms + RMS_EPS) * g


def _dot(a, b):
    return jnp.dot(a, b, preferred_element_type=F32)


def _dot_nt(a, b):
    return lax.dot_general(a, b, (((1,), (1,)), ((), ())), preferred_element_type=F32)


def _softplus(x):
    return jnp.maximum(x, 0.0) + jnp.log1p(jnp.exp(-jnp.abs(x)))


def _cast_slab(src_ref, dst_ref):
    n_tiles, _, tc = dst_ref.shape
    for t in range(n_tiles):
        dst_ref[t] = src_ref[:, t * tc:(t + 1) * tc].astype(BF16)


def _slab_specs(w, layer, n_slabs, tc, step_of):
    _, R, C = w.shape
    rows = R // n_slabs
    slab = lambda *ids: jnp.minimum(step_of(*ids), n_slabs - 1)
    return (pl.BlockSpec((None, rows, C), lambda *ids: (layer, slab(*ids), 0)),
            pl.BlockSpec((C // tc, rows, tc), lambda *ids: (0, slab(*ids), 0)),
            jax.ShapeDtypeStruct((C // tc, R, tc), BF16))


def _params(*sem):
    return pltpu.CompilerParams(dimension_semantics=sem,
                                vmem_limit_bytes=V7X_VMEM_LIMIT_BYTES)


def _wstage_kernel(a_ref, b_ref, w_ref, wf_ref, *, f_tile):
    t = pl.program_id(0)

    @pl.when(t < f_tile)
    def _():
        w_ref[...] = a_ref[...].astype(BF16)

    @pl.when(t >= f_tile)
    def _():
        w_ref[...] = jnp.concatenate([a_ref[FOX_HEADS:, :], b_ref[...]], axis=0).astype(BF16)

    @pl.when(t == f_tile)
    def _():
        wf_ref[:FOX_HEADS, :] = a_ref[:FOX_HEADS, :].astype(BF16)
        wf_ref[FOX_HEADS:, :] = jnp.zeros((F_ROWS - FOX_HEADS, wf_ref.shape[1]), BF16)


def _stage_in_weights(w_t, layer, *, tr):
    _, N, D = w_t.shape
    n_rows = N - FOX_HEADS
    f_tile = 3 * FOX_W // tr
    return pl.pallas_call(
        functools.partial(_wstage_kernel, f_tile=f_tile),
        grid=(n_rows // tr,),
        in_specs=[pl.BlockSpec((None, tr, D), lambda t: (layer, t, 0)),
                  pl.BlockSpec((None, FOX_HEADS, D),
                               lambda t: (layer, (jnp.maximum(t, f_tile) + 1) * (tr // FOX_HEADS), 0))],
        out_specs=[pl.BlockSpec((tr, D), lambda t: (t, 0)),
                   pl.BlockSpec((F_ROWS, D), lambda t: (0, 0))],
        out_shape=[jax.ShapeDtypeStruct((n_rows, D), BF16),
                   jax.ShapeDtypeStruct((F_ROWS, D), BF16)],
        compiler_params=_params("arbitrary"),
    )(w_t, w_t)


def _inproj_kernel(x_ref, g_ref, w_ref, gh_ref, wf_ref,
                   qk_ref, vt_ref, ug_ref, f_ref, h_scr, *, tn, n_qk, n_v):
    j = pl.program_id(1)
    h_ref = h_scr.at[pl.program_id(2)]

    chunks = [slice(c * PROJ_CHUNK, (c + 1) * PROJ_CHUNK) for c in range(tn // PROJ_CHUNK)]

    def qk_tile():
        for sl in chunks:
            acc = _dot_nt(h_ref[...], w_ref[sl, :])
            for hh in range(PROJ_CHUNK // HEAD_DIM):
                lo = sl.start + hh * HEAD_DIM
                qk_ref[:, lo:lo + HEAD_DIM] = _rms(
                    acc[:, hh * HEAD_DIM:(hh + 1) * HEAD_DIM], gh_ref[...]).astype(BF16)

    @pl.when(j == 0)
    def _():
        h_ref[...] = _rms(x_ref[...], g_ref[...]).astype(BF16)
        qk_tile()

    pl.when((j > 0) & (j < n_qk))(qk_tile)

    @pl.when((j >= n_qk) & (j < n_qk + n_v))
    def _():
        for sl in chunks[:-1]:
            vt_ref[sl, :] = _dot_nt(w_ref[sl, :], h_ref[...]).astype(BF16)
        sl = chunks[-1]
        acc = _dot_nt(jnp.concatenate([w_ref[sl, :], wf_ref[...]], axis=0), h_ref[...])
        vt_ref[sl, :] = acc[:PROJ_CHUNK].astype(BF16)
        f_ref[...] = acc[PROJ_CHUNK:]

    @pl.when(j >= n_qk + n_v)
    def _():
        for sl in chunks:
            ug_ref[:, sl] = _dot_nt(h_ref[...], w_ref[sl, :])


def _in_proj(x2d, g, w_t, g_qk, wf, *, tm, tn):
    T, D = x2d.shape
    n_qk = 2 * FOX_W // tn
    n_v = FOX_W // tn
    assert n_v == 1, "the forget logits are produced by the single v step"
    n_ug = 2 * LRU_W // tn
    tiles_per_gain = FOX_W // tn
    clip = lambda v, n: jnp.clip(v, 0, n - 1)
    n_steps = n_qk + n_v + n_ug
    per = ROW_TILES_PER_WEIGHT
    n_rows = T // tm
    first = lambda p: per * p
    last = lambda p: per * p + per - 1

    def row_of(p, j, r, j_lo, j_hi):
        return jnp.where(j < j_lo, first(p), jnp.where(j < j_hi, first(p) + r, last(p)))

    x_row = lambda p, j, r: jnp.where(j == 0, first(p) + r, jnp.minimum(first(p + 1), n_rows - 1))
    return pl.pallas_call(
        functools.partial(_inproj_kernel, tn=tn, n_qk=n_qk, n_v=n_v),
        grid=(n_rows // per, n_steps, per),
        in_specs=[
            pl.BlockSpec((tm, D), lambda p, j, r: (x_row(p, j, r), 0)),
            pl.BlockSpec((1, D), lambda p, j, r: (0, 0)),
            pl.BlockSpec((tn, D), lambda p, j, r: (j, 0)),
            pl.BlockSpec((None, 1, HEAD_DIM),
                         lambda p, j, r: (clip(j // tiles_per_gain, 2), 0, 0)),
            pl.BlockSpec((F_ROWS, D), lambda p, j, r: (0, 0)),
        ],
        out_specs=[
            pl.BlockSpec((tm, tn), lambda p, j, r: (row_of(p, j, r, 0, n_qk), clip(j, n_qk))),
            pl.BlockSpec((tn, tm), lambda p, j, r: (0, row_of(p, j, r, n_qk, n_qk + n_v))),
            pl.BlockSpec((tm, tn), lambda p, j, r: (row_of(p, j, r, n_qk + n_v, n_steps),
                                                    clip(j - n_qk - n_v, n_ug))),
            pl.BlockSpec((F_ROWS, tm), lambda p, j, r: (0, row_of(p, j, r, n_qk, n_qk + n_v))),
        ],
        out_shape=[
            jax.ShapeDtypeStruct((T, 2 * FOX_W), BF16),
            jax.ShapeDtypeStruct((FOX_W, T), BF16),
            jax.ShapeDtypeStruct((T, 2 * LRU_W), F32),
            jax.ShapeDtypeStruct((F_ROWS, T), F32),
        ],
        scratch_shapes=[pltpu.VMEM((per, tm, D), BF16)],
        compiler_params=_params("arbitrary", "arbitrary", "arbitrary"),
    )(x2d, g, w_t, g_qk, wf)


def _fox_kernel(q_ref, k_ref, vt_ref, f_ref, bf_ref, side_ref, o_ref, side_out_ref,
                bias_scr, vaug_scr, *, seq):
    hp = pl.program_id(1)

    @pl.when(hp == 0)
    def _():
        z = f_ref[...] + bf_ref[...]
        c = jnp.minimum(z, 0.0) - jnp.log1p(jnp.exp(-jnp.abs(z)))
        lane = lax.broadcasted_iota(jnp.int32, c.shape, 1)
        d = 1
        while d < seq:
            c = c + jnp.where(lane >= d, pltpu.roll(c, d, axis=1), 0.0)
            d *= 2
        bias = c[:FOX_HEADS] * (-math.log2(math.e))
        pieces = []
        for _ in range(BIAS_PIECES):
            p = bias.astype(BF16).astype(F32)
            pieces.append(p)
            bias = bias - p
        pad = jnp.zeros((HEAD_DIM - BIAS_PIECES * FOX_HEADS, seq), F32)
        bias_scr[...] = jnp.concatenate(pieces + [pad], axis=0).T.astype(BF16)

    _cast_slab(side_ref, side_out_ref)
    ones_row = lax.broadcasted_iota(jnp.int32, (V_PAD_ROWS, seq), 0) == 0
    for e in range(HEADS_PER_STEP):
        lanes = slice(e * HEAD_DIM, (e + 1) * HEAD_DIM)
        vaug_scr[e, :HEAD_DIM, :] = vt_ref[lanes, :]
        vaug_scr[e, HEAD_DIM:, :] = jnp.where(ones_row, 1.0, 0.0).astype(BF16)

    tq, tk = Q_BLOCK, K_CHUNK
    dc = DIAG_CHUNK

    def selector(e, n):
        lane = lax.broadcasted_iota(jnp.int32, (n, HEAD_DIM), 1)
        h = hp * HEADS_PER_STEP + e
        mine = lane == h
        for p in range(1, BIAS_PIECES):
            mine = mine | (lane == h + p * FOX_HEADS)
        return jnp.where(mine, 1.0, 0.0).astype(BF16)

    selectors = {(e, n): selector(e, n) for e in range(HEADS_PER_STEP)
                 for n in range(dc, tq + 1, dc)}
    below_diag = (lax.broadcasted_iota(jnp.int32, (dc, dc), 0)
                  <= lax.broadcasted_iota(jnp.int32, (dc, dc), 1))

    def scores(e, lo, q0, k0, nk):
        q = q_ref[lo + q0:lo + tq, e * HEAD_DIM:(e + 1) * HEAD_DIM]
        q_aug = jnp.concatenate([q, selectors[e, tq - q0]], axis=1)
        k_aug = jnp.concatenate([k_ref[k0:k0 + nk, e * HEAD_DIM:(e + 1) * HEAD_DIM],
                                 bias_scr[k0:k0 + nk, :]], axis=1)
        return _dot_nt(k_aug, q_aug)

    steps = []
    for qi in range(seq // tq):
        lo = qi * tq
        pieces = [(0, kc * tk, tk) for kc in range(lo // tk)]
        pieces += [(j * dc, lo + j * dc, dc) for j in range(tq // dc)]
        steps += [(e, lo) + piece for piece in pieces for e in range(HEADS_PER_STEP)]
    pending = [scores(*st) for st in steps[:SCORE_LOOKAHEAD]]
    m = [None] * HEADS_PER_STEP
    acc = [None] * HEADS_PER_STEP
    for idx, (e, lo, q0, k0, nk) in enumerate(steps):
        t = pending.pop(0)
        if idx + SCORE_LOOKAHEAD < len(steps):
            pending.append(scores(*steps[idx + SCORE_LOOKAHEAD]))
        if k0 >= lo:
            masked = jnp.where(below_diag, t[:, :dc], -jnp.inf)
            t = jnp.concatenate([masked, t[:, dc:]], axis=1) if q0 + dc < tq else masked
        cm = jnp.max(t, axis=0, keepdims=True)
        if k0 == 0:
            m_new = cm
        else:
            m_old = m[e][:, q0:]
            m_new = jnp.maximum(m_old, cm)
        p = jnp.exp2(t - m_new).astype(BF16)
        pv = _dot(vaug_scr[e, :, k0:k0 + nk], p)
        if k0 == 0:
            acc[e], m[e] = pv, m_new
        else:
            upd = acc[e][:, q0:] * jnp.exp2(m_old - m_new) + pv
            acc[e] = jnp.concatenate([acc[e][:, :q0], upd], axis=1) if q0 else upd
            m[e] = jnp.concatenate([m[e][:, :q0], m_new], axis=1) if q0 else m_new
        if k0 + nk == lo + tq:
            inv_l = 1.0 / acc[e][HEAD_DIM:HEAD_DIM + 1, :]
            o_ref[lo:lo + tq, e * HEAD_DIM:(e + 1) * HEAD_DIM] = (acc[e][:HEAD_DIM, :] * inv_l).T


def _fox_attention(qk, v_t, f_t, b_f, w_side, layer, *, batch, seq, n_slabs, side_tc):
    n_pairs = FOX_HEADS // HEADS_PER_STEP
    wide = HEADS_PER_STEP * HEAD_DIM
    side_in, side_out, side_shape = _slab_specs(w_side, layer, n_slabs, side_tc,
                                                lambda b, hp: b * n_pairs + hp)
    return pl.pallas_call(
        functools.partial(_fox_kernel, seq=seq),
        grid=(batch, n_pairs),
        in_specs=[
            pl.BlockSpec((None, seq, wide), lambda b, hp: (b, 0, hp)),
            pl.BlockSpec((None, seq, wide), lambda b, hp: (b, 0, n_pairs + hp)),
            pl.BlockSpec((wide, seq), lambda b, hp: (hp, b)),
            pl.BlockSpec((F_ROWS, seq), lambda b, hp: (0, b)),
            pl.BlockSpec((F_ROWS, 1), lambda b, hp: (0, 0)),
            side_in,
        ],
        out_specs=[pl.BlockSpec((None, seq, wide), lambda b, hp: (b, 0, hp)), side_out],
        out_shape=[jax.ShapeDtypeStruct((batch, seq, FOX_W), F32), side_shape],
        scratch_shapes=[pltpu.VMEM((seq, HEAD_DIM), BF16),
                        pltpu.VMEM((HEADS_PER_STEP, HEAD_DIM + V_PAD_ROWS, seq), BF16)],
        compiler_params=_params("arbitrary", "arbitrary"),
    )(qk, qk, v_t, f_t, b_f, w_side)


def _lru_kernel(*refs, seq, n_side):
    u_refs = refs[:LRU_BLOCKS_PER_STEP]
    refs = refs[LRU_BLOCKS_PER_STEP:]
    gate_ref, cw_ref, cb_ref, wg_ref, bra_ref, bri_ref, lam_ref = refs[:7]
    side_refs = refs[7:7 + n_side]
    o_ref = refs[7 + n_side]
    side_out_refs = refs[8 + n_side:8 + 2 * n_side]
    at_scr, bt_scr = refs[8 + 2 * n_side:]
    for src_ref, dst_ref in zip(side_refs, side_out_refs):
        _cast_slab(src_ref, dst_ref)
    width = o_ref.shape[1]
    row = lax.broadcasted_iota(jnp.int32, (SUBLANES, LRU_BLOCK), 0)

    def shifted(d):
        cols = []
        for u_ref in u_refs:
            head = jnp.where(row >= d, pltpu.roll(u_ref[0:SUBLANES, :], d, axis=0), 0.0)
            cols.append(jnp.concatenate([head, u_ref[SUBLANES - d:seq - d, :]], axis=0))
        return jnp.concatenate(cols, axis=1)

    cw = cw_ref[...]
    uc = cw[0:1, :] * shifted(3)
    uc = uc + cw[1:2, :] * shifted(2)
    uc = uc + cw[2:3, :] * shifted(1)
    uc = uc + cw[3:4, :] * jnp.concatenate([u_ref[...] for u_ref in u_refs], axis=1)
    uc = cb_ref[...] + uc

    ucb = uc.astype(BF16)
    gates = [_dot(ucb[:, n * LRU_BLOCK:(n + 1) * LRU_BLOCK], wg_ref[n] * 0.5)
             for n in range(width // LRU_BLOCK)]
    pre_r = jnp.concatenate([g[:, :LRU_BLOCK] for g in gates], axis=1)
    pre_i = jnp.concatenate([g[:, LRU_BLOCK:] for g in gates], axis=1)
    half_c = (-0.5 * LRU_C) * _softplus(-lam_ref[...])
    log_a = half_c * jnp.tanh(pre_r + 0.5 * bra_ref[...]) + half_c
    neg_2i = -1.0 - jnp.tanh(pre_i + 0.5 * bri_ref[...])
    a = jnp.exp(log_a)
    t = jnp.tanh(log_a)
    neg_quarter_root = jnp.where(t < 0.0, t * lax.rsqrt((-8.0 * t) * (1.0 - t)), 0.0)
    b = neg_quarter_root * (neg_2i * uc)

    n_blocks = width // LRU_BLOCK
    seg = seq // SUBLANES
    pad = SUBLANES
    blank = jnp.zeros((pad, LRU_BLOCK), F32)
    for n in range(n_blocks):
        for scr, val in ((at_scr, a), (bt_scr, b)):
            scr[n, :pad, :] = blank
            scr[n, pad:pad + seq, :] = val[:, n * LRU_BLOCK:(n + 1) * LRU_BLOCK]
            scr[n, pad + seq:, :] = blank

    def local(g, state, edge):
        steps = pl.ds(pad + g, SUBLANES, stride=seg + 1)
        inside = (row + g >= 0) & (row + g < seg)
        out = []
        for n, (h, p) in enumerate(state):
            a_g, b_g = at_scr[n, steps, :], bt_scr[n, steps, :]
            if edge:
                h_new = jnp.where(inside, a_g * h + b_g, h)
                p_new = jnp.where(inside, a_g * p, p)
                bt_scr[n, steps, :] = jnp.where(inside, h_new, b_g)
                at_scr[n, steps, :] = jnp.where(inside, p_new, a_g)
            else:
                h_new, p_new = a_g * h + b_g, a_g * p
                bt_scr[n, steps, :] = h_new
                at_scr[n, steps, :] = p_new
            out.append((h_new, p_new))
        return tuple(out)

    zero = jnp.zeros((SUBLANES, LRU_BLOCK), F32)
    state = ((zero, zero + 1.0),) * n_blocks
    full = seg - SUBLANES
    for g in range(1 - SUBLANES, 0):
        state = local(g, state, True)
    state = lax.fori_loop(0, full, functools.partial(local, edge=False), state, unroll=8)
    for g in range(full, seg):
        state = local(g, state, True)
    cols = []
    for n, (h, p) in enumerate(state):
        d = 1
        while d < SUBLANES:
            keep = row >= d
            h = p * jnp.where(keep, pltpu.roll(h, d, axis=0), 0.0) + h
            p = p * jnp.where(keep, pltpu.roll(p, d, axis=0), 1.0)
            d *= 2
        h_in = jnp.where(row >= 1, pltpu.roll(h, 1, axis=0), 0.0)
        cols.append(jnp.concatenate(
            [bt_scr[n, pad + j * seg:pad + (j + 1) * seg, :]
             + at_scr[n, pad + j * seg:pad + (j + 1) * seg, :] * h_in[j:j + 1, :]
             for j in range(SUBLANES)], axis=0))
    gate = gate_ref[...]
    inner = gate * (GELU_C1 + (GELU_C1 * GELU_C3) * (gate * gate))
    o_ref[...] = jnp.concatenate(cols, axis=1) * (gate * (1.0 + jnp.tanh(inner)))


def _rg_lru(ug, cw, cb, wg, bra, bri, lam, side_weights, layer, *, batch, seq):
    per = LRU_BLOCKS_PER_STEP
    nb = LRU_BLOCKS // per
    width = per * LRU_BLOCK
    vec = lambda: pl.BlockSpec((1, width), lambda b, n: (0, n))
    n_slabs = batch * nb
    sides = [_slab_specs(w, layer, n_slabs, w.shape[2], lambda b, n: b * nb + n)
             for w in side_weights]
    res = pl.pallas_call(
        functools.partial(_lru_kernel, seq=seq, n_side=len(sides)),
        grid=(batch, nb),
        in_specs=[
            *[pl.BlockSpec((None, seq, LRU_BLOCK), lambda b, n, e=e: (b, 0, per * n + e))
              for e in range(per)],
            pl.BlockSpec((None, seq, width), lambda b, n: (b, 0, nb + n)),
            pl.BlockSpec((CONV_W, width), lambda b, n: (0, n)),
            vec(),
            pl.BlockSpec((per, LRU_BLOCK, 2 * LRU_BLOCK), lambda b, n: (n, 0, 0)),
            vec(), vec(), vec(),
            *[sd[0] for sd in sides],
        ],
        out_specs=[pl.BlockSpec((None, seq, width), lambda b, n: (b, 0, n)),
                   *[sd[1] for sd in sides]],
        out_shape=[jax.ShapeDtypeStruct((batch, seq, LRU_W), F32), *[sd[2] for sd in sides]],
        scratch_shapes=[pltpu.VMEM((per, seq + 2 * SUBLANES, LRU_BLOCK), F32)] * 2,
        compiler_params=_params("arbitrary", "arbitrary"),
    )(*[ug] * per, ug, cw, cb, wg, bra, bri, lam, *side_weights)
    return res[0], [r[0] for r in res[1:]]


def _mix_xattn_kernel(of_ref, yl_ref, gf_ref, gl_ref, wout_ref, x_ref, gx_ref, wq_ref, gq_ref,
                      mem_ref, gm_ref, wkv_ref, gk_ref, wo_ref, side_ref, o_ref, side_out_ref,
                      ckv_ref, *, chunk):
    @pl.when(pl.program_id(1) == 0)
    def _():
        kv = _dot(_rms(mem_ref[...], gm_ref[...]).astype(BF16), wkv_ref[...])
        for hh in range(XATT_HEADS):
            sl = slice(hh * HEAD_DIM, (hh + 1) * HEAD_DIM)
            ckv_ref[:, sl] = _rms(kv[:, sl], gk_ref[...]).astype(BF16)
        ckv_ref[:, XATT_W:] = kv[:, XATT_W:].astype(BF16)

    _cast_slab(side_ref, side_out_ref)
    mf = _rms(of_ref[...], gf_ref[...]).astype(BF16)
    ml = _rms(yl_ref[...], gl_ref[...]).astype(BF16)
    for c in range(o_ref.shape[1] // chunk):
        sl = slice(c * chunk, (c + 1) * chunk)
        acc = _dot(mf, wout_ref[:FOX_W, sl]) + _dot(ml, wout_ref[FOX_W:, sl])
        o_ref[:, sl] = x_ref[:, sl] + acc

    x1 = o_ref[...]
    cq = _dot(_rms(x1, gx_ref[...]).astype(BF16), wq_ref[...])
    scale = 1.0 / math.sqrt(HEAD_DIM)
    scores = []
    for hh in range(XATT_HEADS):
        sl = slice(hh * HEAD_DIM, (hh + 1) * HEAD_DIM)
        qh = _rms(cq[:, sl], gq_ref[...]).astype(BF16)
        scores.append(_dot_nt(qh, ckv_ref[:, sl]))
    heads = []
    for hh in range(XATT_HEADS):
        vh = ckv_ref[:, XATT_W + hh * HEAD_DIM:XATT_W + (hh + 1) * HEAD_DIM]
        s = scores[hh] * scale
        e = jnp.exp(s - jnp.max(s, axis=-1, keepdims=True))
        l = jnp.sum(e, axis=-1, keepdims=True)
        heads.append((_dot(e.astype(BF16), vh) / l).astype(BF16))
    ox = jnp.concatenate(heads, axis=-1)
    o_ref[...] = x1 + _dot(ox, wo_ref[...])


def _mix_xattn(o_fox, y_lru, gf, gl, w_out, x3d, gx, w_cq, gq, mem, gm, w_ckv, gk, w_co,
               w_side, layer, *, tm):
    B, S, D = x3d.shape
    M = mem.shape[1]
    n_tiles = S // tm
    side_in, side_out, side_shape = _slab_specs(w_side, layer, B * n_tiles, w_side.shape[2],
                                                lambda b, i: b * n_tiles + i)
    const = lambda shape: pl.BlockSpec(shape, lambda b, i: (0,) * len(shape),
                                       pipeline_mode=pl.Buffered(1))
    tile = lambda w: pl.BlockSpec((None, tm, w), lambda b, i: (b, i, 0))
    return pl.pallas_call(
        functools.partial(_mix_xattn_kernel, chunk=2 * PROJ_CHUNK),
        grid=(B, S // tm),
        in_specs=[
            tile(FOX_W), tile(LRU_W), const((1, FOX_W)), const((1, LRU_W)),
            const((FOX_W + LRU_W, D)), tile(D), const((1, D)), const((D, XATT_W)),
            const((1, HEAD_DIM)),
            pl.BlockSpec((None, M, D), lambda b, i: (b, 0, 0)),
            const((1, D)), const((D, 2 * XATT_W)), const((1, HEAD_DIM)),
            const((XATT_W, D)),
            side_in,
        ],
        out_specs=[tile(D), side_out],
        out_shape=[jax.ShapeDtypeStruct((B, S, D), F32), side_shape],
        scratch_shapes=[pltpu.VMEM((M, 2 * XATT_W), BF16)],
        compiler_params=_params("arbitrary", "arbitrary"),
    )(o_fox, y_lru, gf, gl, w_out, x3d, gx, w_cq, gq, mem, gm, w_ckv, gk, w_co, w_side)


def _ffn_kernel(x_ref, g_ref, wg_ref, wu_ref, wd_ref, o_hbm, acc_scr, h_scr, sems,
                *, tm, nk, n_groups):
    p, k, r = pl.program_id(0), pl.program_id(1), pl.program_id(2)
    acc_ref = acc_scr.at[r]
    h_ref = h_scr.at[r]
    row_tile = p * FFN_ROW_TILES + r

    def writeback(tile, slot):
        rows = pl.ds(pl.multiple_of(tile * tm, tm), tm)
        return pltpu.make_async_copy(acc_scr.at[slot], o_hbm.at[rows, :], sems.at[slot])

    def hidden_tile(base_ref):
        h = h_ref[...]
        gate = _dot(h, wg_ref[...])
        up = _dot(h, wu_ref[...])
        act = (jax.nn.silu(gate) * up).astype(BF16)
        acc_ref[...] = base_ref[...] + _dot(act, wd_ref[...])

    @pl.when((k == 0) & (p > 0))
    def _():
        writeback(row_tile - FFN_ROW_TILES, r).wait()

    @pl.when(k == 0)
    def _():
        h_ref[...] = _rms(x_ref[...], g_ref[...]).astype(BF16)
        hidden_tile(x_ref)

    pl.when(k > 0)(lambda: hidden_tile(acc_ref))

    @pl.when(k == nk - 1)
    def _():
        writeback(row_tile, r).start()

    @pl.when((k == nk - 1) & (p == n_groups - 1) & (r == FFN_ROW_TILES - 1))
    def _():
        for slot in range(FFN_ROW_TILES):
            writeback((n_groups - 1) * FFN_ROW_TILES + slot, slot).wait()


def _ffn(x2d, g, w_gu, w_d, *, tm):
    T, D = x2d.shape
    H = w_d.shape[0]
    th = w_gu.shape[2]
    nk = H // th
    per = FFN_ROW_TILES
    n_rows = T // tm
    n_groups = n_rows // per
    x_row = lambda p, k, r: jnp.where(k == 0, per * p + r,
                                      jnp.minimum(per * (p + 1), n_rows - 1))
    return pl.pallas_call(
        functools.partial(_ffn_kernel, tm=tm, nk=nk, n_groups=n_groups),
        grid=(n_groups, nk, per),
        in_specs=[
            pl.BlockSpec((tm, D), lambda p, k, r: (x_row(p, k, r), 0)),
            pl.BlockSpec((1, D), lambda p, k, r: (0, 0)),
            pl.BlockSpec((None, D, th), lambda p, k, r: (k, 0, 0)),
            pl.BlockSpec((None, D, th), lambda p, k, r: (nk + k, 0, 0)),
            pl.BlockSpec((th, D), lambda p, k, r: (k, 0)),
        ],
        out_specs=pl.BlockSpec(memory_space=pl.ANY),
        out_shape=jax.ShapeDtypeStruct((T, D), F32),
        scratch_shapes=[pltpu.VMEM((per, tm, D), F32), pltpu.VMEM((per, tm, D), BF16),
                        pltpu.SemaphoreType.DMA((per,))],
        compiler_params=_params("arbitrary", "arbitrary", "arbitrary"),
    )(x2d, g, w_gu, w_gu, w_d)


def kernel(x, mem, g_mix, w_in, b_f, g_q, g_k, conv_w, conv_b, w_ra, b_ra, w_ri, b_ri, lam,
           g_fox_out, g_lru_out, w_out, g_xattn, g_mem, w_cq, w_ckv, g_cq, g_ck, w_co, g_ffn,
           w_gate_up, w_down):
    B, S, D = x.shape
    T = B * S
    depth = g_mix.shape[0]
    row = lambda v: v.reshape(1, -1).astype(F32)

    for l in range(depth):
        w_t, w_f = _stage_in_weights(jnp.swapaxes(w_in, 1, 2), l, tr=512)
        g_qk = jnp.stack([g_q[l] * FOX_Q_SCALE, g_k[l]]).reshape(2, 1, HEAD_DIM).astype(F32)
        bf_col = jnp.zeros((F_ROWS, 1), F32).at[:FOX_HEADS, 0].set(b_f[l])
        w_gates = jnp.concatenate([w_ra[l], w_ri[l]], axis=-1).astype(BF16)

        x2d = x.reshape(T, D)

        qk, v_t, ug, f_t = _in_proj(x2d, row(g_mix[l]), w_t, g_qk, w_f, tm=1024, tn=1024)
        o_fox, w_gu_bf = _fox_attention(qk.reshape(B, S, 2 * FOX_W), v_t, f_t, bf_col,
                                        w_gate_up, l, batch=B, seq=S,
                                        n_slabs=B * FOX_HEADS // HEADS_PER_STEP,
                                        side_tc=FFN_TILE)
        y_lru, (w_out_bf, w_cq_bf, w_ckv_bf, w_co_bf) = _rg_lru(
            ug.reshape(B, S, 2 * LRU_W), conv_w[l], row(conv_b[l]), w_gates, row(b_ra[l]),
            row(b_ri[l]), row(lam[l]), [w_out, w_cq, w_ckv, w_co], l, batch=B, seq=S)

        x3d, w_down_bf = _mix_xattn(
            o_fox, y_lru, row(g_fox_out[l]), row(g_lru_out[l]), w_out_bf, x2d.reshape(B, S, D),
            row(g_xattn[l]), w_cq_bf, row(g_cq[l]), mem, row(g_mem[l]), w_ckv_bf, row(g_ck[l]),
            w_co_bf, w_down, l, tm=512)

        x2d = _ffn(x3d.reshape(T, D), row(g_ffn[l]), w_gu_bf, w_down_bf[0], tm=1024)
        x = x2d.reshape(B, S, D)
    return x
```

```python
import functools
import math

import jax
import jax.numpy as jnp
from jax import lax
from jax.experimental import pallas as pl
from jax.experimental.pallas import tpu as pltpu

F32 = jnp.float32
BF16 = jnp.bfloat16

HEAD_DIM = 128
FOX_HEADS = 8
FOX_W = FOX_HEADS * HEAD_DIM
LRU_BLOCKS = 8
LRU_BLOCK = 128
LRU_W = LRU_BLOCKS * LRU_BLOCK
LRU_C = 8.0
CONV_W = 4
XATT_HEADS = 4
XATT_W = XATT_HEADS * HEAD_DIM
RMS_EPS = 1e-6
F_ROWS = 16
Q_BLOCK = 512
K_CHUNK = 256
GELU_C1 = math.sqrt(2.0 / math.pi)
GELU_C3 = 0.044715
DIAG_CHUNK = 256
HEADS_PER_STEP = 2
SCORE_LOOKAHEAD = 10
V_PAD_ROWS = 16
PROJ_CHUNK = 256
SUBLANES = 8
ROW_TILES_PER_WEIGHT = 2
FFN_ROW_TILES = 2
LRU_BLOCKS_PER_STEP = 4
FFN_TILE = 512
FOX_Q_SCALE = math.log2(math.e) / math.sqrt(HEAD_DIM)
BIAS_PIECES = 3

V7X_VMEM_LIMIT_BYTES = 60 * 1024 * 1024


def _rms(x, g):
    ms = jnp.mean(x * x, axis=-1, keepdims=True)
    return x * lax.rsqrt(ms + RMS_EPS) * g


def _dot(a, b):
    return jnp.dot(a, b, preferred_element_type=F32)


def _dot_nt(a, b):
    return lax.dot_general(a, b, (((1,), (1,)), ((), ())), preferred_element_type=F32)


def _softplus(x):
    return jnp.maximum(x, 0.0) + jnp.log1p(jnp.exp(-jnp.abs(x)))


def _cast_slab(src_ref, dst_ref):
    n_tiles, _, tc = dst_ref.shape
    for t in range(n_tiles):
        dst_ref[t] = src_ref[:, t * tc:(t + 1) * tc].astype(BF16)


def _slab_specs(w, layer, n_slabs, tc, step_of):
    _, R, C = w.shape
    rows = R // n_slabs
    slab = lambda *ids: jnp.minimum(step_of(*ids), n_slabs - 1)
    return (pl.BlockSpec((None, rows, C), lambda *ids: (layer, slab(*ids), 0)),
            pl.BlockSpec((C // tc, rows, tc), lambda *ids: (0, slab(*ids), 0)),
            jax.ShapeDtypeStruct((C // tc, R, tc), BF16))


def _params(*sem):
    return pltpu.CompilerParams(dimension_semantics=sem,
                                vmem_limit_bytes=V7X_VMEM_LIMIT_BYTES)


def _wstage_kernel(a_ref, b_ref, w_ref, wf_ref, *, f_tile):
    t = pl.program_id(0)

    @pl.when(t < f_tile)
    def _():
        w_ref[...] = a_ref[...].astype(BF16)

    @pl.when(t >= f_tile)
    def _():
        w_ref[...] = jnp.concatenate([a_ref[FOX_HEADS:, :], b_ref[...]], axis=0).astype(BF16)

    @pl.when(t == f_tile)
    def _():
        wf_ref[:FOX_HEADS, :] = a_ref[:FOX_HEADS, :].astype(BF16)
        wf_ref[FOX_HEADS:, :] = jnp.zeros((F_ROWS - FOX_HEADS, wf_ref.shape[1]), BF16)


def _stage_in_weights(w_t, layer, *, tr):
    _, N, D = w_t.shape
    n_rows = N - FOX_HEADS
    f_tile = 3 * FOX_W // tr
    return pl.pallas_call(
        functools.partial(_wstage_kernel, f_tile=f_tile),
        grid=(n_rows // tr,),
        in_specs=[pl.BlockSpec((None, tr, D), lambda t: (layer, t, 0)),
                  pl.BlockSpec((None, FOX_HEADS, D),
                               lambda t: (layer, (jnp.maximum(t, f_tile) + 1) * (tr // FOX_HEADS), 0))],
        out_specs=[pl.BlockSpec((tr, D), lambda t: (t, 0)),
                   pl.BlockSpec((F_ROWS, D), lambda t: (0, 0))],
        out_shape=[jax.ShapeDtypeStruct((n_rows, D), BF16),
                   jax.ShapeDtypeStruct((F_ROWS, D), BF16)],
        compiler_params=_params("arbitrary"),
    )(w_t, w_t)


def _inproj_kernel(x_ref, g_ref, w_ref, gh_ref, wf_ref,
                   qk_ref, vt_ref, ug_ref, f_ref, h_scr, *, tn, n_qk, n_v):
    j = pl.program_id(1)
    h_ref = h_scr.at[pl.program_id(2)]

    chunks = [slice(c * PROJ_CHUNK, (c + 1) * PROJ_CHUNK) for c in range(tn // PROJ_CHUNK)]

    def qk_tile():
        for sl in chunks:
            acc = _dot_nt(h_ref[...], w_ref[sl, :])
            for hh in range(PROJ_CHUNK // HEAD_DIM):
                lo = sl.start + hh * HEAD_DIM
                qk_ref[:, lo:lo + HEAD_DIM] = _rms(
                    acc[:, hh * HEAD_DIM:(hh + 1) * HEAD_DIM], gh_ref[...]).astype(BF16)

    @pl.when(j == 0)
    def _():
        h_ref[...] = _rms(x_ref[...], g_ref[...]).astype(BF16)
        qk_tile()

    pl.when((j > 0) & (j < n_qk))(qk_tile)

    @pl.when((j >= n_qk) & (j < n_qk + n_v))
    def _():
        for sl in chunks[:-1]:
            vt_ref[sl, :] = _dot_nt(w_ref[sl, :], h_ref[...]).astype(BF16)
        sl = chunks[-1]
        acc = _dot_nt(jnp.concatenate([w_ref[sl, :], wf_ref[...]], axis=0), h_ref[...])
        vt_ref[sl, :] = acc[:PROJ_CHUNK].astype(BF16)
        f_ref[...] = acc[PROJ_CHUNK:]

    @pl.when(j >= n_qk + n_v)
    def _():
        for sl in chunks:
            ug_ref[:, sl] = _dot_nt(h_ref[...], w_ref[sl, :])


def _in_proj(x2d, g, w_t, g_qk, wf, *, tm, tn):
    T, D = x2d.shape
    n_qk = 2 * FOX_W // tn
    n_v = FOX_W // tn
    assert n_v == 1, "the forget logits are produced by the single v step"
    n_ug = 2 * LRU_W // tn
    tiles_per_gain = FOX_W // tn
    clip = lambda v, n: jnp.clip(v, 0, n - 1)
    n_steps = n_qk + n_v + n_ug
    per = ROW_TILES_PER_WEIGHT
    n_rows = T // tm
    first = lambda p: per * p
    last = lambda p: per * p + per - 1

    def row_of(p, j, r, j_lo, j_hi):
        return jnp.where(j < j_lo, first(p), jnp.where(j < j_hi, first(p) + r, last(p)))

    x_row = lambda p, j, r: jnp.where(j == 0, first(p) + r, jnp.minimum(first(p + 1), n_rows - 1))
    return pl.pallas_call(
        functools.partial(_inproj_kernel, tn=tn, n_qk=n_qk, n_v=n_v),
        grid=(n_rows // per, n_steps, per),
        in_specs=[
            pl.BlockSpec((tm, D), lambda p, j, r: (x_row(p, j, r), 0)),
            pl.BlockSpec((1, D), lambda p, j, r: (0, 0)),
            pl.BlockSpec((tn, D), lambda p, j, r: (j, 0)),
            pl.BlockSpec((None, 1, HEAD_DIM),
                         lambda p, j, r: (clip(j // tiles_per_gain, 2), 0, 0)),
            pl.BlockSpec((F_ROWS, D), lambda p, j, r: (0, 0)),
        ],
        out_specs=[
            pl.BlockSpec((tm, tn), lambda p, j, r: (row_of(p, j, r, 0, n_qk), clip(j, n_qk))),
            pl.BlockSpec((tn, tm), lambda p, j, r: (0, row_of(p, j, r, n_qk, n_qk + n_v))),
            pl.BlockSpec((tm, tn), lambda p, j, r: (row_of(p, j, r, n_qk + n_v, n_steps),
                                                    clip(j - n_qk - n_v, n_ug))),
            pl.BlockSpec((F_ROWS, tm), lambda p, j, r: (0, row_of(p, j, r, n_qk, n_qk + n_v))),
        ],
        out_shape=[
            jax.ShapeDtypeStruct((T, 2 * FOX_W), BF16),
            jax.ShapeDtypeStruct((FOX_W, T), BF16),
            jax.ShapeDtypeStruct((T, 2 * LRU_W), F32),
            jax.ShapeDtypeStruct((F_ROWS, T), F32),
        ],
        scratch_shapes=[pltpu.VMEM((per, tm, D), BF16)],
        compiler_params=_params("arbitrary", "arbitrary", "arbitrary"),
    )(x2d, g, w_t, g_qk, wf)


def _fox_kernel(q_ref, k_ref, vt_ref, f_ref, bf_ref, side_ref, o_ref, side_out_ref,
                bias_scr, vaug_scr, *, seq):
    hp = pl.program_id(1)

    @pl.when(hp == 0)
    def _():
        z = f_ref[...] + bf_ref[...]
        c = jnp.minimum(z, 0.0) - jnp.log1p(jnp.exp(-jnp.abs(z)))
        lane = lax.broadcasted_iota(jnp.int32, c.shape, 1)
        d = 1
        while d < seq:
            c = c + jnp.where(lane >= d, pltpu.roll(c, d, axis=1), 0.0)
            d *= 2
        bias = c[:FOX_HEADS] * (-math.log2(math.e))
        pieces = []
        for _ in range(BIAS_PIECES):
            p = bias.astype(BF16).astype(F32)
            pieces.append(p)
            bias = bias - p
        pad = jnp.zeros((HEAD_DIM - BIAS_PIECES * FOX_HEADS, seq), F32)
        bias_scr[...] = jnp.concatenate(pieces + [pad], axis=0).T.astype(BF16)

    _cast_slab(side_ref, side_out_ref)
    ones_row = lax.broadcasted_iota(jnp.int32, (V_PAD_ROWS, seq), 0) == 0
    for e in range(HEADS_PER_STEP):
        lanes = slice(e * HEAD_DIM, (e + 1) * HEAD_DIM)
        vaug_scr[e, :HEAD_DIM, :] = vt_ref[lanes, :]
        vaug_scr[e, HEAD_DIM:, :] = jnp.where(ones_row, 1.0, 0.0).astype(BF16)

    tq, tk = Q_BLOCK, K_CHUNK
    dc = DIAG_CHUNK

    def selector(e, n):
        lane = lax.broadcasted_iota(jnp.int32, (n, HEAD_DIM), 1)
        h = hp * HEADS_PER_STEP + e
        mine = lane == h
        for p in range(1, BIAS_PIECES):
            mine = mine | (lane == h + p * FOX_HEADS)
        return jnp.where(mine, 1.0, 0.0).astype(BF16)

    selectors = {(e, n): selector(e, n) for e in range(HEADS_PER_STEP)
                 for n in range(dc, tq + 1, dc)}
    below_diag = (lax.broadcasted_iota(jnp.int32, (dc, dc), 0)
                  <= lax.broadcasted_iota(jnp.int32, (dc, dc), 1))

    def scores(e, lo, q0, k0, nk):
        q = q_ref[lo + q0:lo + tq, e * HEAD_DIM:(e + 1) * HEAD_DIM]
        q_aug = jnp.concatenate([q, selectors[e, tq - q0]], axis=1)
        k_aug = jnp.concatenate([k_ref[k0:k0 + nk, e * HEAD_DIM:(e + 1) * HEAD_DIM],
                                 bias_scr[k0:k0 + nk, :]], axis=1)
        return _dot_nt(k_aug, q_aug)

    steps = []
    for qi in range(seq // tq):
        lo = qi * tq
        pieces = [(0, kc * tk, tk) for kc in range(lo // tk)]
        pieces += [(j * dc, lo + j * dc, dc) for j in range(tq // dc)]
        steps += [(e, lo) + piece for piece in pieces for e in range(HEADS_PER_STEP)]
    pending = [scores(*st) for st in steps[:SCORE_LOOKAHEAD]]
    m = [None] * HEADS_PER_STEP
    acc = [None] * HEADS_PER_STEP
    for idx, (e, lo, q0, k0, nk) in enumerate(steps):
        t = pending.pop(0)
        if idx + SCORE_LOOKAHEAD < len(steps):
            pending.append(scores(*steps[idx + SCORE_LOOKAHEAD]))
        if k0 >= lo:
            masked = jnp.where(below_diag, t[:, :dc], -jnp.inf)
            t = jnp.concatenate([masked, t[:, dc:]], axis=1) if q0 + dc < tq else masked
        cm = jnp.max(t, axis=0, keepdims=True)
        if k0 == 0:
            m_new = cm
        else:
            m_old = m[e][:, q0:]
            m_new = jnp.maximum(m_old, cm)
        p = jnp.exp2(t - m_new).astype(BF16)
        pv = _dot(vaug_scr[e, :, k0:k0 + nk], p)
        if k0 == 0:
            acc[e], m[e] = pv, m_new
        else:
            upd = acc[e][:, q0:] * jnp.exp2(m_old - m_new) + pv
            acc[e] = jnp.concatenate([acc[e][:, :q0], upd], axis=1) if q0 else upd
            m[e] = jnp.concatenate([m[e][:, :q0], m_new], axis=1) if q0 else m_new
        if k0 + nk == lo + tq:
            inv_l = 1.0 / acc[e][HEAD_DIM:HEAD_DIM + 1, :]
            o_ref[lo:lo + tq, e * HEAD_DIM:(e + 1) * HEAD_DIM] = (acc[e][:HEAD_DIM, :] * inv_l).T


def _fox_attention(qk, v_t, f_t, b_f, w_side, layer, *, batch, seq, n_slabs, side_tc):
    n_pairs = FOX_HEADS // HEADS_PER_STEP
    wide = HEADS_PER_STEP * HEAD_DIM
    side_in, side_out, side_shape = _slab_specs(w_side, layer, n_slabs, side_tc,
                                                lambda b, hp: b * n_pairs + hp)
    return pl.pallas_call(
        functools.partial(_fox_kernel, seq=seq),
        grid=(batch, n_pairs),
        in_specs=[
            pl.BlockSpec((None, seq, wide), lambda b, hp: (b, 0, hp)),
            pl.BlockSpec((None, seq, wide), lambda b, hp: (b, 0, n_pairs + hp)),
            pl.BlockSpec((wide, seq), lambda b, hp: (hp, b)),
            pl.BlockSpec((F_ROWS, seq), lambda b, hp: (0, b)),
            pl.BlockSpec((F_ROWS, 1), lambda b, hp: (0, 0)),
            side_in,
        ],
        out_specs=[pl.BlockSpec((None, seq, wide), lambda b, hp: (b, 0, hp)), side_out],
        out_shape=[jax.ShapeDtypeStruct((batch, seq, FOX_W), F32), side_shape],
        scratch_shapes=[pltpu.VMEM((seq, HEAD_DIM), BF16),
                        pltpu.VMEM((HEADS_PER_STEP, HEAD_DIM + V_PAD_ROWS, seq), BF16)],
        compiler_params=_params("arbitrary", "arbitrary"),
    )(qk, qk, v_t, f_t, b_f, w_side)


def _lru_kernel(*refs, seq, n_side):
    u_refs = refs[:LRU_BLOCKS_PER_STEP]
    refs = refs[LRU_BLOCKS_PER_STEP:]
    gate_ref, cw_ref, cb_ref, wg_ref, bra_ref, bri_ref, lam_ref = refs[:7]
    side_refs = refs[7:7 + n_side]
    o_ref = refs[7 + n_side]
    side_out_refs = refs[8 + n_side:8 + 2 * n_side]
    at_scr, bt_scr = refs[8 + 2 * n_side:]
    for src_ref, dst_ref in zip(side_refs, side_out_refs):
        _cast_slab(src_ref, dst_ref)
    width = o_ref.shape[1]
    row = lax.broadcasted_iota(jnp.int32, (SUBLANES, LRU_BLOCK), 0)

    def shifted(d):
        cols = []
        for u_ref in u_refs:
            head = jnp.where(row >= d, pltpu.roll(u_ref[0:SUBLANES, :], d, axis=0), 0.0)
            cols.append(jnp.concatenate([head, u_ref[SUBLANES - d:seq - d, :]], axis=0))
        return jnp.concatenate(cols, axis=1)

    cw = cw_ref[...]
    uc = cw[0:1, :] * shifted(3)
    uc = uc + cw[1:2, :] * shifted(2)
    uc = uc + cw[2:3, :] * shifted(1)
    uc = uc + cw[3:4, :] * jnp.concatenate([u_ref[...] for u_ref in u_refs], axis=1)
    uc = cb_ref[...] + uc

    ucb = uc.astype(BF16)
    gates = [_dot(ucb[:, n * LRU_BLOCK:(n + 1) * LRU_BLOCK], wg_ref[n] * 0.5)
             for n in range(width // LRU_BLOCK)]
    pre_r = jnp.concatenate([g[:, :LRU_BLOCK] for g in gates], axis=1)
    pre_i = jnp.concatenate([g[:, LRU_BLOCK:] for g in gates], axis=1)
    half_c = (-0.5 * LRU_C) * _softplus(-lam_ref[...])
    log_a = half_c * jnp.tanh(pre_r + 0.5 * bra_ref[...]) + half_c
    neg_2i = -1.0 - jnp.tanh(pre_i + 0.5 * bri_ref[...])
    a = jnp.exp(log_a)
    t = jnp.tanh(log_a)
    neg_quarter_root = jnp.where(t < 0.0, t * lax.rsqrt((-8.0 * t) * (1.0 - t)), 0.0)
    b = neg_quarter_root * (neg_2i * uc)

    n_blocks = width // LRU_BLOCK
    seg = seq // SUBLANES
    pad = SUBLANES
    blank = jnp.zeros((pad, LRU_BLOCK), F32)
    for n in range(n_blocks):
        for scr, val in ((at_scr, a), (bt_scr, b)):
            scr[n, :pad, :] = blank
            scr[n, pad:pad + seq, :] = val[:, n * LRU_BLOCK:(n + 1) * LRU_BLOCK]
            scr[n, pad + seq:, :] = blank

    def local(g, state, edge):
        steps = pl.ds(pad + g, SUBLANES, stride=seg + 1)
        inside = (row + g >= 0) & (row + g < seg)
        out = []
        for n, (h, p) in enumerate(state):
            a_g, b_g = at_scr[n, steps, :], bt_scr[n, steps, :]
            if edge:
                h_new = jnp.where(inside, a_g * h + b_g, h)
                p_new = jnp.where(inside, a_g * p, p)
                bt_scr[n, steps, :] = jnp.where(inside, h_new, b_g)
                at_scr[n, steps, :] = jnp.where(inside, p_new, a_g)
            else:
                h_new, p_new = a_g * h + b_g, a_g * p
                bt_scr[n, steps, :] = h_new
                at_scr[n, steps, :] = p_new
            out.append((h_new, p_new))
        return tuple(out)

    zero = jnp.zeros((SUBLANES, LRU_BLOCK), F32)
    state = ((zero, zero + 1.0),) * n_blocks
    full = seg - SUBLANES
    for g in range(1 - SUBLANES, 0):
        state = local(g, state, True)
    state = lax.fori_loop(0, full, functools.partial(local, edge=False), state, unroll=8)
    for g in range(full, seg):
        state = local(g, state, True)
    cols = []
    for n, (h, p) in enumerate(state):
        d = 1
        while d < SUBLANES:
            keep = row >= d
            h = p * jnp.where(keep, pltpu.roll(h, d, axis=0), 0.0) + h
            p = p * jnp.where(keep, pltpu.roll(p, d, axis=0), 1.0)
            d *= 2
        h_in = jnp.where(row >= 1, pltpu.roll(h, 1, axis=0), 0.0)
        cols.append(jnp.concatenate(
            [bt_scr[n, pad + j * seg:pad + (j + 1) * seg, :]
             + at_scr[n, pad + j * seg:pad + (j + 1) * seg, :] * h_in[j:j + 1, :]
             for j in range(SUBLANES)], axis=0))
    gate = gate_ref[...]
    inner = gate * (GELU_C1 + (GELU_C1 * GELU_C3) * (gate * gate))
    o_ref[...] = jnp.concatenate(cols, axis=1) * (gate * (1.0 + jnp.tanh(inner)))


def _rg_lru(ug, cw, cb, wg, bra, bri, lam, side_weights, layer, *, batch, seq):
    per = LRU_BLOCKS_PER_STEP
    nb = LRU_BLOCKS // per
    width = per * LRU_BLOCK
    vec = lambda: pl.BlockSpec((1, width), lambda b, n: (0, n))
    n_slabs = batch * nb
    sides = [_slab_specs(w, layer, n_slabs, w.shape[2], lambda b, n: b * nb + n)
             for w in side_weights]
    res = pl.pallas_call(
        functools.partial(_lru_kernel, seq=seq, n_side=len(sides)),
        grid=(batch, nb),
        in_specs=[
            *[pl.BlockSpec((None, seq, LRU_BLOCK), lambda b, n, e=e: (b, 0, per * n + e))
              for e in range(per)],
            pl.BlockSpec((None, seq, width), lambda b, n: (b, 0, nb + n)),
            pl.BlockSpec((CONV_W, width), lambda b, n: (0, n)),
            vec(),
            pl.BlockSpec((per, LRU_BLOCK, 2 * LRU_BLOCK), lambda b, n: (n, 0, 0)),
            vec(), vec(), vec(),
            *[sd[0] for sd in sides],
        ],
        out_specs=[pl.BlockSpec((None, seq, width), lambda b, n: (b, 0, n)),
                   *[sd[1] for sd in sides]],
        out_shape=[jax.ShapeDtypeStruct((batch, seq, LRU_W), F32), *[sd[2] for sd in sides]],
        scratch_shapes=[pltpu.VMEM((per, seq + 2 * SUBLANES, LRU_BLOCK), F32)] * 2,
        compiler_params=_params("arbitrary", "arbitrary"),
    )(*[ug] * per, ug, cw, cb, wg, bra, bri, lam, *side_weights)
    return res[0], [r[0] for r in res[1:]]


def _mix_xattn_kernel(of_ref, yl_ref, gf_ref, gl_ref, wout_ref, x_ref, gx_ref, wq_ref, gq_ref,
                      mem_ref, gm_ref, wkv_ref, gk_ref, wo_ref, side_ref, o_ref, side_out_ref,
                      ckv_ref, *, chunk):
    @pl.when(pl.program_id(1) == 0)
    def _():
        kv = _dot(_rms(mem_ref[...], gm_ref[...]).astype(BF16), wkv_ref[...])
        for hh in range(XATT_HEADS):
            sl = slice(hh * HEAD_DIM, (hh + 1) * HEAD_DIM)
            ckv_ref[:, sl] = _rms(kv[:, sl], gk_ref[...]).astype(BF16)
        ckv_ref[:, XATT_W:] = kv[:, XATT_W:].astype(BF16)

    _cast_slab(side_ref, side_out_ref)
    mf = _rms(of_ref[...], gf_ref[...]).astype(BF16)
    ml = _rms(yl_ref[...], gl_ref[...]).astype(BF16)
    for c in range(o_ref.shape[1] // chunk):
        sl = slice(c * chunk, (c + 1) * chunk)
        acc = _dot(mf, wout_ref[:FOX_W, sl]) + _dot(ml, wout_ref[FOX_W:, sl])
        o_ref[:, sl] = x_ref[:, sl] + acc

    x1 = o_ref[...]
    cq = _dot(_rms(x1, gx_ref[...]).astype(BF16), wq_ref[...])
    scale = 1.0 / math.sqrt(HEAD_DIM)
    scores = []
    for hh in range(XATT_HEADS):
        sl = slice(hh * HEAD_DIM, (hh + 1) * HEAD_DIM)
        qh = _rms(cq[:, sl], gq_ref[...]).astype(BF16)
        scores.append(_dot_nt(qh, ckv_ref[:, sl]))
    heads = []
    for hh in range(XATT_HEADS):
        vh = ckv_ref[:, XATT_W + hh * HEAD_DIM:XATT_W + (hh + 1) * HEAD_DIM]
        s = scores[hh] * scale
        e = jnp.exp(s - jnp.max(s, axis=-1, keepdims=True))
        l = jnp.sum(e, axis=-1, keepdims=True)
        heads.append((_dot(e.astype(BF16), vh) / l).astype(BF16))
    ox = jnp.concatenate(heads, axis=-1)
    o_ref[...] = x1 + _dot(ox, wo_ref[...])


def _mix_xattn(o_fox, y_lru, gf, gl, w_out, x3d, gx, w_cq, gq, mem, gm, w_ckv, gk, w_co,
               w_side, layer, *, tm):
    B, S, D = x3d.shape
    M = mem.shape[1]
    n_tiles = S // tm
    side_in, side_out, side_shape = _slab_specs(w_side, layer, B * n_tiles, w_side.shape[2],
                                                lambda b, i: b * n_tiles + i)
    const = lambda shape: pl.BlockSpec(shape, lambda b, i: (0,) * len(shape),
                                       pipeline_mode=pl.Buffered(1))
    tile = lambda w: pl.BlockSpec((None, tm, w), lambda b, i: (b, i, 0))
    return pl.pallas_call(
        functools.partial(_mix_xattn_kernel, chunk=2 * PROJ_CHUNK),
        grid=(B, S // tm),
        in_specs=[
            tile(FOX_W), tile(LRU_W), const((1, FOX_W)), const((1, LRU_W)),
            const((FOX_W + LRU_W, D)), tile(D), const((1, D)), const((D, XATT_W)),
            const((1, HEAD_DIM)),
            pl.BlockSpec((None, M, D), lambda b, i: (b, 0, 0)),
            const((1, D)), const((D, 2 * XATT_W)), const((1, HEAD_DIM)),
            const((XATT_W, D)),
            side_in,
        ],
        out_specs=[tile(D), side_out],
        out_shape=[jax.ShapeDtypeStruct((B, S, D), F32), side_shape],
        scratch_shapes=[pltpu.VMEM((M, 2 * XATT_W), BF16)],
        compiler_params=_params("arbitrary", "arbitrary"),
    )(o_fox, y_lru, gf, gl, w_out, x3d, gx, w_cq, gq, mem, gm, w_ckv, gk, w_co, w_side)


def _ffn_kernel(x_ref, g_ref, wg_ref, wu_ref, wd_ref, o_hbm, acc_scr, h_scr, sems,
                *, tm, nk, n_groups):
    p, k, r = pl.program_id(0), pl.program_id(1), pl.program_id(2)
    acc_ref = acc_scr.at[r]
    h_ref = h_scr.at[r]
    row_tile = p * FFN_ROW_TILES + r

    def writeback(tile, slot):
        rows = pl.ds(pl.multiple_of(tile * tm, tm), tm)
        return pltpu.make_async_copy(acc_scr.at[slot], o_hbm.at[rows, :], sems.at[slot])

    def hidden_tile(base_ref):
        h = h_ref[...]
        gate = _dot(h, wg_ref[...])
        up = _dot(h, wu_ref[...])
        half = 0.5 * gate
        act = ((half + half * jnp.tanh(half)) * up).astype(BF16)
        acc_ref[...] = base_ref[...] + _dot(act, wd_ref[...])

    @pl.when((k == 0) & (p > 0))
    def _():
        writeback(row_tile - FFN_ROW_TILES, r).wait()

    @pl.when(k == 0)
    def _():
        h_ref[...] = _rms(x_ref[...], g_ref[...]).astype(BF16)
        hidden_tile(x_ref)

    pl.when(k > 0)(lambda: hidden_tile(acc_ref))

    @pl.when(k == nk - 1)
    def _():
        writeback(row_tile, r).start()

    @pl.when((k == nk - 1) & (p == n_groups - 1) & (r == FFN_ROW_TILES - 1))
    def _():
        for slot in range(FFN_ROW_TILES):
            writeback((n_groups - 1) * FFN_ROW_TILES + slot, slot).wait()


def _ffn(x2d, g, w_gu, w_d, *, tm):
    T, D = x2d.shape
    H = w_d.shape[0]
    th = w_gu.shape[2]
    nk = H // th
    per = FFN_ROW_TILES
    n_rows = T // tm
    n_groups = n_rows // per
    x_row = lambda p, k, r: jnp.where(k == 0, per * p + r,
                                      jnp.minimum(per * (p + 1), n_rows - 1))
    return pl.pallas_call(
        functools.partial(_ffn_kernel, tm=tm, nk=nk, n_groups=n_groups),
        grid=(n_groups, nk, per),
        in_specs=[
            pl.BlockSpec((tm, D), lambda p, k, r: (x_row(p, k, r), 0)),
            pl.BlockSpec((1, D), lambda p, k, r: (0, 0)),
            pl.BlockSpec((None, D, th), lambda p, k, r: (k, 0, 0)),
            pl.BlockSpec((None, D, th), lambda p, k, r: (nk + k, 0, 0)),
            pl.BlockSpec((th, D), lambda p, k, r: (k, 0)),
        ],
        out_specs=pl.BlockSpec(memory_space=pl.ANY),
        out_shape=jax.ShapeDtypeStruct((T, D), F32),
        scratch_shapes=[pltpu.VMEM((per, tm, D), F32), pltpu.VMEM((per, tm, D), BF16),
                        pltpu.SemaphoreType.DMA((per,))],
        compiler_params=_params("arbitrary", "arbitrary", "arbitrary"),
    )(x2d, g, w_gu, w_gu, w_d)


def kernel(x, mem, g_mix, w_in, b_f, g_q, g_k, conv_w, conv_b, w_ra, b_ra, w_ri, b_ri, lam,
           g_fox_out, g_lru_out, w_out, g_xattn, g_mem, w_cq, w_ckv, g_cq, g_ck, w_co, g_ffn,
           w_gate_up, w_down):
    B, S, D = x.shape
    T = B * S
    depth = g_mix.shape[0]
    row = lambda v: v.reshape(1, -1).astype(F32)

    for l in range(depth):
        w_t, w_f = _stage_in_weights(jnp.swapaxes(w_in, 1, 2), l, tr=512)
        g_qk = jnp.stack([g_q[l] * FOX_Q_SCALE, g_k[l]]).reshape(2, 1, HEAD_DIM).astype(F32)
        bf_col = jnp.zeros((F_ROWS, 1), F32).at[:FOX_HEADS, 0].set(b_f[l])
        w_gates = jnp.concatenate([w_ra[l], w_ri[l]], axis=-1).astype(BF16)

        x2d = x.reshape(T, D)

        qk, v_t, ug, f_t = _in_proj(x2d, row(g_mix[l]), w_t, g_qk, w_f, tm=1024, tn=1024)
        o_fox, w_gu_bf = _fox_attention(qk.reshape(B, S, 2 * FOX_W), v_t, f_t, bf_col,
                                        w_gate_up, l, batch=B, seq=S,
                                        n_slabs=B * FOX_HEADS // HEADS_PER_STEP,
                                        side_tc=FFN_TILE)
        y_lru, (w_out_bf, w_cq_bf, w_ckv_bf, w_co_bf) = _rg_lru(
            ug.reshape(B, S, 2 * LRU_W), conv_w[l], row(conv_b[l]), w_gates, row(b_ra[l]),
            row(b_ri[l]), row(lam[l]), [w_out, w_cq, w_ckv, w_co], l, batch=B, seq=S)

        x3d, w_down_bf = _mix_xattn(
            o_fox, y_lru, row(g_fox_out[l]), row(g_lru_out[l]), w_out_bf, x2d.reshape(B, S, D),
            row(g_xattn[l]), w_cq_bf, row(g_cq[l]), mem, row(g_mem[l]), w_ckv_bf, row(g_ck[l]),
            w_co_bf, w_down, l, tm=512)

        x2d = _ffn(x3d.reshape(T, D), row(g_ffn[l]), w_gu_bf, w_down_bf[0], tm=1024)
        x = x2d.reshape(B, S, D)
    return x
```

```python
import functools
import math

import jax
import jax.numpy as jnp
from jax import lax
from jax.experimental import pallas as pl
from jax.experimental.pallas import tpu as pltpu

F32 = jnp.float32
BF16 = jnp.bfloat16

HEAD_DIM = 128
FOX_HEADS = 8
FOX_W = FOX_HEADS * HEAD_DIM
LRU_BLOCKS = 8
LRU_BLOCK = 128
LRU_W = LRU_BLOCKS * LRU_BLOCK
LRU_C = 8.0
CONV_W = 4
XATT_HEADS = 4
XATT_W = XATT_HEADS * HEAD_DIM
RMS_EPS = 1e-6
F_ROWS = 16
Q_BLOCK = 512
K_CHUNK = 256
GELU_C1 = math.sqrt(2.0 / math.pi)
GELU_C3 = 0.044715
SOFTMAX_LANES = 256
DIAG_CHUNK = 256
HEADS_PER_STEP = 2
SCORE_LOOKAHEAD = 10
V_PAD_ROWS = 16
PROJ_CHUNK = 256
SUBLANES = 8
ROW_TILES_PER_WEIGHT = 2
FFN_ROW_TILES = 2
LRU_BLOCKS_PER_STEP = 4
FFN_TILE = 512
FOX_Q_SCALE = math.log2(math.e) / math.sqrt(HEAD_DIM)
BIAS_PIECES = 3

V7X_VMEM_LIMIT_BYTES = 60 * 1024 * 1024


def _rms(x, g):
    ms = jnp.mean(x * x, axis=-1, keepdims=True)
    return x * lax.rsqrt(ms + RMS_EPS) * g


def _dot(a, b):
    return jnp.dot(a, b, preferred_element_type=F32)


def _dot_nt(a, b):
    return lax.dot_general(a, b, (((1,), (1,)), ((), ())), preferred_element_type=F32)


def _softplus(x):
    return jnp.maximum(x, 0.0) + jnp.log1p(jnp.exp(-jnp.abs(x)))


def _cast_slab(src_ref, dst_ref):
    n_tiles, _, tc = dst_ref.shape
    for t in range(n_tiles):
        dst_ref[t] = src_ref[:, t * tc:(t + 1) * tc].astype(BF16)


def _slab_specs(w, layer, n_slabs, tc, step_of):
    _, R, C = w.shape
    rows = R // n_slabs
    slab = lambda *ids: jnp.minimum(step_of(*ids), n_slabs - 1)
    return (pl.BlockSpec((None, rows, C), lambda *ids: (layer, slab(*ids), 0)),
            pl.BlockSpec((C // tc, rows, tc), lambda *ids: (0, slab(*ids), 0)),
            jax.ShapeDtypeStruct((C // tc, R, tc), BF16))


def _params(*sem):
    return pltpu.CompilerParams(dimension_semantics=sem,
                                vmem_limit_bytes=V7X_VMEM_LIMIT_BYTES)


def _wstage_kernel(a_ref, b_ref, w_ref, wf_ref, *, f_tile):
    t = pl.program_id(0)

    @pl.when(t < f_tile)
    def _():
        w_ref[...] = a_ref[...].astype(BF16)

    @pl.when(t >= f_tile)
    def _():
        w_ref[...] = jnp.concatenate([a_ref[FOX_HEADS:, :], b_ref[...]], axis=0).astype(BF16)

    @pl.when(t == f_tile)
    def _():
        wf_ref[:FOX_HEADS, :] = a_ref[:FOX_HEADS, :].astype(BF16)
        wf_ref[FOX_HEADS:, :] = jnp.zeros((F_ROWS - FOX_HEADS, wf_ref.shape[1]), BF16)


def _stage_in_weights(w_t, layer, *, tr):
    _, N, D = w_t.shape
    n_rows = N - FOX_HEADS
    f_tile = 3 * FOX_W // tr
    return pl.pallas_call(
        functools.partial(_wstage_kernel, f_tile=f_tile),
        grid=(n_rows // tr,),
        in_specs=[pl.BlockSpec((None, tr, D), lambda t: (layer, t, 0)),
                  pl.BlockSpec((None, FOX_HEADS, D),
                               lambda t: (layer, (jnp.maximum(t, f_tile) + 1) * (tr // FOX_HEADS), 0))],
        out_specs=[pl.BlockSpec((tr, D), lambda t: (t, 0)),
                   pl.BlockSpec((F_ROWS, D), lambda t: (0, 0))],
        out_shape=[jax.ShapeDtypeStruct((n_rows, D), BF16),
                   jax.ShapeDtypeStruct((F_ROWS, D), BF16)],
        compiler_params=_params("arbitrary"),
    )(w_t, w_t)


def _inproj_kernel(x_ref, g_ref, w_ref, gh_ref, wf_ref,
                   qk_ref, vt_ref, ug_ref, f_ref, h_scr, *, tn, n_qk, n_v):
    j = pl.program_id(1)
    h_ref = h_scr.at[pl.program_id(2)]

    chunks = [slice(c * PROJ_CHUNK, (c + 1) * PROJ_CHUNK) for c in range(tn // PROJ_CHUNK)]

    def qk_tile():
        for sl in chunks:
            acc = _dot_nt(h_ref[...], w_ref[sl, :])
            for hh in range(PROJ_CHUNK // HEAD_DIM):
                lo = sl.start + hh * HEAD_DIM
                qk_ref[:, lo:lo + HEAD_DIM] = _rms(
                    acc[:, hh * HEAD_DIM:(hh + 1) * HEAD_DIM], gh_ref[...]).astype(BF16)

    @pl.when(j == 0)
    def _():
        h_ref[...] = _rms(x_ref[...], g_ref[...]).astype(BF16)
        qk_tile()

    pl.when((j > 0) & (j < n_qk))(qk_tile)

    @pl.when((j >= n_qk) & (j < n_qk + n_v))
    def _():
        for sl in chunks[:-1]:
            vt_ref[sl, :] = _dot_nt(w_ref[sl, :], h_ref[...]).astype(BF16)
        sl = chunks[-1]
        acc = _dot_nt(jnp.concatenate([w_ref[sl, :], wf_ref[...]], axis=0), h_ref[...])
        vt_ref[sl, :] = acc[:PROJ_CHUNK].astype(BF16)
        f_ref[...] = acc[PROJ_CHUNK:]

    @pl.when(j >= n_qk + n_v)
    def _():
        for sl in chunks:
            ug_ref[:, sl] = _dot_nt(h_ref[...], w_ref[sl, :])


def _in_proj(x2d, g, w_t, g_qk, wf, *, tm, tn):
    T, D = x2d.shape
    n_qk = 2 * FOX_W // tn
    n_v = FOX_W // tn
    assert n_v == 1, "the forget logits are produced by the single v step"
    n_ug = 2 * LRU_W // tn
    tiles_per_gain = FOX_W // tn
    clip = lambda v, n: jnp.clip(v, 0, n - 1)
    n_steps = n_qk + n_v + n_ug
    per = ROW_TILES_PER_WEIGHT
    n_rows = T // tm
    first = lambda p: per * p
    last = lambda p: per * p + per - 1

    def row_of(p, j, r, j_lo, j_hi):
        return jnp.where(j < j_lo, first(p), jnp.where(j < j_hi, first(p) + r, last(p)))

    x_row = lambda p, j, r: jnp.where(j == 0, first(p) + r, jnp.minimum(first(p + 1), n_rows - 1))
    return pl.pallas_call(
        functools.partial(_inproj_kernel, tn=tn, n_qk=n_qk, n_v=n_v),
        grid=(n_rows // per, n_steps, per),
        in_specs=[
            pl.BlockSpec((tm, D), lambda p, j, r: (x_row(p, j, r), 0)),
            pl.BlockSpec((1, D), lambda p, j, r: (0, 0)),
            pl.BlockSpec((tn, D), lambda p, j, r: (j, 0)),
            pl.BlockSpec((None, 1, HEAD_DIM),
                         lambda p, j, r: (clip(j // tiles_per_gain, 2), 0, 0)),
            pl.BlockSpec((F_ROWS, D), lambda p, j, r: (0, 0)),
        ],
        out_specs=[
            pl.BlockSpec((tm, tn), lambda p, j, r: (row_of(p, j, r, 0, n_qk), clip(j, n_qk))),
            pl.BlockSpec((tn, tm), lambda p, j, r: (0, row_of(p, j, r, n_qk, n_qk + n_v))),
            pl.BlockSpec((tm, tn), lambda p, j, r: (row_of(p, j, r, n_qk + n_v, n_steps),
                                                    clip(j - n_qk - n_v, n_ug))),
            pl.BlockSpec((F_ROWS, tm), lambda p, j, r: (0, row_of(p, j, r, n_qk, n_qk + n_v))),
        ],
        out_shape=[
            jax.ShapeDtypeStruct((T, 2 * FOX_W), BF16),
            jax.ShapeDtypeStruct((FOX_W, T), BF16),
            jax.ShapeDtypeStruct((T, 2 * LRU_W), F32),
            jax.ShapeDtypeStruct((F_ROWS, T), F32),
        ],
        scratch_shapes=[pltpu.VMEM((per, tm, D), BF16)],
        compiler_params=_params("arbitrary", "arbitrary", "arbitrary"),
    )(x2d, g, w_t, g_qk, wf)


def _fox_kernel(q_ref, k_ref, vt_ref, f_ref, bf_ref, side_ref, o_ref, side_out_ref,
                bias_scr, vaug_scr, *, seq):
    hp = pl.program_id(1)

    @pl.when(hp == 0)
    def _():
        z = f_ref[...] + bf_ref[...]
        c = jnp.minimum(z, 0.0) - jnp.log1p(jnp.exp(-jnp.abs(z)))
        lane = lax.broadcasted_iota(jnp.int32, c.shape, 1)
        d = 1
        while d < seq:
            c = c + jnp.where(lane >= d, pltpu.roll(c, d, axis=1), 0.0)
            d *= 2
        bias = c[:FOX_HEADS] * (-math.log2(math.e))
        pieces = []
        for _ in range(BIAS_PIECES):
            p = bias.astype(BF16).astype(F32)
            pieces.append(p)
            bias = bias - p
        pad = jnp.zeros((HEAD_DIM - BIAS_PIECES * FOX_HEADS, seq), F32)
        bias_scr[...] = jnp.concatenate(pieces + [pad], axis=0).T.astype(BF16)

    _cast_slab(side_ref, side_out_ref)
    ones_row = lax.broadcasted_iota(jnp.int32, (V_PAD_ROWS, seq), 0) == 0
    for e in range(HEADS_PER_STEP):
        lanes = slice(e * HEAD_DIM, (e + 1) * HEAD_DIM)
        vaug_scr[e, :HEAD_DIM, :] = vt_ref[lanes, :]
        vaug_scr[e, HEAD_DIM:, :] = jnp.where(ones_row, 1.0, 0.0).astype(BF16)

    tq, tk = Q_BLOCK, K_CHUNK
    dc = DIAG_CHUNK

    def selector(e, n):
        lane = lax.broadcasted_iota(jnp.int32, (n, HEAD_DIM), 1)
        h = hp * HEADS_PER_STEP + e
        mine = lane == h
        for p in range(1, BIAS_PIECES):
            mine = mine | (lane == h + p * FOX_HEADS)
        return jnp.where(mine, 1.0, 0.0).astype(BF16)

    selectors = {(e, n): selector(e, n) for e in range(HEADS_PER_STEP)
                 for n in range(dc, tq + 1, dc)}
    below_diag = (lax.broadcasted_iota(jnp.int32, (dc, dc), 0)
                  <= lax.broadcasted_iota(jnp.int32, (dc, dc), 1))

    def scores(e, lo, q0, k0, nk):
        q = q_ref[lo + q0:lo + tq, e * HEAD_DIM:(e + 1) * HEAD_DIM]
        q_aug = jnp.concatenate([q, selectors[e, tq - q0]], axis=1)
        k_aug = jnp.concatenate([k_ref[k0:k0 + nk, e * HEAD_DIM:(e + 1) * HEAD_DIM],
                                 bias_scr[k0:k0 + nk, :]], axis=1)
        return _dot_nt(k_aug, q_aug)

    steps = []
    for qi in range(seq // tq):
        lo = qi * tq
        pieces = [(0, kc * tk, tk) for kc in range(lo // tk)]
        pieces += [(j * dc, lo + j * dc, dc) for j in range(tq // dc)]
        steps += [(e, lo) + piece for piece in pieces for e in range(HEADS_PER_STEP)]
    pending = [scores(*st) for st in steps[:SCORE_LOOKAHEAD]]
    m = [None] * HEADS_PER_STEP
    acc = [None] * HEADS_PER_STEP
    for idx, (e, lo, q0, k0, nk) in enumerate(steps):
        t = pending.pop(0)
        if idx + SCORE_LOOKAHEAD < len(steps):
            pending.append(scores(*steps[idx + SCORE_LOOKAHEAD]))
        if k0 >= lo:
            masked = jnp.where(below_diag, t[:, :dc], -jnp.inf)
            t = jnp.concatenate([masked, t[:, dc:]], axis=1) if q0 + dc < tq else masked
        m_old = None if k0 == 0 else m[e][:, q0:]
        m_parts, p_parts = [], []
        for c in range(0, t.shape[1], SOFTMAX_LANES):
            t_c = t[:, c:c + SOFTMAX_LANES]
            m_c = jnp.max(t_c, axis=0, keepdims=True)
            if k0 > 0:
                m_c = jnp.maximum(m_old[:, c:c + SOFTMAX_LANES], m_c)
            m_parts.append(m_c)
            p_parts.append(jnp.exp2(t_c - m_c).astype(BF16))
        m_new = jnp.concatenate(m_parts, axis=1) if len(m_parts) > 1 else m_parts[0]
        p = jnp.concatenate(p_parts, axis=1) if len(p_parts) > 1 else p_parts[0]
        pv = _dot(vaug_scr[e, :, k0:k0 + nk], p)
        if k0 == 0:
            acc[e], m[e] = pv, m_new
        else:
            upd = acc[e][:, q0:] * jnp.exp2(m_old - m_new) + pv
            acc[e] = jnp.concatenate([acc[e][:, :q0], upd], axis=1) if q0 else upd
            m[e] = jnp.concatenate([m[e][:, :q0], m_new], axis=1) if q0 else m_new
        if k0 + nk == lo + tq:
            inv_l = 1.0 / acc[e][HEAD_DIM:HEAD_DIM + 1, :]
            o_ref[lo:lo + tq, e * HEAD_DIM:(e + 1) * HEAD_DIM] = (acc[e][:HEAD_DIM, :] * inv_l).T


def _fox_attention(qk, v_t, f_t, b_f, w_side, layer, *, batch, seq, n_slabs, side_tc):
    n_pairs = FOX_HEADS // HEADS_PER_STEP
    wide = HEADS_PER_STEP * HEAD_DIM
    side_in, side_out, side_shape = _slab_specs(w_side, layer, n_slabs, side_tc,
                                                lambda b, hp: b * n_pairs + hp)
    return pl.pallas_call(
        functools.partial(_fox_kernel, seq=seq),
        grid=(batch, n_pairs),
        in_specs=[
            pl.BlockSpec((None, seq, wide), lambda b, hp: (b, 0, hp)),
            pl.BlockSpec((None, seq, wide), lambda b, hp: (b, 0, n_pairs + hp)),
            pl.BlockSpec((wide, seq), lambda b, hp: (hp, b)),
            pl.BlockSpec((F_ROWS, seq), lambda b, hp: (0, b)),
            pl.BlockSpec((F_ROWS, 1), lambda b, hp: (0, 0)),
            side_in,
        ],
        out_specs=[pl.BlockSpec((None, seq, wide), lambda b, hp: (b, 0, hp)), side_out],
        out_shape=[jax.ShapeDtypeStruct((batch, seq, FOX_W), F32), side_shape],
        scratch_shapes=[pltpu.VMEM((seq, HEAD_DIM), BF16),
                        pltpu.VMEM((HEADS_PER_STEP, HEAD_DIM + V_PAD_ROWS, seq), BF16)],
        compiler_params=_params("arbitrary", "arbitrary"),
    )(qk, qk, v_t, f_t, b_f, w_side)


def _lru_kernel(*refs, seq, n_side):
    u_refs = refs[:LRU_BLOCKS_PER_STEP]
    refs = refs[LRU_BLOCKS_PER_STEP:]
    gate_ref, cw_ref, cb_ref, wg_ref, bra_ref, bri_ref, lam_ref = refs[:7]
    side_refs = refs[7:7 + n_side]
    o_ref = refs[7 + n_side]
    side_out_refs = refs[8 + n_side:8 + 2 * n_side]
    at_scr, bt_scr = refs[8 + 2 * n_side:]
    for src_ref, dst_ref in zip(side_refs, side_out_refs):
        _cast_slab(src_ref, dst_ref)
    width = o_ref.shape[1]
    row = lax.broadcasted_iota(jnp.int32, (SUBLANES, LRU_BLOCK), 0)

    def shifted(d):
        cols = []
        for u_ref in u_refs:
            head = jnp.where(row >= d, pltpu.roll(u_ref[0:SUBLANES, :], d, axis=0), 0.0)
            cols.append(jnp.concatenate([head, u_ref[SUBLANES - d:seq - d, :]], axis=0))
        return jnp.concatenate(cols, axis=1)

    cw = cw_ref[...]
    uc = cw[0:1, :] * shifted(3)
    uc = uc + cw[1:2, :] * shifted(2)
    uc = uc + cw[2:3, :] * shifted(1)
    uc = uc + cw[3:4, :] * jnp.concatenate([u_ref[...] for u_ref in u_refs], axis=1)
    uc = cb_ref[...] + uc

    ucb = uc.astype(BF16)
    gates = [_dot(ucb[:, n * LRU_BLOCK:(n + 1) * LRU_BLOCK], wg_ref[n] * 0.5)
             for n in range(width // LRU_BLOCK)]
    pre_r = jnp.concatenate([g[:, :LRU_BLOCK] for g in gates], axis=1)
    pre_i = jnp.concatenate([g[:, LRU_BLOCK:] for g in gates], axis=1)
    half_c = (-0.5 * LRU_C) * _softplus(-lam_ref[...])
    log_a = half_c * jnp.tanh(pre_r + 0.5 * bra_ref[...]) + half_c
    neg_2i = -1.0 - jnp.tanh(pre_i + 0.5 * bri_ref[...])
    a = jnp.exp(log_a)
    t = jnp.tanh(log_a)
    neg_quarter_root = jnp.where(t < 0.0, t * lax.rsqrt((-8.0 * t) * (1.0 - t)), 0.0)
    b = neg_quarter_root * (neg_2i * uc)

    n_blocks = width // LRU_BLOCK
    seg = seq // SUBLANES
    pad = SUBLANES
    blank = jnp.zeros((pad, LRU_BLOCK), F32)
    for n in range(n_blocks):
        for scr, val in ((at_scr, a), (bt_scr, b)):
            scr[n, :pad, :] = blank
            scr[n, pad:pad + seq, :] = val[:, n * LRU_BLOCK:(n + 1) * LRU_BLOCK]
            scr[n, pad + seq:, :] = blank

    def local(g, state, edge):
        steps = pl.ds(pad + g, SUBLANES, stride=seg + 1)
        inside = (row + g >= 0) & (row + g < seg)
        out = []
        for n, (h, p) in enumerate(state):
            a_g, b_g = at_scr[n, steps, :], bt_scr[n, steps, :]
            if edge:
                h_new = jnp.where(inside, a_g * h + b_g, h)
                p_new = jnp.where(inside, a_g * p, p)
                bt_scr[n, steps, :] = jnp.where(inside, h_new, b_g)
                at_scr[n, steps, :] = jnp.where(inside, p_new, a_g)
            else:
                h_new, p_new = a_g * h + b_g, a_g * p
                bt_scr[n, steps, :] = h_new
                at_scr[n, steps, :] = p_new
            out.append((h_new, p_new))
        return tuple(out)

    zero = jnp.zeros((SUBLANES, LRU_BLOCK), F32)
    state = ((zero, zero + 1.0),) * n_blocks
    full = seg - SUBLANES
    for g in range(1 - SUBLANES, 0):
        state = local(g, state, True)
    state = lax.fori_loop(0, full, functools.partial(local, edge=False), state, unroll=8)
    for g in range(full, seg):
        state = local(g, state, True)
    cols = []
    for n, (h, p) in enumerate(state):
        d = 1
        while d < SUBLANES:
            keep = row >= d
            h = p * jnp.where(keep, pltpu.roll(h, d, axis=0), 0.0) + h
            p = p * jnp.where(keep, pltpu.roll(p, d, axis=0), 1.0)
            d *= 2
        h_in = jnp.where(row >= 1, pltpu.roll(h, 1, axis=0), 0.0)
        cols.append(jnp.concatenate(
            [bt_scr[n, pad + j * seg:pad + (j + 1) * seg, :]
             + at_scr[n, pad + j * seg:pad + (j + 1) * seg, :] * h_in[j:j + 1, :]
             for j in range(SUBLANES)], axis=0))
    gate = gate_ref[...]
    inner = gate * (GELU_C1 + (GELU_C1 * GELU_C3) * (gate * gate))
    o_ref[...] = jnp.concatenate(cols, axis=1) * (gate * (1.0 + jnp.tanh(inner)))


def _rg_lru(ug, cw, cb, wg, bra, bri, lam, side_weights, layer, *, batch, seq):
    per = LRU_BLOCKS_PER_STEP
    nb = LRU_BLOCKS // per
    width = per * LRU_BLOCK
    vec = lambda: pl.BlockSpec((1, width), lambda b, n: (0, n))
    n_slabs = batch * nb
    sides = [_slab_specs(w, layer, n_slabs, w.shape[2], lambda b, n: b * nb + n)
             for w in side_weights]
    res = pl.pallas_call(
        functools.partial(_lru_kernel, seq=seq, n_side=len(sides)),
        grid=(batch, nb),
        in_specs=[
            *[pl.BlockSpec((None, seq, LRU_BLOCK), lambda b, n, e=e: (b, 0, per * n + e))
              for e in range(per)],
            pl.BlockSpec((None, seq, width), lambda b, n: (b, 0, nb + n)),
            pl.BlockSpec((CONV_W, width), lambda b, n: (0, n)),
            vec(),
            pl.BlockSpec((per, LRU_BLOCK, 2 * LRU_BLOCK), lambda b, n: (n, 0, 0)),
            vec(), vec(), vec(),
            *[sd[0] for sd in sides],
        ],
        out_specs=[pl.BlockSpec((None, seq, width), lambda b, n: (b, 0, n)),
                   *[sd[1] for sd in sides]],
        out_shape=[jax.ShapeDtypeStruct((batch, seq, LRU_W), F32), *[sd[2] for sd in sides]],
        scratch_shapes=[pltpu.VMEM((per, seq + 2 * SUBLANES, LRU_BLOCK), F32)] * 2,
        compiler_params=_params("arbitrary", "arbitrary"),
    )(*[ug] * per, ug, cw, cb, wg, bra, bri, lam, *side_weights)
    return res[0], [r[0] for r in res[1:]]


def _mix_xattn_kernel(of_ref, yl_ref, gf_ref, gl_ref, wout_ref, x_ref, gx_ref, wq_ref, gq_ref,
                      mem_ref, gm_ref, wkv_ref, gk_ref, wo_ref, side_ref, o_ref, side_out_ref,
                      ckv_ref, *, chunk):
    @pl.when(pl.program_id(1) == 0)
    def _():
        kv = _dot(_rms(mem_ref[...], gm_ref[...]).astype(BF16), wkv_ref[...])
        for hh in range(XATT_HEADS):
            sl = slice(hh * HEAD_DIM, (hh + 1) * HEAD_DIM)
            ckv_ref[:, sl] = _rms(kv[:, sl], gk_ref[...]).astype(BF16)
        ckv_ref[:, XATT_W:] = kv[:, XATT_W:].astype(BF16)

    _cast_slab(side_ref, side_out_ref)
    mf = _rms(of_ref[...], gf_ref[...]).astype(BF16)
    ml = _rms(yl_ref[...], gl_ref[...]).astype(BF16)
    for c in range(o_ref.shape[1] // chunk):
        sl = slice(c * chunk, (c + 1) * chunk)
        acc = _dot(mf, wout_ref[:FOX_W, sl]) + _dot(ml, wout_ref[FOX_W:, sl])
        o_ref[:, sl] = x_ref[:, sl] + acc

    x1 = o_ref[...]
    cq = _dot(_rms(x1, gx_ref[...]).astype(BF16), wq_ref[...])
    scale = 1.0 / math.sqrt(HEAD_DIM)
    scores = []
    for hh in range(XATT_HEADS):
        sl = slice(hh * HEAD_DIM, (hh + 1) * HEAD_DIM)
        qh = _rms(cq[:, sl], gq_ref[...]).astype(BF16)
        scores.append(_dot_nt(qh, ckv_ref[:, sl]))
    heads = []
    for hh in range(XATT_HEADS):
        vh = ckv_ref[:, XATT_W + hh * HEAD_DIM:XATT_W + (hh + 1) * HEAD_DIM]
        s = scores[hh] * scale
        e = jnp.exp(s - jnp.max(s, axis=-1, keepdims=True))
        l = jnp.sum(e, axis=-1, keepdims=True)
        heads.append((_dot(e.astype(BF16), vh) / l).astype(BF16))
    ox = jnp.concatenate(heads, axis=-1)
    o_ref[...] = x1 + _dot(ox, wo_ref[...])


def _mix_xattn(o_fox, y_lru, gf, gl, w_out, x3d, gx, w_cq, gq, mem, gm, w_ckv, gk, w_co,
               w_side, layer, *, tm):
    B, S, D = x3d.shape
    M = mem.shape[1]
    n_tiles = S // tm
    side_in, side_out, side_shape = _slab_specs(w_side, layer, B * n_tiles, w_side.shape[2],
                                                lambda b, i: b * n_tiles + i)
    const = lambda shape: pl.BlockSpec(shape, lambda b, i: (0,) * len(shape),
                                       pipeline_mode=pl.Buffered(1))
    tile = lambda w: pl.BlockSpec((None, tm, w), lambda b, i: (b, i, 0))
    return pl.pallas_call(
        functools.partial(_mix_xattn_kernel, chunk=2 * PROJ_CHUNK),
        grid=(B, S // tm),
        in_specs=[
            tile(FOX_W), tile(LRU_W), const((1, FOX_W)), const((1, LRU_W)),
            const((FOX_W + LRU_W, D)), tile(D), const((1, D)), const((D, XATT_W)),
            const((1, HEAD_DIM)),
            pl.BlockSpec((None, M, D), lambda b, i: (b, 0, 0)),
            const((1, D)), const((D, 2 * XATT_W)), const((1, HEAD_DIM)),
            const((XATT_W, D)),
            side_in,
        ],
        out_specs=[tile(D), side_out],
        out_shape=[jax.ShapeDtypeStruct((B, S, D), F32), side_shape],
        scratch_shapes=[pltpu.VMEM((M, 2 * XATT_W), BF16)],
        compiler_params=_params("arbitrary", "arbitrary"),
    )(o_fox, y_lru, gf, gl, w_out, x3d, gx, w_cq, gq, mem, gm, w_ckv, gk, w_co, w_side)


def _ffn_kernel(x_ref, g_ref, wg_ref, wu_ref, wd_ref, o_hbm, acc_scr, h_scr, sems,
                *, tm, nk, n_groups):
    p, k, r = pl.program_id(0), pl.program_id(1), pl.program_id(2)
    acc_ref = acc_scr.at[r]
    h_ref = h_scr.at[r]
    row_tile = p * FFN_ROW_TILES + r

    def writeback(tile, slot):
        rows = pl.ds(pl.multiple_of(tile * tm, tm), tm)
        return pltpu.make_async_copy(acc_scr.at[slot], o_hbm.at[rows, :], sems.at[slot])

    def hidden_tile(base_ref):
        h = h_ref[...]
        gate = _dot(h, wg_ref[...])
        up = _dot(h, wu_ref[...])
        half = 0.5 * gate
        act = ((half + half * jnp.tanh(half)) * up).astype(BF16)
        acc_ref[...] = base_ref[...] + _dot(act, wd_ref[...])

    @pl.when((k == 0) & (p > 0))
    def _():
        writeback(row_tile - FFN_ROW_TILES, r).wait()

    @pl.when(k == 0)
    def _():
        h_ref[...] = _rms(x_ref[...], g_ref[...]).astype(BF16)
        hidden_tile(x_ref)

    pl.when(k > 0)(lambda: hidden_tile(acc_ref))

    @pl.when(k == nk - 1)
    def _():
        writeback(row_tile, r).start()

    @pl.when((k == nk - 1) & (p == n_groups - 1) & (r == FFN_ROW_TILES - 1))
    def _():
        for slot in range(FFN_ROW_TILES):
            writeback((n_groups - 1) * FFN_ROW_TILES + slot, slot).wait()


def _ffn(x2d, g, w_gu, w_d, *, tm):
    T, D = x2d.shape
    H = w_d.shape[0]
    th = w_gu.shape[2]
    nk = H // th
    per = FFN_ROW_TILES
    n_rows = T // tm
    n_groups = n_rows // per
    x_row = lambda p, k, r: jnp.where(k == 0, per * p + r,
                                      jnp.minimum(per * (p + 1), n_rows - 1))
    return pl.pallas_call(
        functools.partial(_ffn_kernel, tm=tm, nk=nk, n_groups=n_groups),
        grid=(n_groups, nk, per),
        in_specs=[
            pl.BlockSpec((tm, D), lambda p, k, r: (x_row(p, k, r), 0)),
            pl.BlockSpec((1, D), lambda p, k, r: (0, 0)),
            pl.BlockSpec((None, D, th), lambda p, k, r: (k, 0, 0)),
            pl.BlockSpec((None, D, th), lambda p, k, r: (nk + k, 0, 0)),
            pl.BlockSpec((th, D), lambda p, k, r: (k, 0)),
        ],
        out_specs=pl.BlockSpec(memory_space=pl.ANY),
        out_shape=jax.ShapeDtypeStruct((T, D), F32),
        scratch_shapes=[pltpu.VMEM((per, tm, D), F32), pltpu.VMEM((per, tm, D), BF16),
                        pltpu.SemaphoreType.DMA((per,))],
        compiler_params=_params("arbitrary", "arbitrary", "arbitrary"),
    )(x2d, g, w_gu, w_gu, w_d)


def kernel(x, mem, g_mix, w_in, b_f, g_q, g_k, conv_w, conv_b, w_ra, b_ra, w_ri, b_ri, lam,
           g_fox_out, g_lru_out, w_out, g_xattn, g_mem, w_cq, w_ckv, g_cq, g_ck, w_co, g_ffn,
           w_gate_up, w_down):
    B, S, D = x.shape
    T = B * S
    depth = g_mix.shape[0]
    row = lambda v: v.reshape(1, -1).astype(F32)

    for l in range(depth):
        w_t, w_f = _stage_in_weights(jnp.swapaxes(w_in, 1, 2), l, tr=512)
        g_qk = jnp.stack([g_q[l] * FOX_Q_SCALE, g_k[l]]).reshape(2, 1, HEAD_DIM).astype(F32)
        bf_col = jnp.zeros((F_ROWS, 1), F32).at[:FOX_HEADS, 0].set(b_f[l])
        w_gates = jnp.concatenate([w_ra[l], w_ri[l]], axis=-1).astype(BF16)

        x2d = x.reshape(T, D)

        qk, v_t, ug, f_t = _in_proj(x2d, row(g_mix[l]), w_t, g_qk, w_f, tm=1024, tn=1024)
        o_fox, w_gu_bf = _fox_attention(qk.reshape(B, S, 2 * FOX_W), v_t, f_t, bf_col,
                                        w_gate_up, l, batch=B, seq=S,
                                        n_slabs=B * FOX_HEADS // HEADS_PER_STEP,
                                        side_tc=FFN_TILE)
        y_lru, (w_out_bf, w_cq_bf, w_ckv_bf, w_co_bf) = _rg_lru(
            ug.reshape(B, S, 2 * LRU_W), conv_w[l], row(conv_b[l]), w_gates, row(b_ra[l]),
            row(b_ri[l]), row(lam[l]), [w_out, w_cq, w_ckv, w_co], l, batch=B, seq=S)

        x3d, w_down_bf = _mix_xattn(
            o_fox, y_lru, row(g_fox_out[l]), row(g_lru_out[l]), w_out_bf, x2d.reshape(B, S, D),
            row(g_xattn[l]), w_cq_bf, row(g_cq[l]), mem, row(g_mem[l]), w_ckv_bf, row(g_ck[l]),
            w_co_bf, w_down, l, tm=512)

        x2d = _ffn(x3d.reshape(T, D), row(g_ffn[l]), w_gu_bf, w_down_bf[0], tm=1024)
        x = x2d.reshape(B, S, D)
    return x
```
